```python
import math
import jax, jax.numpy as jnp
from jax import lax
import numpy as np

D_MODEL = 2048
BATCH = 2
SEQ = 4096
DEPTH = 2

HEAD_DIM = 128
DILATED_GROUPS = ((128, 1), (512, 4), (2048, 16))
HEADS_PER_GROUP = 4
N_DIL_GROUPS = len(DILATED_GROUPS)
N_SELF_HEADS = N_DIL_GROUPS * HEADS_PER_GROUP
N_RET_HEADS = 12
SELF_WIDTH = N_SELF_HEADS * HEAD_DIM
MEM_HEADS = 4
MEM_LEN = 256
MEM_WIDTH = MEM_HEADS * HEAD_DIM
MIX_WIDTH = SELF_WIDTH + MEM_WIDTH
RET_CHUNK = 128
N_EXPERTS = 32
TOP_K = 4
D_EXPERT = D_MODEL
SWIGLU_ALPHA = 1.702
SWIGLU_LIMIT = 7.0
MOE_BLOCK = 128
N_MIXERS = 2
DEEPNORM_ALPHA = (2 * DEPTH) ** 0.25
DEEPNORM_BETA = (8 * DEPTH) ** -0.25
LN_EPS = 1e-5
NEG_INF = -1e30

kernel_name = "hybrid_dilated_retention_moe_deepnorm"


def _alibi_slopes(n):
    def pow2(m):
        start = 2.0 ** (-8.0 / m)
        return [start ** (i + 1) for i in range(m)]
    if math.log2(n).is_integer():
        s = pow2(n)
    else:
        c = 2 ** math.floor(math.log2(n))
        s = pow2(c) + pow2(2 * c)[0::2][: n - c]
    return sorted(s, reverse=True)


def _layer_norm(x, g, b):
    xf = x.astype(jnp.float32)
    mu = xf.mean(-1, keepdims=True)
    var = jnp.square(xf - mu).mean(-1, keepdims=True)
    return ((xf - mu) * lax.rsqrt(var + LN_EPS) * g + b).astype(x.dtype)


def _dilated_window_group(q, k, v, slopes, window, dilation):
    B, S, h, hd = q.shape
    steps = window // dilation
    L = S // dilation
    nb = -(-L // steps)
    Lp = nb * steps

    def to_sub(t):
        t = t.reshape(B, L, dilation, h, hd).transpose(0, 2, 3, 1, 4)
        t = jnp.pad(t, ((0, 0), (0, 0), (0, 0), (0, Lp - L), (0, 0)))
        return t.reshape(B, dilation, h, nb, steps, hd)

    def with_prev(t):
        prev = jnp.pad(t[:, :, :, :-1], ((0, 0), (0, 0), (0, 0), (1, 0), (0, 0), (0, 0)))
        return jnp.concatenate([prev, t], axis=4)

    qb, kb, vb = to_sub(q), to_sub(k), to_sub(v)
    kc, vc = with_prev(kb), with_prev(vb)
    s = jnp.einsum('brhnqd,brhnkd->brhnqk', qb, kc,
                   preferred_element_type=jnp.float32) * (hd ** -0.5)
    qi = jnp.arange(steps)[:, None]
    kj = jnp.arange(2 * steps)[None, :]
    dist = steps + qi - kj
    key_pos = jnp.arange(nb)[:, None, None] * steps + kj[None] - steps
    valid = (dist >= 0) & (dist <= steps) & (key_pos >= 0)
    bias = -slopes[:, None, None, None] * (dist * dilation).astype(jnp.float32)[None, None]
    s = jnp.where(valid, s + bias, NEG_INF)
    lse = jax.nn.logsumexp(s, axis=-1)
    p = jnp.exp(s - lse[..., None])
    o = jnp.einsum('brhnqk,brhnkd->brhnqd', p.astype(v.dtype), vc)
    o = o.reshape(B, dilation, h, Lp, hd)[:, :, :, :L].transpose(0, 3, 1, 2, 4).reshape(B, S, h, hd)
    lse = lse.reshape(B, dilation, h, Lp)[..., :L].transpose(0, 3, 1, 2).reshape(B, S, h)
    return o, lse


def _dilated_attention(q, k, v, slopes):
    B, S, _ = q.shape
    shp = (B, S, N_DIL_GROUPS, HEADS_PER_GROUP, HEAD_DIM)
    qh, kh, vh = q.reshape(shp), k.reshape(shp), v.reshape(shp)
    outs, lses = [], []
    for g, (window, dil) in enumerate(DILATED_GROUPS):
        o, l = _dilated_window_group(qh[:, :, g], kh[:, :, g], vh[:, :, g],
                                     slopes[g * HEADS_PER_GROUP:(g + 1) * HEADS_PER_GROUP], window, dil)
        outs.append(o)
        lses.append(l)
    w = jax.nn.softmax(jnp.stack(lses, axis=2), axis=2)
    o = jnp.stack(outs, axis=2) * w[..., None].astype(q.dtype)
    return o.reshape(B, S, SELF_WIDTH)


def _retention(q, k, v, g, gn_gain, log_gamma):
    B, S, _ = q.shape
    H, hd, C = N_RET_HEADS, HEAD_DIM, RET_CHUNK
    n = S // C

    def heads(t):
        return t.astype(jnp.float32).reshape(B, n, C, H, hd)

    qc, kc, vc = heads(q), heads(k) * (hd ** -0.5), heads(v)
    idx = jnp.arange(C, dtype=jnp.float32)
    diff = idx[:, None] - idx[None, :]
    decay = jnp.where(diff >= 0, jnp.exp(jnp.maximum(diff, 0.0)[None] * log_gamma[:, None, None]), 0.0)
    scores = jnp.einsum('bnqhd,bnkhd->bnhqk', qc, kc) * decay
    intra = jnp.einsum('bnhqk,bnkhd->bnqhd', scores, vc)
    k_decay = jnp.exp((C - 1 - idx)[:, None] * log_gamma[None, :])
    chunk_kv = jnp.einsum('bnkhd,kh,bnkhe->nbhde', kc, k_decay, vc)
    chunk_decay = jnp.exp(C * log_gamma)[:, None, None]

    def step(state, kv):
        return state * chunk_decay + kv, state

    _, prev = lax.scan(step, jnp.zeros((B, H, hd, hd), jnp.float32), chunk_kv)
    q_decay = jnp.exp((idx + 1.0)[:, None] * log_gamma[None, :])
    cross = jnp.einsum('bnqhd,qh,nbhde->bnqhe', qc, q_decay, prev)
    r = (intra + cross).reshape(B, S, H, hd)
    mu = r.mean(-1, keepdims=True)
    var = jnp.square(r - mu).mean(-1, keepdims=True)
    r = ((r - mu) * lax.rsqrt(var + LN_EPS)).reshape(B, S, H * hd) * gn_gain.astype(jnp.float32)
    return (jax.nn.silu(g.astype(jnp.float32)) * r).astype(q.dtype)


def _memory_attention(q, mem_k, mem_v):
    B, S, _ = q.shape
    M = mem_k.shape[1]
    qh = q.reshape(B, S, MEM_HEADS, HEAD_DIM)
    kh = mem_k.reshape(B, M, MEM_HEADS, HEAD_DIM)
    vh = mem_v.reshape(B, M, MEM_HEADS, HEAD_DIM)
    s = jnp.einsum('bshd,bmhd->bhsm', qh, kh, preferred_element_type=jnp.float32) * (HEAD_DIM ** -0.5)
    p = jax.nn.softmax(s, axis=-1).astype(q.dtype)
    return jnp.einsum('bhsm,bmhd->bshd', p, vh).reshape(B, S, MEM_WIDTH)


def _moe(h, layer, router_w, router_b, w_in, b_in, w_out, b_out):
    B, S, D = h.shape
    N = B * S
    xf = h.reshape(N, D)
    logits = (xf @ router_w[layer] + router_b[layer]).astype(jnp.float32)
    top_v, top_e = lax.top_k(logits, TOP_K)
    gates = jax.nn.softmax(top_v, axis=-1)
    e_flat = top_e.reshape(-1)
    g_flat = gates.reshape(-1)
    n_assign = N * TOP_K
    order = jnp.argsort(e_flat)
    e_sorted = e_flat[order]
    tok_sorted = (order // TOP_K).astype(jnp.int32)
    counts = jnp.bincount(e_flat, length=N_EXPERTS)
    starts = jnp.cumsum(counts) - counts
    padded = (counts + MOE_BLOCK - 1) // MOE_BLOCK * MOE_BLOCK
    pad_ends = jnp.cumsum(padded)
    pad_starts = pad_ends - padded
    dest = (pad_starts[e_sorted] + jnp.arange(n_assign) - starts[e_sorted]).astype(jnp.int32)
    n_blocks = (n_assign + N_EXPERTS * (MOE_BLOCK - 1) + MOE_BLOCK - 1) // MOE_BLOCK
    row_tok = jnp.full((n_blocks * MOE_BLOCK,), N, jnp.int32).at[dest].set(tok_sorted)
    block_e = jnp.minimum(jnp.searchsorted(pad_ends, jnp.arange(n_blocks) * MOE_BLOCK, side='right'),
                          N_EXPERTS - 1).astype(jnp.int32)
    x_rows = jnp.concatenate([xf, jnp.zeros((1, D), xf.dtype)], 0)[row_tok].reshape(n_blocks, MOE_BLOCK, D)

    def expert_block(args):
        xb, e = args
        hcat = xb @ w_in[layer, e] + b_in[layer, e]
        gate, lin = hcat[:, :D_EXPERT], hcat[:, D_EXPERT:]
        gate = jnp.minimum(gate, SWIGLU_LIMIT)
        lin = jnp.clip(lin, -SWIGLU_LIMIT, SWIGLU_LIMIT)
        act = gate * jax.nn.sigmoid(SWIGLU_ALPHA * gate) * (lin + 1.0)
        return act @ w_out[layer, e] + b_out[layer, e]

    y_rows = lax.map(expert_block, (x_rows, block_e)).reshape(-1, D)
    y_sel = y_rows[dest] * g_flat[order][:, None].astype(y_rows.dtype)
    out = jax.ops.segment_sum(y_sel, tok_sorted, num_segments=N)
    return out.reshape(B, S, D).astype(h.dtype)


def setup_inputs(seed: int = 0) -> dict:
    key = jax.random.key(seed)
    ks = jax.random.split(key, 20)
    n_dil = (DEPTH + 1) // 2
    n_ret = DEPTH // 2
    beta = DEEPNORM_BETA
    W, MW, D = SELF_WIDTH, MEM_WIDTH, D_MODEL

    def normal(k, shape, scale):
        return jax.random.normal(k, shape, jnp.float32) * scale

    col_dil = jnp.concatenate([jnp.ones((2 * W,)), jnp.full((W,), beta), jnp.ones((MW,))]).astype(jnp.float32)
    col_ret = jnp.concatenate([jnp.ones((2 * W,)), jnp.full((W,), beta), jnp.ones((W + MW,))]).astype(jnp.float32)
    col_mem = jnp.concatenate([jnp.ones((MW,)), jnp.full((MW,), beta)]).astype(jnp.float32)
    return {
        "x": normal(ks[0], (BATCH, SEQ, D), 1.0),
        "mem": normal(ks[1], (BATCH, MEM_LEN, D), 1.0),
        "w_in_dil": normal(ks[2], (n_dil, D, 3 * W + MW), D ** -0.5) * col_dil,
        "w_in_ret": normal(ks[3], (n_ret, D, 4 * W + MW), D ** -0.5) * col_ret,
        "ret_gn_g": 1.0 + normal(ks[4], (n_ret, W), 0.02),
        "w_mem_kv": normal(ks[5], (DEPTH, D, 2 * MW), D ** -0.5) * col_mem,
        "w_mix_out": normal(ks[6], (DEPTH, MIX_WIDTH, D), beta * MIX_WIDTH ** -0.5),
        "ln_mix_g": 1.0 + normal(ks[7], (DEPTH, D), 0.02),
        "ln_mix_b": normal(ks[8], (DEPTH, D), 0.01),
        "router_w": normal(ks[9], (DEPTH, D, N_EXPERTS), D ** -0.5),
        "router_b": normal(ks[10], (DEPTH, N_EXPERTS), 0.01),
        "moe_w_in": normal(ks[11], (DEPTH, N_EXPERTS, D, 2 * D_EXPERT), D ** -0.5),
        "moe_b_in": normal(ks[12], (DEPTH, N_EXPERTS, 2 * D_EXPERT), 0.02),
        "moe_w_out": normal(ks[13], (DEPTH, N_EXPERTS, D_EXPERT, D), beta * D_EXPERT ** -0.5),
        "moe_b_out": normal(ks[14], (DEPTH, N_EXPERTS, D), 0.01),
        "ln_ffn_g": 1.0 + normal(ks[15], (DEPTH, D), 0.02),
        "ln_ffn_b": normal(ks[16], (DEPTH, D), 0.01),
    }


def reference(x, mem, w_in_dil, w_in_ret, ret_gn_g, w_mem_kv, w_mix_out, ln_mix_g, ln_mix_b,
              router_w, router_b, moe_w_in, moe_b_in, moe_w_out, moe_b_out, ln_ffn_g, ln_ffn_b):
    W = SELF_WIDTH
    slopes = jnp.asarray(_alibi_slopes(N_SELF_HEADS), jnp.float32)
    log_gamma = jnp.log1p(-jnp.exp2(-(5.0 + jnp.arange(N_RET_HEADS, dtype=jnp.float32))))
    h = x
    for layer in range(DEPTH):
        kind = layer % N_MIXERS
        slot = layer // N_MIXERS
        if kind == 0:
            proj = h @ w_in_dil[slot]
            q, k, v, q_mem = jnp.split(proj, [W, 2 * W, 3 * W], axis=-1)
            self_out = _dilated_attention(q, k, v, slopes)
        else:
            proj = h @ w_in_ret[slot]
            q, k, v, g, q_mem = jnp.split(proj, [W, 2 * W, 3 * W, 4 * W], axis=-1)
            self_out = _retention(q, k, v, g, ret_gn_g[slot], log_gamma)
        mem_k, mem_v = jnp.split(mem @ w_mem_kv[layer], [MEM_WIDTH], axis=-1)
        mem_out = _memory_attention(q_mem, mem_k, mem_v)
        mix = jnp.concatenate([self_out, mem_out], axis=-1) @ w_mix_out[layer]
        h = _layer_norm(DEEPNORM_ALPHA * h + mix, ln_mix_g[layer], ln_mix_b[layer])
        ffn = _moe(h, layer, router_w, router_b, moe_w_in, moe_b_in, moe_w_out, moe_b_out)
        h = _layer_norm(DEEPNORM_ALPHA * h + ffn, ln_ffn_g[layer], ln_ffn_b[layer])
    return h
```

```python
import functools
import math

import jax
import jax.numpy as jnp
from jax import lax
from jax.experimental import pallas as pl
from jax.experimental.pallas import tpu as pltpu

D_MODEL = 2048
BATCH = 2
SEQ = 4096
DEPTH = 2
HEAD_DIM = 128
DILATED_GROUPS = ((128, 1), (512, 4), (2048, 16))
HEADS_PER_GROUP = 4
N_SELF_HEADS = len(DILATED_GROUPS) * HEADS_PER_GROUP
N_RET_HEADS = 12
SELF_WIDTH = N_SELF_HEADS * HEAD_DIM
MEM_HEADS = 4
MEM_LEN = 256
MEM_WIDTH = MEM_HEADS * HEAD_DIM
RET_CHUNK = 128
N_EXPERTS = 32
TOP_K = 4
D_EXPERT = D_MODEL
SWIGLU_ALPHA = 1.702
SWIGLU_LIMIT = 7.0
DEEPNORM_ALPHA = (2 * DEPTH) ** 0.25
LN_EPS = 1e-5
NEG_INF = -1e30

N_TOK = BATCH * SEQ
GROUP_WIDTH = HEADS_PER_GROUP * HEAD_DIM
ATTN_STEPS = 128

LANES = 128
VMEM_LIMIT = 56 * 1024 * 1024

MM_TM = 1024
MM_TN = 512
ROW_TILE = 512
ROUTER_TM = 256

MOE_SUB = 256
MOE_TILE = 1280
MOE_SUBS = MOE_TILE // MOE_SUB
MOE_TF = 512
MOE_TILES = (N_TOK * TOP_K) // MOE_TILE + N_EXPERTS
MOE_ROWS = (MOE_TILES + 1) * MOE_TILE
DISPATCH_TB = 256
COMBINE_TB = 128


def _alibi_slopes(n):
    def pow2(m):
        start = 2.0 ** (-8.0 / m)
        return [start ** (i + 1) for i in range(m)]

    if math.log2(n).is_integer():
        s = pow2(n)
    else:
        c = 2 ** math.floor(math.log2(n))
        s = pow2(c) + pow2(2 * c)[0::2][: n - c]
    return sorted(s, reverse=True)


def _params(*sem):
    return pltpu.CompilerParams(dimension_semantics=sem, vmem_limit_bytes=VMEM_LIMIT)


def _layer_norm_rows(z, g, b):
    mu = jnp.mean(z, axis=-1, keepdims=True)
    zc = z - mu
    var = jnp.mean(zc * zc, axis=-1, keepdims=True)
    return zc * lax.rsqrt(var + LN_EPS) * g + b


def _dot_nt(a, b):
    return lax.dot_general(a, b, (((1,), (1,)), ((), ())), preferred_element_type=jnp.float32)


def _dot(a, b):
    return jnp.dot(a, b, preferred_element_type=jnp.float32)


def _mm_kernel(x_ref, w_ref, o_ref):
    o_ref[...] = _dot(x_ref[...], w_ref[...]).astype(o_ref.dtype)


def _matmul(x, w, out_dtype):
    m, k = x.shape
    n = w.shape[1]
    tm = min(MM_TM, m)
    return pl.pallas_call(
        _mm_kernel,
        out_shape=jax.ShapeDtypeStruct((m, n), out_dtype),
        grid=(m // tm, n // MM_TN),
        in_specs=[pl.BlockSpec((tm, k), lambda i, j: (i, 0)),
                  pl.BlockSpec((k, MM_TN), lambda i, j: (0, j))],
        out_specs=pl.BlockSpec((tm, MM_TN), lambda i, j: (i, j)),
        compiler_params=_params("parallel", "parallel"),
        name="dense_matmul",
    )(x, w)


def _dil_attn_kernel(q_ref, kp_ref, kc_ref, vp_ref, vc_ref, o_ref, lse_ref, *, slopes, dilation):
    n = pl.program_id(2)
    qi = lax.broadcasted_iota(jnp.int32, (ATTN_STEPS, ATTN_STEPS), 0)
    kj = lax.broadcasted_iota(jnp.int32, (ATTN_STEPS, ATTN_STEPS), 1)
    diff = qi - kj
    valid_c = diff >= 0
    valid_p = jnp.logical_and(diff <= 0, n > 0)
    dist_c = (diff * dilation).astype(jnp.float32)
    dist_p = ((diff + ATTN_STEPS) * dilation).astype(jnp.float32)
    scale = HEAD_DIM ** -0.5
    for h in range(HEADS_PER_GROUP):
        sl = slice(h * HEAD_DIM, (h + 1) * HEAD_DIM)
        q = q_ref[:, sl]
        s_c = _dot_nt(q, kc_ref[:, sl]) * scale
        s_p = _dot_nt(q, kp_ref[:, sl]) * scale
        s_c = jnp.where(valid_c, s_c - slopes[h] * dist_c, NEG_INF)
        s_p = jnp.where(valid_p, s_p - slopes[h] * dist_p, NEG_INF)
        m = jnp.maximum(jnp.max(s_c, axis=-1, keepdims=True), jnp.max(s_p, axis=-1, keepdims=True))
        e_c = jnp.exp(s_c - m)
        e_p = jnp.exp(s_p - m)
        l = jnp.sum(e_c, axis=-1, keepdims=True) + jnp.sum(e_p, axis=-1, keepdims=True)
        inv_l = 1.0 / l
        p_c = (e_c * inv_l).astype(jnp.bfloat16)
        p_p = (e_p * inv_l).astype(jnp.bfloat16)
        o_ref[:, sl] = _dot(p_c, vc_ref[:, sl]) + _dot(p_p, vp_ref[:, sl])
        lse_ref[:, sl] = jnp.broadcast_to(m + jnp.log(l), (ATTN_STEPS, HEAD_DIM))


def _dilated_group(proj, group):
    _, dilation = DILATED_GROUPS[group]
    c = proj.shape[1]
    cb = c // GROUP_WIDTH
    length = SEQ // dilation
    nb = length // ATTN_STEPS
    view = proj.reshape(BATCH, length, dilation * c)
    blk = (None, ATTN_STEPS, GROUP_WIDTH)
    kb = SELF_WIDTH // GROUP_WIDTH

    def col(section):
        return lambda b, r, n: (b, n, r * cb + section * kb + group)

    def col_prev(section):
        return lambda b, r, n: (b, jnp.maximum(n - 1, 0), r * cb + section * kb + group)

    slopes = tuple(_alibi_slopes(N_SELF_HEADS)[group * HEADS_PER_GROUP:(group + 1) * HEADS_PER_GROUP])
    out_shape = jax.ShapeDtypeStruct((BATCH, length, dilation * GROUP_WIDTH), jnp.float32)
    out_spec = pl.BlockSpec(blk, lambda b, r, n: (b, n, r))
    o, lse = pl.pallas_call(
        functools.partial(_dil_attn_kernel, slopes=slopes, dilation=dilation),
        out_shape=(out_shape, out_shape),
        grid=(BATCH, dilation, nb),
        in_specs=[pl.BlockSpec(blk, col(0)),
                  pl.BlockSpec(blk, col_prev(1)), pl.BlockSpec(blk, col(1)),
                  pl.BlockSpec(blk, col_prev(2)), pl.BlockSpec(blk, col(2))],
        out_specs=(out_spec, out_spec),
        compiler_params=_params("parallel", "parallel", "arbitrary"),
        name=f"dilated_attn_g{group}",
    )(view, view, view, view, view)
    return o.reshape(N_TOK, GROUP_WIDTH), lse.reshape(N_TOK, GROUP_WIDTH)


def _dil_combine_kernel(o0_ref, o1_ref, o2_ref, l0_ref, l1_ref, l2_ref, out_ref):
    l0, l1, l2 = l0_ref[...], l1_ref[...], l2_ref[...]
    m = jnp.maximum(jnp.maximum(l0, l1), l2)
    e0, e1, e2 = jnp.exp(l0 - m), jnp.exp(l1 - m), jnp.exp(l2 - m)
    inv = 1.0 / (e0 + e1 + e2)
    out_ref[:, 0 * GROUP_WIDTH:1 * GROUP_WIDTH] = (o0_ref[...] * (e0 * inv)).astype(out_ref.dtype)
    out_ref[:, 1 * GROUP_WIDTH:2 * GROUP_WIDTH] = (o1_ref[...] * (e1 * inv)).astype(out_ref.dtype)
    out_ref[:, 2 * GROUP_WIDTH:3 * GROUP_WIDTH] = (o2_ref[...] * (e2 * inv)).astype(out_ref.dtype)


def _dilated_attention(proj):
    outs, lses = zip(*[_dilated_group(proj, g) for g in range(len(DILATED_GROUPS))])
    spec = pl.BlockSpec((ROW_TILE, GROUP_WIDTH), lambda i: (i, 0))
    return pl.pallas_call(
        _dil_combine_kernel,
        out_shape=jax.ShapeDtypeStruct((N_TOK, SELF_WIDTH), jnp.bfloat16),
        grid=(N_TOK // ROW_TILE,),
        in_specs=[spec] * 6,
        out_specs=pl.BlockSpec((ROW_TILE, SELF_WIDTH), lambda i: (i, 0)),
        compiler_params=_params("parallel"),
        name="dilated_combine",
    )(*outs, *lses)


def _retention_kernel(q_ref, k_ref, v_ref, g_ref, dmat_ref, kdec_ref, qdec_ref, cdec_ref, gn_ref,
                      o_ref, state_ref):
    @pl.when(pl.program_id(1) == 0)
    def _():
        state_ref[...] = jnp.zeros_like(state_ref)

    for h in range(N_RET_HEADS):
        sl = slice(h * HEAD_DIM, (h + 1) * HEAD_DIM)
        q, k, v = q_ref[:, sl], k_ref[:, sl], v_ref[:, sl]
        scores = _dot_nt(q, k) * dmat_ref[h]
        intra = _dot(scores.astype(jnp.bfloat16), v)
        state = state_ref[h]
        cross = _dot(q, state.astype(jnp.bfloat16)) * qdec_ref[:, sl]
        kw = (k.astype(jnp.float32) * kdec_ref[:, sl]).T.astype(jnp.bfloat16)
        state_ref[h] = state * cdec_ref[h] + _dot(kw, v)
        r = intra + cross
        mu = jnp.mean(r, axis=-1, keepdims=True)
        rc = r - mu
        var = jnp.mean(rc * rc, axis=-1, keepdims=True)
        rn = rc * lax.rsqrt(var + LN_EPS) * gn_ref[:, sl]
        gate = g_ref[:, sl].astype(jnp.float32)
        o_ref[:, sl] = (gate * (1.0 / (1.0 + jnp.exp(-gate))) * rn).astype(o_ref.dtype)


def _retention(proj, gn_gain):
    c = RET_CHUNK
    log_gamma = jnp.log1p(-jnp.exp2(-(5.0 + jnp.arange(N_RET_HEADS, dtype=jnp.float32))))
    idx = jnp.arange(c, dtype=jnp.float32)
    diff = idx[:, None] - idx[None, :]
    decay = jnp.where(diff >= 0, jnp.exp(jnp.maximum(diff, 0.0)[None] * log_gamma[:, None, None]), 0.0)
    scale = HEAD_DIM ** -0.5
    dmat = decay * scale
    k_decay = jnp.exp((c - 1 - idx)[:, None] * log_gamma[None, :]) * scale
    q_decay = jnp.exp((idx + 1.0)[:, None] * log_gamma[None, :])
    kdec = jnp.repeat(k_decay, HEAD_DIM, axis=1)
    qdec = jnp.repeat(q_decay, HEAD_DIM, axis=1)
    cdec = jnp.broadcast_to(jnp.exp(c * log_gamma)[:, None, None], (N_RET_HEADS, 1, HEAD_DIM))
    gn = gn_gain.reshape(1, SELF_WIDTH).astype(jnp.float32)
    nchunk = SEQ // c
    blk = (c, SELF_WIDTH)

    def section(s):
        return pl.BlockSpec(blk, lambda b, n: (b * nchunk + n, s))

    const2 = lambda b, n: (0, 0)
    const3 = lambda b, n: (0, 0, 0)
    return pl.pallas_call(
        _retention_kernel,
        out_shape=jax.ShapeDtypeStruct((N_TOK, SELF_WIDTH), jnp.bfloat16),
        grid=(BATCH, nchunk),
        in_specs=[section(0), section(1), section(2), section(3),
                  pl.BlockSpec((N_RET_HEADS, c, c), const3),
                  pl.BlockSpec(blk, const2), pl.BlockSpec(blk, const2),
                  pl.BlockSpec((N_RET_HEADS, 1, HEAD_DIM), const3),
                  pl.BlockSpec((1, SELF_WIDTH), const2)],
        out_specs=pl.BlockSpec(blk, lambda b, n: (b * nchunk + n, 0)),
        scratch_shapes=[pltpu.VMEM((N_RET_HEADS, HEAD_DIM, HEAD_DIM), jnp.float32)],
        compiler_params=_params("parallel", "arbitrary"),
        name="retention",
    )(proj, proj, proj, proj, dmat, kdec, qdec, cdec, gn)


def _mem_attn_kernel(q_ref, k_ref, v_ref, o_ref):
    scale = HEAD_DIM ** -0.5
    for h in range(MEM_HEADS):
        sl = slice(h * HEAD_DIM, (h + 1) * HEAD_DIM)
        s = _dot_nt(q_ref[:, sl], k_ref[:, sl]) * scale
        e = jnp.exp(s - jnp.max(s, axis=-1, keepdims=True))
        p = e * (1.0 / jnp.sum(e, axis=-1, keepdims=True))
        o_ref[:, sl] = _dot(p.astype(jnp.bfloat16), v_ref[:, sl]).astype(o_ref.dtype)


def _memory_attention(proj, memkv):
    qcol = proj.shape[1] // MEM_WIDTH - 1
    per_b = SEQ // ROW_TILE
    return pl.pallas_call(
        _mem_attn_kernel,
        out_shape=jax.ShapeDtypeStruct((N_TOK, MEM_WIDTH), jnp.bfloat16),
        grid=(BATCH, per_b),
        in_specs=[pl.BlockSpec((ROW_TILE, MEM_WIDTH), lambda b, i: (b * per_b + i, qcol)),
                  pl.BlockSpec((MEM_LEN, MEM_WIDTH), lambda b, i: (b, 0)),
                  pl.BlockSpec((MEM_LEN, MEM_WIDTH), lambda b, i: (b, 1))],
        out_specs=pl.BlockSpec((ROW_TILE, MEM_WIDTH), lambda b, i: (b * per_b + i, 0)),
        compiler_params=_params("parallel", "parallel"),
        name="memory_attn",
    )(proj, memkv, memkv)


def _mix_ln_kernel(so_ref, mo_ref, wt_ref, wb_ref, h_ref, g_ref, b_ref, hf_ref, hb_ref):
    mix = _dot(so_ref[...], wt_ref[...]) + _dot(mo_ref[...], wb_ref[...])
    y = _layer_norm_rows(DEEPNORM_ALPHA * h_ref[...] + mix, g_ref[...], b_ref[...])
    hf_ref[...] = y
    hb_ref[...] = y.astype(hb_ref.dtype)


def _mix_ln(self_out, mem_out, w_mix, h, g, b):
    row = lambda i: (i, 0)
    const = lambda i: (0, 0)
    return pl.pallas_call(
        _mix_ln_kernel,
        out_shape=(jax.ShapeDtypeStruct((N_TOK, D_MODEL), jnp.float32),
                   jax.ShapeDtypeStruct((N_TOK, D_MODEL), jnp.bfloat16)),
        grid=(N_TOK // ROW_TILE,),
        in_specs=[pl.BlockSpec((ROW_TILE, SELF_WIDTH), row),
                  pl.BlockSpec((ROW_TILE, MEM_WIDTH), row),
                  pl.BlockSpec((SELF_WIDTH, D_MODEL), const),
                  pl.BlockSpec((MEM_WIDTH, D_MODEL), lambda i: (SELF_WIDTH // MEM_WIDTH, 0)),
                  pl.BlockSpec((ROW_TILE, D_MODEL), row),
                  pl.BlockSpec((1, D_MODEL), const), pl.BlockSpec((1, D_MODEL), const)],
        out_specs=(pl.BlockSpec((ROW_TILE, D_MODEL), row), pl.BlockSpec((ROW_TILE, D_MODEL), row)),
        compiler_params=_params("parallel"),
        name="mix_ln",
    )(self_out, mem_out, w_mix, w_mix, h, g.reshape(1, D_MODEL), b.reshape(1, D_MODEL))


def _router_kernel(h_ref, w_ref, b_ref, e_ref, g_ref, r_ref, cnt_ref, run_ref):
    tm = ROUTER_TM

    @pl.when(pl.program_id(0) == 0)
    def _():
        run_ref[...] = jnp.zeros_like(run_ref)

    logits = jnp.dot(h_ref[...], w_ref[...], preferred_element_type=jnp.float32,
                     precision=lax.Precision.HIGHEST) + b_ref[...]
    lane = lax.broadcasted_iota(jnp.int32, (tm, LANES), 1).astype(jnp.float32)
    work = logits
    vals, idxs, hots = [], [], []
    for _ in range(TOP_K):
        m = jnp.max(work, axis=-1, keepdims=True)
        idx = jnp.min(jnp.where(work == m, lane, float(LANES)), axis=-1, keepdims=True)
        hot = lane == idx
        vals.append(m)
        idxs.append(idx)
        hots.append(hot)
        work = jnp.where(hot, -jnp.inf, work)
    exps = [jnp.exp(v - vals[0]) for v in vals]
    inv = 1.0 / (exps[0] + exps[1] + exps[2] + exps[3])
    cnt = jnp.zeros((tm, LANES), jnp.float32)
    for hot in hots:
        cnt = cnt + hot.astype(jnp.float32)
    row = lax.broadcasted_iota(jnp.int32, (tm, tm), 0)
    colm = lax.broadcasted_iota(jnp.int32, (tm, tm), 1)
    tri = (row > colm).astype(jnp.bfloat16)
    before = run_ref[...] + _dot(tri, cnt.astype(jnp.bfloat16))
    lane_i = lax.broadcasted_iota(jnp.int32, (tm, LANES), 1)
    e_out = jnp.zeros((tm, LANES), jnp.float32)
    g_out = jnp.zeros((tm, LANES), jnp.float32)
    r_out = jnp.zeros((tm, LANES), jnp.float32)
    for k in range(TOP_K):
        rank = jnp.sum(jnp.where(hots[k], before, 0.0), axis=-1, keepdims=True)
        e_out = jnp.where(lane_i == k, idxs[k], e_out)
        g_out = jnp.where(lane_i == k, exps[k] * inv, g_out)
        r_out = jnp.where(lane_i == k, rank, r_out)
    e_ref[...] = e_out.astype(jnp.int32)
    g_ref[...] = g_out
    r_ref[...] = r_out.astype(jnp.int32)
    run_ref[...] = run_ref[...] + jnp.sum(cnt, axis=0, keepdims=True)
    cnt_ref[...] = run_ref[...].astype(jnp.int32)


def _router(h, router_w, router_b):
    w = jnp.zeros((D_MODEL, LANES), jnp.float32).at[:, :N_EXPERTS].set(router_w)
    b = jnp.full((1, LANES), NEG_INF, jnp.float32).at[0, :N_EXPERTS].set(router_b)
    tm = ROUTER_TM
    row = lambda i: (i, 0)
    const = lambda i: (0, 0)
    lanes_out = pl.BlockSpec((tm, LANES), row)
    e, g, r, cnt = pl.pallas_call(
        _router_kernel,
        out_shape=(jax.ShapeDtypeStruct((N_TOK, LANES), jnp.int32),
                   jax.ShapeDtypeStruct((N_TOK, LANES), jnp.float32),
                   jax.ShapeDtypeStruct((N_TOK, LANES), jnp.int32),
                   jax.ShapeDtypeStruct((1, LANES), jnp.int32)),
        grid=(N_TOK // tm,),
        in_specs=[pl.BlockSpec((tm, D_MODEL), row), pl.BlockSpec((D_MODEL, LANES), const),
                  pl.BlockSpec((1, LANES), const)],
        out_specs=(lanes_out, lanes_out, lanes_out, pl.BlockSpec((1, LANES), const)),
        scratch_shapes=[pltpu.VMEM((1, LANES), jnp.float32)],
        compiler_params=_params("arbitrary"),
        name="router",
    )(h, w, b)
    return e[:, :TOP_K], g, r[:, :TOP_K], cnt[0, :N_EXPERTS]


def _route_tables(top_e, rank, cnt):
    ntile = (cnt + MOE_TILE - 1) // MOE_TILE
    tile_end = jnp.cumsum(ntile)
    tile_base = tile_end - ntile
    pos = (tile_base[top_e] * MOE_TILE + rank).astype(jnp.int32).reshape(-1)
    total = tile_end[-1]
    w = jnp.arange(MOE_TILES, dtype=jnp.int32)
    used = w < total
    wc = jnp.minimum(w, total - 1)
    tile_e = jnp.minimum(jnp.searchsorted(tile_end, wc, side="right"), N_EXPERTS - 1).astype(jnp.int32)
    rows = jnp.clip(cnt[tile_e] - (wc - tile_base[tile_e]) * MOE_TILE, 0, MOE_TILE)
    nsub = jnp.where(used, (rows + MOE_SUB - 1) // MOE_SUB, 0).astype(jnp.int32)
    tile_idx = jnp.where(used, w, MOE_TILES).astype(jnp.int32)
    pad_start = (tile_base * MOE_TILE + cnt).astype(jnp.int32)
    pad_n = ((-cnt) % MOE_SUB).astype(jnp.int32)
    return pos, tile_idx, tile_e, nsub, pad_start, pad_n


def _row_copy(src_hbm, src_row, dst_hbm, dst_row, sem):
    return pltpu.make_async_copy(src_hbm.at[pl.ds(src_row, 1)], dst_hbm.at[pl.ds(dst_row, 1)], sem)


def _dispatch_kernel(pos_ref, pad_start_ref, pad_n_ref, h_hbm, xs_hbm, sem):
    step = pl.program_id(0)
    base = step * DISPATCH_TB

    def issue(r, carry):
        t = base + r
        for k in range(TOP_K):
            _row_copy(h_hbm, t, xs_hbm, pos_ref[t * TOP_K + k], sem).start()
        return carry

    lax.fori_loop(0, DISPATCH_TB, issue, 0)

    def drain(_, carry):
        _row_copy(h_hbm, 0, xs_hbm, 0, sem).wait()
        return carry

    lax.fori_loop(0, DISPATCH_TB * TOP_K, drain, 0)

    @pl.when(step == 0)
    def _():
        def per_expert(e, carry):
            start = pad_start_ref[e]
            n = pad_n_ref[e]

            def fill(i, c):
                _row_copy(h_hbm, 0, xs_hbm, start + i, sem).start()
                return c

            lax.fori_loop(0, n, fill, 0)
            lax.fori_loop(0, n, drain, 0)
            return carry

        lax.fori_loop(0, N_EXPERTS, per_expert, 0)


def _dispatch(h, pos, pad_start, pad_n):
    return pl.pallas_call(
        _dispatch_kernel,
        out_shape=jax.ShapeDtypeStruct((MOE_ROWS, D_MODEL), jnp.float32),
        grid_spec=pltpu.PrefetchScalarGridSpec(
            num_scalar_prefetch=3,
            grid=(N_TOK // DISPATCH_TB,),
            in_specs=[pl.BlockSpec(memory_space=pl.ANY)],
            out_specs=pl.BlockSpec(memory_space=pl.ANY),
            scratch_shapes=[pltpu.SemaphoreType.DMA(())],
        ),
        compiler_params=_params("arbitrary"),
        name="moe_dispatch",
    )(pos, pad_start, pad_n, h)


def _moe_up_kernel(tile_ref, exp_ref, nsub_ref, x_ref, wg_ref, wl_ref, bg_ref, bl_ref, o_ref,
                   xb_ref, wgb_ref, wlb_ref):
    w = pl.program_id(0)
    j = pl.program_id(1)
    nsub = nsub_ref[w]

    @pl.when(nsub == 0)
    def _():
        o_ref[...] = jnp.zeros_like(o_ref)

    @pl.when(nsub > 0)
    def _():
        wgb_ref[...] = wg_ref[...].astype(jnp.bfloat16)
        wlb_ref[...] = wl_ref[...].astype(jnp.bfloat16)

    for i in range(MOE_SUBS):
        rows = slice(i * MOE_SUB, (i + 1) * MOE_SUB)

        @pl.when(jnp.logical_and(i < nsub, j == 0))
        def _():
            xb_ref[rows, :] = x_ref[rows, :].astype(jnp.bfloat16)

        @pl.when(i < nsub)
        def _():
            x = xb_ref[rows, :]
            gate = jnp.minimum(_dot(x, wgb_ref[...]) + bg_ref[...], SWIGLU_LIMIT)
            lin = jnp.clip(_dot(x, wlb_ref[...]) + bl_ref[...], -SWIGLU_LIMIT, SWIGLU_LIMIT)
            act = gate * (1.0 / (1.0 + jnp.exp(-SWIGLU_ALPHA * gate))) * (lin + 1.0)
            o_ref[rows, :] = act.astype(o_ref.dtype)

        @pl.when(jnp.logical_and(i >= nsub, nsub > 0))
        def _():
            o_ref[rows, :] = jnp.zeros((MOE_SUB, MOE_TF), o_ref.dtype)


def _moe_up(xs, w_in, b_in, layer, tile_idx, tile_e, nsub):
    nj = D_EXPERT // MOE_TF
    b_in4 = b_in.reshape(DEPTH, N_EXPERTS, 1, 2 * D_EXPERT)
    wblk = (None, None, D_MODEL, MOE_TF)
    bblk = (None, None, 1, MOE_TF)
    return pl.pallas_call(
        _moe_up_kernel,
        out_shape=jax.ShapeDtypeStruct((MOE_ROWS, D_EXPERT), jnp.bfloat16),
        grid_spec=pltpu.PrefetchScalarGridSpec(
            num_scalar_prefetch=3,
            grid=(MOE_TILES, nj),
            in_specs=[pl.BlockSpec((MOE_TILE, D_MODEL), lambda w, j, t, e, n: (t[w], 0)),
                      pl.BlockSpec(wblk, lambda w, j, t, e, n: (layer, e[w], 0, j)),
                      pl.BlockSpec(wblk, lambda w, j, t, e, n: (layer, e[w], 0, nj + j)),
                      pl.BlockSpec(bblk, lambda w, j, t, e, n: (layer, e[w], 0, j)),
                      pl.BlockSpec(bblk, lambda w, j, t, e, n: (layer, e[w], 0, nj + j))],
            out_specs=pl.BlockSpec((MOE_TILE, MOE_TF),
                                   lambda w, j, t, e, n: (t[w], jnp.where(n[w] > 0, j, 0))),
            scratch_shapes=[pltpu.VMEM((MOE_TILE, D_MODEL), jnp.bfloat16),
                            pltpu.VMEM((D_MODEL, MOE_TF), jnp.bfloat16),
                            pltpu.VMEM((D_MODEL, MOE_TF), jnp.bfloat16)],
        ),
        compiler_params=_params("arbitrary", "arbitrary"),
        name="moe_up",
    )(tile_idx, tile_e, nsub, xs, w_in, w_in, b_in4, b_in4)


def _moe_down_kernel(tile_ref, exp_ref, nsub_ref, a_ref, w_ref, b_ref, o_ref, wb_ref):
    w = pl.program_id(0)
    nsub = nsub_ref[w]

    @pl.when(nsub == 0)
    def _():
        o_ref[...] = jnp.zeros_like(o_ref)

    @pl.when(nsub > 0)
    def _():
        wb_ref[...] = w_ref[...].astype(jnp.bfloat16)

    for i in range(MOE_SUBS):
        rows = slice(i * MOE_SUB, (i + 1) * MOE_SUB)

        @pl.when(i < nsub)
        def _():
            o_ref[rows, :] = _dot(a_ref[rows, :], wb_ref[...]) + b_ref[...]

        @pl.when(jnp.logical_and(i >= nsub, nsub > 0))
        def _():
            o_ref[rows, :] = jnp.zeros((MOE_SUB, MOE_TF), o_ref.dtype)


def _moe_down(act, w_out, b_out, layer, tile_idx, tile_e, nsub):
    nj = D_MODEL // MOE_TF
    b_out4 = b_out.reshape(DEPTH, N_EXPERTS, 1, D_MODEL)
    return pl.pallas_call(
        _moe_down_kernel,
        out_shape=jax.ShapeDtypeStruct((MOE_ROWS, D_MODEL), jnp.float32),
        grid_spec=pltpu.PrefetchScalarGridSpec(
            num_scalar_prefetch=3,
            grid=(MOE_TILES, nj),
            in_specs=[pl.BlockSpec((MOE_TILE, D_EXPERT), lambda w, j, t, e, n: (t[w], 0)),
                      pl.BlockSpec((None, None, D_EXPERT, MOE_TF), lambda w, j, t, e, n: (layer, e[w], 0, j)),
                      pl.BlockSpec((None, None, 1, MOE_TF), lambda w, j, t, e, n: (layer, e[w], 0, j))],
            out_specs=pl.BlockSpec((MOE_TILE, MOE_TF),
                                   lambda w, j, t, e, n: (t[w], jnp.where(n[w] > 0, j, 0))),
            scratch_shapes=[pltpu.VMEM((D_EXPERT, MOE_TF), jnp.bfloat16)],
        ),
        compiler_params=_params("arbitrary", "arbitrary"),
        name="moe_down",
    )(tile_idx, tile_e, nsub, act, w_out, b_out4)


def _combine_ln_kernel(pos_ref, y_hbm, gates_ref, h_ref, g_ref, b_ref, hf_ref, hb_ref, ybuf_ref, sem):
    base = pl.program_id(0) * COMBINE_TB

    def issue(r, carry):
        for k in range(TOP_K):
            src = y_hbm.at[pl.ds(pos_ref[(base + r) * TOP_K + k], 1)]
            pltpu.make_async_copy(src, ybuf_ref.at[k, pl.ds(r, 1)], sem).start()
        return carry

    lax.fori_loop(0, COMBINE_TB, issue, 0)

    def drain(_, carry):
        pltpu.make_async_copy(y_hbm.at[pl.ds(0, 1)], ybuf_ref.at[0, pl.ds(0, 1)], sem).wait()
        return carry

    lax.fori_loop(0, COMBINE_TB * TOP_K, drain, 0)

    gates = gates_ref[...]
    ffn = gates[:, 0:1] * ybuf_ref[0]
    for k in range(1, TOP_K):
        ffn = ffn + gates[:, k:k + 1] * ybuf_ref[k]
    y = _layer_norm_rows(DEEPNORM_ALPHA * h_ref[...] + ffn, g_ref[...], b_ref[...])
    hf_ref[...] = y
    hb_ref[...] = y.astype(hb_ref.dtype)


def _combine_ln(y, pos, gates, h, g, b):
    tb = COMBINE_TB
    row = lambda i, p: (i, 0)
    const = lambda i, p: (0, 0)
    return pl.pallas_call(
        _combine_ln_kernel,
        out_shape=(jax.ShapeDtypeStruct((N_TOK, D_MODEL), jnp.float32),
                   jax.ShapeDtypeStruct((N_TOK, D_MODEL), jnp.bfloat16)),
        grid_spec=pltpu.PrefetchScalarGridSpec(
            num_scalar_prefetch=1,
            grid=(N_TOK // tb,),
            in_specs=[pl.BlockSpec(memory_space=pl.ANY),
                      pl.BlockSpec((tb, LANES), row),
                      pl.BlockSpec((tb, D_MODEL), row),
                      pl.BlockSpec((1, D_MODEL), const), pl.BlockSpec((1, D_MODEL), const)],
            out_specs=(pl.BlockSpec((tb, D_MODEL), row), pl.BlockSpec((tb, D_MODEL), row)),
            scratch_shapes=[pltpu.VMEM((TOP_K, tb, D_MODEL), jnp.float32),
                            pltpu.SemaphoreType.DMA(())],
        ),
        compiler_params=_params("arbitrary"),
        name="moe_combine_ln",
    )(pos, y, gates, h, g.reshape(1, D_MODEL), b.reshape(1, D_MODEL))


def _moe(h, layer, router_w, router_b, w_in, b_in, w_out, b_out, ln_g, ln_b):
    top_e, gates, rank, cnt = _router(h, router_w[layer], router_b[layer])
    pos, tile_idx, tile_e, nsub, pad_start, pad_n = _route_tables(top_e, rank, cnt)
    xs = _dispatch(h, pos, pad_start, pad_n)
    act = _moe_up(xs, w_in, b_in, layer, tile_idx, tile_e, nsub)
    y = _moe_down(act, w_out, b_out, layer, tile_idx, tile_e, nsub)
    return _combine_ln(y, pos, gates, h, ln_g[layer], ln_b[layer])


def kernel(x, mem, w_in_dil, w_in_ret, ret_gn_g, w_mem_kv, w_mix_out, ln_mix_g, ln_mix_b, router_w, router_b, moe_w_in, moe_b_in, moe_w_out, moe_b_out, ln_ffn_g, ln_ffn_b):
    bf16 = jnp.bfloat16
    h = x.reshape(N_TOK, D_MODEL)
    hb = h.astype(bf16)
    memb = mem.reshape(BATCH * MEM_LEN, D_MODEL).astype(bf16)
    for layer in range(DEPTH):
        slot = layer // 2
        if layer % 2 == 0:
            proj = _matmul(hb, w_in_dil[slot].astype(bf16), bf16)
            self_out = _dilated_attention(proj)
        else:
            proj = _matmul(hb, w_in_ret[slot].astype(bf16), bf16)
            self_out = _retention(proj, ret_gn_g[slot])
        memkv = _matmul(memb, w_mem_kv[layer].astype(bf16), bf16)
        mem_out = _memory_attention(proj, memkv)
        h, hb = _mix_ln(self_out, mem_out, w_mix_out[layer].astype(bf16), h,
                        ln_mix_g[layer], ln_mix_b[layer])
        h, hb = _moe(h, layer, router_w, router_b, moe_w_in, moe_b_in, moe_w_out, moe_b_out,
                     ln_ffn_g, ln_ffn_b)
    return h.reshape(BATCH, SEQ, D_MODEL)
```

```python
import functools
import math

import jax
import jax.numpy as jnp
from jax import lax
from jax.experimental import pallas as pl
from jax.experimental.pallas import tpu as pltpu

D_MODEL = 2048
BATCH = 2
SEQ = 4096
DEPTH = 2
HEAD_DIM = 128
DILATED_GROUPS = ((128, 1), (512, 4), (2048, 16))
HEADS_PER_GROUP = 4
N_SELF_HEADS = len(DILATED_GROUPS) * HEADS_PER_GROUP
N_RET_HEADS = 12
SELF_WIDTH = N_SELF_HEADS * HEAD_DIM
MEM_HEADS = 4
MEM_LEN = 256
MEM_WIDTH = MEM_HEADS * HEAD_DIM
RET_CHUNK = 128
N_EXPERTS = 32
TOP_K = 4
D_EXPERT = D_MODEL
SWIGLU_ALPHA = 1.702
SWIGLU_LIMIT = 7.0
DEEPNORM_ALPHA = (2 * DEPTH) ** 0.25
LN_EPS = 1e-5
NEG_INF = -1e30

N_TOK = BATCH * SEQ
GROUP_WIDTH = HEADS_PER_GROUP * HEAD_DIM
ATTN_STEPS = 128

LANES = 128
VMEM_LIMIT = 56 * 1024 * 1024

MM_TM = 1024
MM_TN = 512
ROW_TILE = 512
ROUTER_TM = 256

MOE_SUB = 256
MOE_TILE = 1280
MOE_SUBS = MOE_TILE // MOE_SUB
MOE_TF = 512
MOE_TILES = (N_TOK * TOP_K) // MOE_TILE + N_EXPERTS
MOE_ROWS = (MOE_TILES + 1) * MOE_TILE
DISPATCH_TB = 256
COMBINE_TB = 128


def _alibi_slopes(n):
    def pow2(m):
        start = 2.0 ** (-8.0 / m)
        return [start ** (i + 1) for i in range(m)]

    if math.log2(n).is_integer():
        s = pow2(n)
    else:
        c = 2 ** math.floor(math.log2(n))
        s = pow2(c) + pow2(2 * c)[0::2][: n - c]
    return sorted(s, reverse=True)


def _params(*sem):
    return pltpu.CompilerParams(dimension_semantics=sem, vmem_limit_bytes=VMEM_LIMIT)


def _layer_norm_rows(z, g, b):
    mu = jnp.mean(z, axis=-1, keepdims=True)
    zc = z - mu
    var = jnp.mean(zc * zc, axis=-1, keepdims=True)
    return zc * lax.rsqrt(var + LN_EPS) * g + b


def _dot_nt(a, b):
    return lax.dot_general(a, b, (((1,), (1,)), ((), ())), preferred_element_type=jnp.float32)


def _dot(a, b):
    return jnp.dot(a, b, preferred_element_type=jnp.float32)


def _mm_kernel(x_ref, w_ref, o_ref):
    o_ref[...] = _dot(x_ref[...], w_ref[...]).astype(o_ref.dtype)


def _matmul(x, w, out_dtype):
    m, k = x.shape
    n = w.shape[1]
    tm = min(MM_TM, m)
    return pl.pallas_call(
        _mm_kernel,
        out_shape=jax.ShapeDtypeStruct((m, n), out_dtype),
        grid=(m // tm, n // MM_TN),
        in_specs=[pl.BlockSpec((tm, k), lambda i, j: (i, 0)),
                  pl.BlockSpec((k, MM_TN), lambda i, j: (0, j))],
        out_specs=pl.BlockSpec((tm, MM_TN), lambda i, j: (i, j)),
        compiler_params=_params("parallel", "parallel"),
        name="dense_matmul",
    )(x, w)


def _dil_attn_kernel(q_ref, kp_ref, kc_ref, vp_ref, vc_ref, o_ref, lse_ref, *, slopes, dilation):
    n = pl.program_id(2)
    qi = lax.broadcasted_iota(jnp.int32, (ATTN_STEPS, ATTN_STEPS), 0)
    kj = lax.broadcasted_iota(jnp.int32, (ATTN_STEPS, ATTN_STEPS), 1)
    diff = qi - kj
    valid_c = diff >= 0
    valid_p = jnp.logical_and(diff <= 0, n > 0)
    dist_c = (diff * dilation).astype(jnp.float32)
    dist_p = ((diff + ATTN_STEPS) * dilation).astype(jnp.float32)
    scale = HEAD_DIM ** -0.5
    for h in range(HEADS_PER_GROUP):
        sl = slice(h * HEAD_DIM, (h + 1) * HEAD_DIM)
        q = q_ref[:, sl]
        s_c = _dot_nt(q, kc_ref[:, sl]) * scale
        s_p = _dot_nt(q, kp_ref[:, sl]) * scale
        s_c = jnp.where(valid_c, s_c - slopes[h] * dist_c, NEG_INF)
        s_p = jnp.where(valid_p, s_p - slopes[h] * dist_p, NEG_INF)
        m = jnp.maximum(jnp.max(s_c, axis=-1, keepdims=True), jnp.max(s_p, axis=-1, keepdims=True))
        e_c = jnp.exp(s_c - m)
        e_p = jnp.exp(s_p - m)
        l = jnp.sum(e_c, axis=-1, keepdims=True) + jnp.sum(e_p, axis=-1, keepdims=True)
        inv_l = 1.0 / l
        p_c = (e_c * inv_l).astype(jnp.bfloat16)
        p_p = (e_p * inv_l).astype(jnp.bfloat16)
        o_ref[:, sl] = _dot(p_c, vc_ref[:, sl]) + _dot(p_p, vp_ref[:, sl])
        lse_ref[:, sl] = jnp.broadcast_to(m + jnp.log(l), (ATTN_STEPS, HEAD_DIM))


def _dilated_group(proj, group):
    _, dilation = DILATED_GROUPS[group]
    length = SEQ // dilation
    nb = length // ATTN_STEPS
    kb = SELF_WIDTH // GROUP_WIDTH
    if dilation == 1:
        src, first, stride = proj, group, kb
    else:
        src = jnp.concatenate([proj[:, s * SELF_WIDTH + group * GROUP_WIDTH:][:, :GROUP_WIDTH]
                               for s in range(3)], axis=1)
        first, stride = 0, 1
    c = src.shape[1]
    cb = c // GROUP_WIDTH
    view = src.reshape(BATCH, length, dilation * c)
    blk = (None, ATTN_STEPS, GROUP_WIDTH)

    def col(section):
        return lambda b, r, n: (b, n, r * cb + section * stride + first)

    def col_prev(section):
        return lambda b, r, n: (b, jnp.maximum(n - 1, 0), r * cb + section * stride + first)

    slopes = tuple(_alibi_slopes(N_SELF_HEADS)[group * HEADS_PER_GROUP:(group + 1) * HEADS_PER_GROUP])
    out_shape = jax.ShapeDtypeStruct((BATCH, length, dilation * GROUP_WIDTH), jnp.float32)
    out_spec = pl.BlockSpec(blk, lambda b, r, n: (b, n, r))
    o, lse = pl.pallas_call(
        functools.partial(_dil_attn_kernel, slopes=slopes, dilation=dilation),
        out_shape=(out_shape, out_shape),
        grid=(BATCH, dilation, nb),
        in_specs=[pl.BlockSpec(blk, col(0)),
                  pl.BlockSpec(blk, col_prev(1)), pl.BlockSpec(blk, col(1)),
                  pl.BlockSpec(blk, col_prev(2)), pl.BlockSpec(blk, col(2))],
        out_specs=(out_spec, out_spec),
        compiler_params=_params("parallel", "parallel", "arbitrary"),
        name=f"dilated_attn_g{group}",
    )(view, view, view, view, view)
    return o.reshape(N_TOK, GROUP_WIDTH), lse.reshape(N_TOK, GROUP_WIDTH)


def _dil_combine_kernel(o0_ref, o1_ref, o2_ref, l0_ref, l1_ref, l2_ref, out_ref):
    l0, l1, l2 = l0_ref[...], l1_ref[...], l2_ref[...]
    m = jnp.maximum(jnp.maximum(l0, l1), l2)
    e0, e1, e2 = jnp.exp(l0 - m), jnp.exp(l1 - m), jnp.exp(l2 - m)
    inv = 1.0 / (e0 + e1 + e2)
    out_ref[:, 0 * GROUP_WIDTH:1 * GROUP_WIDTH] = (o0_ref[...] * (e0 * inv)).astype(out_ref.dtype)
    out_ref[:, 1 * GROUP_WIDTH:2 * GROUP_WIDTH] = (o1_ref[...] * (e1 * inv)).astype(out_ref.dtype)
    out_ref[:, 2 * GROUP_WIDTH:3 * GROUP_WIDTH] = (o2_ref[...] * (e2 * inv)).astype(out_ref.dtype)


def _dilated_attention(proj):
    outs, lses = zip(*[_dilated_group(proj, g) for g in range(len(DILATED_GROUPS))])
    spec = pl.BlockSpec((ROW_TILE, GROUP_WIDTH), lambda i: (i, 0))
    return pl.pallas_call(
        _dil_combine_kernel,
        out_shape=jax.ShapeDtypeStruct((N_TOK, SELF_WIDTH), jnp.bfloat16),
        grid=(N_TOK // ROW_TILE,),
        in_specs=[spec] * 6,
        out_specs=pl.BlockSpec((ROW_TILE, SELF_WIDTH), lambda i: (i, 0)),
        compiler_params=_params("parallel"),
        name="dilated_combine",
    )(*outs, *lses)


def _retention_kernel(q_ref, k_ref, v_ref, g_ref, dmat_ref, kdec_ref, qdec_ref, cdec_ref, gn_ref,
                      o_ref, state_ref):
    @pl.when(pl.program_id(1) == 0)
    def _():
        state_ref[...] = jnp.zeros_like(state_ref)

    for h in range(N_RET_HEADS):
        sl = slice(h * HEAD_DIM, (h + 1) * HEAD_DIM)
        q, k, v = q_ref[:, sl], k_ref[:, sl], v_ref[:, sl]
        scores = _dot_nt(q, k) * dmat_ref[h]
        intra = _dot(scores.astype(jnp.bfloat16), v)
        state = state_ref[h]
        cross = _dot(q, state.astype(jnp.bfloat16)) * qdec_ref[:, sl]
        kw = (k.astype(jnp.float32) * kdec_ref[:, sl]).T.astype(jnp.bfloat16)
        state_ref[h] = state * cdec_ref[h] + _dot(kw, v)
        r = intra + cross
        mu = jnp.mean(r, axis=-1, keepdims=True)
        rc = r - mu
        var = jnp.mean(rc * rc, axis=-1, keepdims=True)
        rn = rc * lax.rsqrt(var + LN_EPS) * gn_ref[:, sl]
        gate = g_ref[:, sl].astype(jnp.float32)
        o_ref[:, sl] = (gate * (1.0 / (1.0 + jnp.exp(-gate))) * rn).astype(o_ref.dtype)


def _retention(proj, gn_gain):
    c = RET_CHUNK
    log_gamma = jnp.log1p(-jnp.exp2(-(5.0 + jnp.arange(N_RET_HEADS, dtype=jnp.float32))))
    idx = jnp.arange(c, dtype=jnp.float32)
    diff = idx[:, None] - idx[None, :]
    decay = jnp.where(diff >= 0, jnp.exp(jnp.maximum(diff, 0.0)[None] * log_gamma[:, None, None]), 0.0)
    scale = HEAD_DIM ** -0.5
    dmat = decay * scale
    k_decay = jnp.exp((c - 1 - idx)[:, None] * log_gamma[None, :]) * scale
    q_decay = jnp.exp((idx + 1.0)[:, None] * log_gamma[None, :])
    kdec = jnp.repeat(k_decay, HEAD_DIM, axis=1)
    qdec = jnp.repeat(q_decay, HEAD_DIM, axis=1)
    cdec = jnp.broadcast_to(jnp.exp(c * log_gamma)[:, None, None], (N_RET_HEADS, 1, HEAD_DIM))
    gn = gn_gain.reshape(1, SELF_WIDTH).astype(jnp.float32)
    nchunk = SEQ // c
    blk = (c, SELF_WIDTH)

    def section(s):
        return pl.BlockSpec(blk, lambda b, n: (b * nchunk + n, s))

    const2 = lambda b, n: (0, 0)
    const3 = lambda b, n: (0, 0, 0)
    return pl.pallas_call(
        _retention_kernel,
        out_shape=jax.ShapeDtypeStruct((N_TOK, SELF_WIDTH), jnp.bfloat16),
        grid=(BATCH, nchunk),
        in_specs=[section(0), section(1), section(2), section(3),
                  pl.BlockSpec((N_RET_HEADS, c, c), const3),
                  pl.BlockSpec(blk, const2), pl.BlockSpec(blk, const2),
                  pl.BlockSpec((N_RET_HEADS, 1, HEAD_DIM), const3),
                  pl.BlockSpec((1, SELF_WIDTH), const2)],
        out_specs=pl.BlockSpec(blk, lambda b, n: (b * nchunk + n, 0)),
        scratch_shapes=[pltpu.VMEM((N_RET_HEADS, HEAD_DIM, HEAD_DIM), jnp.float32)],
        compiler_params=_params("parallel", "arbitrary"),
        name="retention",
    )(proj, proj, proj, proj, dmat, kdec, qdec, cdec, gn)


def _mem_attn_kernel(q_ref, k_ref, v_ref, o_ref):
    scale = HEAD_DIM ** -0.5
    for h in range(MEM_HEADS):
        sl = slice(h * HEAD_DIM, (h + 1) * HEAD_DIM)
        s = _dot_nt(q_ref[:, sl], k_ref[:, sl]) * scale
        e = jnp.exp(s - jnp.max(s, axis=-1, keepdims=True))
        p = e * (1.0 / jnp.sum(e, axis=-1, keepdims=True))
        o_ref[:, sl] = _dot(p.astype(jnp.bfloat16), v_ref[:, sl]).astype(o_ref.dtype)


def _memory_attention(proj, memkv):
    qcol = proj.shape[1] // MEM_WIDTH - 1
    per_b = SEQ // ROW_TILE
    return pl.pallas_call(
        _mem_attn_kernel,
        out_shape=jax.ShapeDtypeStruct((N_TOK, MEM_WIDTH), jnp.bfloat16),
        grid=(BATCH, per_b),
        in_specs=[pl.BlockSpec((ROW_TILE, MEM_WIDTH), lambda b, i: (b * per_b + i, qcol)),
                  pl.BlockSpec((MEM_LEN, MEM_WIDTH), lambda b, i: (b, 0)),
                  pl.BlockSpec((MEM_LEN, MEM_WIDTH), lambda b, i: (b, 1))],
        out_specs=pl.BlockSpec((ROW_TILE, MEM_WIDTH), lambda b, i: (b * per_b + i, 0)),
        compiler_params=_params("parallel", "parallel"),
        name="memory_attn",
    )(proj, memkv, memkv)


def _mix_ln_kernel(so_ref, mo_ref, wt_ref, wb_ref, h_ref, g_ref, b_ref, hf_ref, hb_ref):
    mix = _dot(so_ref[...], wt_ref[...]) + _dot(mo_ref[...], wb_ref[...])
    y = _layer_norm_rows(DEEPNORM_ALPHA * h_ref[...] + mix, g_ref[...], b_ref[...])
    hf_ref[...] = y
    hb_ref[...] = y.astype(hb_ref.dtype)


def _mix_ln(self_out, mem_out, w_mix, h, g, b):
    row = lambda i: (i, 0)
    const = lambda i: (0, 0)
    return pl.pallas_call(
        _mix_ln_kernel,
        out_shape=(jax.ShapeDtypeStruct((N_TOK, D_MODEL), jnp.float32),
                   jax.ShapeDtypeStruct((N_TOK, D_MODEL), jnp.bfloat16)),
        grid=(N_TOK // ROW_TILE,),
        in_specs=[pl.BlockSpec((ROW_TILE, SELF_WIDTH), row),
                  pl.BlockSpec((ROW_TILE, MEM_WIDTH), row),
                  pl.BlockSpec((SELF_WIDTH, D_MODEL), const),
                  pl.BlockSpec((MEM_WIDTH, D_MODEL), lambda i: (SELF_WIDTH // MEM_WIDTH, 0)),
                  pl.BlockSpec((ROW_TILE, D_MODEL), row),
                  pl.BlockSpec((1, D_MODEL), const), pl.BlockSpec((1, D_MODEL), const)],
        out_specs=(pl.BlockSpec((ROW_TILE, D_MODEL), row), pl.BlockSpec((ROW_TILE, D_MODEL), row)),
        compiler_params=_params("parallel"),
        name="mix_ln",
    )(self_out, mem_out, w_mix, w_mix, h, g.reshape(1, D_MODEL), b.reshape(1, D_MODEL))


def _router_kernel(h_ref, w_ref, b_ref, e_ref, g_ref, r_ref, cnt_ref, run_ref):
    tm = ROUTER_TM

    @pl.when(pl.program_id(0) == 0)
    def _():
        run_ref[...] = jnp.zeros_like(run_ref)

    logits = jnp.dot(h_ref[...], w_ref[...], preferred_element_type=jnp.float32,
                     precision=lax.Precision.HIGHEST) + b_ref[...]
    lane = lax.broadcasted_iota(jnp.int32, (tm, LANES), 1).astype(jnp.float32)
    work = logits
    vals, idxs, hots = [], [], []
    for _ in range(TOP_K):
        m = jnp.max(work, axis=-1, keepdims=True)
        idx = jnp.min(jnp.where(work == m, lane, float(LANES)), axis=-1, keepdims=True)
        hot = lane == idx
        vals.append(m)
        idxs.append(idx)
        hots.append(hot)
        work = jnp.where(hot, -jnp.inf, work)
    exps = [jnp.exp(v - vals[0]) for v in vals]
    inv = 1.0 / (exps[0] + exps[1] + exps[2] + exps[3])
    cnt = jnp.zeros((tm, LANES), jnp.float32)
    for hot in hots:
        cnt = cnt + hot.astype(jnp.float32)
    row = lax.broadcasted_iota(jnp.int32, (tm, tm), 0)
    colm = lax.broadcasted_iota(jnp.int32, (tm, tm), 1)
    tri = (row > colm).astype(jnp.bfloat16)
    before = run_ref[...] + _dot(tri, cnt.astype(jnp.bfloat16))
    lane_i = lax.broadcasted_iota(jnp.int32, (tm, LANES), 1)
    e_out = jnp.zeros((tm, LANES), jnp.float32)
    g_out = jnp.zeros((tm, LANES), jnp.float32)
    r_out = jnp.zeros((tm, LANES), jnp.float32)
    for k in range(TOP_K):
        rank = jnp.sum(jnp.where(hots[k], before, 0.0), axis=-1, keepdims=True)
        e_out = jnp.where(lane_i == k, idxs[k], e_out)
        g_out = jnp.where(lane_i == k, exps[k] * inv, g_out)
        r_out = jnp.where(lane_i == k, rank, r_out)
    e_ref[...] = e_out.astype(jnp.int32)
    g_ref[...] = g_out
    r_ref[...] = r_out.astype(jnp.int32)
    run_ref[...] = run_ref[...] + jnp.sum(cnt, axis=0, keepdims=True)
    cnt_ref[...] = run_ref[...].astype(jnp.int32)


def _router(h, router_w, router_b):
    w = jnp.zeros((D_MODEL, LANES), jnp.float32).at[:, :N_EXPERTS].set(router_w)
    b = jnp.full((1, LANES), NEG_INF, jnp.float32).at[0, :N_EXPERTS].set(router_b)
    tm = ROUTER_TM
    row = lambda i: (i, 0)
    const = lambda i: (0, 0)
    lanes_out = pl.BlockSpec((tm, LANES), row)
    e, g, r, cnt = pl.pallas_call(
        _router_kernel,
        out_shape=(jax.ShapeDtypeStruct((N_TOK, LANES), jnp.int32),
                   jax.ShapeDtypeStruct((N_TOK, LANES), jnp.float32),
                   jax.ShapeDtypeStruct((N_TOK, LANES), jnp.int32),
                   jax.ShapeDtypeStruct((1, LANES), jnp.int32)),
        grid=(N_TOK // tm,),
        in_specs=[pl.BlockSpec((tm, D_MODEL), row), pl.BlockSpec((D_MODEL, LANES), const),
                  pl.BlockSpec((1, LANES), const)],
        out_specs=(lanes_out, lanes_out, lanes_out, pl.BlockSpec((1, LANES), const)),
        scratch_shapes=[pltpu.VMEM((1, LANES), jnp.float32)],
        compiler_params=_params("arbitrary"),
        name="router",
    )(h, w, b)
    return e[:, :TOP_K], g, r[:, :TOP_K], cnt[0, :N_EXPERTS]


def _route_tables(top_e, rank, cnt):
    ntile = (cnt + MOE_TILE - 1) // MOE_TILE
    tile_end = jnp.cumsum(ntile)
    tile_base = tile_end - ntile
    pos = (tile_base[top_e] * MOE_TILE + rank).astype(jnp.int32).reshape(-1)
    total = tile_end[-1]
    w = jnp.arange(MOE_TILES, dtype=jnp.int32)
    used = w < total
    wc = jnp.minimum(w, total - 1)
    tile_e = jnp.minimum(jnp.searchsorted(tile_end, wc, side="right"), N_EXPERTS - 1).astype(jnp.int32)
    rows = jnp.clip(cnt[tile_e] - (wc - tile_base[tile_e]) * MOE_TILE, 0, MOE_TILE)
    nsub = jnp.where(used, (rows + MOE_SUB - 1) // MOE_SUB, 0).astype(jnp.int32)
    tile_idx = jnp.where(used, w, MOE_TILES).astype(jnp.int32)
    pad_start = (tile_base * MOE_TILE + cnt).astype(jnp.int32)
    pad_n = ((-cnt) % MOE_SUB).astype(jnp.int32)
    return pos, tile_idx, tile_e, nsub, pad_start, pad_n


def _row_copy(src_vmem, src_row, dst_hbm, dst_row, sem):
    return pltpu.make_async_copy(src_vmem.at[pl.ds(src_row, 1)], dst_hbm.at[pl.ds(dst_row, 1)], sem)


def _dispatch_kernel(pos_ref, pad_start_ref, pad_n_ref, h_ref, xs_hbm, sem):
    step = pl.program_id(0)
    base = step * DISPATCH_TB

    def issue(r, carry):
        for k in range(TOP_K):
            _row_copy(h_ref, r, xs_hbm, pos_ref[(base + r) * TOP_K + k], sem).start()
        return carry

    lax.fori_loop(0, DISPATCH_TB, issue, 0)

    @pl.when(step == 0)
    def _():
        def per_expert(e, carry):
            start = pad_start_ref[e]
            n = pad_n_ref[e]

            def fill(i, c):
                _row_copy(h_ref, 0, xs_hbm, start + i, sem).start()
                return c

            def drain(_, c):
                _row_copy(h_ref, 0, xs_hbm, 0, sem).wait()
                return c

            lax.fori_loop(0, n, fill, 0)
            lax.fori_loop(0, n, drain, 0)
            return carry

        lax.fori_loop(0, N_EXPERTS, per_expert, 0)

    for _ in range(TOP_K):
        pltpu.make_async_copy(h_ref, xs_hbm.at[pl.ds(0, DISPATCH_TB)], sem).wait()


def _dispatch(h, pos, pad_start, pad_n):
    return pl.pallas_call(
        _dispatch_kernel,
        out_shape=jax.ShapeDtypeStruct((MOE_ROWS, D_MODEL), jnp.float32),
        grid_spec=pltpu.PrefetchScalarGridSpec(
            num_scalar_prefetch=3,
            grid=(N_TOK // DISPATCH_TB,),
            in_specs=[pl.BlockSpec((DISPATCH_TB, D_MODEL), lambda i, p, s, n: (i, 0))],
            out_specs=pl.BlockSpec(memory_space=pl.ANY),
            scratch_shapes=[pltpu.SemaphoreType.DMA(())],
        ),
        compiler_params=_params("arbitrary"),
        name="moe_dispatch",
    )(pos, pad_start, pad_n, h)


def _moe_up_kernel(tile_ref, exp_ref, nsub_ref, x_ref, wg_ref, wl_ref, bg_ref, bl_ref, o_ref, xb_ref):
    w = pl.program_id(0)
    j = pl.program_id(1)
    nsub = nsub_ref[w]

    @pl.when(nsub == 0)
    def _():
        o_ref[...] = jnp.zeros_like(o_ref)

    for i in range(MOE_SUBS):
        rows = slice(i * MOE_SUB, (i + 1) * MOE_SUB)

        @pl.when(jnp.logical_and(i < nsub, j == 0))
        def _():
            xb_ref[rows, :] = x_ref[rows, :].astype(jnp.bfloat16)

    for s in range(1, MOE_SUBS + 1):
        m = s * MOE_SUB

        @pl.when(nsub == s)
        def _():
            x = xb_ref[0:m, :]
            gate = _dot(x, wg_ref[...].astype(jnp.bfloat16)) + bg_ref[...]
            lin = _dot(x, wl_ref[...].astype(jnp.bfloat16)) + bl_ref[...]
            gate = jnp.minimum(gate, SWIGLU_LIMIT)
            lin = jnp.clip(lin, -SWIGLU_LIMIT, SWIGLU_LIMIT)
            act = gate * (1.0 / (1.0 + jnp.exp(-SWIGLU_ALPHA * gate))) * (lin + 1.0)
            o_ref[0:m, :] = act.astype(o_ref.dtype)
            if m < MOE_TILE:
                o_ref[m:MOE_TILE, :] = jnp.zeros((MOE_TILE - m, MOE_TF), o_ref.dtype)


def _moe_up(xs, w_in, b_in, layer, tile_idx, tile_e, nsub):
    nj = D_EXPERT // MOE_TF
    b_in4 = b_in.reshape(DEPTH, N_EXPERTS, 1, 2 * D_EXPERT)
    wblk = (None, None, D_MODEL, MOE_TF)
    bblk = (None, None, 1, MOE_TF)
    return pl.pallas_call(
        _moe_up_kernel,
        out_shape=jax.ShapeDtypeStruct((MOE_ROWS, D_EXPERT), jnp.bfloat16),
        grid_spec=pltpu.PrefetchScalarGridSpec(
            num_scalar_prefetch=3,
            grid=(MOE_TILES, nj),
            in_specs=[pl.BlockSpec((MOE_TILE, D_MODEL), lambda w, j, t, e, n: (t[w], 0)),
                      pl.BlockSpec(wblk, lambda w, j, t, e, n: (layer, e[w], 0, j)),
                      pl.BlockSpec(wblk, lambda w, j, t, e, n: (layer, e[w], 0, nj + j)),
                      pl.BlockSpec(bblk, lambda w, j, t, e, n: (layer, e[w], 0, j)),
                      pl.BlockSpec(bblk, lambda w, j, t, e, n: (layer, e[w], 0, nj + j))],
            out_specs=pl.BlockSpec((MOE_TILE, MOE_TF),
                                   lambda w, j, t, e, n: (t[w], jnp.where(n[w] > 0, j, 0))),
            scratch_shapes=[pltpu.VMEM((MOE_TILE, D_MODEL), jnp.bfloat16)],
        ),
        compiler_params=_params("arbitrary", "arbitrary"),
        name="moe_up",
    )(tile_idx, tile_e, nsub, xs, w_in, w_in, b_in4, b_in4)


def _moe_down_kernel(tile_ref, exp_ref, nsub_ref, a_ref, w_ref, b_ref, o_ref):
    w = pl.program_id(0)
    nsub = nsub_ref[w]

    @pl.when(nsub == 0)
    def _():
        o_ref[...] = jnp.zeros_like(o_ref)

    for s in range(1, MOE_SUBS + 1):
        m = s * MOE_SUB

        @pl.when(nsub == s)
        def _():
            o_ref[0:m, :] = _dot(a_ref[0:m, :], w_ref[...].astype(jnp.bfloat16)) + b_ref[...]
            if m < MOE_TILE:
                o_ref[m:MOE_TILE, :] = jnp.zeros((MOE_TILE - m, MOE_TF), o_ref.dtype)


def _moe_down(act, w_out, b_out, layer, tile_idx, tile_e, nsub):
    nj = D_MODEL // MOE_TF
    b_out4 = b_out.reshape(DEPTH, N_EXPERTS, 1, D_MODEL)
    return pl.pallas_call(
        _moe_down_kernel,
        out_shape=jax.ShapeDtypeStruct((MOE_ROWS, D_MODEL), jnp.float32),
        grid_spec=pltpu.PrefetchScalarGridSpec(
            num_scalar_prefetch=3,
            grid=(MOE_TILES, nj),
            in_specs=[pl.BlockSpec((MOE_TILE, D_EXPERT), lambda w, j, t, e, n: (t[w], 0)),
                      pl.BlockSpec((None, None, D_EXPERT, MOE_TF), lambda w, j, t, e, n: (layer, e[w], 0, j)),
                      pl.BlockSpec((None, None, 1, MOE_TF), lambda w, j, t, e, n: (layer, e[w], 0, j))],
            out_specs=pl.BlockSpec((MOE_TILE, MOE_TF),
                                   lambda w, j, t, e, n: (t[w], jnp.where(n[w] > 0, j, 0))),
        ),
        compiler_params=_params("arbitrary", "arbitrary"),
        name="moe_down",
    )(tile_idx, tile_e, nsub, act, w_out, b_out4)


def _combine_ln_kernel(pos_ref, y_hbm, gates_ref, h_ref, g_ref, b_ref, hf_ref, hb_ref, ybuf_ref, sem):
    base = pl.program_id(0) * COMBINE_TB

    def issue(r, carry):
        for k in range(TOP_K):
            src = y_hbm.at[pl.ds(pos_ref[(base + r) * TOP_K + k], 1)]
            pltpu.make_async_copy(src, ybuf_ref.at[k, pl.ds(r, 1)], sem).start()
        return carry

    lax.fori_loop(0, COMBINE_TB, issue, 0)

    for k in range(TOP_K):
        pltpu.make_async_copy(y_hbm.at[pl.ds(0, COMBINE_TB)], ybuf_ref.at[k], sem).wait()

    gates = gates_ref[...]
    ffn = gates[:, 0:1] * ybuf_ref[0]
    for k in range(1, TOP_K):
        ffn = ffn + gates[:, k:k + 1] * ybuf_ref[k]
    y = _layer_norm_rows(DEEPNORM_ALPHA * h_ref[...] + ffn, g_ref[...], b_ref[...])
    hf_ref[...] = y
    hb_ref[...] = y.astype(hb_ref.dtype)


def _combine_ln(y, pos, gates, h, g, b):
    tb = COMBINE_TB
    row = lambda i, p: (i, 0)
    const = lambda i, p: (0, 0)
    return pl.pallas_call(
        _combine_ln_kernel,
        out_shape=(jax.ShapeDtypeStruct((N_TOK, D_MODEL), jnp.float32),
                   jax.ShapeDtypeStruct((N_TOK, D_MODEL), jnp.bfloat16)),
        grid_spec=pltpu.PrefetchScalarGridSpec(
            num_scalar_prefetch=1,
            grid=(N_TOK // tb,),
            in_specs=[pl.BlockSpec(memory_space=pl.ANY),
                      pl.BlockSpec((tb, LANES), row),
                      pl.BlockSpec((tb, D_MODEL), row),
                      pl.BlockSpec((1, D_MODEL), const), pl.BlockSpec((1, D_MODEL), const)],
            out_specs=(pl.BlockSpec((tb, D_MODEL), row), pl.BlockSpec((tb, D_MODEL), row)),
            scratch_shapes=[pltpu.VMEM((TOP_K, tb, D_MODEL), jnp.float32),
                            pltpu.SemaphoreType.DMA(())],
        ),
        compiler_params=_params("arbitrary"),
        name="moe_combine_ln",
    )(pos, y, gates, h, g.reshape(1, D_MODEL), b.reshape(1, D_MODEL))


def _moe(h, layer, router_w, router_b, w_in, b_in, w_out, b_out, ln_g, ln_b):
    top_e, gates, rank, cnt = _router(h, router_w[layer], router_b[layer])
    pos, tile_idx, tile_e, nsub, pad_start, pad_n = _route_tables(top_e, rank, cnt)
    xs = _dispatch(h, pos, pad_start, pad_n)
    act = _moe_up(xs, w_in, b_in, layer, tile_idx, tile_e, nsub)
    y = _moe_down(act, w_out, b_out, layer, tile_idx, tile_e, nsub)
    return _combine_ln(y, pos, gates, h, ln_g[layer], ln_b[layer])


def kernel(x, mem, w_in_dil, w_in_ret, ret_gn_g, w_mem_kv, w_mix_out, ln_mix_g, ln_mix_b, router_w, router_b, moe_w_in, moe_b_in, moe_w_out, moe_b_out, ln_ffn_g, ln_ffn_b):
    bf16 = jnp.bfloat16
    h = x.reshape(N_TOK, D_MODEL)
    hb = h.astype(bf16)
    memb = mem.reshape(BATCH * MEM_LEN, D_MODEL).astype(bf16)
    for layer in range(DEPTH):
        slot = layer // 2
        if layer % 2 == 0:
            proj = _matmul(hb, w_in_dil[slot].astype(bf16), bf16)
            self_out = _dilated_attention(proj)
        else:
            proj = _matmul(hb, w_in_ret[slot].astype(bf16), bf16)
            self_out = _retention(proj, ret_gn_g[slot])
        memkv = _matmul(memb, w_mem_kv[layer].astype(bf16), bf16)
        mem_out = _memory_attention(proj, memkv)
        h, hb = _mix_ln(self_out, mem_out, w_mix_out[layer].astype(bf16), h,
                        ln_mix_g[layer], ln_mix_b[layer])
        h, hb = _moe(h, layer, router_w, router_b, moe_w_in, moe_b_in, moe_w_out, moe_b_out,
                     ln_ffn_g, ln_ffn_b)
    return h.reshape(BATCH, SEQ, D_MODEL)
```

```python
import functools
import math

import jax
import jax.numpy as jnp
from jax import lax
from jax.experimental import pallas as pl
from jax.experimental.pallas import tpu as pltpu

D_MODEL = 2048
BATCH = 2
SEQ = 4096
DEPTH = 2
HEAD_DIM = 128
DILATED_GROUPS = ((128, 1), (512, 4), (2048, 16))
HEADS_PER_GROUP = 4
N_SELF_HEADS = len(DILATED_GROUPS) * HEADS_PER_GROUP
N_RET_HEADS = 12
SELF_WIDTH = N_SELF_HEADS * HEAD_DIM
MEM_HEADS = 4
MEM_LEN = 256
MEM_WIDTH = MEM_HEADS * HEAD_DIM
RET_CHUNK = 128
N_EXPERTS = 32
TOP_K = 4
D_EXPERT = D_MODEL
SWIGLU_ALPHA = 1.702
SWIGLU_LIMIT = 7.0
DEEPNORM_ALPHA = (2 * DEPTH) ** 0.25
LN_EPS = 1e-5
NEG_INF = -1e30

N_TOK = BATCH * SEQ
GROUP_WIDTH = HEADS_PER_GROUP * HEAD_DIM
ATTN_STEPS = 128

LANES = 128
VMEM_LIMIT = 56 * 1024 * 1024

MM_TM = 1024
MM_TN = 512
ROW_TILE = 512
ROUTER_TM = 256

MOE_SUB = 128
MOE_TILE = 1280
MOE_SUBS = MOE_TILE // MOE_SUB
MOE_FAST_SUBS = (8, 9, 10)
MOE_TF = 512
MOE_TILES = (N_TOK * TOP_K) // MOE_TILE + N_EXPERTS
MOE_ROWS = (MOE_TILES + 1) * MOE_TILE
DISPATCH_TB = 256
COMBINE_TB = 128


def _alibi_slopes(n):
    def pow2(m):
        start = 2.0 ** (-8.0 / m)
        return [start ** (i + 1) for i in range(m)]

    if math.log2(n).is_integer():
        s = pow2(n)
    else:
        c = 2 ** math.floor(math.log2(n))
        s = pow2(c) + pow2(2 * c)[0::2][: n - c]
    return sorted(s, reverse=True)


def _params(*sem):
    return pltpu.CompilerParams(dimension_semantics=sem, vmem_limit_bytes=VMEM_LIMIT)


def _layer_norm_rows(z, g, b):
    mu = jnp.mean(z, axis=-1, keepdims=True)
    zc = z - mu
    var = jnp.mean(zc * zc, axis=-1, keepdims=True)
    return zc * lax.rsqrt(var + LN_EPS) * g + b


def _dot_nt(a, b):
    return lax.dot_general(a, b, (((1,), (1,)), ((), ())), preferred_element_type=jnp.float32)


def _dot(a, b):
    return jnp.dot(a, b, preferred_element_type=jnp.float32)


def _mm_kernel(x_ref, w_ref, o_ref, wb_ref):
    @pl.when(pl.program_id(1) == 0)
    def _():
        wb_ref[...] = w_ref[...].astype(jnp.bfloat16)

    o_ref[...] = _dot(x_ref[...], wb_ref[...]).astype(o_ref.dtype)


def _matmul(x, w, layer, out_dtype):
    m, k = x.shape
    n = w.shape[2]
    tm = min(MM_TM, m)
    return pl.pallas_call(
        _mm_kernel,
        out_shape=jax.ShapeDtypeStruct((m, n), out_dtype),
        grid=(n // MM_TN, m // tm),
        in_specs=[pl.BlockSpec((tm, k), lambda j, i: (i, 0)),
                  pl.BlockSpec((None, k, MM_TN), lambda j, i: (layer, 0, j))],
        out_specs=pl.BlockSpec((tm, MM_TN), lambda j, i: (i, j)),
        scratch_shapes=[pltpu.VMEM((k, MM_TN), jnp.bfloat16)],
        compiler_params=_params("parallel", "arbitrary"),
        name="dense_matmul",
    )(x, w)


def _dil_attn_kernel(q_ref, kp_ref, kc_ref, vp_ref, vc_ref, o_ref, lse_ref, *, slopes, dilation):
    n = pl.program_id(2)
    qi = lax.broadcasted_iota(jnp.int32, (ATTN_STEPS, ATTN_STEPS), 0)
    kj = lax.broadcasted_iota(jnp.int32, (ATTN_STEPS, ATTN_STEPS), 1)
    diff = qi - kj
    valid_c = diff >= 0
    valid_p = jnp.logical_and(diff <= 0, n > 0)
    dist_c = (diff * dilation).astype(jnp.float32)
    dist_p = ((diff + ATTN_STEPS) * dilation).astype(jnp.float32)
    scale = HEAD_DIM ** -0.5
    for h in range(HEADS_PER_GROUP):
        sl = slice(h * HEAD_DIM, (h + 1) * HEAD_DIM)
        q = q_ref[:, sl]
        s_c = _dot_nt(q, kc_ref[:, sl]) * scale
        s_p = _dot_nt(q, kp_ref[:, sl]) * scale
        s_c = jnp.where(valid_c, s_c - slopes[h] * dist_c, NEG_INF)
        s_p = jnp.where(valid_p, s_p - slopes[h] * dist_p, NEG_INF)
        m = jnp.maximum(jnp.max(s_c, axis=-1, keepdims=True), jnp.max(s_p, axis=-1, keepdims=True))
        e_c = jnp.exp(s_c - m)
        e_p = jnp.exp(s_p - m)
        l = jnp.sum(e_c, axis=-1, keepdims=True) + jnp.sum(e_p, axis=-1, keepdims=True)
        inv_l = 1.0 / l
        p_c = (e_c * inv_l).astype(jnp.bfloat16)
        p_p = (e_p * inv_l).astype(jnp.bfloat16)
        o_ref[:, sl] = _dot(p_c, vc_ref[:, sl]) + _dot(p_p, vp_ref[:, sl])
        lse_ref[:, sl] = jnp.broadcast_to(m + jnp.log(l), (ATTN_STEPS, HEAD_DIM))


def _dilated_group(proj, group):
    _, dilation = DILATED_GROUPS[group]
    length = SEQ // dilation
    nb = length // ATTN_STEPS
    kb = SELF_WIDTH // GROUP_WIDTH
    if dilation == 1:
        src, first, stride = proj, group, kb
    else:
        src = jnp.concatenate([proj[:, s * SELF_WIDTH + group * GROUP_WIDTH:][:, :GROUP_WIDTH]
                               for s in range(3)], axis=1)
        first, stride = 0, 1
    c = src.shape[1]
    cb = c // GROUP_WIDTH
    view = src.reshape(BATCH, length, dilation * c)
    blk = (None, ATTN_STEPS, GROUP_WIDTH)

    def col(section):
        return lambda b, r, n: (b, n, r * cb + section * stride + first)

    def col_prev(section):
        return lambda b, r, n: (b, jnp.maximum(n - 1, 0), r * cb + section * stride + first)

    slopes = tuple(_alibi_slopes(N_SELF_HEADS)[group * HEADS_PER_GROUP:(group + 1) * HEADS_PER_GROUP])
    out_shape = jax.ShapeDtypeStruct((BATCH, length, dilation * GROUP_WIDTH), jnp.float32)
    out_spec = pl.BlockSpec(blk, lambda b, r, n: (b, n, r))
    o, lse = pl.pallas_call(
        functools.partial(_dil_attn_kernel, slopes=slopes, dilation=dilation),
        out_shape=(out_shape, out_shape),
        grid=(BATCH, dilation, nb),
        in_specs=[pl.BlockSpec(blk, col(0)),
                  pl.BlockSpec(blk, col_prev(1)), pl.BlockSpec(blk, col(1)),
                  pl.BlockSpec(blk, col_prev(2)), pl.BlockSpec(blk, col(2))],
        out_specs=(out_spec, out_spec),
        compiler_params=_params("parallel", "parallel", "arbitrary"),
        name=f"dilated_attn_g{group}",
    )(view, view, view, view, view)
    return o.reshape(N_TOK, GROUP_WIDTH), lse.reshape(N_TOK, GROUP_WIDTH)


def _dil_combine_kernel(o0_ref, o1_ref, o2_ref, l0_ref, l1_ref, l2_ref, out_ref):
    l0, l1, l2 = l0_ref[...], l1_ref[...], l2_ref[...]
    m = jnp.maximum(jnp.maximum(l0, l1), l2)
    e0, e1, e2 = jnp.exp(l0 - m), jnp.exp(l1 - m), jnp.exp(l2 - m)
    inv = 1.0 / (e0 + e1 + e2)
    out_ref[:, 0 * GROUP_WIDTH:1 * GROUP_WIDTH] = (o0_ref[...] * (e0 * inv)).astype(out_ref.dtype)
    out_ref[:, 1 * GROUP_WIDTH:2 * GROUP_WIDTH] = (o1_ref[...] * (e1 * inv)).astype(out_ref.dtype)
    out_ref[:, 2 * GROUP_WIDTH:3 * GROUP_WIDTH] = (o2_ref[...] * (e2 * inv)).astype(out_ref.dtype)


def _dilated_attention(proj):
    outs, lses = zip(*[_dilated_group(proj, g) for g in range(len(DILATED_GROUPS))])
    spec = pl.BlockSpec((ROW_TILE, GROUP_WIDTH), lambda i: (i, 0))
    return pl.pallas_call(
        _dil_combine_kernel,
        out_shape=jax.ShapeDtypeStruct((N_TOK, SELF_WIDTH), jnp.bfloat16),
        grid=(N_TOK // ROW_TILE,),
        in_specs=[spec] * 6,
        out_specs=pl.BlockSpec((ROW_TILE, SELF_WIDTH), lambda i: (i, 0)),
        compiler_params=_params("parallel"),
        name="dilated_combine",
    )(*outs, *lses)


def _retention_kernel(q_ref, k_ref, v_ref, g_ref, dmat_ref, kdec_ref, qdec_ref, cdec_ref, gn_ref,
                      o_ref, state_ref):
    @pl.when(pl.program_id(1) == 0)
    def _():
        state_ref[...] = jnp.zeros_like(state_ref)

    for h in range(N_RET_HEADS):
        sl = slice(h * HEAD_DIM, (h + 1) * HEAD_DIM)
        q, k, v = q_ref[:, sl], k_ref[:, sl], v_ref[:, sl]
        scores = _dot_nt(q, k) * dmat_ref[h]
        intra = _dot(scores.astype(jnp.bfloat16), v)
        state = state_ref[h]
        cross = _dot(q, state.astype(jnp.bfloat16)) * qdec_ref[:, sl]
        kw = (k.astype(jnp.float32) * kdec_ref[:, sl]).T.astype(jnp.bfloat16)
        state_ref[h] = state * cdec_ref[h] + _dot(kw, v)
        r = intra + cross
        mu = jnp.mean(r, axis=-1, keepdims=True)
        rc = r - mu
        var = jnp.mean(rc * rc, axis=-1, keepdims=True)
        rn = rc * lax.rsqrt(var + LN_EPS) * gn_ref[:, sl]
        gate = g_ref[:, sl].astype(jnp.float32)
        o_ref[:, sl] = (gate * (1.0 / (1.0 + jnp.exp(-gate))) * rn).astype(o_ref.dtype)


def _retention(proj, gn_gain):
    c = RET_CHUNK
    log_gamma = jnp.log1p(-jnp.exp2(-(5.0 + jnp.arange(N_RET_HEADS, dtype=jnp.float32))))
    idx = jnp.arange(c, dtype=jnp.float32)
    diff = idx[:, None] - idx[None, :]
    decay = jnp.where(diff >= 0, jnp.exp(jnp.maximum(diff, 0.0)[None] * log_gamma[:, None, None]), 0.0)
    scale = HEAD_DIM ** -0.5
    dmat = decay * scale
    k_decay = jnp.exp((c - 1 - idx)[:, None] * log_gamma[None, :]) * scale
    q_decay = jnp.exp((idx + 1.0)[:, None] * log_gamma[None, :])
    kdec = jnp.repeat(k_decay, HEAD_DIM, axis=1)
    qdec = jnp.repeat(q_decay, HEAD_DIM, axis=1)
    cdec = jnp.broadcast_to(jnp.exp(c * log_gamma)[:, None, None], (N_RET_HEADS, 1, HEAD_DIM))
    gn = gn_gain.reshape(1, SELF_WIDTH).astype(jnp.float32)
    nchunk = SEQ // c
    blk = (c, SELF_WIDTH)

    def section(s):
        return pl.BlockSpec(blk, lambda b, n: (b * nchunk + n, s))

    const2 = lambda b, n: (0, 0)
    const3 = lambda b, n: (0, 0, 0)
    return pl.pallas_call(
        _retention_kernel,
        out_shape=jax.ShapeDtypeStruct((N_TOK, SELF_WIDTH), jnp.bfloat16),
        grid=(BATCH, nchunk),
        in_specs=[section(0), section(1), section(2), section(3),
                  pl.BlockSpec((N_RET_HEADS, c, c), const3),
                  pl.BlockSpec(blk, const2), pl.BlockSpec(blk, const2),
                  pl.BlockSpec((N_RET_HEADS, 1, HEAD_DIM), const3),
                  pl.BlockSpec((1, SELF_WIDTH), const2)],
        out_specs=pl.BlockSpec(blk, lambda b, n: (b * nchunk + n, 0)),
        scratch_shapes=[pltpu.VMEM((N_RET_HEADS, HEAD_DIM, HEAD_DIM), jnp.float32)],
        compiler_params=_params("parallel", "arbitrary"),
        name="retention",
    )(proj, proj, proj, proj, dmat, kdec, qdec, cdec, gn)


def _mem_attn_kernel(q_ref, k_ref, v_ref, o_ref):
    scale = HEAD_DIM ** -0.5
    for h in range(MEM_HEADS):
        sl = slice(h * HEAD_DIM, (h + 1) * HEAD_DIM)
        s = _dot_nt(q_ref[:, sl], k_ref[:, sl]) * scale
        e = jnp.exp(s - jnp.max(s, axis=-1, keepdims=True))
        p = e * (1.0 / jnp.sum(e, axis=-1, keepdims=True))
        o_ref[:, sl] = _dot(p.astype(jnp.bfloat16), v_ref[:, sl]).astype(o_ref.dtype)


def _memory_attention(proj, memkv):
    qcol = proj.shape[1] // MEM_WIDTH - 1
    per_b = SEQ // ROW_TILE
    return pl.pallas_call(
        _mem_attn_kernel,
        out_shape=jax.ShapeDtypeStruct((N_TOK, MEM_WIDTH), jnp.bfloat16),
        grid=(BATCH, per_b),
        in_specs=[pl.BlockSpec((ROW_TILE, MEM_WIDTH), lambda b, i: (b * per_b + i, qcol)),
                  pl.BlockSpec((MEM_LEN, MEM_WIDTH), lambda b, i: (b, 0)),
                  pl.BlockSpec((MEM_LEN, MEM_WIDTH), lambda b, i: (b, 1))],
        out_specs=pl.BlockSpec((ROW_TILE, MEM_WIDTH), lambda b, i: (b * per_b + i, 0)),
        compiler_params=_params("parallel", "parallel"),
        name="memory_attn",
    )(proj, memkv, memkv)


def _mix_ln_kernel(so_ref, mo_ref, wt_ref, wb_ref, h_ref, g_ref, b_ref, hf_ref, hb_ref):
    mix = _dot(so_ref[...], wt_ref[...]) + _dot(mo_ref[...], wb_ref[...])
    y = _layer_norm_rows(DEEPNORM_ALPHA * h_ref[...] + mix, g_ref[...], b_ref[...])
    hf_ref[...] = y
    hb_ref[...] = y.astype(hb_ref.dtype)


def _mix_ln(self_out, mem_out, w_mix, h, g, b):
    row = lambda i: (i, 0)
    const = lambda i: (0, 0)
    return pl.pallas_call(
        _mix_ln_kernel,
        out_shape=(jax.ShapeDtypeStruct((N_TOK, D_MODEL), jnp.float32),
                   jax.ShapeDtypeStruct((N_TOK, D_MODEL), jnp.bfloat16)),
        grid=(N_TOK // ROW_TILE,),
        in_specs=[pl.BlockSpec((ROW_TILE, SELF_WIDTH), row),
                  pl.BlockSpec((ROW_TILE, MEM_WIDTH), row),
                  pl.BlockSpec((SELF_WIDTH, D_MODEL), const),
                  pl.BlockSpec((MEM_WIDTH, D_MODEL), lambda i: (SELF_WIDTH // MEM_WIDTH, 0)),
                  pl.BlockSpec((ROW_TILE, D_MODEL), row),
                  pl.BlockSpec((1, D_MODEL), const), pl.BlockSpec((1, D_MODEL), const)],
        out_specs=(pl.BlockSpec((ROW_TILE, D_MODEL), row), pl.BlockSpec((ROW_TILE, D_MODEL), row)),
        compiler_params=_params("parallel"),
        name="mix_ln",
    )(self_out, mem_out, w_mix, w_mix, h, g.reshape(1, D_MODEL), b.reshape(1, D_MODEL))


def _router_kernel(h_ref, w_ref, b_ref, e_ref, g_ref, r_ref, cnt_ref, run_ref):
    tm = ROUTER_TM

    @pl.when(pl.program_id(0) == 0)
    def _():
        run_ref[...] = jnp.zeros_like(run_ref)

    logits = jnp.dot(h_ref[...], w_ref[...], preferred_element_type=jnp.float32,
                     precision=lax.Precision.HIGHEST) + b_ref[...]
    lane = lax.broadcasted_iota(jnp.int32, (tm, LANES), 1).astype(jnp.float32)
    work = logits
    vals, idxs, hots = [], [], []
    for _ in range(TOP_K):
        m = jnp.max(work, axis=-1, keepdims=True)
        idx = jnp.min(jnp.where(work == m, lane, float(LANES)), axis=-1, keepdims=True)
        hot = lane == idx
        vals.append(m)
        idxs.append(idx)
        hots.append(hot)
        work = jnp.where(hot, -jnp.inf, work)
    exps = [jnp.exp(v - vals[0]) for v in vals]
    inv = 1.0 / (exps[0] + exps[1] + exps[2] + exps[3])
    cnt = jnp.zeros((tm, LANES), jnp.float32)
    for hot in hots:
        cnt = cnt + hot.astype(jnp.float32)
    row = lax.broadcasted_iota(jnp.int32, (tm, tm), 0)
    colm = lax.broadcasted_iota(jnp.int32, (tm, tm), 1)
    tri = (row > colm).astype(jnp.bfloat16)
    before = run_ref[...] + _dot(tri, cnt.astype(jnp.bfloat16))
    lane_i = lax.broadcasted_iota(jnp.int32, (tm, LANES), 1)
    e_out = jnp.zeros((tm, LANES), jnp.float32)
    g_out = jnp.zeros((tm, LANES), jnp.float32)
    r_out = jnp.zeros((tm, LANES), jnp.float32)
    for k in range(TOP_K):
        rank = jnp.sum(jnp.where(hots[k], before, 0.0), axis=-1, keepdims=True)
        e_out = jnp.where(lane_i == k, idxs[k], e_out)
        g_out = jnp.where(lane_i == k, exps[k] * inv, g_out)
        r_out = jnp.where(lane_i == k, rank, r_out)
    e_ref[...] = e_out.astype(jnp.int32)
    g_ref[...] = g_out
    r_ref[...] = r_out.astype(jnp.int32)
    run_ref[...] = run_ref[...] + jnp.sum(cnt, axis=0, keepdims=True)
    cnt_ref[...] = run_ref[...].astype(jnp.int32)


def _router(h, router_w, router_b):
    w = jnp.zeros((D_MODEL, LANES), jnp.float32).at[:, :N_EXPERTS].set(router_w)
    b = jnp.full((1, LANES), NEG_INF, jnp.float32).at[0, :N_EXPERTS].set(router_b)
    tm = ROUTER_TM
    row = lambda i: (i, 0)
    const = lambda i: (0, 0)
    lanes_out = pl.BlockSpec((tm, LANES), row)
    e, g, r, cnt = pl.pallas_call(
        _router_kernel,
        out_shape=(jax.ShapeDtypeStruct((N_TOK, LANES), jnp.int32),
                   jax.ShapeDtypeStruct((N_TOK, LANES), jnp.float32),
                   jax.ShapeDtypeStruct((N_TOK, LANES), jnp.int32),
                   jax.ShapeDtypeStruct((1, LANES), jnp.int32)),
        grid=(N_TOK // tm,),
        in_specs=[pl.BlockSpec((tm, D_MODEL), row), pl.BlockSpec((D_MODEL, LANES), const),
                  pl.BlockSpec((1, LANES), const)],
        out_specs=(lanes_out, lanes_out, lanes_out, pl.BlockSpec((1, LANES), const)),
        scratch_shapes=[pltpu.VMEM((1, LANES), jnp.float32)],
        compiler_params=_params("arbitrary"),
        name="router",
    )(h, w, b)
    return e[:, :TOP_K], g, r[:, :TOP_K], cnt[0, :N_EXPERTS]


def _route_tables(top_e, rank, cnt):
    ntile = (cnt + MOE_TILE - 1) // MOE_TILE
    tile_end = jnp.cumsum(ntile)
    tile_base = tile_end - ntile
    pos = (tile_base[top_e] * MOE_TILE + rank).astype(jnp.int32).reshape(-1)
    total = tile_end[-1]
    w = jnp.arange(MOE_TILES, dtype=jnp.int32)
    used = w < total
    wc = jnp.minimum(w, total - 1)
    tile_e = jnp.minimum(jnp.searchsorted(tile_end, wc, side="right"), N_EXPERTS - 1).astype(jnp.int32)
    rows = jnp.clip(cnt[tile_e] - (wc - tile_base[tile_e]) * MOE_TILE, 0, MOE_TILE)
    nsub = jnp.where(used, (rows + MOE_SUB - 1) // MOE_SUB, 0).astype(jnp.int32)
    tile_idx = jnp.where(used, w, MOE_TILES).astype(jnp.int32)
    pad_start = (tile_base * MOE_TILE + cnt).astype(jnp.int32)
    pad_n = ((-cnt) % MOE_SUB).astype(jnp.int32)
    return pos, tile_idx, tile_e, nsub, pad_start, pad_n


def _row_copy(src_vmem, src_row, dst_hbm, dst_row, sem):
    return pltpu.make_async_copy(src_vmem.at[pl.ds(src_row, 1)], dst_hbm.at[pl.ds(dst_row, 1)], sem)


def _dispatch_kernel(pos_ref, pad_start_ref, pad_n_ref, h_ref, xs_hbm, sem):
    step = pl.program_id(0)
    base = step * DISPATCH_TB

    def issue(r, carry):
        for k in range(TOP_K):
            _row_copy(h_ref, r, xs_hbm, pos_ref[(base + r) * TOP_K + k], sem).start()
        return carry

    lax.fori_loop(0, DISPATCH_TB, issue, 0, unroll=8)

    @pl.when(step == 0)
    def _():
        def per_expert(e, carry):
            start = pad_start_ref[e]
            n = pad_n_ref[e]

            def fill(i, c):
                _row_copy(h_ref, 0, xs_hbm, start + i, sem).start()
                return c

            def drain(_, c):
                _row_copy(h_ref, 0, xs_hbm, 0, sem).wait()
                return c

            lax.fori_loop(0, n, fill, 0)
            lax.fori_loop(0, n, drain, 0)
            return carry

        lax.fori_loop(0, N_EXPERTS, per_expert, 0)

    for _ in range(TOP_K):
        pltpu.make_async_copy(h_ref, xs_hbm.at[pl.ds(0, DISPATCH_TB)], sem).wait()


def _dispatch(h, pos, pad_start, pad_n):
    return pl.pallas_call(
        _dispatch_kernel,
        out_shape=jax.ShapeDtypeStruct((MOE_ROWS, D_MODEL), jnp.float32),
        grid_spec=pltpu.PrefetchScalarGridSpec(
            num_scalar_prefetch=3,
            grid=(N_TOK // DISPATCH_TB,),
            in_specs=[pl.BlockSpec((DISPATCH_TB, D_MODEL), lambda i, p, s, n: (i, 0))],
            out_specs=pl.BlockSpec(memory_space=pl.ANY),
            scratch_shapes=[pltpu.SemaphoreType.DMA(())],
        ),
        compiler_params=_params("arbitrary"),
        name="moe_dispatch",
    )(pos, pad_start, pad_n, h)


def _moe_weight_map(layer, nj, first):
    def index_map(w, j, tile_idx, tile_e, nsub):
        return layer, tile_e[w], 0, first + jnp.where(nsub[w] > 0, j, nj - 1)

    return index_map


def _moe_rows(nsub, compute, o_ref):
    ncol = o_ref.shape[1]
    fast = nsub < 0
    for s in MOE_FAST_SUBS:
        m = s * MOE_SUB
        fast = jnp.logical_or(fast, nsub == s)

        @pl.when(nsub == s)
        def _():
            o_ref[0:m, :] = compute(slice(0, m))
            if m < MOE_TILE:
                o_ref[m:MOE_TILE, :] = jnp.zeros((MOE_TILE - m, ncol), o_ref.dtype)

    @pl.when(jnp.logical_and(nsub > 0, jnp.logical_not(fast)))
    def _():
        def block(i):
            return pl.ds(pl.multiple_of(i * MOE_SUB, MOE_SUB), MOE_SUB)

        def one(i, carry):
            o_ref[block(i), :] = compute(block(i))
            return carry

        def zero(i, carry):
            o_ref[block(i), :] = jnp.zeros((MOE_SUB, ncol), o_ref.dtype)
            return carry

        lax.fori_loop(0, nsub, one, 0)
        lax.fori_loop(nsub, MOE_SUBS, zero, 0)


def _moe_up_kernel(tile_ref, exp_ref, nsub_ref, x_ref, wg_ref, wl_ref, bg_ref, bl_ref, o_ref, xb_ref):
    w = pl.program_id(0)
    j = pl.program_id(1)
    nsub = nsub_ref[w]

    @pl.when(nsub == 0)
    def _():
        o_ref[...] = jnp.zeros_like(o_ref)

    for i in range(MOE_SUBS):
        rows = slice(i * MOE_SUB, (i + 1) * MOE_SUB)

        @pl.when(jnp.logical_and(i < nsub, j == 0))
        def _():
            xb_ref[rows, :] = x_ref[rows, :].astype(jnp.bfloat16)

    def swiglu(x):
        gate = _dot(x, wg_ref[...].astype(jnp.bfloat16)) + bg_ref[...]
        lin = _dot(x, wl_ref[...].astype(jnp.bfloat16)) + bl_ref[...]
        gate = jnp.minimum(gate, SWIGLU_LIMIT)
        lin = jnp.clip(lin, -SWIGLU_LIMIT, SWIGLU_LIMIT)
        return (gate * (1.0 / (1.0 + jnp.exp(-SWIGLU_ALPHA * gate))) * (lin + 1.0)).astype(o_ref.dtype)

    _moe_rows(nsub, lambda rows: swiglu(xb_ref[rows, :]), o_ref)


def _moe_up(xs, w_in, b_in, layer, tile_idx, tile_e, nsub):
    nj = D_EXPERT // MOE_TF
    b_in4 = b_in.reshape(DEPTH, N_EXPERTS, 1, 2 * D_EXPERT)
    wblk = (None, None, D_MODEL, MOE_TF)
    bblk = (None, None, 1, MOE_TF)
    return pl.pallas_call(
        _moe_up_kernel,
        out_shape=jax.ShapeDtypeStruct((MOE_ROWS, D_EXPERT), jnp.bfloat16),
        grid_spec=pltpu.PrefetchScalarGridSpec(
            num_scalar_prefetch=3,
            grid=(MOE_TILES, nj),
            in_specs=[pl.BlockSpec((MOE_TILE, D_MODEL), lambda w, j, t, e, n: (t[w], 0)),
                      pl.BlockSpec(wblk, _moe_weight_map(layer, nj, 0)),
                      pl.BlockSpec(wblk, _moe_weight_map(layer, nj, nj)),
                      pl.BlockSpec(bblk, _moe_weight_map(layer, nj, 0)),
                      pl.BlockSpec(bblk, _moe_weight_map(layer, nj, nj))],
            out_specs=pl.BlockSpec((MOE_TILE, MOE_TF),
                                   lambda w, j, t, e, n: (t[w], jnp.where(n[w] > 0, j, 0))),
            scratch_shapes=[pltpu.VMEM((MOE_TILE, D_MODEL), jnp.bfloat16)],
        ),
        compiler_params=_params("arbitrary", "arbitrary"),
        name="moe_up",
    )(tile_idx, tile_e, nsub, xs, w_in, w_in, b_in4, b_in4)


def _moe_down_kernel(tile_ref, exp_ref, nsub_ref, a_ref, w_ref, b_ref, o_ref):
    w = pl.program_id(0)
    nsub = nsub_ref[w]

    @pl.when(nsub == 0)
    def _():
        o_ref[...] = jnp.zeros_like(o_ref)

    _moe_rows(nsub, lambda rows: _dot(a_ref[rows, :], w_ref[...].astype(jnp.bfloat16)) + b_ref[...], o_ref)


def _moe_down(act, w_out, b_out, layer, tile_idx, tile_e, nsub):
    nj = D_MODEL // MOE_TF
    b_out4 = b_out.reshape(DEPTH, N_EXPERTS, 1, D_MODEL)
    return pl.pallas_call(
        _moe_down_kernel,
        out_shape=jax.ShapeDtypeStruct((MOE_ROWS, D_MODEL), jnp.float32),
        grid_spec=pltpu.PrefetchScalarGridSpec(
            num_scalar_prefetch=3,
            grid=(MOE_TILES, nj),
            in_specs=[pl.BlockSpec((MOE_TILE, D_EXPERT), lambda w, j, t, e, n: (t[w], 0)),
                      pl.BlockSpec((None, None, D_EXPERT, MOE_TF), _moe_weight_map(layer, nj, 0)),
                      pl.BlockSpec((None, None, 1, MOE_TF), _moe_weight_map(layer, nj, 0))],
            out_specs=pl.BlockSpec((MOE_TILE, MOE_TF),
                                   lambda w, j, t, e, n: (t[w], jnp.where(n[w] > 0, j, 0))),
        ),
        compiler_params=_params("arbitrary", "arbitrary"),
        name="moe_down",
    )(tile_idx, tile_e, nsub, act, w_out, b_out4)


def _combine_ln_kernel(pos_ref, y_hbm, gates_ref, h_ref, g_ref, b_ref, hf_ref, hb_ref, ybuf_ref, sems):
    step = pl.program_id(0)
    slot = step % 2

    def gather(tile, into):
        base = tile * COMBINE_TB

        def issue(r, carry):
            for k in range(TOP_K):
                src = y_hbm.at[pl.ds(pos_ref[(base + r) * TOP_K + k], 1)]
                pltpu.make_async_copy(src, ybuf_ref.at[into, k, pl.ds(r, 1)], sems.at[into]).start()
            return carry

        lax.fori_loop(0, COMBINE_TB, issue, 0, unroll=8)

    @pl.when(step == 0)
    def _():
        gather(0, 0)

    @pl.when(step + 1 < pl.num_programs(0))
    def _():
        gather(step + 1, 1 - slot)

    for k in range(TOP_K):
        pltpu.make_async_copy(y_hbm.at[pl.ds(0, COMBINE_TB)], ybuf_ref.at[slot, k], sems.at[slot]).wait()

    gates = gates_ref[...]
    ffn = gates[:, 0:1] * ybuf_ref[slot, 0]
    for k in range(1, TOP_K):
        ffn = ffn + gates[:, k:k + 1] * ybuf_ref[slot, k]
    y = _layer_norm_rows(DEEPNORM_ALPHA * h_ref[...] + ffn, g_ref[...], b_ref[...])
    hf_ref[...] = y
    hb_ref[...] = y.astype(hb_ref.dtype)


def _combine_ln(y, pos, gates, h, g, b):
    tb = COMBINE_TB
    row = lambda i, p: (i, 0)
    const = lambda i, p: (0, 0)
    return pl.pallas_call(
        _combine_ln_kernel,
        out_shape=(jax.ShapeDtypeStruct((N_TOK, D_MODEL), jnp.float32),
                   jax.ShapeDtypeStruct((N_TOK, D_MODEL), jnp.bfloat16)),
        grid_spec=pltpu.PrefetchScalarGridSpec(
            num_scalar_prefetch=1,
            grid=(N_TOK // tb,),
            in_specs=[pl.BlockSpec(memory_space=pl.ANY),
                      pl.BlockSpec((tb, LANES), row),
                      pl.BlockSpec((tb, D_MODEL), row),
                      pl.BlockSpec((1, D_MODEL), const), pl.BlockSpec((1, D_MODEL), const)],
            out_specs=(pl.BlockSpec((tb, D_MODEL), row), pl.BlockSpec((tb, D_MODEL), row)),
            scratch_shapes=[pltpu.VMEM((2, TOP_K, tb, D_MODEL), jnp.float32),
                            pltpu.SemaphoreType.DMA((2,))],
        ),
        compiler_params=_params("arbitrary"),
        name="moe_combine_ln",
    )(pos, y, gates, h, g.reshape(1, D_MODEL), b.reshape(1, D_MODEL))


def _moe(h, layer, router_w, router_b, w_in, b_in, w_out, b_out, ln_g, ln_b):
    top_e, gates, rank, cnt = _router(h, router_w[layer], router_b[layer])
    pos, tile_idx, tile_e, nsub, pad_start, pad_n = _route_tables(top_e, rank, cnt)
    xs = _dispatch(h, pos, pad_start, pad_n)
    act = _moe_up(xs, w_in, b_in, layer, tile_idx, tile_e, nsub)
    y = _moe_down(act, w_out, b_out, layer, tile_idx, tile_e, nsub)
    return _combine_ln(y, pos, gates, h, ln_g[layer], ln_b[layer])


def kernel(x, mem, w_in_dil, w_in_ret, ret_gn_g, w_mem_kv, w_mix_out, ln_mix_g, ln_mix_b, router_w, router_b, moe_w_in, moe_b_in, moe_w_out, moe_b_out, ln_ffn_g, ln_ffn_b):
    bf16 = jnp.bfloat16
    h = x.reshape(N_TOK, D_MODEL)
    hb = h.astype(bf16)
    memb = mem.reshape(BATCH * MEM_LEN, D_MODEL).astype(bf16)
    for layer in range(DEPTH):
        slot = layer // 2
        if layer % 2 == 0:
            proj = _matmul(hb, w_in_dil, slot, bf16)
            self_out = _dilated_attention(proj)
        else:
            proj = _matmul(hb, w_in_ret, slot, bf16)
            self_out = _retention(proj, ret_gn_g[slot])
        memkv = _matmul(memb, w_mem_kv, layer, bf16)
        mem_out = _memory_attention(proj, memkv)
        h, hb = _mix_ln(self_out, mem_out, w_mix_out[layer].astype(bf16), h,
                        ln_mix_g[layer], ln_mix_b[layer])
        h, hb = _moe(h, layer, router_w, router_b, moe_w_in, moe_b_in, moe_w_out, moe_b_out,
                     ln_ffn_g, ln_ffn_b)
    return h.reshape(BATCH, SEQ, D_MODEL)
```

```python
import functools
import math

import jax
import jax.numpy as jnp
from jax import lax
from jax.experimental import pallas as pl
from jax.experimental.pallas import tpu as pltpu

D_MODEL = 2048
BATCH = 2
SEQ = 4096
DEPTH = 2
HEAD_DIM = 128
DILATED_GROUPS = ((128, 1), (512, 4), (2048, 16))
HEADS_PER_GROUP = 4
N_SELF_HEADS = len(DILATED_GROUPS) * HEADS_PER_GROUP
N_RET_HEADS = 12
SELF_WIDTH = N_SELF_HEADS * HEAD_DIM
MEM_HEADS = 4
MEM_LEN = 256
MEM_WIDTH = MEM_HEADS * HEAD_DIM
RET_CHUNK = 128
N_EXPERTS = 32
TOP_K = 4
D_EXPERT = D_MODEL
SWIGLU_ALPHA = 1.702
SWIGLU_LIMIT = 7.0
DEEPNORM_ALPHA = (2 * DEPTH) ** 0.25
LN_EPS = 1e-5
NEG_INF = -1e30

N_TOK = BATCH * SEQ
GROUP_WIDTH = HEADS_PER_GROUP * HEAD_DIM
ATTN_STEPS = 128

LANES = 128
VMEM_LIMIT = 56 * 1024 * 1024

MM_TM = 1024
MM_TN = 512
ROW_TILE = 512
ROUTER_TM = 256

MOE_SUB = 128
MOE_TILE = 1280
MOE_SUBS = MOE_TILE // MOE_SUB
MOE_FAST_SUBS = (8, 9, 10)
MOE_TF = 256
MOE_DOWN_TN = 512
MOE_TILES = (N_TOK * TOP_K) // MOE_TILE + N_EXPERTS
MOE_ROWS = (MOE_TILES + 1) * MOE_TILE
DISPATCH_TB = 256
COMBINE_TB = 128


def _alibi_slopes(n):
    def pow2(m):
        start = 2.0 ** (-8.0 / m)
        return [start ** (i + 1) for i in range(m)]

    if math.log2(n).is_integer():
        s = pow2(n)
    else:
        c = 2 ** math.floor(math.log2(n))
        s = pow2(c) + pow2(2 * c)[0::2][: n - c]
    return sorted(s, reverse=True)


def _params(*sem):
    return pltpu.CompilerParams(dimension_semantics=sem, vmem_limit_bytes=VMEM_LIMIT)


def _layer_norm_rows(z, g, b):
    mu = jnp.mean(z, axis=-1, keepdims=True)
    zc = z - mu
    var = jnp.mean(zc * zc, axis=-1, keepdims=True)
    return zc * lax.rsqrt(var + LN_EPS) * g + b


def _dot_nt(a, b):
    return lax.dot_general(a, b, (((1,), (1,)), ((), ())), preferred_element_type=jnp.float32)


def _dot(a, b):
    return jnp.dot(a, b, preferred_element_type=jnp.float32)


def _mm_kernel(x_ref, w_ref, o_ref, wb_ref):
    @pl.when(pl.program_id(1) == 0)
    def _():
        wb_ref[...] = w_ref[...].astype(jnp.bfloat16)

    o_ref[...] = _dot(x_ref[...], wb_ref[...]).astype(o_ref.dtype)


def _matmul(x, w, layer, out_dtype):
    m, k = x.shape
    n = w.shape[2]
    tm = min(MM_TM, m)
    return pl.pallas_call(
        _mm_kernel,
        out_shape=jax.ShapeDtypeStruct((m, n), out_dtype),
        grid=(n // MM_TN, m // tm),
        in_specs=[pl.BlockSpec((tm, k), lambda j, i: (i, 0)),
                  pl.BlockSpec((None, k, MM_TN), lambda j, i: (layer, 0, j))],
        out_specs=pl.BlockSpec((tm, MM_TN), lambda j, i: (i, j)),
        scratch_shapes=[pltpu.VMEM((k, MM_TN), jnp.bfloat16)],
        compiler_params=_params("parallel", "arbitrary"),
        name="dense_matmul",
    )(x, w)


def _dil_attn_kernel(q_ref, kp_ref, kc_ref, vp_ref, vc_ref, o_ref, lse_ref, *, slopes, dilation):
    n = pl.program_id(2)
    qi = lax.broadcasted_iota(jnp.int32, (ATTN_STEPS, ATTN_STEPS), 0)
    kj = lax.broadcasted_iota(jnp.int32, (ATTN_STEPS, ATTN_STEPS), 1)
    diff = qi - kj
    valid_c = diff >= 0
    valid_p = jnp.logical_and(diff <= 0, n > 0)
    dist_c = (diff * dilation).astype(jnp.float32)
    dist_p = ((diff + ATTN_STEPS) * dilation).astype(jnp.float32)
    scale = HEAD_DIM ** -0.5
    for h in range(HEADS_PER_GROUP):
        sl = slice(h * HEAD_DIM, (h + 1) * HEAD_DIM)
        q = q_ref[:, sl]
        s_c = _dot_nt(q, kc_ref[:, sl]) * scale
        s_p = _dot_nt(q, kp_ref[:, sl]) * scale
        s_c = jnp.where(valid_c, s_c - slopes[h] * dist_c, NEG_INF)
        s_p = jnp.where(valid_p, s_p - slopes[h] * dist_p, NEG_INF)
        m = jnp.maximum(jnp.max(s_c, axis=-1, keepdims=True), jnp.max(s_p, axis=-1, keepdims=True))
        e_c = jnp.exp(s_c - m)
        e_p = jnp.exp(s_p - m)
        l = jnp.sum(e_c, axis=-1, keepdims=True) + jnp.sum(e_p, axis=-1, keepdims=True)
        inv_l = 1.0 / l
        p_c = (e_c * inv_l).astype(jnp.bfloat16)
        p_p = (e_p * inv_l).astype(jnp.bfloat16)
        o_ref[:, sl] = _dot(p_c, vc_ref[:, sl]) + _dot(p_p, vp_ref[:, sl])
        lse_ref[:, sl] = jnp.broadcast_to(m + jnp.log(l), (ATTN_STEPS, HEAD_DIM))


def _dilated_group(proj, group):
    _, dilation = DILATED_GROUPS[group]
    length = SEQ // dilation
    nb = length // ATTN_STEPS
    kb = SELF_WIDTH // GROUP_WIDTH
    if dilation == 1:
        src, first, stride = proj, group, kb
    else:
        src = jnp.concatenate([proj[:, s * SELF_WIDTH + group * GROUP_WIDTH:][:, :GROUP_WIDTH]
                               for s in range(3)], axis=1)
        first, stride = 0, 1
    c = src.shape[1]
    cb = c // GROUP_WIDTH
    view = src.reshape(BATCH, length, dilation * c)
    blk = (None, ATTN_STEPS, GROUP_WIDTH)

    def col(section):
        return lambda b, r, n: (b, n, r * cb + section * stride + first)

    def col_prev(section):
        return lambda b, r, n: (b, jnp.maximum(n - 1, 0), r * cb + section * stride + first)

    slopes = tuple(_alibi_slopes(N_SELF_HEADS)[group * HEADS_PER_GROUP:(group + 1) * HEADS_PER_GROUP])
    out_shape = jax.ShapeDtypeStruct((BATCH, length, dilation * GROUP_WIDTH), jnp.float32)
    out_spec = pl.BlockSpec(blk, lambda b, r, n: (b, n, r))
    o, lse = pl.pallas_call(
        functools.partial(_dil_attn_kernel, slopes=slopes, dilation=dilation),
        out_shape=(out_shape, out_shape),
        grid=(BATCH, dilation, nb),
        in_specs=[pl.BlockSpec(blk, col(0)),
                  pl.BlockSpec(blk, col_prev(1)), pl.BlockSpec(blk, col(1)),
                  pl.BlockSpec(blk, col_prev(2)), pl.BlockSpec(blk, col(2))],
        out_specs=(out_spec, out_spec),
        compiler_params=_params("parallel", "parallel", "arbitrary"),
        name=f"dilated_attn_g{group}",
    )(view, view, view, view, view)
    return o.reshape(N_TOK, GROUP_WIDTH), lse.reshape(N_TOK, GROUP_WIDTH)


def _dil_combine_kernel(o0_ref, o1_ref, o2_ref, l0_ref, l1_ref, l2_ref, out_ref):
    l0, l1, l2 = l0_ref[...], l1_ref[...], l2_ref[...]
    m = jnp.maximum(jnp.maximum(l0, l1), l2)
    e0, e1, e2 = jnp.exp(l0 - m), jnp.exp(l1 - m), jnp.exp(l2 - m)
    inv = 1.0 / (e0 + e1 + e2)
    out_ref[:, 0 * GROUP_WIDTH:1 * GROUP_WIDTH] = (o0_ref[...] * (e0 * inv)).astype(out_ref.dtype)
    out_ref[:, 1 * GROUP_WIDTH:2 * GROUP_WIDTH] = (o1_ref[...] * (e1 * inv)).astype(out_ref.dtype)
    out_ref[:, 2 * GROUP_WIDTH:3 * GROUP_WIDTH] = (o2_ref[...] * (e2 * inv)).astype(out_ref.dtype)


def _dilated_attention(proj):
    outs, lses = zip(*[_dilated_group(proj, g) for g in range(len(DILATED_GROUPS))])
    spec = pl.BlockSpec((ROW_TILE, GROUP_WIDTH), lambda i: (i, 0))
    return pl.pallas_call(
        _dil_combine_kernel,
        out_shape=jax.ShapeDtypeStruct((N_TOK, SELF_WIDTH), jnp.bfloat16),
        grid=(N_TOK // ROW_TILE,),
        in_specs=[spec] * 6,
        out_specs=pl.BlockSpec((ROW_TILE, SELF_WIDTH), lambda i: (i, 0)),
        compiler_params=_params("parallel"),
        name="dilated_combine",
    )(*outs, *lses)


def _retention_kernel(q_ref, k_ref, v_ref, g_ref, dmat_ref, kdec_ref, qdec_ref, cdec_ref, gn_ref,
                      o_ref, state_ref):
    @pl.when(pl.program_id(1) == 0)
    def _():
        state_ref[...] = jnp.zeros_like(state_ref)

    for h in range(N_RET_HEADS):
        sl = slice(h * HEAD_DIM, (h + 1) * HEAD_DIM)
        q, k, v = q_ref[:, sl], k_ref[:, sl], v_ref[:, sl]
        scores = _dot_nt(q, k) * dmat_ref[h]
        intra = _dot(scores.astype(jnp.bfloat16), v)
        state = state_ref[h]
        cross = _dot(q, state.astype(jnp.bfloat16)) * qdec_ref[:, sl]
        kw = (k.astype(jnp.float32) * kdec_ref[:, sl]).T.astype(jnp.bfloat16)
        state_ref[h] = state * cdec_ref[h] + _dot(kw, v)
        r = intra + cross
        mu = jnp.mean(r, axis=-1, keepdims=True)
        rc = r - mu
        var = jnp.mean(rc * rc, axis=-1, keepdims=True)
        rn = rc * lax.rsqrt(var + LN_EPS) * gn_ref[:, sl]
        gate = g_ref[:, sl].astype(jnp.float32)
        o_ref[:, sl] = (gate * (1.0 / (1.0 + jnp.exp(-gate))) * rn).astype(o_ref.dtype)


def _retention(proj, gn_gain):
    c = RET_CHUNK
    log_gamma = jnp.log1p(-jnp.exp2(-(5.0 + jnp.arange(N_RET_HEADS, dtype=jnp.float32))))
    idx = jnp.arange(c, dtype=jnp.float32)
    diff = idx[:, None] - idx[None, :]
    decay = jnp.where(diff >= 0, jnp.exp(jnp.maximum(diff, 0.0)[None] * log_gamma[:, None, None]), 0.0)
    scale = HEAD_DIM ** -0.5
    dmat = decay * scale
    k_decay = jnp.exp((c - 1 - idx)[:, None] * log_gamma[None, :]) * scale
    q_decay = jnp.exp((idx + 1.0)[:, None] * log_gamma[None, :])
    kdec = jnp.repeat(k_decay, HEAD_DIM, axis=1)
    qdec = jnp.repeat(q_decay, HEAD_DIM, axis=1)
    cdec = jnp.broadcast_to(jnp.exp(c * log_gamma)[:, None, None], (N_RET_HEADS, 1, HEAD_DIM))
    gn = gn_gain.reshape(1, SELF_WIDTH).astype(jnp.float32)
    nchunk = SEQ // c
    blk = (c, SELF_WIDTH)

    def section(s):
        return pl.BlockSpec(blk, lambda b, n: (b * nchunk + n, s))

    const2 = lambda b, n: (0, 0)
    const3 = lambda b, n: (0, 0, 0)
    return pl.pallas_call(
        _retention_kernel,
        out_shape=jax.ShapeDtypeStruct((N_TOK, SELF_WIDTH), jnp.bfloat16),
        grid=(BATCH, nchunk),
        in_specs=[section(0), section(1), section(2), section(3),
                  pl.BlockSpec((N_RET_HEADS, c, c), const3),
                  pl.BlockSpec(blk, const2), pl.BlockSpec(blk, const2),
                  pl.BlockSpec((N_RET_HEADS, 1, HEAD_DIM), const3),
                  pl.BlockSpec((1, SELF_WIDTH), const2)],
        out_specs=pl.BlockSpec(blk, lambda b, n: (b * nchunk + n, 0)),
        scratch_shapes=[pltpu.VMEM((N_RET_HEADS, HEAD_DIM, HEAD_DIM), jnp.float32)],
        compiler_params=_params("parallel", "arbitrary"),
        name="retention",
    )(proj, proj, proj, proj, dmat, kdec, qdec, cdec, gn)


def _mem_attn_kernel(q_ref, k_ref, v_ref, o_ref):
    scale = HEAD_DIM ** -0.5
    for h in range(MEM_HEADS):
        sl = slice(h * HEAD_DIM, (h + 1) * HEAD_DIM)
        s = _dot_nt(q_ref[:, sl], k_ref[:, sl]) * scale
        e = jnp.exp(s - jnp.max(s, axis=-1, keepdims=True))
        p = e * (1.0 / jnp.sum(e, axis=-1, keepdims=True))
        o_ref[:, sl] = _dot(p.astype(jnp.bfloat16), v_ref[:, sl]).astype(o_ref.dtype)


def _memory_attention(proj, memkv):
    qcol = proj.shape[1] // MEM_WIDTH - 1
    per_b = SEQ // ROW_TILE
    return pl.pallas_call(
        _mem_attn_kernel,
        out_shape=jax.ShapeDtypeStruct((N_TOK, MEM_WIDTH), jnp.bfloat16),
        grid=(BATCH, per_b),
        in_specs=[pl.BlockSpec((ROW_TILE, MEM_WIDTH), lambda b, i: (b * per_b + i, qcol)),
                  pl.BlockSpec((MEM_LEN, MEM_WIDTH), lambda b, i: (b, 0)),
                  pl.BlockSpec((MEM_LEN, MEM_WIDTH), lambda b, i: (b, 1))],
        out_specs=pl.BlockSpec((ROW_TILE, MEM_WIDTH), lambda b, i: (b * per_b + i, 0)),
        compiler_params=_params("parallel", "parallel"),
        name="memory_attn",
    )(proj, memkv, memkv)


def _mix_ln_kernel(so_ref, mo_ref, wt_ref, wb_ref, h_ref, g_ref, b_ref, hf_ref, hb_ref):
    mix = _dot(so_ref[...], wt_ref[...]) + _dot(mo_ref[...], wb_ref[...])
    y = _layer_norm_rows(DEEPNORM_ALPHA * h_ref[...] + mix, g_ref[...], b_ref[...])
    hf_ref[...] = y
    hb_ref[...] = y.astype(hb_ref.dtype)


def _mix_ln(self_out, mem_out, w_mix, h, g, b):
    row = lambda i: (i, 0)
    const = lambda i: (0, 0)
    return pl.pallas_call(
        _mix_ln_kernel,
        out_shape=(jax.ShapeDtypeStruct((N_TOK, D_MODEL), jnp.float32),
                   jax.ShapeDtypeStruct((N_TOK, D_MODEL), jnp.bfloat16)),
        grid=(N_TOK // ROW_TILE,),
        in_specs=[pl.BlockSpec((ROW_TILE, SELF_WIDTH), row),
                  pl.BlockSpec((ROW_TILE, MEM_WIDTH), row),
                  pl.BlockSpec((SELF_WIDTH, D_MODEL), const),
                  pl.BlockSpec((MEM_WIDTH, D_MODEL), lambda i: (SELF_WIDTH // MEM_WIDTH, 0)),
                  pl.BlockSpec((ROW_TILE, D_MODEL), row),
                  pl.BlockSpec((1, D_MODEL), const), pl.BlockSpec((1, D_MODEL), const)],
        out_specs=(pl.BlockSpec((ROW_TILE, D_MODEL), row), pl.BlockSpec((ROW_TILE, D_MODEL), row)),
        compiler_params=_params("parallel"),
        name="mix_ln",
    )(self_out, mem_out, w_mix, w_mix, h, g.reshape(1, D_MODEL), b.reshape(1, D_MODEL))


def _router_kernel(h_ref, w_ref, b_ref, e_ref, g_ref, r_ref, cnt_ref, run_ref):
    tm = ROUTER_TM

    @pl.when(pl.program_id(0) == 0)
    def _():
        run_ref[...] = jnp.zeros_like(run_ref)

    logits = jnp.dot(h_ref[...], w_ref[...], preferred_element_type=jnp.float32,
                     precision=lax.Precision.HIGHEST) + b_ref[...]
    lane = lax.broadcasted_iota(jnp.int32, (tm, LANES), 1).astype(jnp.float32)
    work = logits
    vals, idxs, hots = [], [], []
    for _ in range(TOP_K):
        m = jnp.max(work, axis=-1, keepdims=True)
        idx = jnp.min(jnp.where(work == m, lane, float(LANES)), axis=-1, keepdims=True)
        hot = lane == idx
        vals.append(m)
        idxs.append(idx)
        hots.append(hot)
        work = jnp.where(hot, -jnp.inf, work)
    exps = [jnp.exp(v - vals[0]) for v in vals]
    inv = 1.0 / (exps[0] + exps[1] + exps[2] + exps[3])
    cnt = jnp.zeros((tm, LANES), jnp.float32)
    for hot in hots:
        cnt = cnt + hot.astype(jnp.float32)
    row = lax.broadcasted_iota(jnp.int32, (tm, tm), 0)
    colm = lax.broadcasted_iota(jnp.int32, (tm, tm), 1)
    tri = (row > colm).astype(jnp.bfloat16)
    before = run_ref[...] + _dot(tri, cnt.astype(jnp.bfloat16))
    lane_i = lax.broadcasted_iota(jnp.int32, (tm, LANES), 1)
    e_out = jnp.zeros((tm, LANES), jnp.float32)
    g_out = jnp.zeros((tm, LANES), jnp.float32)
    r_out = jnp.zeros((tm, LANES), jnp.float32)
    for k in range(TOP_K):
        rank = jnp.sum(jnp.where(hots[k], before, 0.0), axis=-1, keepdims=True)
        e_out = jnp.where(lane_i == k, idxs[k], e_out)
        g_out = jnp.where(lane_i == k, exps[k] * inv, g_out)
        r_out = jnp.where(lane_i == k, rank, r_out)
    e_ref[...] = e_out.astype(jnp.int32)
    g_ref[...] = g_out
    r_ref[...] = r_out.astype(jnp.int32)
    run_ref[...] = run_ref[...] + jnp.sum(cnt, axis=0, keepdims=True)
    cnt_ref[...] = run_ref[...].astype(jnp.int32)


def _router(h, router_w, router_b):
    w = jnp.zeros((D_MODEL, LANES), jnp.float32).at[:, :N_EXPERTS].set(router_w)
    b = jnp.full((1, LANES), NEG_INF, jnp.float32).at[0, :N_EXPERTS].set(router_b)
    tm = ROUTER_TM
    row = lambda i: (i, 0)
    const = lambda i: (0, 0)
    lanes_out = pl.BlockSpec((tm, LANES), row)
    e, g, r, cnt = pl.pallas_call(
        _router_kernel,
        out_shape=(jax.ShapeDtypeStruct((N_TOK, LANES), jnp.int32),
                   jax.ShapeDtypeStruct((N_TOK, LANES), jnp.float32),
                   jax.ShapeDtypeStruct((N_TOK, LANES), jnp.int32),
                   jax.ShapeDtypeStruct((1, LANES), jnp.int32)),
        grid=(N_TOK // tm,),
        in_specs=[pl.BlockSpec((tm, D_MODEL), row), pl.BlockSpec((D_MODEL, LANES), const),
                  pl.BlockSpec((1, LANES), const)],
        out_specs=(lanes_out, lanes_out, lanes_out, pl.BlockSpec((1, LANES), const)),
        scratch_shapes=[pltpu.VMEM((1, LANES), jnp.float32)],
        compiler_params=_params("arbitrary"),
        name="router",
    )(h, w, b)
    return e[:, :TOP_K], g, r[:, :TOP_K], cnt[0, :N_EXPERTS]


def _route_tables(top_e, rank, cnt):
    ntile = (cnt + MOE_TILE - 1) // MOE_TILE
    tile_end = jnp.cumsum(ntile)
    tile_base = tile_end - ntile
    flat_e, flat_rank = top_e.reshape(-1), rank.reshape(-1)
    pos = (tile_base[flat_e] * MOE_TILE + flat_rank).astype(jnp.int32)
    total = tile_end[-1]
    w = jnp.arange(MOE_TILES, dtype=jnp.int32)
    used = w < total
    wc = jnp.minimum(w, total - 1)
    tile_e = jnp.minimum(jnp.searchsorted(tile_end, wc, side="right"), N_EXPERTS - 1).astype(jnp.int32)
    rows = jnp.clip(cnt[tile_e] - (wc - tile_base[tile_e]) * MOE_TILE, 0, MOE_TILE)
    nsub = jnp.where(used, (rows + MOE_SUB - 1) // MOE_SUB, 0).astype(jnp.int32)
    tile_idx = jnp.where(used, w, MOE_TILES).astype(jnp.int32)
    pad_start = (tile_base * MOE_TILE + cnt).astype(jnp.int32)
    pad_n = ((-cnt) % MOE_SUB).astype(jnp.int32)
    return pos, tile_idx, tile_e, nsub, pad_start, pad_n


def _row_copy(src_vmem, src_row, dst_hbm, dst_row, sem):
    return pltpu.make_async_copy(src_vmem.at[pl.ds(src_row, 1)], dst_hbm.at[pl.ds(dst_row, 1)], sem)


def _pack_bf16_pairs(x):
    half = x.shape[1] // 2
    lo = lax.bitcast_convert_type(x[:, :half].astype(jnp.float32), jnp.uint32)
    hi = lax.bitcast_convert_type(x[:, half:].astype(jnp.float32), jnp.uint32)
    return (lo >> 16) | (hi & jnp.uint32(0xFFFF0000))


def _unpack_bf16_pairs(words):
    lo = lax.bitcast_convert_type(words << 16, jnp.float32)
    hi = lax.bitcast_convert_type(words & jnp.uint32(0xFFFF0000), jnp.float32)
    return lo.astype(jnp.bfloat16), hi.astype(jnp.bfloat16)


def _dispatch_kernel(pos_ref, pad_start_ref, pad_n_ref, hb_ref, xs_hbm, h_ref, sem):
    step = pl.program_id(0)
    base = step * DISPATCH_TB
    h_ref[...] = _pack_bf16_pairs(hb_ref[...])

    def issue(r, carry):
        for k in range(TOP_K):
            _row_copy(h_ref, r, xs_hbm, pos_ref[(base + r) * TOP_K + k], sem).start()
        return carry

    lax.fori_loop(0, DISPATCH_TB, issue, 0, unroll=8)

    @pl.when(step == 0)
    def _():
        def per_expert(e, carry):
            start = pad_start_ref[e]
            n = pad_n_ref[e]

            def fill(i, c):
                _row_copy(h_ref, 0, xs_hbm, start + i, sem).start()
                return c

            def drain(_, c):
                _row_copy(h_ref, 0, xs_hbm, 0, sem).wait()
                return c

            lax.fori_loop(0, n, fill, 0)
            lax.fori_loop(0, n, drain, 0)
            return carry

        lax.fori_loop(0, N_EXPERTS, per_expert, 0)

    for _ in range(TOP_K):
        pltpu.make_async_copy(h_ref, xs_hbm.at[pl.ds(0, DISPATCH_TB)], sem).wait()


def _dispatch(hb, pos, pad_start, pad_n):
    return pl.pallas_call(
        _dispatch_kernel,
        out_shape=jax.ShapeDtypeStruct((MOE_ROWS, D_MODEL // 2), jnp.uint32),
        grid_spec=pltpu.PrefetchScalarGridSpec(
            num_scalar_prefetch=3,
            grid=(N_TOK // DISPATCH_TB,),
            in_specs=[pl.BlockSpec((DISPATCH_TB, D_MODEL), lambda i, p, s, n: (i, 0))],
            out_specs=pl.BlockSpec(memory_space=pl.ANY),
            scratch_shapes=[pltpu.VMEM((DISPATCH_TB, D_MODEL // 2), jnp.uint32),
                            pltpu.SemaphoreType.DMA(())],
        ),
        compiler_params=_params("arbitrary"),
        name="moe_dispatch",
    )(pos, pad_start, pad_n, hb)


def _moe_weight_map(layer, nj, first, hidden_axis):
    def index_map(w, j, tile_idx, tile_e, nsub):
        hidden = first + jnp.where(nsub[w] > 0, j, nj - 1)
        return (layer, tile_e[w], hidden, 0) if hidden_axis == 2 else (layer, tile_e[w], 0, hidden)

    return index_map


def _moe_rows(nsub, accumulate):
    fast = nsub < 0
    for s in MOE_FAST_SUBS:
        fast = jnp.logical_or(fast, nsub == s)
        pl.when(nsub == s)(functools.partial(accumulate, slice(0, s * MOE_SUB)))

    @pl.when(jnp.logical_and(nsub > 0, jnp.logical_not(fast)))
    def _():
        def one(i, carry):
            accumulate(pl.ds(pl.multiple_of(i * MOE_SUB, MOE_SUB), MOE_SUB))
            return carry

        lax.fori_loop(0, nsub, one, 0)


def _moe_expert_kernel(tile_ref, exp_ref, nsub_ref, x_ref, wg_ref, wl_ref, bg_ref, bl_ref, wo_ref, bo_ref,
                       o_ref, xb_ref):
    w = pl.program_id(0)
    first = pl.program_id(1) == 0
    nsub = nsub_ref[w]
    half = D_MODEL // 2

    @pl.when(jnp.logical_and(nsub == 0, first))
    def _():
        o_ref[...] = jnp.zeros_like(o_ref)

    for i in range(MOE_SUBS):
        rows = slice(i * MOE_SUB, (i + 1) * MOE_SUB)

        @pl.when(jnp.logical_and(i < nsub, first))
        def _():
            lo, hi = _unpack_bf16_pairs(x_ref[rows, :])
            xb_ref[rows, 0:half] = lo
            xb_ref[rows, half:D_MODEL] = hi
            o_ref[rows, :] = jnp.broadcast_to(bo_ref[...], (MOE_SUB, D_MODEL))

        @pl.when(jnp.logical_and(jnp.logical_and(i >= nsub, nsub > 0), first))
        def _():
            o_ref[rows, :] = jnp.zeros((MOE_SUB, D_MODEL), o_ref.dtype)

    def accumulate(rows):
        x = xb_ref[rows, :]
        gate = _dot(x, wg_ref[...].astype(jnp.bfloat16)) + bg_ref[...]
        lin = _dot(x, wl_ref[...].astype(jnp.bfloat16)) + bl_ref[...]
        gate = jnp.minimum(gate, SWIGLU_LIMIT)
        lin = jnp.clip(lin, -SWIGLU_LIMIT, SWIGLU_LIMIT)
        act = (gate * (1.0 / (1.0 + jnp.exp(-SWIGLU_ALPHA * gate))) * (lin + 1.0)).astype(jnp.bfloat16)
        for c in range(D_MODEL // MOE_DOWN_TN):
            cols = slice(c * MOE_DOWN_TN, (c + 1) * MOE_DOWN_TN)
            o_ref[rows, cols] = o_ref[rows, cols] + _dot(act, wo_ref[:, cols].astype(jnp.bfloat16))

    _moe_rows(nsub, accumulate)


def _moe_experts(xs, w_in, b_in, w_out, b_out, layer, tile_idx, tile_e, nsub):
    nj = D_EXPERT // MOE_TF
    b_in4 = b_in.reshape(DEPTH, N_EXPERTS, 1, 2 * D_EXPERT)
    b_out4 = b_out.reshape(DEPTH, N_EXPERTS, 1, D_MODEL)
    wblk = (None, None, D_MODEL, MOE_TF)
    bblk = (None, None, 1, MOE_TF)
    tile_map = lambda w, j, t, e, n: (t[w], 0)
    return pl.pallas_call(
        _moe_expert_kernel,
        out_shape=jax.ShapeDtypeStruct((MOE_ROWS, D_MODEL), jnp.float32),
        grid_spec=pltpu.PrefetchScalarGridSpec(
            num_scalar_prefetch=3,
            grid=(MOE_TILES, nj),
            in_specs=[pl.BlockSpec((MOE_TILE, D_MODEL // 2), tile_map),
                      pl.BlockSpec(wblk, _moe_weight_map(layer, nj, 0, 3)),
                      pl.BlockSpec(wblk, _moe_weight_map(layer, nj, nj, 3)),
                      pl.BlockSpec(bblk, _moe_weight_map(layer, nj, 0, 3)),
                      pl.BlockSpec(bblk, _moe_weight_map(layer, nj, nj, 3)),
                      pl.BlockSpec((None, None, MOE_TF, D_MODEL), _moe_weight_map(layer, nj, 0, 2)),
                      pl.BlockSpec((None, None, 1, D_MODEL), lambda w, j, t, e, n: (layer, e[w], 0, 0))],
            out_specs=pl.BlockSpec((MOE_TILE, D_MODEL), tile_map),
            scratch_shapes=[pltpu.VMEM((MOE_TILE, D_MODEL), jnp.bfloat16)],
        ),
        compiler_params=_params("arbitrary", "arbitrary"),
        name="moe_experts",
    )(tile_idx, tile_e, nsub, xs, w_in, w_in, b_in4, b_in4, w_out, b_out4)


def _combine_ln_kernel(pos_ref, y_hbm, gates_ref, h_ref, g_ref, b_ref, hf_ref, hb_ref, ybuf_ref, sems):
    step = pl.program_id(0)
    slot = step % 2

    def gather(tile, into):
        base = tile * COMBINE_TB

        def issue(r, carry):
            for k in range(TOP_K):
                src = y_hbm.at[pl.ds(pos_ref[(base + r) * TOP_K + k], 1)]
                pltpu.make_async_copy(src, ybuf_ref.at[into, k, pl.ds(r, 1)], sems.at[into]).start()
            return carry

        lax.fori_loop(0, COMBINE_TB, issue, 0, unroll=8)

    @pl.when(step == 0)
    def _():
        gather(0, 0)

    @pl.when(step + 1 < pl.num_programs(0))
    def _():
        gather(step + 1, 1 - slot)

    for k in range(TOP_K):
        pltpu.make_async_copy(y_hbm.at[pl.ds(0, COMBINE_TB)], ybuf_ref.at[slot, k], sems.at[slot]).wait()

    gates = gates_ref[...]
    ffn = gates[:, 0:1] * ybuf_ref[slot, 0]
    for k in range(1, TOP_K):
        ffn = ffn + gates[:, k:k + 1] * ybuf_ref[slot, k]
    y = _layer_norm_rows(DEEPNORM_ALPHA * h_ref[...] + ffn, g_ref[...], b_ref[...])
    hf_ref[...] = y
    hb_ref[...] = y.astype(hb_ref.dtype)


def _combine_ln(y, pos, gates, h, g, b):
    tb = COMBINE_TB
    row = lambda i, p: (i, 0)
    const = lambda i, p: (0, 0)
    return pl.pallas_call(
        _combine_ln_kernel,
        out_shape=(jax.ShapeDtypeStruct((N_TOK, D_MODEL), jnp.float32),
                   jax.ShapeDtypeStruct((N_TOK, D_MODEL), jnp.bfloat16)),
        grid_spec=pltpu.PrefetchScalarGridSpec(
            num_scalar_prefetch=1,
            grid=(N_TOK // tb,),
            in_specs=[pl.BlockSpec(memory_space=pl.ANY),
                      pl.BlockSpec((tb, LANES), row),
                      pl.BlockSpec((tb, D_MODEL), row),
                      pl.BlockSpec((1, D_MODEL), const), pl.BlockSpec((1, D_MODEL), const)],
            out_specs=(pl.BlockSpec((tb, D_MODEL), row), pl.BlockSpec((tb, D_MODEL), row)),
            scratch_shapes=[pltpu.VMEM((2, TOP_K, tb, D_MODEL), jnp.float32),
                            pltpu.SemaphoreType.DMA((2,))],
        ),
        compiler_params=_params("arbitrary"),
        name="moe_combine_ln",
    )(pos, y, gates, h, g.reshape(1, D_MODEL), b.reshape(1, D_MODEL))


def _moe(h, hb, layer, router_w, router_b, w_in, b_in, w_out, b_out, ln_g, ln_b):
    top_e, gates, rank, cnt = _router(h, router_w[layer], router_b[layer])
    pos, tile_idx, tile_e, nsub, pad_start, pad_n = _route_tables(top_e, rank, cnt)
    xs = _dispatch(hb, pos, pad_start, pad_n)
    y = _moe_experts(xs, w_in, b_in, w_out, b_out, layer, tile_idx, tile_e, nsub)
    return _combine_ln(y, pos, gates, h, ln_g[layer], ln_b[layer])


def kernel(x, mem, w_in_dil, w_in_ret, ret_gn_g, w_mem_kv, w_mix_out, ln_mix_g, ln_mix_b, router_w, router_b, moe_w_in, moe_b_in, moe_w_out, moe_b_out, ln_ffn_g, ln_ffn_b):
    bf16 = jnp.bfloat16
    h = x.reshape(N_TOK, D_MODEL)
    hb = h.astype(bf16)
    memb = mem.reshape(BATCH * MEM_LEN, D_MODEL).astype(bf16)
    for layer in range(DEPTH):
        slot = layer // 2
        if layer % 2 == 0:
            proj = _matmul(hb, w_in_dil, slot, bf16)
            self_out = _dilated_attention(proj)
        else:
            proj = _matmul(hb, w_in_ret, slot, bf16)
            self_out = _retention(proj, ret_gn_g[slot])
        memkv = _matmul(memb, w_mem_kv, layer, bf16)
        mem_out = _memory_attention(proj, memkv)
        h, hb = _mix_ln(self_out, mem_out, w_mix_out[layer].astype(bf16), h,
                        ln_mix_g[layer], ln_mix_b[layer])
        h, hb = _moe(h, hb, layer, router_w, router_b, moe_w_in, moe_b_in, moe_w_out, moe_b_out,
                     ln_ffn_g, ln_ffn_b)
    return h.reshape(BATCH, SEQ, D_MODEL)
```

```python
import functools
import math

import jax
import jax.numpy as jnp
from jax import lax
from jax.experimental import pallas as pl
from jax.experimental.pallas import tpu as pltpu

D_MODEL = 2048
BATCH = 2
SEQ = 4096
DEPTH = 2
HEAD_DIM = 128
DILATED_GROUPS = ((128, 1), (512, 4), (2048, 16))
HEADS_PER_GROUP = 4
N_SELF_HEADS = len(DILATED_GROUPS) * HEADS_PER_GROUP
N_RET_HEADS = 12
SELF_WIDTH = N_SELF_HEADS * HEAD_DIM
MEM_HEADS = 4
MEM_LEN = 256
MEM_WIDTH = MEM_HEADS * HEAD_DIM
RET_CHUNK = 128
N_EXPERTS = 32
TOP_K = 4
D_EXPERT = D_MODEL
SWIGLU_ALPHA = 1.702
SWIGLU_LIMIT = 7.0
DEEPNORM_ALPHA = (2 * DEPTH) ** 0.25
LN_EPS = 1e-5
NEG_INF = -1e30

N_TOK = BATCH * SEQ
GROUP_WIDTH = HEADS_PER_GROUP * HEAD_DIM
ATTN_STEPS = 128

LANES = 128
VMEM_LIMIT = 56 * 1024 * 1024

MM_TM = 1024
MM_TN = 512
ROW_TILE = 512
ROUTER_TM = 256

MOE_SUB = 128
MOE_TILE = 1152
MOE_SUBS = MOE_TILE // MOE_SUB
MOE_FAST_SUBS = (8, 9)
MOE_TF = 512
MOE_DOWN_TN = 512
MOE_VMEM_LIMIT = 60 * 1024 * 1024
MOE_TILES = (N_TOK * TOP_K) // MOE_TILE + N_EXPERTS
MOE_ROWS = MOE_TILES * MOE_TILE
DISPATCH_TB = 256
COMBINE_TB = 128


def _alibi_slopes(n):
    def pow2(m):
        start = 2.0 ** (-8.0 / m)
        return [start ** (i + 1) for i in range(m)]

    if math.log2(n).is_integer():
        s = pow2(n)
    else:
        c = 2 ** math.floor(math.log2(n))
        s = pow2(c) + pow2(2 * c)[0::2][: n - c]
    return sorted(s, reverse=True)


def _params(*sem, vmem_limit=VMEM_LIMIT):
    return pltpu.CompilerParams(dimension_semantics=sem, vmem_limit_bytes=vmem_limit)


def _layer_norm_rows(z, g, b):
    mu = jnp.mean(z, axis=-1, keepdims=True)
    zc = z - mu
    var = jnp.mean(zc * zc, axis=-1, keepdims=True)
    return zc * lax.rsqrt(var + LN_EPS) * g + b


def _dot_nt(a, b):
    return lax.dot_general(a, b, (((1,), (1,)), ((), ())), preferred_element_type=jnp.float32)


def _dot(a, b):
    return jnp.dot(a, b, preferred_element_type=jnp.float32)


def _mm_kernel(x_ref, w_ref, o_ref, wb_ref):
    @pl.when(pl.program_id(1) == 0)
    def _():
        wb_ref[...] = w_ref[...].astype(jnp.bfloat16)

    o_ref[...] = _dot(x_ref[...], wb_ref[...]).astype(o_ref.dtype)


def _matmul(x, w, layer, out_dtype):
    m, k = x.shape
    n = w.shape[2]
    tm = min(MM_TM, m)
    return pl.pallas_call(
        _mm_kernel,
        out_shape=jax.ShapeDtypeStruct((m, n), out_dtype),
        grid=(n // MM_TN, m // tm),
        in_specs=[pl.BlockSpec((tm, k), lambda j, i: (i, 0)),
                  pl.BlockSpec((None, k, MM_TN), lambda j, i: (layer, 0, j))],
        out_specs=pl.BlockSpec((tm, MM_TN), lambda j, i: (i, j)),
        scratch_shapes=[pltpu.VMEM((k, MM_TN), jnp.bfloat16)],
        compiler_params=_params("parallel", "arbitrary"),
        name="dense_matmul",
    )(x, w)


def _dil_attn_kernel(q_ref, kp_ref, kc_ref, vp_ref, vc_ref, o_ref, lse_ref, *, slopes, dilation):
    n = pl.program_id(2)
    qi = lax.broadcasted_iota(jnp.int32, (ATTN_STEPS, ATTN_STEPS), 0)
    kj = lax.broadcasted_iota(jnp.int32, (ATTN_STEPS, ATTN_STEPS), 1)
    diff = qi - kj
    valid_c = diff >= 0
    valid_p = jnp.logical_and(diff <= 0, n > 0)
    dist_c = (diff * dilation).astype(jnp.float32)
    dist_p = ((diff + ATTN_STEPS) * dilation).astype(jnp.float32)
    scale = HEAD_DIM ** -0.5
    for h in range(HEADS_PER_GROUP):
        sl = slice(h * HEAD_DIM, (h + 1) * HEAD_DIM)
        q = q_ref[:, sl]
        s_c = _dot_nt(q, kc_ref[:, sl]) * scale
        s_p = _dot_nt(q, kp_ref[:, sl]) * scale
        s_c = jnp.where(valid_c, s_c - slopes[h] * dist_c, NEG_INF)
        s_p = jnp.where(valid_p, s_p - slopes[h] * dist_p, NEG_INF)
        m = jnp.maximum(jnp.max(s_c, axis=-1, keepdims=True), jnp.max(s_p, axis=-1, keepdims=True))
        e_c = jnp.exp(s_c - m)
        e_p = jnp.exp(s_p - m)
        l = jnp.sum(e_c, axis=-1, keepdims=True) + jnp.sum(e_p, axis=-1, keepdims=True)
        inv_l = 1.0 / l
        p_c = (e_c * inv_l).astype(jnp.bfloat16)
        p_p = (e_p * inv_l).astype(jnp.bfloat16)
        o_ref[:, sl] = _dot(p_c, vc_ref[:, sl]) + _dot(p_p, vp_ref[:, sl])
        lse_ref[:, sl] = jnp.broadcast_to(m + jnp.log(l), (ATTN_STEPS, HEAD_DIM))


def _dilated_group(proj, group):
    _, dilation = DILATED_GROUPS[group]
    length = SEQ // dilation
    nb = length // ATTN_STEPS
    kb = SELF_WIDTH // GROUP_WIDTH
    if dilation == 1:
        src, first, stride = proj, group, kb
    else:
        src = jnp.concatenate([proj[:, s * SELF_WIDTH + group * GROUP_WIDTH:][:, :GROUP_WIDTH]
                               for s in range(3)], axis=1)
        first, stride = 0, 1
    c = src.shape[1]
    cb = c // GROUP_WIDTH
    view = src.reshape(BATCH, length, dilation * c)
    blk = (None, ATTN_STEPS, GROUP_WIDTH)

    def col(section):
        return lambda b, r, n: (b, n, r * cb + section * stride + first)

    def col_prev(section):
        return lambda b, r, n: (b, jnp.maximum(n - 1, 0), r * cb + section * stride + first)

    slopes = tuple(_alibi_slopes(N_SELF_HEADS)[group * HEADS_PER_GROUP:(group + 1) * HEADS_PER_GROUP])
    out_shape = jax.ShapeDtypeStruct((BATCH, length, dilation * GROUP_WIDTH), jnp.float32)
    out_spec = pl.BlockSpec(blk, lambda b, r, n: (b, n, r))
    o, lse = pl.pallas_call(
        functools.partial(_dil_attn_kernel, slopes=slopes, dilation=dilation),
        out_shape=(out_shape, out_shape),
        grid=(BATCH, dilation, nb),
        in_specs=[pl.BlockSpec(blk, col(0)),
                  pl.BlockSpec(blk, col_prev(1)), pl.BlockSpec(blk, col(1)),
                  pl.BlockSpec(blk, col_prev(2)), pl.BlockSpec(blk, col(2))],
        out_specs=(out_spec, out_spec),
        compiler_params=_params("parallel", "parallel", "arbitrary"),
        name=f"dilated_attn_g{group}",
    )(view, view, view, view, view)
    return o.reshape(N_TOK, GROUP_WIDTH), lse.reshape(N_TOK, GROUP_WIDTH)


def _dil_combine_kernel(o0_ref, o1_ref, o2_ref, l0_ref, l1_ref, l2_ref, out_ref):
    l0, l1, l2 = l0_ref[...], l1_ref[...], l2_ref[...]
    m = jnp.maximum(jnp.maximum(l0, l1), l2)
    e0, e1, e2 = jnp.exp(l0 - m), jnp.exp(l1 - m), jnp.exp(l2 - m)
    inv = 1.0 / (e0 + e1 + e2)
    out_ref[:, 0 * GROUP_WIDTH:1 * GROUP_WIDTH] = (o0_ref[...] * (e0 * inv)).astype(out_ref.dtype)
    out_ref[:, 1 * GROUP_WIDTH:2 * GROUP_WIDTH] = (o1_ref[...] * (e1 * inv)).astype(out_ref.dtype)
    out_ref[:, 2 * GROUP_WIDTH:3 * GROUP_WIDTH] = (o2_ref[...] * (e2 * inv)).astype(out_ref.dtype)


def _dilated_attention(proj):
    outs, lses = zip(*[_dilated_group(proj, g) for g in range(len(DILATED_GROUPS))])
    spec = pl.BlockSpec((ROW_TILE, GROUP_WIDTH), lambda i: (i, 0))
    return pl.pallas_call(
        _dil_combine_kernel,
        out_shape=jax.ShapeDtypeStruct((N_TOK, SELF_WIDTH), jnp.bfloat16),
        grid=(N_TOK // ROW_TILE,),
        in_specs=[spec] * 6,
        out_specs=pl.BlockSpec((ROW_TILE, SELF_WIDTH), lambda i: (i, 0)),
        compiler_params=_params("parallel"),
        name="dilated_combine",
    )(*outs, *lses)


def _retention_kernel(q_ref, k_ref, v_ref, g_ref, dmat_ref, kdec_ref, qdec_ref, cdec_ref, gn_ref,
                      o_ref, state_ref):
    @pl.when(pl.program_id(1) == 0)
    def _():
        state_ref[...] = jnp.zeros_like(state_ref)

    for h in range(N_RET_HEADS):
        sl = slice(h * HEAD_DIM, (h + 1) * HEAD_DIM)
        q, k, v = q_ref[:, sl], k_ref[:, sl], v_ref[:, sl]
        scores = _dot_nt(q, k) * dmat_ref[h]
        intra = _dot(scores.astype(jnp.bfloat16), v)
        state = state_ref[h]
        cross = _dot(q, state.astype(jnp.bfloat16)) * qdec_ref[:, sl]
        kw = (k.astype(jnp.float32) * kdec_ref[:, sl]).T.astype(jnp.bfloat16)
        state_ref[h] = state * cdec_ref[h] + _dot(kw, v)
        r = intra + cross
        mu = jnp.mean(r, axis=-1, keepdims=True)
        rc = r - mu
        var = jnp.mean(rc * rc, axis=-1, keepdims=True)
        rn = rc * lax.rsqrt(var + LN_EPS) * gn_ref[:, sl]
        gate = g_ref[:, sl].astype(jnp.float32)
        o_ref[:, sl] = (gate * (1.0 / (1.0 + jnp.exp(-gate))) * rn).astype(o_ref.dtype)


def _retention(proj, gn_gain):
    c = RET_CHUNK
    log_gamma = jnp.log1p(-jnp.exp2(-(5.0 + jnp.arange(N_RET_HEADS, dtype=jnp.float32))))
    idx = jnp.arange(c, dtype=jnp.float32)
    diff = idx[:, None] - idx[None, :]
    decay = jnp.where(diff >= 0, jnp.exp(jnp.maximum(diff, 0.0)[None] * log_gamma[:, None, None]), 0.0)
    scale = HEAD_DIM ** -0.5
    dmat = decay * scale
    k_decay = jnp.exp((c - 1 - idx)[:, None] * log_gamma[None, :]) * scale
    q_decay = jnp.exp((idx + 1.0)[:, None] * log_gamma[None, :])
    kdec = jnp.repeat(k_decay, HEAD_DIM, axis=1)
    qdec = jnp.repeat(q_decay, HEAD_DIM, axis=1)
    cdec = jnp.broadcast_to(jnp.exp(c * log_gamma)[:, None, None], (N_RET_HEADS, 1, HEAD_DIM))
    gn = gn_gain.reshape(1, SELF_WIDTH).astype(jnp.float32)
    nchunk = SEQ // c
    blk = (c, SELF_WIDTH)

    def section(s):
        return pl.BlockSpec(blk, lambda b, n: (b * nchunk + n, s))

    const2 = lambda b, n: (0, 0)
    const3 = lambda b, n: (0, 0, 0)
    return pl.pallas_call(
        _retention_kernel,
        out_shape=jax.ShapeDtypeStruct((N_TOK, SELF_WIDTH), jnp.bfloat16),
        grid=(BATCH, nchunk),
        in_specs=[section(0), section(1), section(2), section(3),
                  pl.BlockSpec((N_RET_HEADS, c, c), const3),
                  pl.BlockSpec(blk, const2), pl.BlockSpec(blk, const2),
                  pl.BlockSpec((N_RET_HEADS, 1, HEAD_DIM), const3),
                  pl.BlockSpec((1, SELF_WIDTH), const2)],
        out_specs=pl.BlockSpec(blk, lambda b, n: (b * nchunk + n, 0)),
        scratch_shapes=[pltpu.VMEM((N_RET_HEADS, HEAD_DIM, HEAD_DIM), jnp.float32)],
        compiler_params=_params("parallel", "arbitrary"),
        name="retention",
    )(proj, proj, proj, proj, dmat, kdec, qdec, cdec, gn)


def _mem_attn_kernel(q_ref, k_ref, v_ref, o_ref):
    scale = HEAD_DIM ** -0.5
    for h in range(MEM_HEADS):
        sl = slice(h * HEAD_DIM, (h + 1) * HEAD_DIM)
        s = _dot_nt(q_ref[:, sl], k_ref[:, sl]) * scale
        e = jnp.exp(s - jnp.max(s, axis=-1, keepdims=True))
        p = e * (1.0 / jnp.sum(e, axis=-1, keepdims=True))
        o_ref[:, sl] = _dot(p.astype(jnp.bfloat16), v_ref[:, sl]).astype(o_ref.dtype)


def _memory_attention(proj, memkv):
    qcol = proj.shape[1] // MEM_WIDTH - 1
    per_b = SEQ // ROW_TILE
    return pl.pallas_call(
        _mem_attn_kernel,
        out_shape=jax.ShapeDtypeStruct((N_TOK, MEM_WIDTH), jnp.bfloat16),
        grid=(BATCH, per_b),
        in_specs=[pl.BlockSpec((ROW_TILE, MEM_WIDTH), lambda b, i: (b * per_b + i, qcol)),
                  pl.BlockSpec((MEM_LEN, MEM_WIDTH), lambda b, i: (b, 0)),
                  pl.BlockSpec((MEM_LEN, MEM_WIDTH), lambda b, i: (b, 1))],
        out_specs=pl.BlockSpec((ROW_TILE, MEM_WIDTH), lambda b, i: (b * per_b + i, 0)),
        compiler_params=_params("parallel", "parallel"),
        name="memory_attn",
    )(proj, memkv, memkv)


def _mix_ln_kernel(so_ref, mo_ref, wt_ref, wb_ref, h_ref, g_ref, b_ref, hf_ref, hb_ref):
    mix = _dot(so_ref[...], wt_ref[...]) + _dot(mo_ref[...], wb_ref[...])
    y = _layer_norm_rows(DEEPNORM_ALPHA * h_ref[...] + mix, g_ref[...], b_ref[...])
    hf_ref[...] = y
    hb_ref[...] = y.astype(hb_ref.dtype)


def _mix_ln(self_out, mem_out, w_mix, h, g, b):
    row = lambda i: (i, 0)
    const = lambda i: (0, 0)
    return pl.pallas_call(
        _mix_ln_kernel,
        out_shape=(jax.ShapeDtypeStruct((N_TOK, D_MODEL), jnp.float32),
                   jax.ShapeDtypeStruct((N_TOK, D_MODEL), jnp.bfloat16)),
        grid=(N_TOK // ROW_TILE,),
        in_specs=[pl.BlockSpec((ROW_TILE, SELF_WIDTH), row),
                  pl.BlockSpec((ROW_TILE, MEM_WIDTH), row),
                  pl.BlockSpec((SELF_WIDTH, D_MODEL), const),
                  pl.BlockSpec((MEM_WIDTH, D_MODEL), lambda i: (SELF_WIDTH // MEM_WIDTH, 0)),
                  pl.BlockSpec((ROW_TILE, D_MODEL), row),
                  pl.BlockSpec((1, D_MODEL), const), pl.BlockSpec((1, D_MODEL), const)],
        out_specs=(pl.BlockSpec((ROW_TILE, D_MODEL), row), pl.BlockSpec((ROW_TILE, D_MODEL), row)),
        compiler_params=_params("parallel"),
        name="mix_ln",
    )(self_out, mem_out, w_mix, w_mix, h, g.reshape(1, D_MODEL), b.reshape(1, D_MODEL))


def _router_kernel(h_ref, w_ref, b_ref, e_ref, g_ref, r_ref, cnt_ref, run_ref):
    tm = ROUTER_TM

    @pl.when(pl.program_id(0) == 0)
    def _():
        run_ref[...] = jnp.zeros_like(run_ref)

    logits = jnp.dot(h_ref[...], w_ref[...], preferred_element_type=jnp.float32,
                     precision=lax.Precision.HIGHEST) + b_ref[...]
    lane = lax.broadcasted_iota(jnp.int32, (tm, LANES), 1).astype(jnp.float32)
    work = logits
    vals, idxs, hots = [], [], []
    for _ in range(TOP_K):
        m = jnp.max(work, axis=-1, keepdims=True)
        idx = jnp.min(jnp.where(work == m, lane, float(LANES)), axis=-1, keepdims=True)
        hot = lane == idx
        vals.append(m)
        idxs.append(idx)
        hots.append(hot)
        work = jnp.where(hot, -jnp.inf, work)
    exps = [jnp.exp(v - vals[0]) for v in vals]
    inv = 1.0 / (exps[0] + exps[1] + exps[2] + exps[3])
    cnt = jnp.zeros((tm, LANES), jnp.float32)
    for hot in hots:
        cnt = cnt + hot.astype(jnp.float32)
    row = lax.broadcasted_iota(jnp.int32, (tm, tm), 0)
    colm = lax.broadcasted_iota(jnp.int32, (tm, tm), 1)
    tri = (row > colm).astype(jnp.bfloat16)
    before = run_ref[...] + _dot(tri, cnt.astype(jnp.bfloat16))
    lane_i = lax.broadcasted_iota(jnp.int32, (tm, LANES), 1)
    e_out = jnp.zeros((tm, LANES), jnp.float32)
    g_out = jnp.zeros((tm, LANES), jnp.float32)
    r_out = jnp.zeros((tm, LANES), jnp.float32)
    for k in range(TOP_K):
        rank = jnp.sum(jnp.where(hots[k], before, 0.0), axis=-1, keepdims=True)
        e_out = jnp.where(lane_i == k, idxs[k], e_out)
        g_out = jnp.where(lane_i == k, exps[k] * inv, g_out)
        r_out = jnp.where(lane_i == k, rank, r_out)
    e_ref[...] = e_out.astype(jnp.int32)
    g_ref[...] = g_out
    r_ref[...] = r_out.astype(jnp.int32)
    run_ref[...] = run_ref[...] + jnp.sum(cnt, axis=0, keepdims=True)
    cnt_ref[...] = run_ref[...].astype(jnp.int32)


def _router(h, router_w, router_b):
    w = jnp.zeros((D_MODEL, LANES), jnp.float32).at[:, :N_EXPERTS].set(router_w)
    b = jnp.full((1, LANES), NEG_INF, jnp.float32).at[0, :N_EXPERTS].set(router_b)
    tm = ROUTER_TM
    row = lambda i: (i, 0)
    const = lambda i: (0, 0)
    lanes_out = pl.BlockSpec((tm, LANES), row)
    e, g, r, cnt = pl.pallas_call(
        _router_kernel,
        out_shape=(jax.ShapeDtypeStruct((N_TOK, LANES), jnp.int32),
                   jax.ShapeDtypeStruct((N_TOK, LANES), jnp.float32),
                   jax.ShapeDtypeStruct((N_TOK, LANES), jnp.int32),
                   jax.ShapeDtypeStruct((1, LANES), jnp.int32)),
        grid=(N_TOK // tm,),
        in_specs=[pl.BlockSpec((tm, D_MODEL), row), pl.BlockSpec((D_MODEL, LANES), const),
                  pl.BlockSpec((1, LANES), const)],
        out_specs=(lanes_out, lanes_out, lanes_out, pl.BlockSpec((1, LANES), const)),
        scratch_shapes=[pltpu.VMEM((1, LANES), jnp.float32)],
        compiler_params=_params("arbitrary"),
        name="router",
    )(h, w, b)
    return e[:, :TOP_K], g, r[:, :TOP_K], cnt[0, :N_EXPERTS]


def _route_tables(top_e, rank, cnt):
    ntile = (cnt + MOE_TILE - 1) // MOE_TILE
    tile_end = jnp.cumsum(ntile)
    tile_base = tile_end - ntile
    flat_e, flat_rank = top_e.reshape(-1), rank.reshape(-1)
    pos = (tile_base[flat_e] * MOE_TILE + flat_rank).astype(jnp.int32)
    n_tiles = tile_end[-1:].astype(jnp.int32)
    w = jnp.minimum(jnp.arange(MOE_TILES, dtype=jnp.int32), n_tiles[0] - 1)
    tile_e = jnp.minimum(jnp.searchsorted(tile_end, w, side="right"), N_EXPERTS - 1).astype(jnp.int32)
    rows = jnp.clip(cnt[tile_e] - (w - tile_base[tile_e]) * MOE_TILE, 0, MOE_TILE)
    nsub = ((rows + MOE_SUB - 1) // MOE_SUB).astype(jnp.int32)
    pad_start = (tile_base * MOE_TILE + cnt).astype(jnp.int32)
    pad_n = ((-cnt) % MOE_SUB).astype(jnp.int32)
    return pos, n_tiles, tile_e, nsub, pad_start, pad_n


def _row_copy(src_vmem, src_row, dst_hbm, dst_row, sem):
    return pltpu.make_async_copy(src_vmem.at[pl.ds(src_row, 1)], dst_hbm.at[pl.ds(dst_row, 1)], sem)


def _pack_bf16_pairs(x):
    half = x.shape[1] // 2
    lo = lax.bitcast_convert_type(x[:, :half].astype(jnp.float32), jnp.uint32)
    hi = lax.bitcast_convert_type(x[:, half:].astype(jnp.float32), jnp.uint32)
    return (lo >> 16) | (hi & jnp.uint32(0xFFFF0000))


def _unpack_bf16_pairs(words):
    lo = lax.bitcast_convert_type(words << 16, jnp.float32)
    hi = lax.bitcast_convert_type(words & jnp.uint32(0xFFFF0000), jnp.float32)
    return lo.astype(jnp.bfloat16), hi.astype(jnp.bfloat16)


def _dispatch_kernel(pos_ref, pad_start_ref, pad_n_ref, hb_ref, xs_hbm, h_ref, sem):
    step = pl.program_id(0)
    base = step * DISPATCH_TB
    h_ref[...] = _pack_bf16_pairs(hb_ref[...])

    def issue(r, carry):
        for k in range(TOP_K):
            _row_copy(h_ref, r, xs_hbm, pos_ref[(base + r) * TOP_K + k], sem).start()
        return carry

    lax.fori_loop(0, DISPATCH_TB, issue, 0, unroll=8)

    @pl.when(step == 0)
    def _():
        def per_expert(e, carry):
            start = pad_start_ref[e]
            n = pad_n_ref[e]

            def fill(i, c):
                _row_copy(h_ref, 0, xs_hbm, start + i, sem).start()
                return c

            def drain(_, c):
                _row_copy(h_ref, 0, xs_hbm, 0, sem).wait()
                return c

            lax.fori_loop(0, n, fill, 0)
            lax.fori_loop(0, n, drain, 0)
            return carry

        lax.fori_loop(0, N_EXPERTS, per_expert, 0)

    for _ in range(TOP_K):
        pltpu.make_async_copy(h_ref, xs_hbm.at[pl.ds(0, DISPATCH_TB)], sem).wait()


def _dispatch(hb, pos, pad_start, pad_n):
    return pl.pallas_call(
        _dispatch_kernel,
        out_shape=jax.ShapeDtypeStruct((MOE_ROWS, D_MODEL // 2), jnp.uint32),
        grid_spec=pltpu.PrefetchScalarGridSpec(
            num_scalar_prefetch=3,
            grid=(N_TOK // DISPATCH_TB,),
            in_specs=[pl.BlockSpec((DISPATCH_TB, D_MODEL), lambda i, p, s, n: (i, 0))],
            out_specs=pl.BlockSpec(memory_space=pl.ANY),
            scratch_shapes=[pltpu.VMEM((DISPATCH_TB, D_MODEL // 2), jnp.uint32),
                            pltpu.SemaphoreType.DMA(())],
        ),
        compiler_params=_params("arbitrary"),
        name="moe_dispatch",
    )(pos, pad_start, pad_n, hb)


def _moe_weight_map(layer, first, hidden_axis):
    def index_map(w, j, tile_e, nsub):
        return (layer, tile_e[w], first + j, 0) if hidden_axis == 2 else (layer, tile_e[w], 0, first + j)

    return index_map


def _moe_rows(nsub, accumulate):
    fast = nsub < 0
    for s in MOE_FAST_SUBS:
        fast = jnp.logical_or(fast, nsub == s)
        pl.when(nsub == s)(functools.partial(accumulate, slice(0, s * MOE_SUB)))

    @pl.when(jnp.logical_and(nsub > 0, jnp.logical_not(fast)))
    def _():
        def one(i, carry):
            accumulate(pl.ds(pl.multiple_of(i * MOE_SUB, MOE_SUB), MOE_SUB))
            return carry

        lax.fori_loop(0, nsub, one, 0)


def _moe_expert_kernel(exp_ref, nsub_ref, x_ref, wg_ref, wl_ref, bg_ref, bl_ref, wo_ref, bo_ref, o_ref):
    first = pl.program_id(1) == 0
    nsub = nsub_ref[pl.program_id(0)]
    half = D_MODEL // 2

    for i in range(MOE_SUBS):
        rows = slice(i * MOE_SUB, (i + 1) * MOE_SUB)

        @pl.when(jnp.logical_and(i < nsub, first))
        def _():
            o_ref[rows, :] = jnp.broadcast_to(bo_ref[...], (MOE_SUB, D_MODEL))

        @pl.when(jnp.logical_and(i >= nsub, first))
        def _():
            o_ref[rows, :] = jnp.zeros((MOE_SUB, D_MODEL), o_ref.dtype)

    def up(lo, hi, w_ref, b_ref):
        return (_dot(lo, w_ref[0:half, :].astype(jnp.bfloat16))
                + _dot(hi, w_ref[half:D_MODEL, :].astype(jnp.bfloat16)) + b_ref[...])

    def accumulate(rows):
        lo, hi = _unpack_bf16_pairs(x_ref[rows, :])
        gate = up(lo, hi, wg_ref, bg_ref)
        lin = up(lo, hi, wl_ref, bl_ref)
        gate = jnp.minimum(gate, SWIGLU_LIMIT)
        lin = jnp.clip(lin, -SWIGLU_LIMIT, SWIGLU_LIMIT)
        act = (gate * (1.0 / (1.0 + jnp.exp(-SWIGLU_ALPHA * gate))) * (lin + 1.0)).astype(jnp.bfloat16)
        for c in range(D_MODEL // MOE_DOWN_TN):
            cols = slice(c * MOE_DOWN_TN, (c + 1) * MOE_DOWN_TN)
            o_ref[rows, cols] = o_ref[rows, cols] + _dot(act, wo_ref[:, cols].astype(jnp.bfloat16))

    _moe_rows(nsub, accumulate)


def _moe_experts(xs, w_in, b_in, w_out, b_out, layer, n_tiles, tile_e, nsub):
    nj = D_EXPERT // MOE_TF
    b_in4 = b_in.reshape(DEPTH, N_EXPERTS, 1, 2 * D_EXPERT)
    b_out4 = b_out.reshape(DEPTH, N_EXPERTS, 1, D_MODEL)
    wblk = (None, None, D_MODEL, MOE_TF)
    bblk = (None, None, 1, MOE_TF)
    tile_map = lambda w, j, e, n: (w, 0)
    return pl.pallas_call(
        _moe_expert_kernel,
        out_shape=jax.ShapeDtypeStruct((MOE_ROWS, D_MODEL), jnp.float32),
        grid_spec=pltpu.PrefetchScalarGridSpec(
            num_scalar_prefetch=2,
            grid=(n_tiles[0], nj),
            in_specs=[pl.BlockSpec((MOE_TILE, D_MODEL // 2), tile_map),
                      pl.BlockSpec(wblk, _moe_weight_map(layer, 0, 3)),
                      pl.BlockSpec(wblk, _moe_weight_map(layer, nj, 3)),
                      pl.BlockSpec(bblk, _moe_weight_map(layer, 0, 3)),
                      pl.BlockSpec(bblk, _moe_weight_map(layer, nj, 3)),
                      pl.BlockSpec((None, None, MOE_TF, D_MODEL), _moe_weight_map(layer, 0, 2)),
                      pl.BlockSpec((None, None, 1, D_MODEL), lambda w, j, e, n: (layer, e[w], 0, 0))],
            out_specs=pl.BlockSpec((MOE_TILE, D_MODEL), tile_map),
        ),
        compiler_params=_params("arbitrary", "arbitrary", vmem_limit=MOE_VMEM_LIMIT),
        name="moe_experts",
    )(tile_e, nsub, xs, w_in, w_in, b_in4, b_in4, w_out, b_out4)


def _combine_ln_kernel(pos_ref, y_hbm, gates_ref, h_ref, g_ref, b_ref, hf_ref, hb_ref, ybuf_ref, sems):
    step = pl.program_id(0)
    slot = step % 2

    def gather(tile, into):
        base = tile * COMBINE_TB

        def issue(r, carry):
            for k in range(TOP_K):
                src = y_hbm.at[pl.ds(pos_ref[(base + r) * TOP_K + k], 1)]
                pltpu.make_async_copy(src, ybuf_ref.at[into, k, pl.ds(r, 1)], sems.at[into]).start()
            return carry

        lax.fori_loop(0, COMBINE_TB, issue, 0, unroll=8)

    @pl.when(step == 0)
    def _():
        gather(0, 0)

    @pl.when(step + 1 < pl.num_programs(0))
    def _():
        gather(step + 1, 1 - slot)

    for k in range(TOP_K):
        pltpu.make_async_copy(y_hbm.at[pl.ds(0, COMBINE_TB)], ybuf_ref.at[slot, k], sems.at[slot]).wait()

    gates = gates_ref[...]
    ffn = gates[:, 0:1] * ybuf_ref[slot, 0]
    for k in range(1, TOP_K):
        ffn = ffn + gates[:, k:k + 1] * ybuf_ref[slot, k]
    y = _layer_norm_rows(DEEPNORM_ALPHA * h_ref[...] + ffn, g_ref[...], b_ref[...])
    hf_ref[...] = y
    hb_ref[...] = y.astype(hb_ref.dtype)


def _combine_ln(y, pos, gates, h, g, b):
    tb = COMBINE_TB
    row = lambda i, p: (i, 0)
    const = lambda i, p: (0, 0)
    return pl.pallas_call(
        _combine_ln_kernel,
        out_shape=(jax.ShapeDtypeStruct((N_TOK, D_MODEL), jnp.float32),
                   jax.ShapeDtypeStruct((N_TOK, D_MODEL), jnp.bfloat16)),
        grid_spec=pltpu.PrefetchScalarGridSpec(
            num_scalar_prefetch=1,
            grid=(N_TOK // tb,),
            in_specs=[pl.BlockSpec(memory_space=pl.ANY),
                      pl.BlockSpec((tb, LANES), row),
                      pl.BlockSpec((tb, D_MODEL), row),
                      pl.BlockSpec((1, D_MODEL), const), pl.BlockSpec((1, D_MODEL), const)],
            out_specs=(pl.BlockSpec((tb, D_MODEL), row), pl.BlockSpec((tb, D_MODEL), row)),
            scratch_shapes=[pltpu.VMEM((2, TOP_K, tb, D_MODEL), jnp.float32),
                            pltpu.SemaphoreType.DMA((2,))],
        ),
        compiler_params=_params("arbitrary"),
        name="moe_combine_ln",
    )(pos, y, gates, h, g.reshape(1, D_MODEL), b.reshape(1, D_MODEL))


def _moe(h, hb, layer, router_w, router_b, w_in, b_in, w_out, b_out, ln_g, ln_b):
    top_e, gates, rank, cnt = _router(h, router_w[layer], router_b[layer])
    pos, n_tiles, tile_e, nsub, pad_start, pad_n = _route_tables(top_e, rank, cnt)
    xs = _dispatch(hb, pos, pad_start, pad_n)
    y = _moe_experts(xs, w_in, b_in, w_out, b_out, layer, n_tiles, tile_e, nsub)
    return _combine_ln(y, pos, gates, h, ln_g[layer], ln_b[layer])


def kernel(x, mem, w_in_dil, w_in_ret, ret_gn_g, w_mem_kv, w_mix_out, ln_mix_g, ln_mix_b, router_w, router_b, moe_w_in, moe_b_in, moe_w_out, moe_b_out, ln_ffn_g, ln_ffn_b):
    bf16 = jnp.bfloat16
    h = x.reshape(N_TOK, D_MODEL)
    hb = h.astype(bf16)
    memb = mem.reshape(BATCH * MEM_LEN, D_MODEL).astype(bf16)
    for layer in range(DEPTH):
        slot = layer // 2
        if layer % 2 == 0:
            proj = _matmul(hb, w_in_dil, slot, bf16)
            self_out = _dilated_attention(proj)
        else:
            proj = _matmul(hb, w_in_ret, slot, bf16)
            self_out = _retention(proj, ret_gn_g[slot])
        memkv = _matmul(memb, w_mem_kv, layer, bf16)
        mem_out = _memory_attention(proj, memkv)
        h, hb = _mix_ln(self_out, mem_out, w_mix_out[layer].astype(bf16), h,
                        ln_mix_g[layer], ln_mix_b[layer])
        h, hb = _moe(h, hb, layer, router_w, router_b, moe_w_in, moe_b_in, moe_w_out, moe_b_out,
                     ln_ffn_g, ln_ffn_b)
    return h.reshape(BATCH, SEQ, D_MODEL)
```

```python
import functools
import math

import jax
import jax.numpy as jnp
from jax import lax
from jax.experimental import pallas as pl
from jax.experimental.pallas import tpu as pltpu

D_MODEL = 2048
BATCH = 2
SEQ = 4096
DEPTH = 2
HEAD_DIM = 128
DILATED_GROUPS = ((128, 1), (512, 4), (2048, 16))
HEADS_PER_GROUP = 4
N_SELF_HEADS = len(DILATED_GROUPS) * HEADS_PER_GROUP
N_RET_HEADS = 12
SELF_WIDTH = N_SELF_HEADS * HEAD_DIM
MEM_HEADS = 4
MEM_LEN = 256
MEM_WIDTH = MEM_HEADS * HEAD_DIM
RET_CHUNK = 128
N_EXPERTS = 32
TOP_K = 4
D_EXPERT = D_MODEL
SWIGLU_ALPHA = 1.702
SWIGLU_LIMIT = 7.0
DEEPNORM_ALPHA = (2 * DEPTH) ** 0.25
LN_EPS = 1e-5
NEG_INF = -1e30

N_TOK = BATCH * SEQ
GROUP_WIDTH = HEADS_PER_GROUP * HEAD_DIM
ATTN_STEPS = 128

LANES = 128
VMEM_LIMIT = 56 * 1024 * 1024

MM_TM = 1024
MM_TN = 512
ROW_TILE = 512
ROUTER_TM = 256

MOE_SUB = 128
MOE_TILE = 1152
MOE_SUBS = MOE_TILE // MOE_SUB
MOE_FAST_SUBS = (8, 9)
MOE_TF = 512
MOE_DOWN_TN = 512
MOE_VMEM_LIMIT = 60 * 1024 * 1024
MOE_TILES = (N_TOK * TOP_K) // MOE_TILE + N_EXPERTS
MOE_ROWS = MOE_TILES * MOE_TILE
DISPATCH_TB = 256
COMBINE_TB = 128


def _alibi_slopes(n):
    def pow2(m):
        start = 2.0 ** (-8.0 / m)
        return [start ** (i + 1) for i in range(m)]

    if math.log2(n).is_integer():
        s = pow2(n)
    else:
        c = 2 ** math.floor(math.log2(n))
        s = pow2(c) + pow2(2 * c)[0::2][: n - c]
    return sorted(s, reverse=True)


def _params(*sem, vmem_limit=VMEM_LIMIT):
    return pltpu.CompilerParams(dimension_semantics=sem, vmem_limit_bytes=vmem_limit)


def _layer_norm_rows(z, g, b):
    mu = jnp.mean(z, axis=-1, keepdims=True)
    zc = z - mu
    var = jnp.mean(zc * zc, axis=-1, keepdims=True)
    return zc * lax.rsqrt(var + LN_EPS) * g + b


def _dot_nt(a, b):
    return lax.dot_general(a, b, (((1,), (1,)), ((), ())), preferred_element_type=jnp.float32)


def _dot(a, b):
    return jnp.dot(a, b, preferred_element_type=jnp.float32)


def _mm_kernel(x_ref, w_ref, o_ref, wb_ref):
    @pl.when(pl.program_id(1) == 0)
    def _():
        wb_ref[...] = w_ref[...].astype(jnp.bfloat16)

    o_ref[...] = _dot(x_ref[...], wb_ref[...]).astype(o_ref.dtype)


def _matmul(x, w, layer, out_dtype):
    m, k = x.shape
    n = w.shape[2]
    tm = min(MM_TM, m)
    return pl.pallas_call(
        _mm_kernel,
        out_shape=jax.ShapeDtypeStruct((m, n), out_dtype),
        grid=(n // MM_TN, m // tm),
        in_specs=[pl.BlockSpec((tm, k), lambda j, i: (i, 0)),
                  pl.BlockSpec((None, k, MM_TN), lambda j, i: (layer, 0, j))],
        out_specs=pl.BlockSpec((tm, MM_TN), lambda j, i: (i, j)),
        scratch_shapes=[pltpu.VMEM((k, MM_TN), jnp.bfloat16)],
        compiler_params=_params("parallel", "arbitrary"),
        name="dense_matmul",
    )(x, w)


def _dil_attn_kernel(q_ref, kp_ref, kc_ref, vp_ref, vc_ref, o_ref, lse_ref, *, slopes, dilation):
    n = pl.program_id(2)
    heads = HEADS_PER_GROUP
    rows = heads * ATTN_STEPS
    head_slices = [slice(h * HEAD_DIM, (h + 1) * HEAD_DIM) for h in range(heads)]
    row = lax.broadcasted_iota(jnp.int32, (rows, ATTN_STEPS), 0)
    kj = lax.broadcasted_iota(jnp.int32, (rows, ATTN_STEPS), 1)
    diff = (row % ATTN_STEPS) - kj
    valid_c = diff >= 0
    valid_p = jnp.logical_and(diff <= 0, n > 0)
    slope = jnp.full((rows, ATTN_STEPS), slopes[0], jnp.float32)
    for h in range(1, heads):
        slope = jnp.where(row >= h * ATTN_STEPS, slopes[h], slope)
    bias_c = slope * (diff * dilation).astype(jnp.float32)
    bias_p = slope * ((diff + ATTN_STEPS) * dilation).astype(jnp.float32)
    scale = HEAD_DIM ** -0.5
    s_c = jnp.concatenate([_dot_nt(q_ref[:, sl], kc_ref[:, sl]) for sl in head_slices], axis=0)
    s_p = jnp.concatenate([_dot_nt(q_ref[:, sl], kp_ref[:, sl]) for sl in head_slices], axis=0)
    s_c = jnp.where(valid_c, s_c * scale - bias_c, NEG_INF)
    s_p = jnp.where(valid_p, s_p * scale - bias_p, NEG_INF)
    m = jnp.max(jnp.maximum(s_c, s_p), axis=-1, keepdims=True)
    e_c = jnp.exp(s_c - m)
    e_p = jnp.exp(s_p - m)
    l = jnp.sum(e_c + e_p, axis=-1, keepdims=True)
    inv_l = 1.0 / l
    p_c = (e_c * inv_l).astype(jnp.bfloat16)
    p_p = (e_p * inv_l).astype(jnp.bfloat16)
    lse = m + jnp.log(l)
    for h, sl in enumerate(head_slices):
        hr = slice(h * ATTN_STEPS, (h + 1) * ATTN_STEPS)
        o_ref[:, sl] = _dot(p_c[hr], vc_ref[:, sl]) + _dot(p_p[hr], vp_ref[:, sl])
        lse_ref[:, sl] = jnp.broadcast_to(lse[hr], (ATTN_STEPS, HEAD_DIM))


def _dilated_group(proj, group):
    _, dilation = DILATED_GROUPS[group]
    length = SEQ // dilation
    nb = length // ATTN_STEPS
    kb = SELF_WIDTH // GROUP_WIDTH
    if dilation == 1:
        src, first, stride = proj, group, kb
    else:
        src = jnp.concatenate([proj[:, s * SELF_WIDTH + group * GROUP_WIDTH:][:, :GROUP_WIDTH]
                               for s in range(3)], axis=1)
        first, stride = 0, 1
    c = src.shape[1]
    cb = c // GROUP_WIDTH
    view = src.reshape(BATCH, length, dilation * c)
    blk = (None, ATTN_STEPS, GROUP_WIDTH)

    def col(section):
        return lambda b, r, n: (b, n, r * cb + section * stride + first)

    def col_prev(section):
        return lambda b, r, n: (b, jnp.maximum(n - 1, 0), r * cb + section * stride + first)

    slopes = tuple(_alibi_slopes(N_SELF_HEADS)[group * HEADS_PER_GROUP:(group + 1) * HEADS_PER_GROUP])
    out_shape = jax.ShapeDtypeStruct((BATCH, length, dilation * GROUP_WIDTH), jnp.float32)
    out_spec = pl.BlockSpec(blk, lambda b, r, n: (b, n, r))
    o, lse = pl.pallas_call(
        functools.partial(_dil_attn_kernel, slopes=slopes, dilation=dilation),
        out_shape=(out_shape, out_shape),
        grid=(BATCH, dilation, nb),
        in_specs=[pl.BlockSpec(blk, col(0)),
                  pl.BlockSpec(blk, col_prev(1)), pl.BlockSpec(blk, col(1)),
                  pl.BlockSpec(blk, col_prev(2)), pl.BlockSpec(blk, col(2))],
        out_specs=(out_spec, out_spec),
        compiler_params=_params("parallel", "parallel", "arbitrary"),
        name=f"dilated_attn_g{group}",
    )(view, view, view, view, view)
    return o.reshape(N_TOK, GROUP_WIDTH), lse.reshape(N_TOK, GROUP_WIDTH)


def _dil_combine_kernel(o0_ref, o1_ref, o2_ref, l0_ref, l1_ref, l2_ref, out_ref):
    l0, l1, l2 = l0_ref[...], l1_ref[...], l2_ref[...]
    m = jnp.maximum(jnp.maximum(l0, l1), l2)
    e0, e1, e2 = jnp.exp(l0 - m), jnp.exp(l1 - m), jnp.exp(l2 - m)
    inv = 1.0 / (e0 + e1 + e2)
    out_ref[:, 0 * GROUP_WIDTH:1 * GROUP_WIDTH] = (o0_ref[...] * (e0 * inv)).astype(out_ref.dtype)
    out_ref[:, 1 * GROUP_WIDTH:2 * GROUP_WIDTH] = (o1_ref[...] * (e1 * inv)).astype(out_ref.dtype)
    out_ref[:, 2 * GROUP_WIDTH:3 * GROUP_WIDTH] = (o2_ref[...] * (e2 * inv)).astype(out_ref.dtype)


def _dilated_attention(proj):
    outs, lses = zip(*[_dilated_group(proj, g) for g in range(len(DILATED_GROUPS))])
    spec = pl.BlockSpec((ROW_TILE, GROUP_WIDTH), lambda i: (i, 0))
    return pl.pallas_call(
        _dil_combine_kernel,
        out_shape=jax.ShapeDtypeStruct((N_TOK, SELF_WIDTH), jnp.bfloat16),
        grid=(N_TOK // ROW_TILE,),
        in_specs=[spec] * 6,
        out_specs=pl.BlockSpec((ROW_TILE, SELF_WIDTH), lambda i: (i, 0)),
        compiler_params=_params("parallel"),
        name="dilated_combine",
    )(*outs, *lses)


def _retention_kernel(q_ref, k_ref, v_ref, g_ref, dmat_ref, kdec_ref, qdec_ref, cdec_ref, gn_ref,
                      o_ref, state_ref):
    @pl.when(pl.program_id(1) == 0)
    def _():
        state_ref[...] = jnp.zeros_like(state_ref)

    heads = range(N_RET_HEADS)
    cols = [slice(h * HEAD_DIM, (h + 1) * HEAD_DIM) for h in heads]
    bf16 = jnp.bfloat16
    scores = [_dot_nt(q_ref[:, c], k_ref[:, c]) * dmat_ref[h] for h, c in zip(heads, cols)]
    states = [state_ref[h] for h in heads]
    cross = [_dot(q_ref[:, c], states[h].astype(bf16)) * qdec_ref[:, c] for h, c in zip(heads, cols)]
    intra = [_dot(scores[h].astype(bf16), v_ref[:, c]) for h, c in zip(heads, cols)]
    kw = [(k_ref[:, c].astype(jnp.float32) * kdec_ref[:, c]).T.astype(bf16) for c in cols]
    for h, c in zip(heads, cols):
        state_ref[h] = states[h] * cdec_ref[h] + _dot(kw[h], v_ref[:, c])
    r = jnp.concatenate([intra[h] + cross[h] for h in heads], axis=0)
    mu = jnp.mean(r, axis=-1, keepdims=True)
    rc = r - mu
    var = jnp.mean(rc * rc, axis=-1, keepdims=True)
    rn = rc * lax.rsqrt(var + LN_EPS)
    for h, c in zip(heads, cols):
        gate = g_ref[:, c].astype(jnp.float32)
        normed = rn[h * RET_CHUNK:(h + 1) * RET_CHUNK] * gn_ref[:, c]
        o_ref[:, c] = (gate * (1.0 / (1.0 + jnp.exp(-gate))) * normed).astype(o_ref.dtype)


def _retention(proj, gn_gain):
    c = RET_CHUNK
    log_gamma = jnp.log1p(-jnp.exp2(-(5.0 + jnp.arange(N_RET_HEADS, dtype=jnp.float32))))
    idx = jnp.arange(c, dtype=jnp.float32)
    diff = idx[:, None] - idx[None, :]
    decay = jnp.where(diff >= 0, jnp.exp(jnp.maximum(diff, 0.0)[None] * log_gamma[:, None, None]), 0.0)
    scale = HEAD_DIM ** -0.5
    dmat = decay * scale
    k_decay = jnp.exp((c - 1 - idx)[:, None] * log_gamma[None, :]) * scale
    q_decay = jnp.exp((idx + 1.0)[:, None] * log_gamma[None, :])
    kdec = jnp.repeat(k_decay, HEAD_DIM, axis=1)
    qdec = jnp.repeat(q_decay, HEAD_DIM, axis=1)
    cdec = jnp.broadcast_to(jnp.exp(c * log_gamma)[:, None, None], (N_RET_HEADS, 1, HEAD_DIM))
    gn = gn_gain.reshape(1, SELF_WIDTH).astype(jnp.float32)
    nchunk = SEQ // c
    blk = (c, SELF_WIDTH)

    def section(s):
        return pl.BlockSpec(blk, lambda b, n: (b * nchunk + n, s))

    const2 = lambda b, n: (0, 0)
    const3 = lambda b, n: (0, 0, 0)
    return pl.pallas_call(
        _retention_kernel,
        out_shape=jax.ShapeDtypeStruct((N_TOK, SELF_WIDTH), jnp.bfloat16),
        grid=(BATCH, nchunk),
        in_specs=[section(0), section(1), section(2), section(3),
                  pl.BlockSpec((N_RET_HEADS, c, c), const3),
                  pl.BlockSpec(blk, const2), pl.BlockSpec(blk, const2),
                  pl.BlockSpec((N_RET_HEADS, 1, HEAD_DIM), const3),
                  pl.BlockSpec((1, SELF_WIDTH), const2)],
        out_specs=pl.BlockSpec(blk, lambda b, n: (b * nchunk + n, 0)),
        scratch_shapes=[pltpu.VMEM((N_RET_HEADS, HEAD_DIM, HEAD_DIM), jnp.float32)],
        compiler_params=_params("parallel", "arbitrary"),
        name="retention",
    )(proj, proj, proj, proj, dmat, kdec, qdec, cdec, gn)


def _mem_attn_kernel(q_ref, k_ref, v_ref, o_ref):
    scale = HEAD_DIM ** -0.5
    for h in range(MEM_HEADS):
        sl = slice(h * HEAD_DIM, (h + 1) * HEAD_DIM)
        s = _dot_nt(q_ref[:, sl], k_ref[:, sl]) * scale
        e = jnp.exp(s - jnp.max(s, axis=-1, keepdims=True))
        p = e * (1.0 / jnp.sum(e, axis=-1, keepdims=True))
        o_ref[:, sl] = _dot(p.astype(jnp.bfloat16), v_ref[:, sl]).astype(o_ref.dtype)


def _memory_attention(proj, memkv):
    qcol = proj.shape[1] // MEM_WIDTH - 1
    per_b = SEQ // ROW_TILE
    return pl.pallas_call(
        _mem_attn_kernel,
        out_shape=jax.ShapeDtypeStruct((N_TOK, MEM_WIDTH), jnp.bfloat16),
        grid=(BATCH, per_b),
        in_specs=[pl.BlockSpec((ROW_TILE, MEM_WIDTH), lambda b, i: (b * per_b + i, qcol)),
                  pl.BlockSpec((MEM_LEN, MEM_WIDTH), lambda b, i: (b, 0)),
                  pl.BlockSpec((MEM_LEN, MEM_WIDTH), lambda b, i: (b, 1))],
        out_specs=pl.BlockSpec((ROW_TILE, MEM_WIDTH), lambda b, i: (b * per_b + i, 0)),
        compiler_params=_params("parallel", "parallel"),
        name="memory_attn",
    )(proj, memkv, memkv)


def _mix_ln_kernel(so_ref, mo_ref, wt_ref, wb_ref, h_ref, g_ref, b_ref, hf_ref, hb_ref):
    mix = _dot(so_ref[...], wt_ref[...]) + _dot(mo_ref[...], wb_ref[...])
    y = _layer_norm_rows(DEEPNORM_ALPHA * h_ref[...] + mix, g_ref[...], b_ref[...])
    hf_ref[...] = y
    hb_ref[...] = y.astype(hb_ref.dtype)


def _mix_ln(self_out, mem_out, w_mix, h, g, b):
    row = lambda i: (i, 0)
    const = lambda i: (0, 0)
    return pl.pallas_call(
        _mix_ln_kernel,
        out_shape=(jax.ShapeDtypeStruct((N_TOK, D_MODEL), jnp.float32),
                   jax.ShapeDtypeStruct((N_TOK, D_MODEL), jnp.bfloat16)),
        grid=(N_TOK // ROW_TILE,),
        in_specs=[pl.BlockSpec((ROW_TILE, SELF_WIDTH), row),
                  pl.BlockSpec((ROW_TILE, MEM_WIDTH), row),
                  pl.BlockSpec((SELF_WIDTH, D_MODEL), const),
                  pl.BlockSpec((MEM_WIDTH, D_MODEL), lambda i: (SELF_WIDTH // MEM_WIDTH, 0)),
                  pl.BlockSpec((ROW_TILE, D_MODEL), row),
                  pl.BlockSpec((1, D_MODEL), const), pl.BlockSpec((1, D_MODEL), const)],
        out_specs=(pl.BlockSpec((ROW_TILE, D_MODEL), row), pl.BlockSpec((ROW_TILE, D_MODEL), row)),
        compiler_params=_params("parallel"),
        name="mix_ln",
    )(self_out, mem_out, w_mix, w_mix, h, g.reshape(1, D_MODEL), b.reshape(1, D_MODEL))


def _router_kernel(h_ref, whi_ref, wlo_ref, b_ref, e_ref, g_ref, r_ref, cnt_ref, run_ref):
    tm = ROUTER_TM

    @pl.when(pl.program_id(0) == 0)
    def _():
        run_ref[...] = jnp.zeros_like(run_ref)

    h = h_ref[...]
    h_hi = h.astype(jnp.bfloat16)
    h_lo = (h - h_hi.astype(jnp.float32)).astype(jnp.bfloat16)
    logits = (_dot(h_hi, whi_ref[...]) + _dot(h_lo, whi_ref[...]) + _dot(h_hi, wlo_ref[...])) + b_ref[...]
    lane = lax.broadcasted_iota(jnp.int32, (tm, LANES), 1).astype(jnp.float32)
    work = logits
    vals, idxs, hots = [], [], []
    for _ in range(TOP_K):
        m = jnp.max(work, axis=-1, keepdims=True)
        idx = jnp.min(jnp.where(work == m, lane, float(LANES)), axis=-1, keepdims=True)
        hot = lane == idx
        vals.append(m)
        idxs.append(idx)
        hots.append(hot)
        work = jnp.where(hot, -jnp.inf, work)
    exps = [jnp.exp(v - vals[0]) for v in vals]
    inv = 1.0 / (exps[0] + exps[1] + exps[2] + exps[3])
    cnt = jnp.zeros((tm, LANES), jnp.float32)
    for hot in hots:
        cnt = cnt + hot.astype(jnp.float32)
    row = lax.broadcasted_iota(jnp.int32, (tm, tm), 0)
    colm = lax.broadcasted_iota(jnp.int32, (tm, tm), 1)
    tri = (row > colm).astype(jnp.bfloat16)
    before = run_ref[...] + _dot(tri, cnt.astype(jnp.bfloat16))
    lane_i = lax.broadcasted_iota(jnp.int32, (tm, LANES), 1)
    e_out = jnp.zeros((tm, LANES), jnp.float32)
    g_out = jnp.zeros((tm, LANES), jnp.float32)
    r_out = jnp.zeros((tm, LANES), jnp.float32)
    for k in range(TOP_K):
        rank = jnp.sum(jnp.where(hots[k], before, 0.0), axis=-1, keepdims=True)
        e_out = jnp.where(lane_i == k, idxs[k], e_out)
        g_out = jnp.where(lane_i == k, exps[k] * inv, g_out)
        r_out = jnp.where(lane_i == k, rank, r_out)
    e_ref[...] = e_out.astype(jnp.int32)
    g_ref[...] = g_out
    r_ref[...] = r_out.astype(jnp.int32)
    run_ref[...] = run_ref[...] + jnp.sum(cnt, axis=0, keepdims=True)
    cnt_ref[...] = run_ref[...].astype(jnp.int32)


def _router(h, router_w, router_b):
    w = jnp.zeros((D_MODEL, LANES), jnp.float32).at[:, :N_EXPERTS].set(router_w)
    w_hi = w.astype(jnp.bfloat16)
    w_lo = (w - w_hi.astype(jnp.float32)).astype(jnp.bfloat16)
    b = jnp.full((1, LANES), NEG_INF, jnp.float32).at[0, :N_EXPERTS].set(router_b)
    tm = ROUTER_TM
    row = lambda i: (i, 0)
    const = lambda i: (0, 0)
    lanes_out = pl.BlockSpec((tm, LANES), row)
    e, g, r, cnt = pl.pallas_call(
        _router_kernel,
        out_shape=(jax.ShapeDtypeStruct((N_TOK, LANES), jnp.int32),
                   jax.ShapeDtypeStruct((N_TOK, LANES), jnp.float32),
                   jax.ShapeDtypeStruct((N_TOK, LANES), jnp.int32),
                   jax.ShapeDtypeStruct((1, LANES), jnp.int32)),
        grid=(N_TOK // tm,),
        in_specs=[pl.BlockSpec((tm, D_MODEL), row), pl.BlockSpec((D_MODEL, LANES), const),
                  pl.BlockSpec((D_MODEL, LANES), const), pl.BlockSpec((1, LANES), const)],
        out_specs=(lanes_out, lanes_out, lanes_out, pl.BlockSpec((1, LANES), const)),
        scratch_shapes=[pltpu.VMEM((1, LANES), jnp.float32)],
        compiler_params=_params("arbitrary"),
        name="router",
    )(h, w_hi, w_lo, b)
    return e[:, :TOP_K], g, r[:, :TOP_K], cnt[0, :N_EXPERTS]


def _route_tables(top_e, rank, cnt):
    ntile = (cnt + MOE_TILE - 1) // MOE_TILE
    tile_end = jnp.cumsum(ntile)
    tile_base = tile_end - ntile
    flat_e, flat_rank = top_e.reshape(-1), rank.reshape(-1)
    pos = (tile_base[flat_e] * MOE_TILE + flat_rank).astype(jnp.int32)
    n_tiles = tile_end[-1:].astype(jnp.int32)
    w = jnp.minimum(jnp.arange(MOE_TILES, dtype=jnp.int32), n_tiles[0] - 1)
    tile_e = jnp.minimum(jnp.searchsorted(tile_end, w, side="right"), N_EXPERTS - 1).astype(jnp.int32)
    rows = jnp.clip(cnt[tile_e] - (w - tile_base[tile_e]) * MOE_TILE, 0, MOE_TILE)
    nsub = ((rows + MOE_SUB - 1) // MOE_SUB).astype(jnp.int32)
    pad_start = (tile_base * MOE_TILE + cnt).astype(jnp.int32)
    pad_n = ((-cnt) % MOE_SUB).astype(jnp.int32)
    return pos, n_tiles, tile_e, nsub, pad_start, pad_n


def _row_copy(src_vmem, src_row, dst_hbm, dst_row, sem):
    return pltpu.make_async_copy(src_vmem.at[pl.ds(src_row, 1)], dst_hbm.at[pl.ds(dst_row, 1)], sem)


def _pack_bf16_pairs(x):
    half = x.shape[1] // 2
    lo = lax.bitcast_convert_type(x[:, :half].astype(jnp.float32), jnp.uint32)
    hi = lax.bitcast_convert_type(x[:, half:].astype(jnp.float32), jnp.uint32)
    return (lo >> 16) | (hi & jnp.uint32(0xFFFF0000))


def _unpack_bf16_pairs(words):
    lo = lax.bitcast_convert_type(words << 16, jnp.float32)
    hi = lax.bitcast_convert_type(words & jnp.uint32(0xFFFF0000), jnp.float32)
    return lo.astype(jnp.bfloat16), hi.astype(jnp.bfloat16)


def _dispatch_kernel(pos_ref, pad_start_ref, pad_n_ref, hb_ref, xs_hbm, h_ref, sem):
    step = pl.program_id(0)
    base = step * DISPATCH_TB
    h_ref[...] = _pack_bf16_pairs(hb_ref[...])

    def issue(r, carry):
        for k in range(TOP_K):
            _row_copy(h_ref, r, xs_hbm, pos_ref[(base + r) * TOP_K + k], sem).start()
        return carry

    lax.fori_loop(0, DISPATCH_TB, issue, 0, unroll=8)

    @pl.when(step == 0)
    def _():
        def per_expert(e, carry):
            start = pad_start_ref[e]
            n = pad_n_ref[e]

            def fill(i, c):
                _row_copy(h_ref, 0, xs_hbm, start + i, sem).start()
                return c

            def drain(_, c):
                _row_copy(h_ref, 0, xs_hbm, 0, sem).wait()
                return c

            lax.fori_loop(0, n, fill, 0)
            lax.fori_loop(0, n, drain, 0)
            return carry

        lax.fori_loop(0, N_EXPERTS, per_expert, 0)

    for _ in range(TOP_K):
        pltpu.make_async_copy(h_ref, xs_hbm.at[pl.ds(0, DISPATCH_TB)], sem).wait()


def _dispatch(hb, pos, pad_start, pad_n):
    return pl.pallas_call(
        _dispatch_kernel,
        out_shape=jax.ShapeDtypeStruct((MOE_ROWS, D_MODEL // 2), jnp.uint32),
        grid_spec=pltpu.PrefetchScalarGridSpec(
            num_scalar_prefetch=3,
            grid=(N_TOK // DISPATCH_TB,),
            in_specs=[pl.BlockSpec((DISPATCH_TB, D_MODEL), lambda i, p, s, n: (i, 0))],
            out_specs=pl.BlockSpec(memory_space=pl.ANY),
            scratch_shapes=[pltpu.VMEM((DISPATCH_TB, D_MODEL // 2), jnp.uint32),
                            pltpu.SemaphoreType.DMA(())],
        ),
        compiler_params=_params("arbitrary"),
        name="moe_dispatch",
    )(pos, pad_start, pad_n, hb)


def _moe_weight_map(layer, first, hidden_axis):
    def index_map(w, j, tile_e, nsub):
        return (layer, tile_e[w], first + j, 0) if hidden_axis == 2 else (layer, tile_e[w], 0, first + j)

    return index_map


def _moe_rows(nsub, accumulate):
    fast = nsub < 0
    for s in MOE_FAST_SUBS:
        fast = jnp.logical_or(fast, nsub == s)
        pl.when(nsub == s)(functools.partial(accumulate, slice(0, s * MOE_SUB)))

    @pl.when(jnp.logical_and(nsub > 0, jnp.logical_not(fast)))
    def _():
        def one(i, carry):
            accumulate(pl.ds(pl.multiple_of(i * MOE_SUB, MOE_SUB), MOE_SUB))
            return carry

        lax.fori_loop(0, nsub, one, 0)


def _moe_expert_kernel(exp_ref, nsub_ref, x_ref, wg_ref, wl_ref, bg_ref, bl_ref, wo_ref, bo_ref, o_ref):
    first = pl.program_id(1) == 0
    nsub = nsub_ref[pl.program_id(0)]
    half = D_MODEL // 2

    for i in range(MOE_SUBS):
        rows = slice(i * MOE_SUB, (i + 1) * MOE_SUB)

        @pl.when(jnp.logical_and(i < nsub, first))
        def _():
            o_ref[rows, :] = jnp.broadcast_to(bo_ref[...], (MOE_SUB, D_MODEL))

        @pl.when(jnp.logical_and(i >= nsub, first))
        def _():
            o_ref[rows, :] = jnp.zeros((MOE_SUB, D_MODEL), o_ref.dtype)

    def up(lo, hi, w_ref, b_ref):
        return (_dot(lo, w_ref[0:half, :].astype(jnp.bfloat16))
                + _dot(hi, w_ref[half:D_MODEL, :].astype(jnp.bfloat16)) + b_ref[...])

    def accumulate(rows):
        lo, hi = _unpack_bf16_pairs(x_ref[rows, :])
        gate = up(lo, hi, wg_ref, bg_ref)
        lin = up(lo, hi, wl_ref, bl_ref)
        gate = jnp.minimum(gate, SWIGLU_LIMIT)
        lin = jnp.clip(lin, -SWIGLU_LIMIT, SWIGLU_LIMIT)
        act = (gate * (1.0 / (1.0 + jnp.exp(-SWIGLU_ALPHA * gate))) * (lin + 1.0)).astype(jnp.bfloat16)
        for c in range(D_MODEL // MOE_DOWN_TN):
            cols = slice(c * MOE_DOWN_TN, (c + 1) * MOE_DOWN_TN)
            o_ref[rows, cols] = o_ref[rows, cols] + _dot(act, wo_ref[:, cols].astype(jnp.bfloat16))

    _moe_rows(nsub, accumulate)


def _moe_experts(xs, w_in, b_in, w_out, b_out, layer, n_tiles, tile_e, nsub):
    nj = D_EXPERT // MOE_TF
    b_in4 = b_in.reshape(DEPTH, N_EXPERTS, 1, 2 * D_EXPERT)
    b_out4 = b_out.reshape(DEPTH, N_EXPERTS, 1, D_MODEL)
    wblk = (None, None, D_MODEL, MOE_TF)
    bblk = (None, None, 1, MOE_TF)
    tile_map = lambda w, j, e, n: (w, 0)
    return pl.pallas_call(
        _moe_expert_kernel,
        out_shape=jax.ShapeDtypeStruct((MOE_ROWS, D_MODEL), jnp.float32),
        grid_spec=pltpu.PrefetchScalarGridSpec(
            num_scalar_prefetch=2,
            grid=(n_tiles[0], nj),
            in_specs=[pl.BlockSpec((MOE_TILE, D_MODEL // 2), tile_map),
                      pl.BlockSpec(wblk, _moe_weight_map(layer, 0, 3)),
                      pl.BlockSpec(wblk, _moe_weight_map(layer, nj, 3)),
                      pl.BlockSpec(bblk, _moe_weight_map(layer, 0, 3)),
                      pl.BlockSpec(bblk, _moe_weight_map(layer, nj, 3)),
                      pl.BlockSpec((None, None, MOE_TF, D_MODEL), _moe_weight_map(layer, 0, 2)),
                      pl.BlockSpec((None, None, 1, D_MODEL), lambda w, j, e, n: (layer, e[w], 0, 0))],
            out_specs=pl.BlockSpec((MOE_TILE, D_MODEL), tile_map),
        ),
        compiler_params=_params("arbitrary", "arbitrary", vmem_limit=MOE_VMEM_LIMIT),
        name="moe_experts",
    )(tile_e, nsub, xs, w_in, w_in, b_in4, b_in4, w_out, b_out4)


def _combine_ln_kernel(pos_ref, y_hbm, gates_ref, h_ref, g_ref, b_ref, hf_ref, hb_ref, ybuf_ref, sems):
    step = pl.program_id(0)
    slot = step % 2

    def gather(tile, into):
        base = tile * COMBINE_TB

        def issue(r, carry):
            for k in range(TOP_K):
                src = y_hbm.at[pl.ds(pos_ref[(base + r) * TOP_K + k], 1)]
                pltpu.make_async_copy(src, ybuf_ref.at[into, k, pl.ds(r, 1)], sems.at[into]).start()
            return carry

        lax.fori_loop(0, COMBINE_TB, issue, 0, unroll=8)

    @pl.when(step == 0)
    def _():
        gather(0, 0)

    @pl.when(step + 1 < pl.num_programs(0))
    def _():
        gather(step + 1, 1 - slot)

    for k in range(TOP_K):
        pltpu.make_async_copy(y_hbm.at[pl.ds(0, COMBINE_TB)], ybuf_ref.at[slot, k], sems.at[slot]).wait()

    gates = gates_ref[...]
    ffn = gates[:, 0:1] * ybuf_ref[slot, 0]
    for k in range(1, TOP_K):
        ffn = ffn + gates[:, k:k + 1] * ybuf_ref[slot, k]
    y = _layer_norm_rows(DEEPNORM_ALPHA * h_ref[...] + ffn, g_ref[...], b_ref[...])
    hf_ref[...] = y
    hb_ref[...] = y.astype(hb_ref.dtype)


def _combine_ln(y, pos, gates, h, g, b):
    tb = COMBINE_TB
    row = lambda i, p: (i, 0)
    const = lambda i, p: (0, 0)
    return pl.pallas_call(
        _combine_ln_kernel,
        out_shape=(jax.ShapeDtypeStruct((N_TOK, D_MODEL), jnp.float32),
                   jax.ShapeDtypeStruct((N_TOK, D_MODEL), jnp.bfloat16)),
        grid_spec=pltpu.PrefetchScalarGridSpec(
            num_scalar_prefetch=1,
            grid=(N_TOK // tb,),
            in_specs=[pl.BlockSpec(memory_space=pl.ANY),
                      pl.BlockSpec((tb, LANES), row),
                      pl.BlockSpec((tb, D_MODEL), row),
                      pl.BlockSpec((1, D_MODEL), const), pl.BlockSpec((1, D_MODEL), const)],
            out_specs=(pl.BlockSpec((tb, D_MODEL), row), pl.BlockSpec((tb, D_MODEL), row)),
            scratch_shapes=[pltpu.VMEM((2, TOP_K, tb, D_MODEL), jnp.float32),
                            pltpu.SemaphoreType.DMA((2,))],
        ),
        compiler_params=_params("arbitrary"),
        name="moe_combine_ln",
    )(pos, y, gates, h, g.reshape(1, D_MODEL), b.reshape(1, D_MODEL))


def _moe(h, hb, layer, router_w, router_b, w_in, b_in, w_out, b_out, ln_g, ln_b):
    top_e, gates, rank, cnt = _router(h, router_w[layer], router_b[layer])
    pos, n_tiles, tile_e, nsub, pad_start, pad_n = _route_tables(top_e, rank, cnt)
    xs = _dispatch(hb, pos, pad_start, pad_n)
    y = _moe_experts(xs, w_in, b_in, w_out, b_out, layer, n_tiles, tile_e, nsub)
    return _combine_ln(y, pos, gates, h, ln_g[layer], ln_b[layer])


def kernel(x, mem, w_in_dil, w_in_ret, ret_gn_g, w_mem_kv, w_mix_out, ln_mix_g, ln_mix_b, router_w, router_b, moe_w_in, moe_b_in, moe_w_out, moe_b_out, ln_ffn_g, ln_ffn_b):
    bf16 = jnp.bfloat16
    h = x.reshape(N_TOK, D_MODEL)
    hb = h.astype(bf16)
    memb = mem.reshape(BATCH * MEM_LEN, D_MODEL).astype(bf16)
    for layer in range(DEPTH):
        slot = layer // 2
        if layer % 2 == 0:
            proj = _matmul(hb, w_in_dil, slot, bf16)
            self_out = _dilated_attention(proj)
        else:
            proj = _matmul(hb, w_in_ret, slot, bf16)
            self_out = _retention(proj, ret_gn_g[slot])
        memkv = _matmul(memb, w_mem_kv, layer, bf16)
        mem_out = _memory_attention(proj, memkv)
        h, hb = _mix_ln(self_out, mem_out, w_mix_out[layer].astype(bf16), h,
                        ln_mix_g[layer], ln_mix_b[layer])
        h, hb = _moe(h, hb, layer, router_w, router_b, moe_w_in, moe_b_in, moe_w_out, moe_b_out,
                     ln_ffn_g, ln_ffn_b)
    return h.reshape(BATCH, SEQ, D_MODEL)
```

```python
import functools
import math

import jax
import jax.numpy as jnp
from jax import lax
from jax.experimental import pallas as pl
from jax.experimental.pallas import tpu as pltpu

D_MODEL = 2048
BATCH = 2
SEQ = 4096
DEPTH = 2
HEAD_DIM = 128
DILATED_GROUPS = ((128, 1), (512, 4), (2048, 16))
HEADS_PER_GROUP = 4
N_SELF_HEADS = len(DILATED_GROUPS) * HEADS_PER_GROUP
N_RET_HEADS = 12
SELF_WIDTH = N_SELF_HEADS * HEAD_DIM
MEM_HEADS = 4
MEM_LEN = 256
MEM_WIDTH = MEM_HEADS * HEAD_DIM
RET_CHUNK = 128
N_EXPERTS = 32
TOP_K = 4
D_EXPERT = D_MODEL
SWIGLU_ALPHA = 1.702
SWIGLU_LIMIT = 7.0
DEEPNORM_ALPHA = (2 * DEPTH) ** 0.25
LN_EPS = 1e-5
NEG_INF = -1e30

N_TOK = BATCH * SEQ
GROUP_WIDTH = HEADS_PER_GROUP * HEAD_DIM
ATTN_STEPS = 128

LANES = 128
VMEM_LIMIT = 56 * 1024 * 1024

MM_TM = 1024
MM_TN = 512
ROW_TILE = 512
ROUTER_TM = 256

MOE_SUB = 64
MOE_TILE = 1152
MOE_SUBS = MOE_TILE // MOE_SUB
MOE_FAST_SUBS = (15, 16, 17, 18)
MOE_TF = 512
MOE_DOWN_TN = 512
MOE_VMEM_LIMIT = 60 * 1024 * 1024
MOE_TILES = (N_TOK * TOP_K) // MOE_TILE + N_EXPERTS
MOE_ROWS = MOE_TILES * MOE_TILE
DISPATCH_TB = 256
COMBINE_TB = 128


def _alibi_slopes(n):
    def pow2(m):
        start = 2.0 ** (-8.0 / m)
        return [start ** (i + 1) for i in range(m)]

    if math.log2(n).is_integer():
        s = pow2(n)
    else:
        c = 2 ** math.floor(math.log2(n))
        s = pow2(c) + pow2(2 * c)[0::2][: n - c]
    return sorted(s, reverse=True)


def _params(*sem, vmem_limit=VMEM_LIMIT):
    return pltpu.CompilerParams(dimension_semantics=sem, vmem_limit_bytes=vmem_limit)


def _layer_norm_rows(z, g, b):
    mu = jnp.mean(z, axis=-1, keepdims=True)
    zc = z - mu
    var = jnp.mean(zc * zc, axis=-1, keepdims=True)
    return zc * lax.rsqrt(var + LN_EPS) * g + b


def _dot_nt(a, b):
    return lax.dot_general(a, b, (((1,), (1,)), ((), ())), preferred_element_type=jnp.float32)


def _dot(a, b):
    return jnp.dot(a, b, preferred_element_type=jnp.float32)


def _mm_kernel(x_ref, w_ref, o_ref):
    o_ref[...] = _dot(x_ref[...].astype(jnp.bfloat16), w_ref[...].astype(jnp.bfloat16)).astype(o_ref.dtype)


def _matmul(x, w, layer, out_dtype):
    m, k = x.shape
    n = w.shape[2]
    tm = min(MM_TM, m)
    return pl.pallas_call(
        _mm_kernel,
        out_shape=jax.ShapeDtypeStruct((m, n), out_dtype),
        grid=(m // tm, n // MM_TN),
        in_specs=[pl.BlockSpec((tm, k), lambda i, j: (i, 0)),
                  pl.BlockSpec((None, k, MM_TN), lambda i, j: (layer, 0, j))],
        out_specs=pl.BlockSpec((tm, MM_TN), lambda i, j: (i, j)),
        compiler_params=_params("parallel", "parallel"),
        name="dense_matmul",
    )(x, w)


def _dil_attn_kernel(q_ref, kp_ref, kc_ref, vp_ref, vc_ref, o_ref, lse_ref, *, slopes, dilation):
    n = pl.program_id(2)
    heads = HEADS_PER_GROUP
    rows = heads * ATTN_STEPS
    head_slices = [slice(h * HEAD_DIM, (h + 1) * HEAD_DIM) for h in range(heads)]
    row = lax.broadcasted_iota(jnp.int32, (rows, ATTN_STEPS), 0)
    kj = lax.broadcasted_iota(jnp.int32, (rows, ATTN_STEPS), 1)
    diff = (row % ATTN_STEPS) - kj
    valid_c = diff >= 0
    valid_p = jnp.logical_and(diff <= 0, n > 0)
    slope = jnp.full((rows, ATTN_STEPS), slopes[0], jnp.float32)
    for h in range(1, heads):
        slope = jnp.where(row >= h * ATTN_STEPS, slopes[h], slope)
    bias_c = slope * (diff * dilation).astype(jnp.float32)
    bias_p = slope * ((diff + ATTN_STEPS) * dilation).astype(jnp.float32)
    scale = HEAD_DIM ** -0.5
    s_c = jnp.concatenate([_dot_nt(q_ref[:, sl], kc_ref[:, sl]) for sl in head_slices], axis=0)
    s_p = jnp.concatenate([_dot_nt(q_ref[:, sl], kp_ref[:, sl]) for sl in head_slices], axis=0)
    s_c = jnp.where(valid_c, s_c * scale - bias_c, NEG_INF)
    s_p = jnp.where(valid_p, s_p * scale - bias_p, NEG_INF)
    m = jnp.max(jnp.maximum(s_c, s_p), axis=-1, keepdims=True)
    e_c = jnp.exp(s_c - m)
    e_p = jnp.exp(s_p - m)
    l = jnp.sum(e_c + e_p, axis=-1, keepdims=True)
    inv_l = 1.0 / l
    p_c = (e_c * inv_l).astype(jnp.bfloat16)
    p_p = (e_p * inv_l).astype(jnp.bfloat16)
    lse = m + jnp.log(l)
    for h, sl in enumerate(head_slices):
        hr = slice(h * ATTN_STEPS, (h + 1) * ATTN_STEPS)
        o_ref[:, sl] = _dot(p_c[hr], vc_ref[:, sl]) + _dot(p_p[hr], vp_ref[:, sl])
        lse_ref[:, sl] = jnp.broadcast_to(lse[hr], (ATTN_STEPS, HEAD_DIM))


def _dilated_group(proj, group):
    _, dilation = DILATED_GROUPS[group]
    length = SEQ // dilation
    nb = length // ATTN_STEPS
    kb = SELF_WIDTH // GROUP_WIDTH
    if dilation == 1:
        src, first, stride = proj, group, kb
    else:
        src = jnp.concatenate([proj[:, s * SELF_WIDTH + group * GROUP_WIDTH:][:, :GROUP_WIDTH]
                               for s in range(3)], axis=1)
        first, stride = 0, 1
    c = src.shape[1]
    cb = c // GROUP_WIDTH
    view = src.reshape(BATCH, length, dilation * c)
    blk = (None, ATTN_STEPS, GROUP_WIDTH)

    def col(section):
        return lambda b, r, n: (b, n, r * cb + section * stride + first)

    def col_prev(section):
        return lambda b, r, n: (b, jnp.maximum(n - 1, 0), r * cb + section * stride + first)

    slopes = tuple(_alibi_slopes(N_SELF_HEADS)[group * HEADS_PER_GROUP:(group + 1) * HEADS_PER_GROUP])
    out_shape = jax.ShapeDtypeStruct((BATCH, length, dilation * GROUP_WIDTH), jnp.float32)
    out_spec = pl.BlockSpec(blk, lambda b, r, n: (b, n, r))
    o, lse = pl.pallas_call(
        functools.partial(_dil_attn_kernel, slopes=slopes, dilation=dilation),
        out_shape=(out_shape, out_shape),
        grid=(BATCH, dilation, nb),
        in_specs=[pl.BlockSpec(blk, col(0)),
                  pl.BlockSpec(blk, col_prev(1)), pl.BlockSpec(blk, col(1)),
                  pl.BlockSpec(blk, col_prev(2)), pl.BlockSpec(blk, col(2))],
        out_specs=(out_spec, out_spec),
        compiler_params=_params("parallel", "parallel", "arbitrary"),
        name=f"dilated_attn_g{group}",
    )(view, view, view, view, view)
    return o.reshape(N_TOK, GROUP_WIDTH), lse.reshape(N_TOK, GROUP_WIDTH)


def _dil_combine_kernel(o0_ref, o1_ref, o2_ref, l0_ref, l1_ref, l2_ref, out_ref):
    l0, l1, l2 = l0_ref[...], l1_ref[...], l2_ref[...]
    m = jnp.maximum(jnp.maximum(l0, l1), l2)
    e0, e1, e2 = jnp.exp(l0 - m), jnp.exp(l1 - m), jnp.exp(l2 - m)
    inv = 1.0 / (e0 + e1 + e2)
    out_ref[:, 0 * GROUP_WIDTH:1 * GROUP_WIDTH] = (o0_ref[...] * (e0 * inv)).astype(out_ref.dtype)
    out_ref[:, 1 * GROUP_WIDTH:2 * GROUP_WIDTH] = (o1_ref[...] * (e1 * inv)).astype(out_ref.dtype)
    out_ref[:, 2 * GROUP_WIDTH:3 * GROUP_WIDTH] = (o2_ref[...] * (e2 * inv)).astype(out_ref.dtype)


def _dilated_attention(proj):
    outs, lses = zip(*[_dilated_group(proj, g) for g in range(len(DILATED_GROUPS))])
    spec = pl.BlockSpec((ROW_TILE, GROUP_WIDTH), lambda i: (i, 0))
    return pl.pallas_call(
        _dil_combine_kernel,
        out_shape=jax.ShapeDtypeStruct((N_TOK, SELF_WIDTH), jnp.bfloat16),
        grid=(N_TOK // ROW_TILE,),
        in_specs=[spec] * 6,
        out_specs=pl.BlockSpec((ROW_TILE, SELF_WIDTH), lambda i: (i, 0)),
        compiler_params=_params("parallel"),
        name="dilated_combine",
    )(*outs, *lses)


def _retention_kernel(q_ref, k_ref, v_ref, g_ref, dmat_ref, kdec_ref, qdec_ref, cdec_ref, gn_ref,
                      o_ref, state_ref):
    @pl.when(pl.program_id(1) == 0)
    def _():
        state_ref[...] = jnp.zeros_like(state_ref)

    heads = range(N_RET_HEADS)
    cols = [slice(h * HEAD_DIM, (h + 1) * HEAD_DIM) for h in heads]
    bf16 = jnp.bfloat16
    scores = [_dot_nt(q_ref[:, c], k_ref[:, c]) * dmat_ref[h] for h, c in zip(heads, cols)]
    states = [state_ref[h] for h in heads]
    cross = [_dot(q_ref[:, c], states[h].astype(bf16)) * qdec_ref[:, c] for h, c in zip(heads, cols)]
    intra = [_dot(scores[h].astype(bf16), v_ref[:, c]) for h, c in zip(heads, cols)]
    kw = [(k_ref[:, c].astype(jnp.float32) * kdec_ref[:, c]).T.astype(bf16) for c in cols]
    for h, c in zip(heads, cols):
        state_ref[h] = states[h] * cdec_ref[h] + _dot(kw[h], v_ref[:, c])
    r = jnp.concatenate([intra[h] + cross[h] for h in heads], axis=0)
    mu = jnp.mean(r, axis=-1, keepdims=True)
    rc = r - mu
    var = jnp.mean(rc * rc, axis=-1, keepdims=True)
    rn = rc * lax.rsqrt(var + LN_EPS)
    for h, c in zip(heads, cols):
        gate = g_ref[:, c].astype(jnp.float32)
        normed = rn[h * RET_CHUNK:(h + 1) * RET_CHUNK] * gn_ref[:, c]
        o_ref[:, c] = (gate * (1.0 / (1.0 + jnp.exp(-gate))) * normed).astype(o_ref.dtype)


def _retention(proj, gn_gain):
    c = RET_CHUNK
    log_gamma = jnp.log1p(-jnp.exp2(-(5.0 + jnp.arange(N_RET_HEADS, dtype=jnp.float32))))
    idx = jnp.arange(c, dtype=jnp.float32)
    diff = idx[:, None] - idx[None, :]
    decay = jnp.where(diff >= 0, jnp.exp(jnp.maximum(diff, 0.0)[None] * log_gamma[:, None, None]), 0.0)
    scale = HEAD_DIM ** -0.5
    dmat = decay * scale
    k_decay = jnp.exp((c - 1 - idx)[:, None] * log_gamma[None, :]) * scale
    q_decay = jnp.exp((idx + 1.0)[:, None] * log_gamma[None, :])
    kdec = jnp.repeat(k_decay, HEAD_DIM, axis=1)
    qdec = jnp.repeat(q_decay, HEAD_DIM, axis=1)
    cdec = jnp.broadcast_to(jnp.exp(c * log_gamma)[:, None, None], (N_RET_HEADS, 1, HEAD_DIM))
    gn = gn_gain.reshape(1, SELF_WIDTH).astype(jnp.float32)
    nchunk = SEQ // c
    blk = (c, SELF_WIDTH)

    def section(s):
        return pl.BlockSpec(blk, lambda b, n: (b * nchunk + n, s))

    const2 = lambda b, n: (0, 0)
    const3 = lambda b, n: (0, 0, 0)
    return pl.pallas_call(
        _retention_kernel,
        out_shape=jax.ShapeDtypeStruct((N_TOK, SELF_WIDTH), jnp.bfloat16),
        grid=(BATCH, nchunk),
        in_specs=[section(0), section(1), section(2), section(3),
                  pl.BlockSpec((N_RET_HEADS, c, c), const3),
                  pl.BlockSpec(blk, const2), pl.BlockSpec(blk, const2),
                  pl.BlockSpec((N_RET_HEADS, 1, HEAD_DIM), const3),
                  pl.BlockSpec((1, SELF_WIDTH), const2)],
        out_specs=pl.BlockSpec(blk, lambda b, n: (b * nchunk + n, 0)),
        scratch_shapes=[pltpu.VMEM((N_RET_HEADS, HEAD_DIM, HEAD_DIM), jnp.float32)],
        compiler_params=_params("parallel", "arbitrary"),
        name="retention",
    )(proj, proj, proj, proj, dmat, kdec, qdec, cdec, gn)


def _mem_attn_kernel(q_ref, k_ref, v_ref, o_ref):
    scale = HEAD_DIM ** -0.5
    for h in range(MEM_HEADS):
        sl = slice(h * HEAD_DIM, (h + 1) * HEAD_DIM)
        s = _dot_nt(q_ref[:, sl], k_ref[:, sl]) * scale
        e = jnp.exp(s - jnp.max(s, axis=-1, keepdims=True))
        p = e * (1.0 / jnp.sum(e, axis=-1, keepdims=True))
        o_ref[:, sl] = _dot(p.astype(jnp.bfloat16), v_ref[:, sl]).astype(o_ref.dtype)


def _memory_attention(proj, memkv):
    qcol = proj.shape[1] // MEM_WIDTH - 1
    per_b = SEQ // ROW_TILE
    return pl.pallas_call(
        _mem_attn_kernel,
        out_shape=jax.ShapeDtypeStruct((N_TOK, MEM_WIDTH), jnp.bfloat16),
        grid=(BATCH, per_b),
        in_specs=[pl.BlockSpec((ROW_TILE, MEM_WIDTH), lambda b, i: (b * per_b + i, qcol)),
                  pl.BlockSpec((MEM_LEN, MEM_WIDTH), lambda b, i: (b, 0)),
                  pl.BlockSpec((MEM_LEN, MEM_WIDTH), lambda b, i: (b, 1))],
        out_specs=pl.BlockSpec((ROW_TILE, MEM_WIDTH), lambda b, i: (b * per_b + i, 0)),
        compiler_params=_params("parallel", "parallel"),
        name="memory_attn",
    )(proj, memkv, memkv)


def _mix_ln_kernel(so_ref, mo_ref, wt_ref, wb_ref, h_ref, g_ref, b_ref, hf_ref, hb_ref):
    mix = _dot(so_ref[...], wt_ref[...]) + _dot(mo_ref[...], wb_ref[...])
    y = _layer_norm_rows(DEEPNORM_ALPHA * h_ref[...] + mix, g_ref[...], b_ref[...])
    hf_ref[...] = y
    hb_ref[...] = y.astype(hb_ref.dtype)


def _mix_ln(self_out, mem_out, w_mix, h, g, b):
    row = lambda i: (i, 0)
    const = lambda i: (0, 0)
    return pl.pallas_call(
        _mix_ln_kernel,
        out_shape=(jax.ShapeDtypeStruct((N_TOK, D_MODEL), jnp.float32),
                   jax.ShapeDtypeStruct((N_TOK, D_MODEL), jnp.bfloat16)),
        grid=(N_TOK // ROW_TILE,),
        in_specs=[pl.BlockSpec((ROW_TILE, SELF_WIDTH), row),
                  pl.BlockSpec((ROW_TILE, MEM_WIDTH), row),
                  pl.BlockSpec((SELF_WIDTH, D_MODEL), const),
                  pl.BlockSpec((MEM_WIDTH, D_MODEL), lambda i: (SELF_WIDTH // MEM_WIDTH, 0)),
                  pl.BlockSpec((ROW_TILE, D_MODEL), row),
                  pl.BlockSpec((1, D_MODEL), const), pl.BlockSpec((1, D_MODEL), const)],
        out_specs=(pl.BlockSpec((ROW_TILE, D_MODEL), row), pl.BlockSpec((ROW_TILE, D_MODEL), row)),
        compiler_params=_params("parallel"),
        name="mix_ln",
    )(self_out, mem_out, w_mix, w_mix, h, g.reshape(1, D_MODEL), b.reshape(1, D_MODEL))


def _router_kernel(h_ref, whi_ref, wlo_ref, b_ref, e_ref, g_ref, r_ref, cnt_ref, run_ref):
    tm = ROUTER_TM

    @pl.when(pl.program_id(0) == 0)
    def _():
        run_ref[...] = jnp.zeros_like(run_ref)

    h = h_ref[...]
    h_hi = h.astype(jnp.bfloat16)
    h_lo = (h - h_hi.astype(jnp.float32)).astype(jnp.bfloat16)
    logits = (_dot(h_hi, whi_ref[...]) + _dot(h_lo, whi_ref[...]) + _dot(h_hi, wlo_ref[...])) + b_ref[...]
    lane = lax.broadcasted_iota(jnp.int32, (tm, LANES), 1).astype(jnp.float32)
    work = logits
    vals, idxs, hots = [], [], []
    for _ in range(TOP_K):
        m = jnp.max(work, axis=-1, keepdims=True)
        idx = jnp.min(jnp.where(work == m, lane, float(LANES)), axis=-1, keepdims=True)
        hot = lane == idx
        vals.append(m)
        idxs.append(idx)
        hots.append(hot)
        work = jnp.where(hot, -jnp.inf, work)
    exps = [jnp.exp(v - vals[0]) for v in vals]
    inv = 1.0 / (exps[0] + exps[1] + exps[2] + exps[3])
    cnt = jnp.zeros((tm, LANES), jnp.float32)
    for hot in hots:
        cnt = cnt + hot.astype(jnp.float32)
    row = lax.broadcasted_iota(jnp.int32, (tm, tm), 0)
    colm = lax.broadcasted_iota(jnp.int32, (tm, tm), 1)
    tri = (row > colm).astype(jnp.bfloat16)
    before = run_ref[...] + _dot(tri, cnt.astype(jnp.bfloat16))
    lane_i = lax.broadcasted_iota(jnp.int32, (tm, LANES), 1)
    e_out = jnp.zeros((tm, LANES), jnp.float32)
    g_out = jnp.zeros((tm, LANES), jnp.float32)
    r_out = jnp.zeros((tm, LANES), jnp.float32)
    for k in range(TOP_K):
        rank = jnp.sum(jnp.where(hots[k], before, 0.0), axis=-1, keepdims=True)
        e_out = jnp.where(lane_i == k, idxs[k], e_out)
        g_out = jnp.where(lane_i == k, exps[k] * inv, g_out)
        r_out = jnp.where(lane_i == k, rank, r_out)
    e_ref[...] = e_out.astype(jnp.int32)
    g_ref[...] = g_out
    r_ref[...] = r_out.astype(jnp.int32)
    run_ref[...] = run_ref[...] + jnp.sum(cnt, axis=0, keepdims=True)
    cnt_ref[...] = run_ref[...].astype(jnp.int32)


def _router(h, router_w, router_b):
    w = jnp.zeros((D_MODEL, LANES), jnp.float32).at[:, :N_EXPERTS].set(router_w)
    w_hi = w.astype(jnp.bfloat16)
    w_lo = (w - w_hi.astype(jnp.float32)).astype(jnp.bfloat16)
    b = jnp.full((1, LANES), NEG_INF, jnp.float32).at[0, :N_EXPERTS].set(router_b)
    tm = ROUTER_TM
    row = lambda i: (i, 0)
    const = lambda i: (0, 0)
    lanes_out = pl.BlockSpec((tm, LANES), row)
    e, g, r, cnt = pl.pallas_call(
        _router_kernel,
        out_shape=(jax.ShapeDtypeStruct((N_TOK, LANES), jnp.int32),
                   jax.ShapeDtypeStruct((N_TOK, LANES), jnp.float32),
                   jax.ShapeDtypeStruct((N_TOK, LANES), jnp.int32),
                   jax.ShapeDtypeStruct((1, LANES), jnp.int32)),
        grid=(N_TOK // tm,),
        in_specs=[pl.BlockSpec((tm, D_MODEL), row), pl.BlockSpec((D_MODEL, LANES), const),
                  pl.BlockSpec((D_MODEL, LANES), const), pl.BlockSpec((1, LANES), const)],
        out_specs=(lanes_out, lanes_out, lanes_out, pl.BlockSpec((1, LANES), const)),
        scratch_shapes=[pltpu.VMEM((1, LANES), jnp.float32)],
        compiler_params=_params("arbitrary"),
        name="router",
    )(h, w_hi, w_lo, b)
    return e[:, :TOP_K], g, r[:, :TOP_K], cnt[0, :N_EXPERTS]


def _route_tables(top_e, rank, cnt):
    ntile = (cnt + MOE_TILE - 1) // MOE_TILE
    tile_end = jnp.cumsum(ntile)
    tile_base = tile_end - ntile
    flat_e, flat_rank = top_e.reshape(-1), rank.reshape(-1)
    pos = (tile_base[flat_e] * MOE_TILE + flat_rank).astype(jnp.int32)
    n_tiles = tile_end[-1:].astype(jnp.int32)
    w = jnp.minimum(jnp.arange(MOE_TILES, dtype=jnp.int32), n_tiles[0] - 1)
    tile_e = jnp.minimum(jnp.searchsorted(tile_end, w, side="right"), N_EXPERTS - 1).astype(jnp.int32)
    rows = jnp.clip(cnt[tile_e] - (w - tile_base[tile_e]) * MOE_TILE, 0, MOE_TILE)
    nsub = ((rows + MOE_SUB - 1) // MOE_SUB).astype(jnp.int32)
    pad_start = (tile_base * MOE_TILE + cnt).astype(jnp.int32)
    pad_n = ((-cnt) % MOE_SUB).astype(jnp.int32)
    return pos, n_tiles, tile_e, nsub, pad_start, pad_n


def _row_copy(src_vmem, src_row, dst_hbm, dst_row, sem):
    return pltpu.make_async_copy(src_vmem.at[pl.ds(src_row, 1)], dst_hbm.at[pl.ds(dst_row, 1)], sem)


def _pack_bf16_pairs(x):
    half = x.shape[1] // 2
    lo = lax.bitcast_convert_type(x[:, :half].astype(jnp.float32), jnp.uint32)
    hi = lax.bitcast_convert_type(x[:, half:].astype(jnp.float32), jnp.uint32)
    return (lo >> 16) | (hi & jnp.uint32(0xFFFF0000))


def _unpack_bf16_pairs(words):
    lo = lax.bitcast_convert_type(words << 16, jnp.float32)
    hi = lax.bitcast_convert_type(words & jnp.uint32(0xFFFF0000), jnp.float32)
    return lo.astype(jnp.bfloat16), hi.astype(jnp.bfloat16)


def _dispatch_kernel(pos_ref, pad_start_ref, pad_n_ref, hb_ref, xs_hbm, h_ref, sem):
    step = pl.program_id(0)
    base = step * DISPATCH_TB
    h_ref[...] = _pack_bf16_pairs(hb_ref[...])

    def issue(r, carry):
        for k in range(TOP_K):
            _row_copy(h_ref, r, xs_hbm, pos_ref[(base + r) * TOP_K + k], sem).start()
        return carry

    lax.fori_loop(0, DISPATCH_TB, issue, 0, unroll=8)

    @pl.when(step == 0)
    def _():
        def per_expert(e, carry):
            start = pad_start_ref[e]
            n = pad_n_ref[e]

            def fill(i, c):
                _row_copy(h_ref, 0, xs_hbm, start + i, sem).start()
                return c

            def drain(_, c):
                _row_copy(h_ref, 0, xs_hbm, 0, sem).wait()
                return c

            lax.fori_loop(0, n, fill, 0)
            lax.fori_loop(0, n, drain, 0)
            return carry

        lax.fori_loop(0, N_EXPERTS, per_expert, 0)

    for _ in range(TOP_K):
        pltpu.make_async_copy(h_ref, xs_hbm.at[pl.ds(0, DISPATCH_TB)], sem).wait()


def _dispatch(hb, pos, pad_start, pad_n):
    return pl.pallas_call(
        _dispatch_kernel,
        out_shape=jax.ShapeDtypeStruct((MOE_ROWS, D_MODEL // 2), jnp.uint32),
        grid_spec=pltpu.PrefetchScalarGridSpec(
            num_scalar_prefetch=3,
            grid=(N_TOK // DISPATCH_TB,),
            in_specs=[pl.BlockSpec((DISPATCH_TB, D_MODEL), lambda i, p, s, n: (i, 0))],
            out_specs=pl.BlockSpec(memory_space=pl.ANY),
            scratch_shapes=[pltpu.VMEM((DISPATCH_TB, D_MODEL // 2), jnp.uint32),
                            pltpu.SemaphoreType.DMA(())],
        ),
        compiler_params=_params("arbitrary"),
        name="moe_dispatch",
    )(pos, pad_start, pad_n, hb)


def _moe_weight_map(layer, first, hidden_axis):
    def index_map(w, j, tile_e, nsub):
        return (layer, tile_e[w], first + j, 0) if hidden_axis == 2 else (layer, tile_e[w], 0, first + j)

    return index_map


def _moe_rows(nsub, accumulate):
    fast = nsub < 0
    for s in MOE_FAST_SUBS:
        fast = jnp.logical_or(fast, nsub == s)
        pl.when(nsub == s)(functools.partial(accumulate, slice(0, s * MOE_SUB)))

    @pl.when(jnp.logical_and(nsub > 0, jnp.logical_not(fast)))
    def _():
        def one(i, carry):
            accumulate(pl.ds(pl.multiple_of(i * MOE_SUB, MOE_SUB), MOE_SUB))
            return carry

        lax.fori_loop(0, nsub, one, 0)


def _moe_expert_kernel(exp_ref, nsub_ref, x_ref, wg_ref, wl_ref, bg_ref, bl_ref, wo_ref, bo_ref, o_ref):
    first = pl.program_id(1) == 0
    nsub = nsub_ref[pl.program_id(0)]
    half = D_MODEL // 2

    for i in range(MOE_SUBS):
        rows = slice(i * MOE_SUB, (i + 1) * MOE_SUB)

        @pl.when(jnp.logical_and(i < nsub, first))
        def _():
            o_ref[rows, :] = jnp.broadcast_to(bo_ref[...], (MOE_SUB, D_MODEL))

        @pl.when(jnp.logical_and(i >= nsub, first))
        def _():
            o_ref[rows, :] = jnp.zeros((MOE_SUB, D_MODEL), o_ref.dtype)

    def up(lo, hi, w_ref, b_ref):
        return (_dot(lo, w_ref[0:half, :].astype(jnp.bfloat16))
                + _dot(hi, w_ref[half:D_MODEL, :].astype(jnp.bfloat16)) + b_ref[...])

    def accumulate(rows):
        lo, hi = _unpack_bf16_pairs(x_ref[rows, :])
        gate = up(lo, hi, wg_ref, bg_ref)
        lin = up(lo, hi, wl_ref, bl_ref)
        gate = jnp.minimum(gate, SWIGLU_LIMIT)
        lin = jnp.clip(lin, -SWIGLU_LIMIT, SWIGLU_LIMIT)
        act = (gate * (1.0 / (1.0 + jnp.exp(-SWIGLU_ALPHA * gate))) * (lin + 1.0)).astype(jnp.bfloat16)
        for c in range(D_MODEL // MOE_DOWN_TN):
            cols = slice(c * MOE_DOWN_TN, (c + 1) * MOE_DOWN_TN)
            o_ref[rows, cols] = o_ref[rows, cols] + _dot(act, wo_ref[:, cols].astype(jnp.bfloat16))

    _moe_rows(nsub, accumulate)


def _moe_experts(xs, w_in, b_in, w_out, b_out, layer, n_tiles, tile_e, nsub):
    nj = D_EXPERT // MOE_TF
    b_in4 = b_in.reshape(DEPTH, N_EXPERTS, 1, 2 * D_EXPERT)
    b_out4 = b_out.reshape(DEPTH, N_EXPERTS, 1, D_MODEL)
    wblk = (None, None, D_MODEL, MOE_TF)
    bblk = (None, None, 1, MOE_TF)
    tile_map = lambda w, j, e, n: (w, 0)
    return pl.pallas_call(
        _moe_expert_kernel,
        out_shape=jax.ShapeDtypeStruct((MOE_ROWS, D_MODEL), jnp.float32),
        grid_spec=pltpu.PrefetchScalarGridSpec(
            num_scalar_prefetch=2,
            grid=(n_tiles[0], nj),
            in_specs=[pl.BlockSpec((MOE_TILE, D_MODEL // 2), tile_map),
                      pl.BlockSpec(wblk, _moe_weight_map(layer, 0, 3)),
                      pl.BlockSpec(wblk, _moe_weight_map(layer, nj, 3)),
                      pl.BlockSpec(bblk, _moe_weight_map(layer, 0, 3)),
                      pl.BlockSpec(bblk, _moe_weight_map(layer, nj, 3)),
                      pl.BlockSpec((None, None, MOE_TF, D_MODEL), _moe_weight_map(layer, 0, 2)),
                      pl.BlockSpec((None, None, 1, D_MODEL), lambda w, j, e, n: (layer, e[w], 0, 0))],
            out_specs=pl.BlockSpec((MOE_TILE, D_MODEL), tile_map),
        ),
        compiler_params=_params("arbitrary", "arbitrary", vmem_limit=MOE_VMEM_LIMIT),
        name="moe_experts",
    )(tile_e, nsub, xs, w_in, w_in, b_in4, b_in4, w_out, b_out4)


def _combine_ln_kernel(pos_ref, y_hbm, gates_ref, h_ref, g_ref, b_ref, hf_ref, hb_ref, ybuf_ref, sems):
    step = pl.program_id(0)
    slot = step % 2

    def gather(tile, into):
        base = tile * COMBINE_TB

        def issue(r, carry):
            for k in range(TOP_K):
                src = y_hbm.at[pl.ds(pos_ref[(base + r) * TOP_K + k], 1)]
                pltpu.make_async_copy(src, ybuf_ref.at[into, k, pl.ds(r, 1)], sems.at[into]).start()
            return carry

        lax.fori_loop(0, COMBINE_TB, issue, 0, unroll=8)

    @pl.when(step == 0)
    def _():
        gather(0, 0)

    @pl.when(step + 1 < pl.num_programs(0))
    def _():
        gather(step + 1, 1 - slot)

    for k in range(TOP_K):
        pltpu.make_async_copy(y_hbm.at[pl.ds(0, COMBINE_TB)], ybuf_ref.at[slot, k], sems.at[slot]).wait()

    gates = gates_ref[...]
    ffn = gates[:, 0:1] * ybuf_ref[slot, 0]
    for k in range(1, TOP_K):
        ffn = ffn + gates[:, k:k + 1] * ybuf_ref[slot, k]
    y = _layer_norm_rows(DEEPNORM_ALPHA * h_ref[...] + ffn, g_ref[...], b_ref[...])
    hf_ref[...] = y
    hb_ref[...] = y.astype(hb_ref.dtype)


def _combine_ln(y, pos, gates, h, g, b):
    tb = COMBINE_TB
    row = lambda i, p: (i, 0)
    const = lambda i, p: (0, 0)
    return pl.pallas_call(
        _combine_ln_kernel,
        out_shape=(jax.ShapeDtypeStruct((N_TOK, D_MODEL), jnp.float32),
                   jax.ShapeDtypeStruct((N_TOK, D_MODEL), jnp.bfloat16)),
        grid_spec=pltpu.PrefetchScalarGridSpec(
            num_scalar_prefetch=1,
            grid=(N_TOK // tb,),
            in_specs=[pl.BlockSpec(memory_space=pl.ANY),
                      pl.BlockSpec((tb, LANES), row),
                      pl.BlockSpec((tb, D_MODEL), row),
                      pl.BlockSpec((1, D_MODEL), const), pl.BlockSpec((1, D_MODEL), const)],
            out_specs=(pl.BlockSpec((tb, D_MODEL), row), pl.BlockSpec((tb, D_MODEL), row)),
            scratch_shapes=[pltpu.VMEM((2, TOP_K, tb, D_MODEL), jnp.float32),
                            pltpu.SemaphoreType.DMA((2,))],
        ),
        compiler_params=_params("arbitrary"),
        name="moe_combine_ln",
    )(pos, y, gates, h, g.reshape(1, D_MODEL), b.reshape(1, D_MODEL))


def _moe(h, hb, layer, router_w, router_b, w_in, b_in, w_out, b_out, ln_g, ln_b):
    top_e, gates, rank, cnt = _router(h, router_w[layer], router_b[layer])
    pos, n_tiles, tile_e, nsub, pad_start, pad_n = _route_tables(top_e, rank, cnt)
    xs = _dispatch(hb, pos, pad_start, pad_n)
    y = _moe_experts(xs, w_in, b_in, w_out, b_out, layer, n_tiles, tile_e, nsub)
    return _combine_ln(y, pos, gates, h, ln_g[layer], ln_b[layer])


def kernel(x, mem, w_in_dil, w_in_ret, ret_gn_g, w_mem_kv, w_mix_out, ln_mix_g, ln_mix_b, router_w, router_b, moe_w_in, moe_b_in, moe_w_out, moe_b_out, ln_ffn_g, ln_ffn_b):
    bf16 = jnp.bfloat16
    h = x.reshape(N_TOK, D_MODEL)
    hb = h
    memb = mem.reshape(BATCH * MEM_LEN, D_MODEL)
    for layer in range(DEPTH):
        slot = layer // 2
        if layer % 2 == 0:
            proj = _matmul(hb, w_in_dil, slot, bf16)
            self_out = _dilated_attention(proj)
        else:
            proj = _matmul(hb, w_in_ret, slot, bf16)
            self_out = _retention(proj, ret_gn_g[slot])
        memkv = _matmul(memb, w_mem_kv, layer, bf16)
        mem_out = _memory_attention(proj, memkv)
        h, hb = _mix_ln(self_out, mem_out, w_mix_out[layer].astype(bf16), h,
                        ln_mix_g[layer], ln_mix_b[layer])
        h, hb = _moe(h, hb, layer, router_w, router_b, moe_w_in, moe_b_in, moe_w_out, moe_b_out,
                     ln_ffn_g, ln_ffn_b)
    return h.reshape(BATCH, SEQ, D_MODEL)
```

```python
import functools
import math

import jax
import jax.numpy as jnp
from jax import lax
from jax.experimental import pallas as pl
from jax.experimental.pallas import tpu as pltpu

D_MODEL = 2048
BATCH = 2
SEQ = 4096
DEPTH = 2
HEAD_DIM = 128
DILATED_GROUPS = ((128, 1), (512, 4), (2048, 16))
HEADS_PER_GROUP = 4
N_SELF_HEADS = len(DILATED_GROUPS) * HEADS_PER_GROUP
N_RET_HEADS = 12
SELF_WIDTH = N_SELF_HEADS * HEAD_DIM
MEM_HEADS = 4
MEM_LEN = 256
MEM_WIDTH = MEM_HEADS * HEAD_DIM
RET_CHUNK = 128
N_EXPERTS = 32
TOP_K = 4
D_EXPERT = D_MODEL
SWIGLU_ALPHA = 1.702
SWIGLU_LIMIT = 7.0
DEEPNORM_ALPHA = (2 * DEPTH) ** 0.25
LN_EPS = 1e-5
NEG_INF = -1e30

N_TOK = BATCH * SEQ
GROUP_WIDTH = HEADS_PER_GROUP * HEAD_DIM
SELF_QKV = 3 * GROUP_WIDTH
ATTN_STEPS = 128

LANES = 128
VMEM_LIMIT = 56 * 1024 * 1024

MM_TM = 1024
MM_TN = 512
ROW_TILE = 512
COMBINE_ROWS = 512
ROUTER_TM = 256

MOE_SUB = 64
MOE_TILE = 1152
MOE_SUBS = MOE_TILE // MOE_SUB
MOE_FAST_SUBS = (15, 16, 17, 18)
MOE_TF = 512
MOE_DOWN_TN = 512
MOE_VMEM_LIMIT = 60 * 1024 * 1024
MOE_TILES = (N_TOK * TOP_K) // MOE_TILE + N_EXPERTS
MOE_ROWS = MOE_TILES * MOE_TILE
DISPATCH_TB = 256
COMBINE_TB = 128


def _alibi_slopes(n):
    def pow2(m):
        start = 2.0 ** (-8.0 / m)
        return [start ** (i + 1) for i in range(m)]

    if math.log2(n).is_integer():
        s = pow2(n)
    else:
        c = 2 ** math.floor(math.log2(n))
        s = pow2(c) + pow2(2 * c)[0::2][: n - c]
    return sorted(s, reverse=True)


def _params(*sem, vmem_limit=VMEM_LIMIT):
    return pltpu.CompilerParams(dimension_semantics=sem, vmem_limit_bytes=vmem_limit)


def _layer_norm_rows(z, g, b):
    mu = jnp.mean(z, axis=-1, keepdims=True)
    zc = z - mu
    var = jnp.mean(zc * zc, axis=-1, keepdims=True)
    return zc * lax.rsqrt(var + LN_EPS) * g + b


def _dot_nt(a, b):
    return lax.dot_general(a, b, (((1,), (1,)), ((), ())), preferred_element_type=jnp.float32)


def _dot(a, b):
    return jnp.dot(a, b, preferred_element_type=jnp.float32)


def _mm_kernel(x_ref, w_ref, o_ref):
    o_ref[...] = _dot(x_ref[...].astype(jnp.bfloat16), w_ref[...].astype(jnp.bfloat16)).astype(o_ref.dtype)


def _matmul(x, w, layer, out_dtype, col_tiles=None):
    m, k = x.shape
    first, stride, count = col_tiles if col_tiles else (0, 1, w.shape[2] // MM_TN)
    tm = min(MM_TM, m)
    return pl.pallas_call(
        _mm_kernel,
        out_shape=jax.ShapeDtypeStruct((m, count * MM_TN), out_dtype),
        grid=(m // tm, count),
        in_specs=[pl.BlockSpec((tm, k), lambda i, j: (i, 0)),
                  pl.BlockSpec((None, k, MM_TN), lambda i, j: (layer, 0, first + stride * j))],
        out_specs=pl.BlockSpec((tm, MM_TN), lambda i, j: (i, j)),
        compiler_params=_params("parallel", "parallel"),
        name="dense_matmul",
    )(x, w)


def _proj_residue_kernel(x_ref, wq_ref, wk_ref, wv_ref, o_ref, scr_ref, *, dilation):
    x = x_ref[...].astype(jnp.bfloat16)
    rows = MM_TM // dilation
    for section, w_ref in enumerate((wq_ref, wk_ref, wv_ref)):
        res = _dot(x, w_ref[...].astype(jnp.bfloat16))
        for c in range(GROUP_WIDTH // LANES):
            scr_ref[c] = res[:, c * LANES:(c + 1) * LANES]
        for r in range(dilation):
            for c in range(GROUP_WIDTH // LANES):
                col = r * SELF_QKV + section * GROUP_WIDTH + c * LANES
                o_ref[:, col:col + LANES] = scr_ref[c, pl.ds(r, rows, stride=dilation), :].astype(o_ref.dtype)


def _proj_residue_major(x, w, layer, group):
    _, dilation = DILATED_GROUPS[group]
    k = x.shape[1]
    kb = SELF_WIDTH // GROUP_WIDTH

    def wspec(section):
        return pl.BlockSpec((None, k, GROUP_WIDTH), lambda i: (layer, 0, section * kb + group))

    return pl.pallas_call(
        functools.partial(_proj_residue_kernel, dilation=dilation),
        out_shape=jax.ShapeDtypeStruct((N_TOK // dilation, dilation * SELF_QKV), jnp.bfloat16),
        grid=(N_TOK // MM_TM,),
        in_specs=[pl.BlockSpec((MM_TM, k), lambda i: (i, 0)), wspec(0), wspec(1), wspec(2)],
        out_specs=pl.BlockSpec((MM_TM // dilation, dilation * SELF_QKV), lambda i: (i, 0)),
        scratch_shapes=[pltpu.VMEM((GROUP_WIDTH // LANES, MM_TM, LANES), jnp.float32)],
        compiler_params=_params("parallel"),
        name=f"proj_residue_g{group}",
    )(x, w, w, w)


def _dil_attn_kernel(q_ref, kp_ref, kc_ref, vp_ref, vc_ref, o_ref, lse_ref, *, slopes, dilation):
    n = pl.program_id(2)
    heads = HEADS_PER_GROUP
    rows = heads * ATTN_STEPS
    head_slices = [slice(h * HEAD_DIM, (h + 1) * HEAD_DIM) for h in range(heads)]
    row = lax.broadcasted_iota(jnp.int32, (rows, ATTN_STEPS), 0)
    kj = lax.broadcasted_iota(jnp.int32, (rows, ATTN_STEPS), 1)
    diff = (row % ATTN_STEPS) - kj
    valid_c = diff >= 0
    valid_p = jnp.logical_and(diff <= 0, n > 0)
    slope = jnp.full((rows, ATTN_STEPS), slopes[0], jnp.float32)
    for h in range(1, heads):
        slope = jnp.where(row >= h * ATTN_STEPS, slopes[h], slope)
    bias_c = slope * (diff * dilation).astype(jnp.float32)
    bias_p = slope * ((diff + ATTN_STEPS) * dilation).astype(jnp.float32)
    scale = HEAD_DIM ** -0.5
    s_c = jnp.concatenate([_dot_nt(q_ref[:, sl], kc_ref[:, sl]) for sl in head_slices], axis=0)
    s_p = jnp.concatenate([_dot_nt(q_ref[:, sl], kp_ref[:, sl]) for sl in head_slices], axis=0)
    s_c = jnp.where(valid_c, s_c * scale - bias_c, NEG_INF)
    s_p = jnp.where(valid_p, s_p * scale - bias_p, NEG_INF)
    m = jnp.max(jnp.maximum(s_c, s_p), axis=-1, keepdims=True)
    e_c = jnp.exp(s_c - m)
    e_p = jnp.exp(s_p - m)
    l = jnp.sum(e_c + e_p, axis=-1, keepdims=True)
    inv_l = 1.0 / l
    p_c = (e_c * inv_l).astype(jnp.bfloat16)
    p_p = (e_p * inv_l).astype(jnp.bfloat16)
    lse = m + jnp.log(l)
    for h, sl in enumerate(head_slices):
        hr = slice(h * ATTN_STEPS, (h + 1) * ATTN_STEPS)
        o_ref[:, sl] = _dot(p_c[hr], vc_ref[:, sl]) + _dot(p_p[hr], vp_ref[:, sl])
        lse_ref[:, sl] = jnp.broadcast_to(lse[hr], (ATTN_STEPS, HEAD_DIM))


def _dilated_group(src, group):
    _, dilation = DILATED_GROUPS[group]
    length = SEQ // dilation
    nb = length // ATTN_STEPS
    cb = src.shape[1] // dilation // GROUP_WIDTH
    view = src.reshape(BATCH, length, src.shape[1])
    blk = (None, ATTN_STEPS, GROUP_WIDTH)

    def col(section):
        return lambda b, r, n: (b, n, r * cb + section)

    def col_prev(section):
        return lambda b, r, n: (b, jnp.maximum(n - 1, 0), r * cb + section)

    slopes = tuple(_alibi_slopes(N_SELF_HEADS)[group * HEADS_PER_GROUP:(group + 1) * HEADS_PER_GROUP])
    out_shape = jax.ShapeDtypeStruct((BATCH, length, dilation * GROUP_WIDTH), jnp.float32)
    out_spec = pl.BlockSpec(blk, lambda b, r, n: (b, n, r))
    o, lse = pl.pallas_call(
        functools.partial(_dil_attn_kernel, slopes=slopes, dilation=dilation),
        out_shape=(out_shape, out_shape),
        grid=(BATCH, dilation, nb),
        in_specs=[pl.BlockSpec(blk, col(0)),
                  pl.BlockSpec(blk, col_prev(1)), pl.BlockSpec(blk, col(1)),
                  pl.BlockSpec(blk, col_prev(2)), pl.BlockSpec(blk, col(2))],
        out_specs=(out_spec, out_spec),
        compiler_params=_params("parallel", "parallel", "arbitrary"),
        name=f"dilated_attn_g{group}",
    )(view, view, view, view, view)
    return o, lse


def _dil_combine_kernel(*refs):
    ngroups = len(DILATED_GROUPS)
    o_refs, l_refs, out_ref = refs[:ngroups], refs[ngroups:2 * ngroups], refs[2 * ngroups]
    scratch = list(refs[2 * ngroups + 1:])
    os, ls = [], []
    for g, (_, dilation) in enumerate(DILATED_GROUPS):
        if dilation == 1:
            os.append(o_refs[g][...])
            ls.append(l_refs[g][...])
            continue
        rows = COMBINE_ROWS // dilation
        planes = GROUP_WIDTH // LANES
        o_scr, l_scr = scratch.pop(0), scratch.pop(0)
        for r in range(dilation):
            for c in range(planes):
                cols = slice(r * GROUP_WIDTH + c * LANES, r * GROUP_WIDTH + (c + 1) * LANES)
                o_scr[c, pl.ds(r, rows, stride=dilation), :] = o_refs[g][:, cols]
                l_scr[c, pl.ds(r, rows, stride=dilation), :] = l_refs[g][:, cols]
        os.append(jnp.concatenate([o_scr[c] for c in range(planes)], axis=1))
        ls.append(jnp.concatenate([l_scr[c] for c in range(planes)], axis=1))
    m = functools.reduce(jnp.maximum, ls)
    es = [jnp.exp(l - m) for l in ls]
    inv = 1.0 / functools.reduce(lambda a, b: a + b, es)
    for g in range(ngroups):
        out_ref[:, g * GROUP_WIDTH:(g + 1) * GROUP_WIDTH] = (os[g] * (es[g] * inv)).astype(out_ref.dtype)


def _dilated_attention(x, w_in, layer, proj):
    sources = [proj] + [_proj_residue_major(x, w_in, layer, g) for g in range(1, len(DILATED_GROUPS))]
    outs, lses = zip(*[_dilated_group(src, g) for g, src in enumerate(sources)])

    def spec(dilation):
        return pl.BlockSpec((COMBINE_ROWS // dilation, dilation * GROUP_WIDTH), lambda i: (i, 0))

    def flat(a):
        return a.reshape(a.shape[0] * a.shape[1], a.shape[2])

    specs = [spec(d) for _, d in DILATED_GROUPS]
    scratch = [pltpu.VMEM((GROUP_WIDTH // LANES, COMBINE_ROWS, LANES), jnp.float32)
               for _, d in DILATED_GROUPS if d > 1 for _ in range(2)]
    return pl.pallas_call(
        _dil_combine_kernel,
        out_shape=jax.ShapeDtypeStruct((N_TOK, SELF_WIDTH), jnp.bfloat16),
        grid=(N_TOK // COMBINE_ROWS,),
        in_specs=specs + specs,
        out_specs=pl.BlockSpec((COMBINE_ROWS, SELF_WIDTH), lambda i: (i, 0)),
        scratch_shapes=scratch,
        compiler_params=_params("parallel"),
        name="dilated_combine",
    )(*[flat(o) for o in outs], *[flat(l) for l in lses])


def _retention_kernel(q_ref, k_ref, v_ref, g_ref, dmat_ref, kdec_ref, qdec_ref, cdec_ref, gn_ref,
                      o_ref, state_ref):
    @pl.when(pl.program_id(1) == 0)
    def _():
        state_ref[...] = jnp.zeros_like(state_ref)

    heads = range(N_RET_HEADS)
    cols = [slice(h * HEAD_DIM, (h + 1) * HEAD_DIM) for h in heads]
    bf16 = jnp.bfloat16
    scores = [_dot_nt(q_ref[:, c], k_ref[:, c]) * dmat_ref[h] for h, c in zip(heads, cols)]
    states = [state_ref[h] for h in heads]
    cross = [_dot(q_ref[:, c], states[h].astype(bf16)) * qdec_ref[:, c] for h, c in zip(heads, cols)]
    intra = [_dot(scores[h].astype(bf16), v_ref[:, c]) for h, c in zip(heads, cols)]
    kw = [(k_ref[:, c].astype(jnp.float32) * kdec_ref[:, c]).T.astype(bf16) for c in cols]
    for h, c in zip(heads, cols):
        state_ref[h] = states[h] * cdec_ref[h] + _dot(kw[h], v_ref[:, c])
    r = jnp.concatenate([intra[h] + cross[h] for h in heads], axis=0)
    mu = jnp.mean(r, axis=-1, keepdims=True)
    rc = r - mu
    var = jnp.mean(rc * rc, axis=-1, keepdims=True)
    rn = rc * lax.rsqrt(var + LN_EPS)
    for h, c in zip(heads, cols):
        gate = g_ref[:, c].astype(jnp.float32)
        normed = rn[h * RET_CHUNK:(h + 1) * RET_CHUNK] * gn_ref[:, c]
        o_ref[:, c] = (gate * (1.0 / (1.0 + jnp.exp(-gate))) * normed).astype(o_ref.dtype)


def _retention(proj, gn_gain):
    c = RET_CHUNK
    log_gamma = jnp.log1p(-jnp.exp2(-(5.0 + jnp.arange(N_RET_HEADS, dtype=jnp.float32))))
    idx = jnp.arange(c, dtype=jnp.float32)
    diff = idx[:, None] - idx[None, :]
    decay = jnp.where(diff >= 0, jnp.exp(jnp.maximum(diff, 0.0)[None] * log_gamma[:, None, None]), 0.0)
    scale = HEAD_DIM ** -0.5
    dmat = decay * scale
    k_decay = jnp.exp((c - 1 - idx)[:, None] * log_gamma[None, :]) * scale
    q_decay = jnp.exp((idx + 1.0)[:, None] * log_gamma[None, :])
    kdec = jnp.repeat(k_decay, HEAD_DIM, axis=1)
    qdec = jnp.repeat(q_decay, HEAD_DIM, axis=1)
    cdec = jnp.broadcast_to(jnp.exp(c * log_gamma)[:, None, None], (N_RET_HEADS, 1, HEAD_DIM))
    gn = gn_gain.reshape(1, SELF_WIDTH).astype(jnp.float32)
    nchunk = SEQ // c
    blk = (c, SELF_WIDTH)

    def section(s):
        return pl.BlockSpec(blk, lambda b, n: (b * nchunk + n, s))

    const2 = lambda b, n: (0, 0)
    const3 = lambda b, n: (0, 0, 0)
    return pl.pallas_call(
        _retention_kernel,
        out_shape=jax.ShapeDtypeStruct((N_TOK, SELF_WIDTH), jnp.bfloat16),
        grid=(BATCH, nchunk),
        in_specs=[section(0), section(1), section(2), section(3),
                  pl.BlockSpec((N_RET_HEADS, c, c), const3),
                  pl.BlockSpec(blk, const2), pl.BlockSpec(blk, const2),
                  pl.BlockSpec((N_RET_HEADS, 1, HEAD_DIM), const3),
                  pl.BlockSpec((1, SELF_WIDTH), const2)],
        out_specs=pl.BlockSpec(blk, lambda b, n: (b * nchunk + n, 0)),
        scratch_shapes=[pltpu.VMEM((N_RET_HEADS, HEAD_DIM, HEAD_DIM), jnp.float32)],
        compiler_params=_params("parallel", "arbitrary"),
        name="retention",
    )(proj, proj, proj, proj, dmat, kdec, qdec, cdec, gn)


def _mem_attn_kernel(q_ref, k_ref, v_ref, o_ref):
    scale = HEAD_DIM ** -0.5
    for h in range(MEM_HEADS):
        sl = slice(h * HEAD_DIM, (h + 1) * HEAD_DIM)
        s = _dot_nt(q_ref[:, sl], k_ref[:, sl]) * scale
        e = jnp.exp(s - jnp.max(s, axis=-1, keepdims=True))
        p = e * (1.0 / jnp.sum(e, axis=-1, keepdims=True))
        o_ref[:, sl] = _dot(p.astype(jnp.bfloat16), v_ref[:, sl]).astype(o_ref.dtype)


def _memory_attention(proj, memkv):
    qcol = proj.shape[1] // MEM_WIDTH - 1
    per_b = SEQ // ROW_TILE
    return pl.pallas_call(
        _mem_attn_kernel,
        out_shape=jax.ShapeDtypeStruct((N_TOK, MEM_WIDTH), jnp.bfloat16),
        grid=(BATCH, per_b),
        in_specs=[pl.BlockSpec((ROW_TILE, MEM_WIDTH), lambda b, i: (b * per_b + i, qcol)),
                  pl.BlockSpec((MEM_LEN, MEM_WIDTH), lambda b, i: (b, 0)),
                  pl.BlockSpec((MEM_LEN, MEM_WIDTH), lambda b, i: (b, 1))],
        out_specs=pl.BlockSpec((ROW_TILE, MEM_WIDTH), lambda b, i: (b * per_b + i, 0)),
        compiler_params=_params("parallel", "parallel"),
        name="memory_attn",
    )(proj, memkv, memkv)


def _mix_ln_kernel(so_ref, mo_ref, wt_ref, wb_ref, h_ref, g_ref, b_ref, hf_ref, hb_ref):
    mix = _dot(so_ref[...], wt_ref[...]) + _dot(mo_ref[...], wb_ref[...])
    y = _layer_norm_rows(DEEPNORM_ALPHA * h_ref[...] + mix, g_ref[...], b_ref[...])
    hf_ref[...] = y
    hb_ref[...] = y.astype(hb_ref.dtype)


def _mix_ln(self_out, mem_out, w_mix, h, g, b):
    row = lambda i: (i, 0)
    const = lambda i: (0, 0)
    return pl.pallas_call(
        _mix_ln_kernel,
        out_shape=(jax.ShapeDtypeStruct((N_TOK, D_MODEL), jnp.float32),
                   jax.ShapeDtypeStruct((N_TOK, D_MODEL), jnp.bfloat16)),
        grid=(N_TOK // ROW_TILE,),
        in_specs=[pl.BlockSpec((ROW_TILE, SELF_WIDTH), row),
                  pl.BlockSpec((ROW_TILE, MEM_WIDTH), row),
                  pl.BlockSpec((SELF_WIDTH, D_MODEL), const),
                  pl.BlockSpec((MEM_WIDTH, D_MODEL), lambda i: (SELF_WIDTH // MEM_WIDTH, 0)),
                  pl.BlockSpec((ROW_TILE, D_MODEL), row),
                  pl.BlockSpec((1, D_MODEL), const), pl.BlockSpec((1, D_MODEL), const)],
        out_specs=(pl.BlockSpec((ROW_TILE, D_MODEL), row), pl.BlockSpec((ROW_TILE, D_MODEL), row)),
        compiler_params=_params("parallel"),
        name="mix_ln",
    )(self_out, mem_out, w_mix, w_mix, h, g.reshape(1, D_MODEL), b.reshape(1, D_MODEL))


def _router_kernel(h_ref, whi_ref, wlo_ref, b_ref, e_ref, g_ref, r_ref, cnt_ref, run_ref):
    tm = ROUTER_TM

    @pl.when(pl.program_id(0) == 0)
    def _():
        run_ref[...] = jnp.zeros_like(run_ref)

    h = h_ref[...]
    h_hi = h.astype(jnp.bfloat16)
    h_lo = (h - h_hi.astype(jnp.float32)).astype(jnp.bfloat16)
    logits = (_dot(h_hi, whi_ref[...]) + _dot(h_lo, whi_ref[...]) + _dot(h_hi, wlo_ref[...])) + b_ref[...]
    lane = lax.broadcasted_iota(jnp.int32, (tm, LANES), 1).astype(jnp.float32)
    work = logits
    vals, idxs, hots = [], [], []
    for _ in range(TOP_K):
        m = jnp.max(work, axis=-1, keepdims=True)
        idx = jnp.min(jnp.where(work == m, lane, float(LANES)), axis=-1, keepdims=True)
        hot = lane == idx
        vals.append(m)
        idxs.append(idx)
        hots.append(hot)
        work = jnp.where(hot, -jnp.inf, work)
    exps = [jnp.exp(v - vals[0]) for v in vals]
    inv = 1.0 / (exps[0] + exps[1] + exps[2] + exps[3])
    cnt = jnp.zeros((tm, LANES), jnp.float32)
    for hot in hots:
        cnt = cnt + hot.astype(jnp.float32)
    row = lax.broadcasted_iota(jnp.int32, (tm, tm), 0)
    colm = lax.broadcasted_iota(jnp.int32, (tm, tm), 1)
    tri = (row > colm).astype(jnp.bfloat16)
    before = run_ref[...] + _dot(tri, cnt.astype(jnp.bfloat16))
    lane_i = lax.broadcasted_iota(jnp.int32, (tm, LANES), 1)
    e_out = jnp.zeros((tm, LANES), jnp.float32)
    g_out = jnp.zeros((tm, LANES), jnp.float32)
    r_out = jnp.zeros((tm, LANES), jnp.float32)
    for k in range(TOP_K):
        rank = jnp.sum(jnp.where(hots[k], before, 0.0), axis=-1, keepdims=True)
        e_out = jnp.where(lane_i == k, idxs[k], e_out)
        g_out = jnp.where(lane_i == k, exps[k] * inv, g_out)
        r_out = jnp.where(lane_i == k, rank, r_out)
    e_ref[...] = e_out.astype(jnp.int32)
    g_ref[...] = g_out
    r_ref[...] = r_out.astype(jnp.int32)
    run_ref[...] = run_ref[...] + jnp.sum(cnt, axis=0, keepdims=True)
    cnt_ref[...] = run_ref[...].astype(jnp.int32)


def _router(h, router_w, router_b):
    w = jnp.zeros((D_MODEL, LANES), jnp.float32).at[:, :N_EXPERTS].set(router_w)
    w_hi = w.astype(jnp.bfloat16)
    w_lo = (w - w_hi.astype(jnp.float32)).astype(jnp.bfloat16)
    b = jnp.full((1, LANES), NEG_INF, jnp.float32).at[0, :N_EXPERTS].set(router_b)
    tm = ROUTER_TM
    row = lambda i: (i, 0)
    const = lambda i: (0, 0)
    lanes_out = pl.BlockSpec((tm, LANES), row)
    e, g, r, cnt = pl.pallas_call(
        _router_kernel,
        out_shape=(jax.ShapeDtypeStruct((N_TOK, LANES), jnp.int32),
                   jax.ShapeDtypeStruct((N_TOK, LANES), jnp.float32),
                   jax.ShapeDtypeStruct((N_TOK, LANES), jnp.int32),
                   jax.ShapeDtypeStruct((1, LANES), jnp.int32)),
        grid=(N_TOK // tm,),
        in_specs=[pl.BlockSpec((tm, D_MODEL), row), pl.BlockSpec((D_MODEL, LANES), const),
                  pl.BlockSpec((D_MODEL, LANES), const), pl.BlockSpec((1, LANES), const)],
        out_specs=(lanes_out, lanes_out, lanes_out, pl.BlockSpec((1, LANES), const)),
        scratch_shapes=[pltpu.VMEM((1, LANES), jnp.float32)],
        compiler_params=_params("arbitrary"),
        name="router",
    )(h, w_hi, w_lo, b)
    return e[:, :TOP_K], g, r[:, :TOP_K], cnt[0, :N_EXPERTS]


def _route_tables(top_e, rank, cnt):
    ntile = (cnt + MOE_TILE - 1) // MOE_TILE
    tile_end = jnp.cumsum(ntile)
    tile_base = tile_end - ntile
    flat_e, flat_rank = top_e.reshape(-1), rank.reshape(-1)
    pos = (tile_base[flat_e] * MOE_TILE + flat_rank).astype(jnp.int32)
    n_tiles = tile_end[-1:].astype(jnp.int32)
    w = jnp.minimum(jnp.arange(MOE_TILES, dtype=jnp.int32), n_tiles[0] - 1)
    tile_e = jnp.minimum(jnp.searchsorted(tile_end, w, side="right"), N_EXPERTS - 1).astype(jnp.int32)
    rows = jnp.clip(cnt[tile_e] - (w - tile_base[tile_e]) * MOE_TILE, 0, MOE_TILE)
    nsub = ((rows + MOE_SUB - 1) // MOE_SUB).astype(jnp.int32)
    pad_start = (tile_base * MOE_TILE + cnt).astype(jnp.int32)
    pad_n = ((-cnt) % MOE_SUB).astype(jnp.int32)
    return pos, n_tiles, tile_e, nsub, pad_start, pad_n


def _row_copy(src_vmem, src_row, dst_hbm, dst_row, sem):
    return pltpu.make_async_copy(src_vmem.at[pl.ds(src_row, 1)], dst_hbm.at[pl.ds(dst_row, 1)], sem)


def _pack_bf16_pairs(x):
    half = x.shape[1] // 2
    lo = lax.bitcast_convert_type(x[:, :half].astype(jnp.float32), jnp.uint32)
    hi = lax.bitcast_convert_type(x[:, half:].astype(jnp.float32), jnp.uint32)
    return (lo >> 16) | (hi & jnp.uint32(0xFFFF0000))


def _unpack_bf16_pairs(words):
    lo = lax.bitcast_convert_type(words << 16, jnp.float32)
    hi = lax.bitcast_convert_type(words & jnp.uint32(0xFFFF0000), jnp.float32)
    return lo.astype(jnp.bfloat16), hi.astype(jnp.bfloat16)


def _dispatch_kernel(pos_ref, pad_start_ref, pad_n_ref, hb_ref, xs_hbm, h_ref, sem):
    step = pl.program_id(0)
    base = step * DISPATCH_TB
    h_ref[...] = _pack_bf16_pairs(hb_ref[...])

    def issue(r, carry):
        for k in range(TOP_K):
            _row_copy(h_ref, r, xs_hbm, pos_ref[(base + r) * TOP_K + k], sem).start()
        return carry

    lax.fori_loop(0, DISPATCH_TB, issue, 0, unroll=8)

    @pl.when(step == 0)
    def _():
        def per_expert(e, carry):
            start = pad_start_ref[e]
            n = pad_n_ref[e]

            def fill(i, c):
                _row_copy(h_ref, 0, xs_hbm, start + i, sem).start()
                return c

            def drain(_, c):
                _row_copy(h_ref, 0, xs_hbm, 0, sem).wait()
                return c

            lax.fori_loop(0, n, fill, 0)
            lax.fori_loop(0, n, drain, 0)
            return carry

        lax.fori_loop(0, N_EXPERTS, per_expert, 0)

    for _ in range(TOP_K):
        pltpu.make_async_copy(h_ref, xs_hbm.at[pl.ds(0, DISPATCH_TB)], sem).wait()


def _dispatch(hb, pos, pad_start, pad_n):
    return pl.pallas_call(
        _dispatch_kernel,
        out_shape=jax.ShapeDtypeStruct((MOE_ROWS, D_MODEL // 2), jnp.uint32),
        grid_spec=pltpu.PrefetchScalarGridSpec(
            num_scalar_prefetch=3,
            grid=(N_TOK // DISPATCH_TB,),
            in_specs=[pl.BlockSpec((DISPATCH_TB, D_MODEL), lambda i, p, s, n: (i, 0))],
            out_specs=pl.BlockSpec(memory_space=pl.ANY),
            scratch_shapes=[pltpu.VMEM((DISPATCH_TB, D_MODEL // 2), jnp.uint32),
                            pltpu.SemaphoreType.DMA(())],
        ),
        compiler_params=_params("arbitrary"),
        name="moe_dispatch",
    )(pos, pad_start, pad_n, hb)


def _moe_weight_map(layer, first, hidden_axis):
    def index_map(w, j, tile_e, nsub):
        return (layer, tile_e[w], first + j, 0) if hidden_axis == 2 else (layer, tile_e[w], 0, first + j)

    return index_map


def _moe_rows(nsub, accumulate):
    fast = nsub < 0
    for s in MOE_FAST_SUBS:
        fast = jnp.logical_or(fast, nsub == s)
        pl.when(nsub == s)(functools.partial(accumulate, slice(0, s * MOE_SUB)))

    @pl.when(jnp.logical_and(nsub > 0, jnp.logical_not(fast)))
    def _():
        def one(i, carry):
            accumulate(pl.ds(pl.multiple_of(i * MOE_SUB, MOE_SUB), MOE_SUB))
            return carry

        lax.fori_loop(0, nsub, one, 0)


def _moe_expert_kernel(exp_ref, nsub_ref, x_ref, wg_ref, wl_ref, bg_ref, bl_ref, wo_ref, bo_ref, o_ref):
    first = pl.program_id(1) == 0
    nsub = nsub_ref[pl.program_id(0)]
    half = D_MODEL // 2

    for i in range(MOE_SUBS):
        rows = slice(i * MOE_SUB, (i + 1) * MOE_SUB)

        @pl.when(jnp.logical_and(i < nsub, first))
        def _():
            o_ref[rows, :] = jnp.broadcast_to(bo_ref[...], (MOE_SUB, D_MODEL))

        @pl.when(jnp.logical_and(i >= nsub, first))
        def _():
            o_ref[rows, :] = jnp.zeros((MOE_SUB, D_MODEL), o_ref.dtype)

    def up(lo, hi, w_ref, b_ref):
        return (_dot(lo, w_ref[0:half, :].astype(jnp.bfloat16))
                + _dot(hi, w_ref[half:D_MODEL, :].astype(jnp.bfloat16)) + b_ref[...])

    def accumulate(rows):
        lo, hi = _unpack_bf16_pairs(x_ref[rows, :])
        gate = up(lo, hi, wg_ref, bg_ref)
        lin = up(lo, hi, wl_ref, bl_ref)
        gate = jnp.minimum(gate, SWIGLU_LIMIT)
        lin = jnp.clip(lin, -SWIGLU_LIMIT, SWIGLU_LIMIT)
        act = (gate * (1.0 / (1.0 + jnp.exp(-SWIGLU_ALPHA * gate))) * (lin + 1.0)).astype(jnp.bfloat16)
        for c in range(D_MODEL // MOE_DOWN_TN):
            cols = slice(c * MOE_DOWN_TN, (c + 1) * MOE_DOWN_TN)
            o_ref[rows, cols] = o_ref[rows, cols] + _dot(act, wo_ref[:, cols].astype(jnp.bfloat16))

    _moe_rows(nsub, accumulate)


def _moe_experts(xs, w_in, b_in, w_out, b_out, layer, n_tiles, tile_e, nsub):
    nj = D_EXPERT // MOE_TF
    b_in4 = b_in.reshape(DEPTH, N_EXPERTS, 1, 2 * D_EXPERT)
    b_out4 = b_out.reshape(DEPTH, N_EXPERTS, 1, D_MODEL)
    wblk = (None, None, D_MODEL, MOE_TF)
    bblk = (None, None, 1, MOE_TF)
    tile_map = lambda w, j, e, n: (w, 0)
    return pl.pallas_call(
        _moe_expert_kernel,
        out_shape=jax.ShapeDtypeStruct((MOE_ROWS, D_MODEL), jnp.float32),
        grid_spec=pltpu.PrefetchScalarGridSpec(
            num_scalar_prefetch=2,
            grid=(n_tiles[0], nj),
            in_specs=[pl.BlockSpec((MOE_TILE, D_MODEL // 2), tile_map),
                      pl.BlockSpec(wblk, _moe_weight_map(layer, 0, 3)),
                      pl.BlockSpec(wblk, _moe_weight_map(layer, nj, 3)),
                      pl.BlockSpec(bblk, _moe_weight_map(layer, 0, 3)),
                      pl.BlockSpec(bblk, _moe_weight_map(layer, nj, 3)),
                      pl.BlockSpec((None, None, MOE_TF, D_MODEL), _moe_weight_map(layer, 0, 2)),
                      pl.BlockSpec((None, None, 1, D_MODEL), lambda w, j, e, n: (layer, e[w], 0, 0))],
            out_specs=pl.BlockSpec((MOE_TILE, D_MODEL), tile_map),
        ),
        compiler_params=_params("arbitrary", "arbitrary", vmem_limit=MOE_VMEM_LIMIT),
        name="moe_experts",
    )(tile_e, nsub, xs, w_in, w_in, b_in4, b_in4, w_out, b_out4)


def _combine_ln_kernel(pos_ref, y_hbm, gates_ref, h_ref, g_ref, b_ref, hf_ref, hb_ref, ybuf_ref, sems):
    step = pl.program_id(0)
    slot = step % 2

    def gather(tile, into):
        base = tile * COMBINE_TB

        def issue(r, carry):
            for k in range(TOP_K):
                src = y_hbm.at[pl.ds(pos_ref[(base + r) * TOP_K + k], 1)]
                pltpu.make_async_copy(src, ybuf_ref.at[into, k, pl.ds(r, 1)], sems.at[into]).start()
            return carry

        lax.fori_loop(0, COMBINE_TB, issue, 0, unroll=8)

    @pl.when(step == 0)
    def _():
        gather(0, 0)

    @pl.when(step + 1 < pl.num_programs(0))
    def _():
        gather(step + 1, 1 - slot)

    for k in range(TOP_K):
        pltpu.make_async_copy(y_hbm.at[pl.ds(0, COMBINE_TB)], ybuf_ref.at[slot, k], sems.at[slot]).wait()

    gates = gates_ref[...]
    ffn = gates[:, 0:1] * ybuf_ref[slot, 0]
    for k in range(1, TOP_K):
        ffn = ffn + gates[:, k:k + 1] * ybuf_ref[slot, k]
    y = _layer_norm_rows(DEEPNORM_ALPHA * h_ref[...] + ffn, g_ref[...], b_ref[...])
    hf_ref[...] = y
    hb_ref[...] = y.astype(hb_ref.dtype)


def _combine_ln(y, pos, gates, h, g, b):
    tb = COMBINE_TB
    row = lambda i, p: (i, 0)
    const = lambda i, p: (0, 0)
    return pl.pallas_call(
        _combine_ln_kernel,
        out_shape=(jax.ShapeDtypeStruct((N_TOK, D_MODEL), jnp.float32),
                   jax.ShapeDtypeStruct((N_TOK, D_MODEL), jnp.bfloat16)),
        grid_spec=pltpu.PrefetchScalarGridSpec(
            num_scalar_prefetch=1,
            grid=(N_TOK // tb,),
            in_specs=[pl.BlockSpec(memory_space=pl.ANY),
                      pl.BlockSpec((tb, LANES), row),
                      pl.BlockSpec((tb, D_MODEL), row),
                      pl.BlockSpec((1, D_MODEL), const), pl.BlockSpec((1, D_MODEL), const)],
            out_specs=(pl.BlockSpec((tb, D_MODEL), row), pl.BlockSpec((tb, D_MODEL), row)),
            scratch_shapes=[pltpu.VMEM((2, TOP_K, tb, D_MODEL), jnp.float32),
                            pltpu.SemaphoreType.DMA((2,))],
        ),
        compiler_params=_params("arbitrary"),
        name="moe_combine_ln",
    )(pos, y, gates, h, g.reshape(1, D_MODEL), b.reshape(1, D_MODEL))


def _moe(h, hb, layer, router_w, router_b, w_in, b_in, w_out, b_out, ln_g, ln_b):
    top_e, gates, rank, cnt = _router(h, router_w[layer], router_b[layer])
    pos, n_tiles, tile_e, nsub, pad_start, pad_n = _route_tables(top_e, rank, cnt)
    xs = _dispatch(hb, pos, pad_start, pad_n)
    y = _moe_experts(xs, w_in, b_in, w_out, b_out, layer, n_tiles, tile_e, nsub)
    return _combine_ln(y, pos, gates, h, ln_g[layer], ln_b[layer])


def kernel(x, mem, w_in_dil, w_in_ret, ret_gn_g, w_mem_kv, w_mix_out, ln_mix_g, ln_mix_b, router_w, router_b, moe_w_in, moe_b_in, moe_w_out, moe_b_out, ln_ffn_g, ln_ffn_b):
    bf16 = jnp.bfloat16
    h = x.reshape(N_TOK, D_MODEL)
    hb = h
    memb = mem.reshape(BATCH * MEM_LEN, D_MODEL)
    for layer in range(DEPTH):
        slot = layer // 2
        if layer % 2 == 0:
            kb = SELF_WIDTH // GROUP_WIDTH
            proj = _matmul(hb, w_in_dil, slot, bf16, col_tiles=(0, kb, kb + 1))
            self_out = _dilated_attention(hb, w_in_dil, slot, proj)
        else:
            proj = _matmul(hb, w_in_ret, slot, bf16)
            self_out = _retention(proj, ret_gn_g[slot])
        memkv = _matmul(memb, w_mem_kv, layer, bf16)
        mem_out = _memory_attention(proj, memkv)
        h, hb = _mix_ln(self_out, mem_out, w_mix_out[layer].astype(bf16), h,
                        ln_mix_g[layer], ln_mix_b[layer])
        h, hb = _moe(h, hb, layer, router_w, router_b, moe_w_in, moe_b_in, moe_w_out, moe_b_out,
                     ln_ffn_g, ln_ffn_b)
    return h.reshape(BATCH, SEQ, D_MODEL)
```

```python
import functools
import math

import jax
import jax.numpy as jnp
from jax import lax
from jax.experimental import pallas as pl
from jax.experimental.pallas import tpu as pltpu

D_MODEL = 2048
BATCH = 2
SEQ = 4096
DEPTH = 2
HEAD_DIM = 128
DILATED_GROUPS = ((128, 1), (512, 4), (2048, 16))
HEADS_PER_GROUP = 4
N_SELF_HEADS = len(DILATED_GROUPS) * HEADS_PER_GROUP
N_RET_HEADS = 12
SELF_WIDTH = N_SELF_HEADS * HEAD_DIM
MEM_HEADS = 4
MEM_LEN = 256
MEM_WIDTH = MEM_HEADS * HEAD_DIM
RET_CHUNK = 128
N_EXPERTS = 32
TOP_K = 4
D_EXPERT = D_MODEL
SWIGLU_ALPHA = 1.702
SWIGLU_LIMIT = 7.0
DEEPNORM_ALPHA = (2 * DEPTH) ** 0.25
LN_EPS = 1e-5
NEG_INF = -1e30

N_TOK = BATCH * SEQ
GROUP_WIDTH = HEADS_PER_GROUP * HEAD_DIM
SELF_QKV = 3 * GROUP_WIDTH
ATTN_STEPS = 128
ATTN_QB = 2

LANES = 128
SUBLANES = 8
VMEM_LIMIT = 56 * 1024 * 1024

MM_TM = 1024
MM_TN = 512
ROW_TILE = 512
COMBINE_ROWS = 512
ROUTER_TM = 256

MOE_SUB = 64
MOE_TILE = 1152
MOE_SUBS = MOE_TILE // MOE_SUB
MOE_FAST_SUBS = (15, 16, 17, 18)
MOE_TF = 512
MOE_DOWN_TN = 512
MOE_VMEM_LIMIT = 60 * 1024 * 1024
MOE_TILES = (N_TOK * TOP_K) // MOE_TILE + N_EXPERTS
MOE_ROWS = MOE_TILES * MOE_TILE
DISPATCH_TB = 256
COMBINE_TB = 256


def _alibi_slopes(n):
    def pow2(m):
        start = 2.0 ** (-8.0 / m)
        return [start ** (i + 1) for i in range(m)]

    if math.log2(n).is_integer():
        s = pow2(n)
    else:
        c = 2 ** math.floor(math.log2(n))
        s = pow2(c) + pow2(2 * c)[0::2][: n - c]
    return sorted(s, reverse=True)


def _params(*sem, vmem_limit=VMEM_LIMIT):
    return pltpu.CompilerParams(dimension_semantics=sem, vmem_limit_bytes=vmem_limit)


def _layer_norm_rows(z, g, b):
    mu = jnp.mean(z, axis=-1, keepdims=True)
    zc = z - mu
    var = jnp.mean(zc * zc, axis=-1, keepdims=True)
    return zc * lax.rsqrt(var + LN_EPS) * g + b


def _dot_nt(a, b):
    return lax.dot_general(a, b, (((1,), (1,)), ((), ())), preferred_element_type=jnp.float32)


def _dot(a, b):
    return jnp.dot(a, b, preferred_element_type=jnp.float32)


def _mm_kernel(x_ref, w_ref, o_ref):
    o_ref[...] = _dot(x_ref[...].astype(jnp.bfloat16), w_ref[...].astype(jnp.bfloat16)).astype(o_ref.dtype)


def _matmul(x, w, layer, out_dtype, col_tiles=None):
    m, k = x.shape
    first, stride, count = col_tiles if col_tiles else (0, 1, w.shape[2] // MM_TN)
    tm = min(MM_TM * (4 // x.dtype.itemsize), m)
    return pl.pallas_call(
        _mm_kernel,
        out_shape=jax.ShapeDtypeStruct((m, count * MM_TN), out_dtype),
        grid=(m // tm, count),
        in_specs=[pl.BlockSpec((tm, k), lambda i, j: (i, 0)),
                  pl.BlockSpec((None, k, MM_TN), lambda i, j: (layer, 0, first + stride * j))],
        out_specs=pl.BlockSpec((tm, MM_TN), lambda i, j: (i, j)),
        compiler_params=_params("parallel", "parallel"),
        name="dense_matmul",
    )(x, w)


def _proj_residue_kernel(x_ref, wq_ref, wk_ref, wv_ref, o_ref, scr_ref, *, dilation):
    x = x_ref[...].astype(jnp.bfloat16)
    rows = MM_TM // dilation
    for section, w_ref in enumerate((wq_ref, wk_ref, wv_ref)):
        res = _dot(x, w_ref[...].astype(jnp.bfloat16))
        for c in range(GROUP_WIDTH // LANES):
            scr_ref[c] = res[:, c * LANES:(c + 1) * LANES]
        for r in range(dilation):
            for c in range(GROUP_WIDTH // LANES):
                col = r * SELF_QKV + section * GROUP_WIDTH + c * LANES
                o_ref[:, col:col + LANES] = scr_ref[c, pl.ds(r, rows, stride=dilation), :].astype(o_ref.dtype)


def _proj_residue_major(x, w, layer, group):
    _, dilation = DILATED_GROUPS[group]
    k = x.shape[1]
    kb = SELF_WIDTH // GROUP_WIDTH

    def wspec(section):
        return pl.BlockSpec((None, k, GROUP_WIDTH), lambda i: (layer, 0, section * kb + group))

    return pl.pallas_call(
        functools.partial(_proj_residue_kernel, dilation=dilation),
        out_shape=jax.ShapeDtypeStruct((N_TOK // dilation, dilation * SELF_QKV), jnp.bfloat16),
        grid=(N_TOK // MM_TM,),
        in_specs=[pl.BlockSpec((MM_TM, k), lambda i: (i, 0)), wspec(0), wspec(1), wspec(2)],
        out_specs=pl.BlockSpec((MM_TM // dilation, dilation * SELF_QKV), lambda i: (i, 0)),
        scratch_shapes=[pltpu.VMEM((GROUP_WIDTH // LANES, MM_TM, LANES), jnp.float32)],
        compiler_params=_params("parallel"),
        name=f"proj_residue_g{group}",
    )(x, w, w, w)


def _dil_attn_kernel(q_ref, kp_ref, kc_ref, vp_ref, vc_ref, o_ref, lse_ref, *, slopes, dilation):
    n = pl.program_id(2)
    heads = HEADS_PER_GROUP
    rows = ATTN_QB * heads * ATTN_STEPS
    head_slices = [slice(h * HEAD_DIM, (h + 1) * HEAD_DIM) for h in range(heads)]
    blocks = [slice(s * ATTN_STEPS, (s + 1) * ATTN_STEPS) for s in range(ATTN_QB)]
    row = lax.broadcasted_iota(jnp.int32, (rows, ATTN_STEPS), 0)
    kj = lax.broadcasted_iota(jnp.int32, (rows, ATTN_STEPS), 1)
    diff = (row % ATTN_STEPS) - kj
    valid_c = diff >= 0
    has_prev = jnp.logical_or(row >= heads * ATTN_STEPS, n > 0)
    valid_p = jnp.logical_and(diff <= 0, has_prev)
    slope = jnp.full((rows, ATTN_STEPS), slopes[0], jnp.float32)
    head_of_row = (row // ATTN_STEPS) % heads
    for h in range(1, heads):
        slope = jnp.where(head_of_row == h, slopes[h], slope)
    bias_c = slope * (diff * dilation).astype(jnp.float32)
    bias_p = slope * ((diff + ATTN_STEPS) * dilation).astype(jnp.float32)
    scale = HEAD_DIM ** -0.5

    def prev_cur(cur_ref, prev_ref, s, sl):
        prev = prev_ref[:, sl] if s == 0 else cur_ref[blocks[s - 1], sl]
        return prev, cur_ref[blocks[s], sl]

    pairs = [(s, sl) for s in range(ATTN_QB) for sl in head_slices]
    s_c = jnp.concatenate([_dot_nt(q_ref[blocks[s], sl], prev_cur(kc_ref, kp_ref, s, sl)[1])
                           for s, sl in pairs], axis=0)
    s_p = jnp.concatenate([_dot_nt(q_ref[blocks[s], sl], prev_cur(kc_ref, kp_ref, s, sl)[0])
                           for s, sl in pairs], axis=0)
    s_c = jnp.where(valid_c, s_c * scale - bias_c, NEG_INF)
    s_p = jnp.where(valid_p, s_p * scale - bias_p, NEG_INF)
    m = jnp.max(jnp.maximum(s_c, s_p), axis=-1, keepdims=True)
    e_c = jnp.exp(s_c - m)
    e_p = jnp.exp(s_p - m)
    l = jnp.sum(e_c + e_p, axis=-1, keepdims=True)
    inv_l = 1.0 / l
    p_c = (e_c * inv_l).astype(jnp.bfloat16)
    p_p = (e_p * inv_l).astype(jnp.bfloat16)
    lse = m + jnp.log(l)
    for i, (s, sl) in enumerate(pairs):
        chain = slice(i * ATTN_STEPS, (i + 1) * ATTN_STEPS)
        v_prev, v_cur = prev_cur(vc_ref, vp_ref, s, sl)
        o_ref[blocks[s], sl] = _dot(p_c[chain], v_cur) + _dot(p_p[chain], v_prev)
        lse_ref[blocks[s], sl] = jnp.broadcast_to(lse[chain], (ATTN_STEPS, HEAD_DIM))


def _dilated_group(src, group):
    _, dilation = DILATED_GROUPS[group]
    length = SEQ // dilation
    span = ATTN_QB * ATTN_STEPS
    cb = src.shape[1] // dilation // GROUP_WIDTH
    view = src.reshape(BATCH, length, src.shape[1])
    blk = (None, span, GROUP_WIDTH)
    prev_blk = (None, ATTN_STEPS, GROUP_WIDTH)

    def col(section):
        return lambda b, r, n: (b, n, r * cb + section)

    def col_prev(section):
        return lambda b, r, n: (b, jnp.maximum(n * ATTN_QB - 1, 0), r * cb + section)

    slopes = tuple(_alibi_slopes(N_SELF_HEADS)[group * HEADS_PER_GROUP:(group + 1) * HEADS_PER_GROUP])
    out_shape = jax.ShapeDtypeStruct((BATCH, length, dilation * GROUP_WIDTH), jnp.float32)
    out_spec = pl.BlockSpec(blk, lambda b, r, n: (b, n, r))
    o, lse = pl.pallas_call(
        functools.partial(_dil_attn_kernel, slopes=slopes, dilation=dilation),
        out_shape=(out_shape, out_shape),
        grid=(BATCH, dilation, length // span),
        in_specs=[pl.BlockSpec(blk, col(0)),
                  pl.BlockSpec(prev_blk, col_prev(1)), pl.BlockSpec(blk, col(1)),
                  pl.BlockSpec(prev_blk, col_prev(2)), pl.BlockSpec(blk, col(2))],
        out_specs=(out_spec, out_spec),
        compiler_params=_params("parallel", "parallel", "arbitrary"),
        name=f"dilated_attn_g{group}",
    )(view, view, view, view, view)
    return o, lse


def _dil_combine_kernel(*refs):
    ngroups = len(DILATED_GROUPS)
    o_refs, l_refs, out_ref = refs[:ngroups], refs[ngroups:2 * ngroups], refs[2 * ngroups]
    scratch = list(refs[2 * ngroups + 1:])
    os, ls = [], []
    for g, (_, dilation) in enumerate(DILATED_GROUPS):
        if dilation == 1:
            os.append(o_refs[g][...])
            ls.append(l_refs[g][...])
            continue
        rows = COMBINE_ROWS // dilation
        planes = GROUP_WIDTH // LANES
        o_scr, l_scr = scratch.pop(0), scratch.pop(0)
        for r in range(dilation):
            for c in range(planes):
                cols = slice(r * GROUP_WIDTH + c * LANES, r * GROUP_WIDTH + (c + 1) * LANES)
                o_scr[c, pl.ds(r, rows, stride=dilation), :] = o_refs[g][:, cols]
                l_scr[c, pl.ds(r, rows, stride=dilation), :] = l_refs[g][:, cols]
        os.append(jnp.concatenate([o_scr[c] for c in range(planes)], axis=1))
        ls.append(jnp.concatenate([l_scr[c] for c in range(planes)], axis=1))
    m = functools.reduce(jnp.maximum, ls)
    es = [jnp.exp(l - m) for l in ls]
    inv = 1.0 / functools.reduce(lambda a, b: a + b, es)
    for g in range(ngroups):
        out_ref[:, g * GROUP_WIDTH:(g + 1) * GROUP_WIDTH] = (os[g] * (es[g] * inv)).astype(out_ref.dtype)


def _dilated_attention(x, w_in, layer, proj):
    sources = [proj] + [_proj_residue_major(x, w_in, layer, g) for g in range(1, len(DILATED_GROUPS))]
    outs, lses = zip(*[_dilated_group(src, g) for g, src in enumerate(sources)])

    def spec(dilation):
        return pl.BlockSpec((COMBINE_ROWS // dilation, dilation * GROUP_WIDTH), lambda i: (i, 0))

    def flat(a):
        return a.reshape(a.shape[0] * a.shape[1], a.shape[2])

    specs = [spec(d) for _, d in DILATED_GROUPS]
    scratch = [pltpu.VMEM((GROUP_WIDTH // LANES, COMBINE_ROWS, LANES), jnp.float32)
               for _, d in DILATED_GROUPS if d > 1 for _ in range(2)]
    return pl.pallas_call(
        _dil_combine_kernel,
        out_shape=jax.ShapeDtypeStruct((N_TOK, SELF_WIDTH), jnp.bfloat16),
        grid=(N_TOK // COMBINE_ROWS,),
        in_specs=specs + specs,
        out_specs=pl.BlockSpec((COMBINE_ROWS, SELF_WIDTH), lambda i: (i, 0)),
        scratch_shapes=scratch,
        compiler_params=_params("parallel"),
        name="dilated_combine",
    )(*[flat(o) for o in outs], *[flat(l) for l in lses])


def _retention_kernel(q_ref, k_ref, v_ref, g_ref, dmat_ref, kdec_ref, qdec_ref, cdec_ref, gn_ref,
                      o_ref, state_ref):
    @pl.when(pl.program_id(1) == 0)
    def _():
        state_ref[...] = jnp.zeros_like(state_ref)

    heads = range(N_RET_HEADS)
    cols = [slice(h * HEAD_DIM, (h + 1) * HEAD_DIM) for h in heads]
    bf16 = jnp.bfloat16
    scores = [_dot_nt(q_ref[:, c], k_ref[:, c]) * dmat_ref[h] for h, c in zip(heads, cols)]
    states = [state_ref[h] for h in heads]
    cross = [_dot(q_ref[:, c], states[h].astype(bf16)) * qdec_ref[:, c] for h, c in zip(heads, cols)]
    intra = [_dot(scores[h].astype(bf16), v_ref[:, c]) for h, c in zip(heads, cols)]
    kw = [(k_ref[:, c].astype(jnp.float32) * kdec_ref[:, c]).astype(bf16) for c in cols]
    for h, c in zip(heads, cols):
        kv = lax.dot_general(kw[h], v_ref[:, c], (((0,), (0,)), ((), ())), preferred_element_type=jnp.float32)
        state_ref[h] = states[h] * cdec_ref[h] + kv
    r = jnp.concatenate([intra[h] + cross[h] for h in heads], axis=0)
    mu = jnp.mean(r, axis=-1, keepdims=True)
    rc = r - mu
    var = jnp.mean(rc * rc, axis=-1, keepdims=True)
    rn = rc * lax.rsqrt(var + LN_EPS)
    for h, c in zip(heads, cols):
        gate = g_ref[:, c].astype(jnp.float32)
        normed = rn[h * RET_CHUNK:(h + 1) * RET_CHUNK] * gn_ref[:, c]
        o_ref[:, c] = (gate * (1.0 / (1.0 + jnp.exp(-gate))) * normed).astype(o_ref.dtype)


def _retention(proj, gn_gain):
    c = RET_CHUNK
    log_gamma = jnp.log1p(-jnp.exp2(-(5.0 + jnp.arange(N_RET_HEADS, dtype=jnp.float32))))
    idx = jnp.arange(c, dtype=jnp.float32)
    diff = idx[:, None] - idx[None, :]
    decay = jnp.where(diff >= 0, jnp.exp(jnp.maximum(diff, 0.0)[None] * log_gamma[:, None, None]), 0.0)
    scale = HEAD_DIM ** -0.5
    dmat = decay * scale
    k_decay = jnp.exp((c - 1 - idx)[:, None] * log_gamma[None, :]) * scale
    q_decay = jnp.exp((idx + 1.0)[:, None] * log_gamma[None, :])
    kdec = jnp.repeat(k_decay, HEAD_DIM, axis=1)
    qdec = jnp.repeat(q_decay, HEAD_DIM, axis=1)
    cdec = jnp.broadcast_to(jnp.exp(c * log_gamma)[:, None, None], (N_RET_HEADS, 1, HEAD_DIM))
    gn = gn_gain.reshape(1, SELF_WIDTH).astype(jnp.float32)
    nchunk = SEQ // c
    blk = (c, SELF_WIDTH)

    def section(s):
        return pl.BlockSpec(blk, lambda b, n: (b * nchunk + n, s))

    const2 = lambda b, n: (0, 0)
    const3 = lambda b, n: (0, 0, 0)
    return pl.pallas_call(
        _retention_kernel,
        out_shape=jax.ShapeDtypeStruct((N_TOK, SELF_WIDTH), jnp.bfloat16),
        grid=(BATCH, nchunk),
        in_specs=[section(0), section(1), section(2), section(3),
                  pl.BlockSpec((N_RET_HEADS, c, c), const3),
                  pl.BlockSpec(blk, const2), pl.BlockSpec(blk, const2),
                  pl.BlockSpec((N_RET_HEADS, 1, HEAD_DIM), const3),
                  pl.BlockSpec((1, SELF_WIDTH), const2)],
        out_specs=pl.BlockSpec(blk, lambda b, n: (b * nchunk + n, 0)),
        scratch_shapes=[pltpu.VMEM((N_RET_HEADS, HEAD_DIM, HEAD_DIM), jnp.float32)],
        compiler_params=_params("parallel", "arbitrary"),
        name="retention",
    )(proj, proj, proj, proj, dmat, kdec, qdec, cdec, gn)


def _mem_attn_kernel(q_ref, k_ref, v_ref, o_ref):
    scale = HEAD_DIM ** -0.5
    for h in range(MEM_HEADS):
        sl = slice(h * HEAD_DIM, (h + 1) * HEAD_DIM)
        s = _dot_nt(q_ref[:, sl], k_ref[:, sl]) * scale
        e = jnp.exp(s - jnp.max(s, axis=-1, keepdims=True))
        p = e * (1.0 / jnp.sum(e, axis=-1, keepdims=True))
        o_ref[:, sl] = _dot(p.astype(jnp.bfloat16), v_ref[:, sl]).astype(o_ref.dtype)


def _memory_attention(proj, memkv):
    qcol = proj.shape[1] // MEM_WIDTH - 1
    per_b = SEQ // ROW_TILE
    return pl.pallas_call(
        _mem_attn_kernel,
        out_shape=jax.ShapeDtypeStruct((N_TOK, MEM_WIDTH), jnp.bfloat16),
        grid=(BATCH, per_b),
        in_specs=[pl.BlockSpec((ROW_TILE, MEM_WIDTH), lambda b, i: (b * per_b + i, qcol)),
                  pl.BlockSpec((MEM_LEN, MEM_WIDTH), lambda b, i: (b, 0)),
                  pl.BlockSpec((MEM_LEN, MEM_WIDTH), lambda b, i: (b, 1))],
        out_specs=pl.BlockSpec((ROW_TILE, MEM_WIDTH), lambda b, i: (b * per_b + i, 0)),
        compiler_params=_params("parallel", "parallel"),
        name="memory_attn",
    )(proj, memkv, memkv)


def _mix_ln_kernel(so_ref, mo_ref, wt_ref, wb_ref, h_ref, g_ref, b_ref, hf_ref, hb_ref):
    half = ROW_TILE // 2
    for rows in (slice(0, half), slice(half, ROW_TILE)):
        mix = _dot(so_ref[rows, :], wt_ref[...]) + _dot(mo_ref[rows, :], wb_ref[...])
        y = _layer_norm_rows(DEEPNORM_ALPHA * h_ref[rows, :] + mix, g_ref[...], b_ref[...])
        hf_ref[rows, :] = y
        hb_ref[rows, :] = y.astype(hb_ref.dtype)


def _mix_ln(self_out, mem_out, w_mix, h, g, b):
    row = lambda i: (i, 0)
    const = lambda i: (0, 0)
    return pl.pallas_call(
        _mix_ln_kernel,
        out_shape=(jax.ShapeDtypeStruct((N_TOK, D_MODEL), jnp.float32),
                   jax.ShapeDtypeStruct((N_TOK, D_MODEL), jnp.bfloat16)),
        grid=(N_TOK // ROW_TILE,),
        in_specs=[pl.BlockSpec((ROW_TILE, SELF_WIDTH), row),
                  pl.BlockSpec((ROW_TILE, MEM_WIDTH), row),
                  pl.BlockSpec((SELF_WIDTH, D_MODEL), const),
                  pl.BlockSpec((MEM_WIDTH, D_MODEL), lambda i: (SELF_WIDTH // MEM_WIDTH, 0)),
                  pl.BlockSpec((ROW_TILE, D_MODEL), row),
                  pl.BlockSpec((1, D_MODEL), const), pl.BlockSpec((1, D_MODEL), const)],
        out_specs=(pl.BlockSpec((ROW_TILE, D_MODEL), row), pl.BlockSpec((ROW_TILE, D_MODEL), row)),
        compiler_params=_params("parallel"),
        name="mix_ln",
    )(self_out, mem_out, w_mix, w_mix, h, g.reshape(1, D_MODEL), b.reshape(1, D_MODEL))


def _router_kernel(h_ref, whi_ref, wlo_ref, b_ref, e_ref, g_ref, r_ref, cnt_ref, run_ref):
    tm = ROUTER_TM

    @pl.when(pl.program_id(0) == 0)
    def _():
        run_ref[...] = jnp.zeros_like(run_ref)

    h = h_ref[...]
    h_hi = h.astype(jnp.bfloat16)
    h_lo = (h - h_hi.astype(jnp.float32)).astype(jnp.bfloat16)
    logits = (_dot(h_hi, whi_ref[...]) + _dot(h_lo, whi_ref[...]) + _dot(h_hi, wlo_ref[...])) + b_ref[...]
    lane = lax.broadcasted_iota(jnp.int32, (tm, LANES), 1).astype(jnp.float32)
    work = logits
    vals, idxs, hots = [], [], []
    for _ in range(TOP_K):
        m = jnp.max(work, axis=-1, keepdims=True)
        idx = jnp.min(jnp.where(work == m, lane, float(LANES)), axis=-1, keepdims=True)
        hot = lane == idx
        vals.append(m)
        idxs.append(idx)
        hots.append(hot)
        work = jnp.where(hot, -jnp.inf, work)
    exps = [jnp.exp(v - vals[0]) for v in vals]
    inv = 1.0 / (exps[0] + exps[1] + exps[2] + exps[3])
    cnt = jnp.zeros((tm, LANES), jnp.float32)
    for hot in hots:
        cnt = cnt + hot.astype(jnp.float32)
    row = lax.broadcasted_iota(jnp.int32, (tm, tm), 0)
    colm = lax.broadcasted_iota(jnp.int32, (tm, tm), 1)
    tri = (row > colm).astype(jnp.bfloat16)
    before = run_ref[...] + _dot(tri, cnt.astype(jnp.bfloat16))
    lane_i = lax.broadcasted_iota(jnp.int32, (tm, LANES), 1)
    e_out = jnp.zeros((tm, LANES), jnp.float32)
    g_out = jnp.zeros((tm, LANES), jnp.float32)
    r_out = jnp.zeros((tm, LANES), jnp.float32)
    for k in range(TOP_K):
        rank = jnp.sum(jnp.where(hots[k], before, 0.0), axis=-1, keepdims=True)
        e_out = jnp.where(lane_i == k, idxs[k], e_out)
        g_out = jnp.where(lane_i == k, exps[k] * inv, g_out)
        r_out = jnp.where(lane_i == k, rank, r_out)
    e_ref[...] = e_out.T[0:SUBLANES, :].astype(jnp.int32)
    g_ref[...] = g_out
    r_ref[...] = r_out.T[0:SUBLANES, :].astype(jnp.int32)
    run_ref[...] = run_ref[...] + jnp.sum(cnt, axis=0, keepdims=True)
    cnt_ref[...] = run_ref[...].astype(jnp.int32)


def _router(h, router_w, router_b):
    w = jnp.zeros((D_MODEL, LANES), jnp.float32).at[:, :N_EXPERTS].set(router_w)
    w_hi = w.astype(jnp.bfloat16)
    w_lo = (w - w_hi.astype(jnp.float32)).astype(jnp.bfloat16)
    b = jnp.full((1, LANES), NEG_INF, jnp.float32).at[0, :N_EXPERTS].set(router_b)
    tm = ROUTER_TM
    row = lambda i: (i, 0)
    const = lambda i: (0, 0)
    lanes_out = pl.BlockSpec((tm, LANES), row)
    slots_out = pl.BlockSpec((SUBLANES, tm), lambda i: (0, i))
    e, g, r, cnt = pl.pallas_call(
        _router_kernel,
        out_shape=(jax.ShapeDtypeStruct((SUBLANES, N_TOK), jnp.int32),
                   jax.ShapeDtypeStruct((N_TOK, LANES), jnp.float32),
                   jax.ShapeDtypeStruct((SUBLANES, N_TOK), jnp.int32),
                   jax.ShapeDtypeStruct((1, LANES), jnp.int32)),
        grid=(N_TOK // tm,),
        in_specs=[pl.BlockSpec((tm, D_MODEL), row), pl.BlockSpec((D_MODEL, LANES), const),
                  pl.BlockSpec((D_MODEL, LANES), const), pl.BlockSpec((1, LANES), const)],
        out_specs=(slots_out, lanes_out, slots_out, pl.BlockSpec((1, LANES), const)),
        scratch_shapes=[pltpu.VMEM((1, LANES), jnp.float32)],
        compiler_params=_params("arbitrary"),
        name="router",
    )(h, w_hi, w_lo, b)
    return e[:TOP_K], g, r[:TOP_K], cnt[0, :N_EXPERTS]


def _route_tables(top_e, rank, cnt):
    ntile = (cnt + MOE_TILE - 1) // MOE_TILE
    tile_end = jnp.cumsum(ntile)
    tile_base = tile_end - ntile
    flat_e, flat_rank = top_e.reshape(-1), rank.reshape(-1)
    pos = (tile_base[flat_e] * MOE_TILE + flat_rank).astype(jnp.int32)
    n_tiles = tile_end[-1:].astype(jnp.int32)
    w = jnp.minimum(jnp.arange(MOE_TILES, dtype=jnp.int32), n_tiles[0] - 1)
    tile_e = jnp.minimum(jnp.searchsorted(tile_end, w, side="right"), N_EXPERTS - 1).astype(jnp.int32)
    rows = jnp.clip(cnt[tile_e] - (w - tile_base[tile_e]) * MOE_TILE, 0, MOE_TILE)
    nsub = ((rows + MOE_SUB - 1) // MOE_SUB).astype(jnp.int32)
    pad_start = (tile_base * MOE_TILE + cnt).astype(jnp.int32)
    pad_n = ((-cnt) % MOE_SUB).astype(jnp.int32)
    return pos, n_tiles, tile_e, nsub, pad_start, pad_n


def _row_copy(src_vmem, src_row, dst_hbm, dst_row, sem):
    return pltpu.make_async_copy(src_vmem.at[pl.ds(src_row, 1)], dst_hbm.at[pl.ds(dst_row, 1)], sem)


def _pack_bf16_pairs(x):
    half = x.shape[1] // 2
    lo = lax.bitcast_convert_type(x[:, :half].astype(jnp.float32), jnp.uint32)
    hi = lax.bitcast_convert_type(x[:, half:].astype(jnp.float32), jnp.uint32)
    return (lo >> 16) | (hi & jnp.uint32(0xFFFF0000))


def _unpack_bf16_pairs(words):
    lo = lax.bitcast_convert_type(words << 16, jnp.float32)
    hi = lax.bitcast_convert_type(words & jnp.uint32(0xFFFF0000), jnp.float32)
    return lo.astype(jnp.bfloat16), hi.astype(jnp.bfloat16)


def _dispatch_kernel(pos_ref, pad_start_ref, pad_n_ref, hb_ref, xs_hbm, h_ref, sem):
    step = pl.program_id(0)
    base = step * DISPATCH_TB
    h_ref[...] = _pack_bf16_pairs(hb_ref[...])

    def issue(r, carry):
        for k in range(TOP_K):
            _row_copy(h_ref, r, xs_hbm, pos_ref[k * N_TOK + base + r], sem).start()
        return carry

    lax.fori_loop(0, DISPATCH_TB, issue, 0, unroll=8)

    @pl.when(step == 0)
    def _():
        def per_expert(e, carry):
            start = pad_start_ref[e]
            n = pad_n_ref[e]

            def fill(i, c):
                _row_copy(h_ref, 0, xs_hbm, start + i, sem).start()
                return c

            def drain(_, c):
                _row_copy(h_ref, 0, xs_hbm, 0, sem).wait()
                return c

            lax.fori_loop(0, n, fill, 0)
            lax.fori_loop(0, n, drain, 0)
            return carry

        lax.fori_loop(0, N_EXPERTS, per_expert, 0)

    for _ in range(TOP_K):
        pltpu.make_async_copy(h_ref, xs_hbm.at[pl.ds(0, DISPATCH_TB)], sem).wait()


def _dispatch(hb, pos, pad_start, pad_n):
    return pl.pallas_call(
        _dispatch_kernel,
        out_shape=jax.ShapeDtypeStruct((MOE_ROWS, D_MODEL // 2), jnp.uint32),
        grid_spec=pltpu.PrefetchScalarGridSpec(
            num_scalar_prefetch=3,
            grid=(N_TOK // DISPATCH_TB,),
            in_specs=[pl.BlockSpec((DISPATCH_TB, D_MODEL), lambda i, p, s, n: (i, 0))],
            out_specs=pl.BlockSpec(memory_space=pl.ANY),
            scratch_shapes=[pltpu.VMEM((DISPATCH_TB, D_MODEL // 2), jnp.uint32),
                            pltpu.SemaphoreType.DMA(())],
        ),
        compiler_params=_params("arbitrary"),
        name="moe_dispatch",
    )(pos, pad_start, pad_n, hb)


def _moe_weight_map(layer, first, hidden_axis):
    def index_map(w, j, tile_e, nsub):
        return (layer, tile_e[w], first + j, 0) if hidden_axis == 2 else (layer, tile_e[w], 0, first + j)

    return index_map


def _moe_rows(nsub, accumulate):
    fast = nsub < 0
    for s in MOE_FAST_SUBS:
        fast = jnp.logical_or(fast, nsub == s)
        pl.when(nsub == s)(functools.partial(accumulate, slice(0, s * MOE_SUB)))

    @pl.when(jnp.logical_and(nsub > 0, jnp.logical_not(fast)))
    def _():
        def one(i, carry):
            accumulate(pl.ds(pl.multiple_of(i * MOE_SUB, MOE_SUB), MOE_SUB))
            return carry

        lax.fori_loop(0, nsub, one, 0)


def _moe_expert_kernel(exp_ref, nsub_ref, x_ref, wg_ref, wl_ref, bg_ref, bl_ref, wo_ref, bo_ref, o_ref):
    first = pl.program_id(1) == 0
    nsub = nsub_ref[pl.program_id(0)]
    half = D_MODEL // 2

    for i in range(MOE_SUBS):
        rows = slice(i * MOE_SUB, (i + 1) * MOE_SUB)

        @pl.when(jnp.logical_and(i < nsub, first))
        def _():
            o_ref[rows, :] = jnp.broadcast_to(bo_ref[...], (MOE_SUB, D_MODEL))

        @pl.when(jnp.logical_and(i >= nsub, first))
        def _():
            o_ref[rows, :] = jnp.zeros((MOE_SUB, D_MODEL), o_ref.dtype)

    def up(lo, hi, w_ref, b_ref):
        return (_dot(lo, w_ref[0:half, :].astype(jnp.bfloat16))
                + _dot(hi, w_ref[half:D_MODEL, :].astype(jnp.bfloat16)) + b_ref[...])

    def accumulate(rows):
        lo, hi = _unpack_bf16_pairs(x_ref[rows, :])
        gate = up(lo, hi, wg_ref, bg_ref)
        lin = up(lo, hi, wl_ref, bl_ref)
        gate = jnp.minimum(gate, SWIGLU_LIMIT)
        lin = jnp.clip(lin, -SWIGLU_LIMIT, SWIGLU_LIMIT)
        act = (gate * (1.0 / (1.0 + jnp.exp(-SWIGLU_ALPHA * gate))) * (lin + 1.0)).astype(jnp.bfloat16)
        for c in range(D_MODEL // MOE_DOWN_TN):
            cols = slice(c * MOE_DOWN_TN, (c + 1) * MOE_DOWN_TN)
            o_ref[rows, cols] = o_ref[rows, cols] + _dot(act, wo_ref[:, cols].astype(jnp.bfloat16))

    _moe_rows(nsub, accumulate)


def _moe_experts(xs, w_in, b_in, w_out, b_out, layer, n_tiles, tile_e, nsub):
    nj = D_EXPERT // MOE_TF
    b_in4 = b_in.reshape(DEPTH, N_EXPERTS, 1, 2 * D_EXPERT)
    b_out4 = b_out.reshape(DEPTH, N_EXPERTS, 1, D_MODEL)
    wblk = (None, None, D_MODEL, MOE_TF)
    bblk = (None, None, 1, MOE_TF)
    tile_map = lambda w, j, e, n: (w, 0)
    return pl.pallas_call(
        _moe_expert_kernel,
        out_shape=jax.ShapeDtypeStruct((MOE_ROWS, D_MODEL), jnp.float32),
        grid_spec=pltpu.PrefetchScalarGridSpec(
            num_scalar_prefetch=2,
            grid=(n_tiles[0], nj),
            in_specs=[pl.BlockSpec((MOE_TILE, D_MODEL // 2), tile_map),
                      pl.BlockSpec(wblk, _moe_weight_map(layer, 0, 3)),
                      pl.BlockSpec(wblk, _moe_weight_map(layer, nj, 3)),
                      pl.BlockSpec(bblk, _moe_weight_map(layer, 0, 3)),
                      pl.BlockSpec(bblk, _moe_weight_map(layer, nj, 3)),
                      pl.BlockSpec((None, None, MOE_TF, D_MODEL), _moe_weight_map(layer, 0, 2)),
                      pl.BlockSpec((None, None, 1, D_MODEL), lambda w, j, e, n: (layer, e[w], 0, 0))],
            out_specs=pl.BlockSpec((MOE_TILE, D_MODEL), tile_map),
        ),
        compiler_params=_params("arbitrary", "arbitrary", vmem_limit=MOE_VMEM_LIMIT),
        name="moe_experts",
    )(tile_e, nsub, xs, w_in, w_in, b_in4, b_in4, w_out, b_out4)


def _combine_ln_kernel(pos_ref, y_hbm, gates_ref, h_ref, g_ref, b_ref, hf_ref, hb_ref, ybuf_ref, sems):
    step = pl.program_id(0)
    slot = step % 2

    def gather(tile, into):
        base = tile * COMBINE_TB

        def issue(r, carry):
            for k in range(TOP_K):
                src = y_hbm.at[pl.ds(pos_ref[k * N_TOK + base + r], 1)]
                pltpu.make_async_copy(src, ybuf_ref.at[into, k, pl.ds(r, 1)], sems.at[into]).start()
            return carry

        lax.fori_loop(0, COMBINE_TB, issue, 0, unroll=8)

    @pl.when(step == 0)
    def _():
        gather(0, 0)

    @pl.when(step + 1 < pl.num_programs(0))
    def _():
        gather(step + 1, 1 - slot)

    for k in range(TOP_K):
        pltpu.make_async_copy(y_hbm.at[pl.ds(0, COMBINE_TB)], ybuf_ref.at[slot, k], sems.at[slot]).wait()

    gates = gates_ref[...]
    ffn = gates[:, 0:1] * ybuf_ref[slot, 0]
    for k in range(1, TOP_K):
        ffn = ffn + gates[:, k:k + 1] * ybuf_ref[slot, k]
    y = _layer_norm_rows(DEEPNORM_ALPHA * h_ref[...] + ffn, g_ref[...], b_ref[...])
    hf_ref[...] = y
    hb_ref[...] = y.astype(hb_ref.dtype)


def _combine_ln(y, pos, gates, h, g, b):
    tb = COMBINE_TB
    row = lambda i, p: (i, 0)
    const = lambda i, p: (0, 0)
    return pl.pallas_call(
        _combine_ln_kernel,
        out_shape=(jax.ShapeDtypeStruct((N_TOK, D_MODEL), jnp.float32),
                   jax.ShapeDtypeStruct((N_TOK, D_MODEL), jnp.bfloat16)),
        grid_spec=pltpu.PrefetchScalarGridSpec(
            num_scalar_prefetch=1,
            grid=(N_TOK // tb,),
            in_specs=[pl.BlockSpec(memory_space=pl.ANY),
                      pl.BlockSpec((tb, LANES), row),
                      pl.BlockSpec((tb, D_MODEL), row),
                      pl.BlockSpec((1, D_MODEL), const), pl.BlockSpec((1, D_MODEL), const)],
            out_specs=(pl.BlockSpec((tb, D_MODEL), row), pl.BlockSpec((tb, D_MODEL), row)),
            scratch_shapes=[pltpu.VMEM((2, TOP_K, tb, D_MODEL), jnp.float32),
                            pltpu.SemaphoreType.DMA((2,))],
        ),
        compiler_params=_params("arbitrary"),
        name="moe_combine_ln",
    )(pos, y, gates, h, g.reshape(1, D_MODEL), b.reshape(1, D_MODEL))


def _moe(h, hb, layer, router_w, router_b, w_in, b_in, w_out, b_out, ln_g, ln_b):
    top_e, gates, rank, cnt = _router(h, router_w[layer], router_b[layer])
    pos, n_tiles, tile_e, nsub, pad_start, pad_n = _route_tables(top_e, rank, cnt)
    xs = _dispatch(hb, pos, pad_start, pad_n)
    y = _moe_experts(xs, w_in, b_in, w_out, b_out, layer, n_tiles, tile_e, nsub)
    return _combine_ln(y, pos, gates, h, ln_g[layer], ln_b[layer])


def kernel(x, mem, w_in_dil, w_in_ret, ret_gn_g, w_mem_kv, w_mix_out, ln_mix_g, ln_mix_b, router_w, router_b, moe_w_in, moe_b_in, moe_w_out, moe_b_out, ln_ffn_g, ln_ffn_b):
    bf16 = jnp.bfloat16
    h = x.reshape(N_TOK, D_MODEL)
    hb = h
    memb = mem.reshape(BATCH * MEM_LEN, D_MODEL)
    for layer in range(DEPTH):
        slot = layer // 2
        if layer % 2 == 0:
            kb = SELF_WIDTH // GROUP_WIDTH
            proj = _matmul(hb, w_in_dil, slot, bf16, col_tiles=(0, kb, kb + 1))
            self_out = _dilated_attention(hb, w_in_dil, slot, proj)
        else:
            proj = _matmul(hb, w_in_ret, slot, bf16)
            self_out = _retention(proj, ret_gn_g[slot])
        memkv = _matmul(memb, w_mem_kv, layer, bf16)
        mem_out = _memory_attention(proj, memkv)
        h, hb = _mix_ln(self_out, mem_out, w_mix_out[layer].astype(bf16), h,
                        ln_mix_g[layer], ln_mix_b[layer])
        h, hb = _moe(h, hb, layer, router_w, router_b, moe_w_in, moe_b_in, moe_w_out, moe_b_out,
                     ln_ffn_g, ln_ffn_b)
    return h.reshape(BATCH, SEQ, D_MODEL)
```

```python
import functools
import math

import jax
import jax.numpy as jnp
from jax import lax
from jax.experimental import pallas as pl
from jax.experimental.pallas import tpu as pltpu

D_MODEL = 2048
BATCH = 2
SEQ = 4096
DEPTH = 2
HEAD_DIM = 128
DILATED_GROUPS = ((128, 1), (512, 4), (2048, 16))
HEADS_PER_GROUP = 4
N_SELF_HEADS = len(DILATED_GROUPS) * HEADS_PER_GROUP
N_RET_HEADS = 12
SELF_WIDTH = N_SELF_HEADS * HEAD_DIM
MEM_HEADS = 4
MEM_LEN = 256
MEM_WIDTH = MEM_HEADS * HEAD_DIM
RET_CHUNK = 128
N_EXPERTS = 32
TOP_K = 4
D_EXPERT = D_MODEL
SWIGLU_ALPHA = 1.702
SWIGLU_LIMIT = 7.0
DEEPNORM_ALPHA = (2 * DEPTH) ** 0.25
LN_EPS = 1e-5
NEG_INF = -1e30

N_TOK = BATCH * SEQ
GROUP_WIDTH = HEADS_PER_GROUP * HEAD_DIM
SELF_QKV = 3 * GROUP_WIDTH
ATTN_STEPS = 128
ATTN_QB = 2

LANES = 128
SUBLANES = 8
VMEM_LIMIT = 56 * 1024 * 1024

MM_TM = 1024
MM_TN = 512
ROW_TILE = 512
COMBINE_ROWS = 512
ROUTER_TM = 256

MOE_SUB = 64
MOE_TILE = 1152
MOE_SUBS = MOE_TILE // MOE_SUB
MOE_FAST_SUBS = (15, 16, 17, 18)
MOE_TF = 512
MOE_DOWN_TN = 512
MOE_VMEM_LIMIT = 60 * 1024 * 1024
MOE_TILES = (N_TOK * TOP_K) // MOE_TILE + N_EXPERTS
MOE_ROWS = MOE_TILES * MOE_TILE
DISPATCH_TB = 256
COMBINE_TB = 128


def _alibi_slopes(n):
    def pow2(m):
        start = 2.0 ** (-8.0 / m)
        return [start ** (i + 1) for i in range(m)]

    if math.log2(n).is_integer():
        s = pow2(n)
    else:
        c = 2 ** math.floor(math.log2(n))
        s = pow2(c) + pow2(2 * c)[0::2][: n - c]
    return sorted(s, reverse=True)


def _params(*sem, vmem_limit=VMEM_LIMIT):
    return pltpu.CompilerParams(dimension_semantics=sem, vmem_limit_bytes=vmem_limit)


def _layer_norm_rows(z, g, b):
    mu = jnp.mean(z, axis=-1, keepdims=True)
    zc = z - mu
    var = jnp.mean(zc * zc, axis=-1, keepdims=True)
    return zc * lax.rsqrt(var + LN_EPS) * g + b


def _dot_nt(a, b):
    return lax.dot_general(a, b, (((1,), (1,)), ((), ())), preferred_element_type=jnp.float32)


def _dot(a, b):
    return jnp.dot(a, b, preferred_element_type=jnp.float32)


def _mm_kernel(x_ref, w_ref, o_ref):
    o_ref[...] = _dot(x_ref[...].astype(jnp.bfloat16), w_ref[...].astype(jnp.bfloat16)).astype(o_ref.dtype)


def _matmul(x, w, layer, out_dtype, col_tiles=None):
    m, k = x.shape
    first, stride, count = col_tiles if col_tiles else (0, 1, w.shape[2] // MM_TN)
    tm = min(MM_TM * (4 // x.dtype.itemsize), m)
    return pl.pallas_call(
        _mm_kernel,
        out_shape=jax.ShapeDtypeStruct((m, count * MM_TN), out_dtype),
        grid=(m // tm, count),
        in_specs=[pl.BlockSpec((tm, k), lambda i, j: (i, 0)),
                  pl.BlockSpec((None, k, MM_TN), lambda i, j: (layer, 0, first + stride * j))],
        out_specs=pl.BlockSpec((tm, MM_TN), lambda i, j: (i, j)),
        compiler_params=_params("parallel", "parallel"),
        name="dense_matmul",
    )(x, w)


def _proj_residue_kernel(x_ref, wq_ref, wk_ref, wv_ref, o_ref, scr_ref, *, dilation):
    x = x_ref[...].astype(jnp.bfloat16)
    rows = MM_TM // dilation
    for section, w_ref in enumerate((wq_ref, wk_ref, wv_ref)):
        res = _dot(x, w_ref[...].astype(jnp.bfloat16))
        for c in range(GROUP_WIDTH // LANES):
            scr_ref[c] = res[:, c * LANES:(c + 1) * LANES]
        for r in range(dilation):
            for c in range(GROUP_WIDTH // LANES):
                col = r * SELF_QKV + section * GROUP_WIDTH + c * LANES
                o_ref[:, col:col + LANES] = scr_ref[c, pl.ds(r, rows, stride=dilation), :].astype(o_ref.dtype)


def _proj_residue_major(x, w, layer, group):
    _, dilation = DILATED_GROUPS[group]
    k = x.shape[1]
    kb = SELF_WIDTH // GROUP_WIDTH

    def wspec(section):
        return pl.BlockSpec((None, k, GROUP_WIDTH), lambda i: (layer, 0, section * kb + group))

    return pl.pallas_call(
        functools.partial(_proj_residue_kernel, dilation=dilation),
        out_shape=jax.ShapeDtypeStruct((N_TOK // dilation, dilation * SELF_QKV), jnp.bfloat16),
        grid=(N_TOK // MM_TM,),
        in_specs=[pl.BlockSpec((MM_TM, k), lambda i: (i, 0)), wspec(0), wspec(1), wspec(2)],
        out_specs=pl.BlockSpec((MM_TM // dilation, dilation * SELF_QKV), lambda i: (i, 0)),
        scratch_shapes=[pltpu.VMEM((GROUP_WIDTH // LANES, MM_TM, LANES), jnp.float32)],
        compiler_params=_params("parallel"),
        name=f"proj_residue_g{group}",
    )(x, w, w, w)


def _dil_attn_kernel(q_ref, kp_ref, kc_ref, vp_ref, vc_ref, o_ref, lse_ref, *, slopes, dilation):
    n = pl.program_id(2)
    heads = HEADS_PER_GROUP
    rows = ATTN_QB * heads * ATTN_STEPS
    head_slices = [slice(h * HEAD_DIM, (h + 1) * HEAD_DIM) for h in range(heads)]
    blocks = [slice(s * ATTN_STEPS, (s + 1) * ATTN_STEPS) for s in range(ATTN_QB)]
    row = lax.broadcasted_iota(jnp.int32, (rows, ATTN_STEPS), 0)
    kj = lax.broadcasted_iota(jnp.int32, (rows, ATTN_STEPS), 1)
    diff = (row % ATTN_STEPS) - kj
    valid_c = diff >= 0
    has_prev = jnp.logical_or(row >= heads * ATTN_STEPS, n > 0)
    valid_p = jnp.logical_and(diff <= 0, has_prev)
    slope = jnp.full((rows, ATTN_STEPS), slopes[0], jnp.float32)
    head_of_row = (row // ATTN_STEPS) % heads
    for h in range(1, heads):
        slope = jnp.where(head_of_row == h, slopes[h], slope)
    bias_c = slope * (diff * dilation).astype(jnp.float32)
    bias_p = slope * ((diff + ATTN_STEPS) * dilation).astype(jnp.float32)
    scale = HEAD_DIM ** -0.5

    def prev_cur(cur_ref, prev_ref, s, sl):
        prev = prev_ref[:, sl] if s == 0 else cur_ref[blocks[s - 1], sl]
        return prev, cur_ref[blocks[s], sl]

    pairs = [(s, sl) for s in range(ATTN_QB) for sl in head_slices]
    s_c = jnp.concatenate([_dot_nt(q_ref[blocks[s], sl], prev_cur(kc_ref, kp_ref, s, sl)[1])
                           for s, sl in pairs], axis=0)
    s_p = jnp.concatenate([_dot_nt(q_ref[blocks[s], sl], prev_cur(kc_ref, kp_ref, s, sl)[0])
                           for s, sl in pairs], axis=0)
    s_c = jnp.where(valid_c, s_c * scale - bias_c, NEG_INF)
    s_p = jnp.where(valid_p, s_p * scale - bias_p, NEG_INF)
    m = jnp.max(jnp.maximum(s_c, s_p), axis=-1, keepdims=True)
    e_c = jnp.exp(s_c - m)
    e_p = jnp.exp(s_p - m)
    l = jnp.sum(e_c + e_p, axis=-1, keepdims=True)
    inv_l = 1.0 / l
    p_c = (e_c * inv_l).astype(jnp.bfloat16)
    p_p = (e_p * inv_l).astype(jnp.bfloat16)
    lse = m + jnp.log(l)
    for i, (s, sl) in enumerate(pairs):
        chain = slice(i * ATTN_STEPS, (i + 1) * ATTN_STEPS)
        v_prev, v_cur = prev_cur(vc_ref, vp_ref, s, sl)
        o_ref[blocks[s], sl] = _dot(p_c[chain], v_cur) + _dot(p_p[chain], v_prev)
        lse_ref[blocks[s], sl] = jnp.broadcast_to(lse[chain], (ATTN_STEPS, HEAD_DIM))


def _dilated_group(src, group):
    _, dilation = DILATED_GROUPS[group]
    length = SEQ // dilation
    span = ATTN_QB * ATTN_STEPS
    cb = src.shape[1] // dilation // GROUP_WIDTH
    view = src.reshape(BATCH, length, src.shape[1])
    blk = (None, span, GROUP_WIDTH)
    prev_blk = (None, ATTN_STEPS, GROUP_WIDTH)

    def col(section):
        return lambda b, r, n: (b, n, r * cb + section)

    def col_prev(section):
        return lambda b, r, n: (b, jnp.maximum(n * ATTN_QB - 1, 0), r * cb + section)

    slopes = tuple(_alibi_slopes(N_SELF_HEADS)[group * HEADS_PER_GROUP:(group + 1) * HEADS_PER_GROUP])
    out_shape = jax.ShapeDtypeStruct((BATCH, length, dilation * GROUP_WIDTH), jnp.float32)
    out_spec = pl.BlockSpec(blk, lambda b, r, n: (b, n, r))
    o, lse = pl.pallas_call(
        functools.partial(_dil_attn_kernel, slopes=slopes, dilation=dilation),
        out_shape=(out_shape, out_shape),
        grid=(BATCH, dilation, length // span),
        in_specs=[pl.BlockSpec(blk, col(0)),
                  pl.BlockSpec(prev_blk, col_prev(1)), pl.BlockSpec(blk, col(1)),
                  pl.BlockSpec(prev_blk, col_prev(2)), pl.BlockSpec(blk, col(2))],
        out_specs=(out_spec, out_spec),
        compiler_params=_params("parallel", "parallel", "arbitrary"),
        name=f"dilated_attn_g{group}",
    )(view, view, view, view, view)
    return o, lse


def _dil_combine_kernel(*refs):
    ngroups = len(DILATED_GROUPS)
    o_refs, l_refs, out_ref = refs[:ngroups], refs[ngroups:2 * ngroups], refs[2 * ngroups]
    scratch = list(refs[2 * ngroups + 1:])
    os, ls = [], []
    for g, (_, dilation) in enumerate(DILATED_GROUPS):
        if dilation == 1:
            os.append(o_refs[g][...])
            ls.append(l_refs[g][...])
            continue
        rows = COMBINE_ROWS // dilation
        planes = GROUP_WIDTH // LANES
        o_scr, l_scr = scratch.pop(0), scratch.pop(0)
        for r in range(dilation):
            for c in range(planes):
                cols = slice(r * GROUP_WIDTH + c * LANES, r * GROUP_WIDTH + (c + 1) * LANES)
                o_scr[c, pl.ds(r, rows, stride=dilation), :] = o_refs[g][:, cols]
                l_scr[c, pl.ds(r, rows, stride=dilation), :] = l_refs[g][:, cols]
        os.append(jnp.concatenate([o_scr[c] for c in range(planes)], axis=1))
        ls.append(jnp.concatenate([l_scr[c] for c in range(planes)], axis=1))
    m = functools.reduce(jnp.maximum, ls)
    es = [jnp.exp(l - m) for l in ls]
    inv = 1.0 / functools.reduce(lambda a, b: a + b, es)
    for g in range(ngroups):
        out_ref[:, g * GROUP_WIDTH:(g + 1) * GROUP_WIDTH] = (os[g] * (es[g] * inv)).astype(out_ref.dtype)


def _dilated_attention(x, w_in, layer, proj):
    sources = [proj] + [_proj_residue_major(x, w_in, layer, g) for g in range(1, len(DILATED_GROUPS))]
    outs, lses = zip(*[_dilated_group(src, g) for g, src in enumerate(sources)])

    def spec(dilation):
        return pl.BlockSpec((COMBINE_ROWS // dilation, dilation * GROUP_WIDTH), lambda i: (i, 0))

    def flat(a):
        return a.reshape(a.shape[0] * a.shape[1], a.shape[2])

    specs = [spec(d) for _, d in DILATED_GROUPS]
    scratch = [pltpu.VMEM((GROUP_WIDTH // LANES, COMBINE_ROWS, LANES), jnp.float32)
               for _, d in DILATED_GROUPS if d > 1 for _ in range(2)]
    return pl.pallas_call(
        _dil_combine_kernel,
        out_shape=jax.ShapeDtypeStruct((N_TOK, SELF_WIDTH), jnp.bfloat16),
        grid=(N_TOK // COMBINE_ROWS,),
        in_specs=specs + specs,
        out_specs=pl.BlockSpec((COMBINE_ROWS, SELF_WIDTH), lambda i: (i, 0)),
        scratch_shapes=scratch,
        compiler_params=_params("parallel"),
        name="dilated_combine",
    )(*[flat(o) for o in outs], *[flat(l) for l in lses])


def _retention_kernel(q_ref, k_ref, v_ref, g_ref, dmat_ref, kdec_ref, qdec_ref, cdec_ref, gn_ref,
                      o_ref, state_ref):
    @pl.when(pl.program_id(1) == 0)
    def _():
        state_ref[...] = jnp.zeros_like(state_ref)

    heads = range(N_RET_HEADS)
    cols = [slice(h * HEAD_DIM, (h + 1) * HEAD_DIM) for h in heads]
    bf16 = jnp.bfloat16
    scores = [_dot_nt(q_ref[:, c], k_ref[:, c]) * dmat_ref[h] for h, c in zip(heads, cols)]
    states = [state_ref[h] for h in heads]
    cross = [_dot(q_ref[:, c], states[h].astype(bf16)) * qdec_ref[:, c] for h, c in zip(heads, cols)]
    intra = [_dot(scores[h].astype(bf16), v_ref[:, c]) for h, c in zip(heads, cols)]
    kw = [(k_ref[:, c].astype(jnp.float32) * kdec_ref[:, c]).astype(bf16) for c in cols]
    for h, c in zip(heads, cols):
        kv = lax.dot_general(kw[h], v_ref[:, c], (((0,), (0,)), ((), ())), preferred_element_type=jnp.float32)
        state_ref[h] = states[h] * cdec_ref[h] + kv
    r = jnp.concatenate([intra[h] + cross[h] for h in heads], axis=0)
    mu = jnp.mean(r, axis=-1, keepdims=True)
    rc = r - mu
    var = jnp.mean(rc * rc, axis=-1, keepdims=True)
    rn = rc * lax.rsqrt(var + LN_EPS)
    for h, c in zip(heads, cols):
        gate = g_ref[:, c].astype(jnp.float32)
        normed = rn[h * RET_CHUNK:(h + 1) * RET_CHUNK] * gn_ref[:, c]
        o_ref[:, c] = (gate * (1.0 / (1.0 + jnp.exp(-gate))) * normed).astype(o_ref.dtype)


def _retention(proj, gn_gain):
    c = RET_CHUNK
    log_gamma = jnp.log1p(-jnp.exp2(-(5.0 + jnp.arange(N_RET_HEADS, dtype=jnp.float32))))
    idx = jnp.arange(c, dtype=jnp.float32)
    diff = idx[:, None] - idx[None, :]
    decay = jnp.where(diff >= 0, jnp.exp(jnp.maximum(diff, 0.0)[None] * log_gamma[:, None, None]), 0.0)
    scale = HEAD_DIM ** -0.5
    dmat = decay * scale
    k_decay = jnp.exp((c - 1 - idx)[:, None] * log_gamma[None, :]) * scale
    q_decay = jnp.exp((idx + 1.0)[:, None] * log_gamma[None, :])
    kdec = jnp.repeat(k_decay, HEAD_DIM, axis=1)
    qdec = jnp.repeat(q_decay, HEAD_DIM, axis=1)
    cdec = jnp.broadcast_to(jnp.exp(c * log_gamma)[:, None, None], (N_RET_HEADS, 1, HEAD_DIM))
    gn = gn_gain.reshape(1, SELF_WIDTH).astype(jnp.float32)
    nchunk = SEQ // c
    blk = (c, SELF_WIDTH)

    def section(s):
        return pl.BlockSpec(blk, lambda b, n: (b * nchunk + n, s))

    const2 = lambda b, n: (0, 0)
    const3 = lambda b, n: (0, 0, 0)
    return pl.pallas_call(
        _retention_kernel,
        out_shape=jax.ShapeDtypeStruct((N_TOK, SELF_WIDTH), jnp.bfloat16),
        grid=(BATCH, nchunk),
        in_specs=[section(0), section(1), section(2), section(3),
                  pl.BlockSpec((N_RET_HEADS, c, c), const3),
                  pl.BlockSpec(blk, const2), pl.BlockSpec(blk, const2),
                  pl.BlockSpec((N_RET_HEADS, 1, HEAD_DIM), const3),
                  pl.BlockSpec((1, SELF_WIDTH), const2)],
        out_specs=pl.BlockSpec(blk, lambda b, n: (b * nchunk + n, 0)),
        scratch_shapes=[pltpu.VMEM((N_RET_HEADS, HEAD_DIM, HEAD_DIM), jnp.float32)],
        compiler_params=_params("parallel", "arbitrary"),
        name="retention",
    )(proj, proj, proj, proj, dmat, kdec, qdec, cdec, gn)


def _mem_attn_kernel(q_ref, k_ref, v_ref, o_ref):
    scale = HEAD_DIM ** -0.5
    for h in range(MEM_HEADS):
        sl = slice(h * HEAD_DIM, (h + 1) * HEAD_DIM)
        s = _dot_nt(q_ref[:, sl], k_ref[:, sl]) * scale
        e = jnp.exp(s - jnp.max(s, axis=-1, keepdims=True))
        p = e * (1.0 / jnp.sum(e, axis=-1, keepdims=True))
        o_ref[:, sl] = _dot(p.astype(jnp.bfloat16), v_ref[:, sl]).astype(o_ref.dtype)


def _memory_attention(proj, memkv):
    qcol = proj.shape[1] // MEM_WIDTH - 1
    per_b = SEQ // ROW_TILE
    return pl.pallas_call(
        _mem_attn_kernel,
        out_shape=jax.ShapeDtypeStruct((N_TOK, MEM_WIDTH), jnp.bfloat16),
        grid=(BATCH, per_b),
        in_specs=[pl.BlockSpec((ROW_TILE, MEM_WIDTH), lambda b, i: (b * per_b + i, qcol)),
                  pl.BlockSpec((MEM_LEN, MEM_WIDTH), lambda b, i: (b, 0)),
                  pl.BlockSpec((MEM_LEN, MEM_WIDTH), lambda b, i: (b, 1))],
        out_specs=pl.BlockSpec((ROW_TILE, MEM_WIDTH), lambda b, i: (b * per_b + i, 0)),
        compiler_params=_params("parallel", "parallel"),
        name="memory_attn",
    )(proj, memkv, memkv)


def _mix_ln_kernel(so_ref, mo_ref, wt_ref, wb_ref, h_ref, g_ref, b_ref, hf_ref, hb_ref):
    half = ROW_TILE // 2
    for rows in (slice(0, half), slice(half, ROW_TILE)):
        mix = _dot(so_ref[rows, :], wt_ref[...]) + _dot(mo_ref[rows, :], wb_ref[...])
        y = _layer_norm_rows(DEEPNORM_ALPHA * h_ref[rows, :] + mix, g_ref[...], b_ref[...])
        hf_ref[rows, :] = y
        hb_ref[rows, :] = y.astype(hb_ref.dtype)


def _mix_ln(self_out, mem_out, w_mix, h, g, b):
    row = lambda i: (i, 0)
    const = lambda i: (0, 0)
    return pl.pallas_call(
        _mix_ln_kernel,
        out_shape=(jax.ShapeDtypeStruct((N_TOK, D_MODEL), jnp.float32),
                   jax.ShapeDtypeStruct((N_TOK, D_MODEL), jnp.bfloat16)),
        grid=(N_TOK // ROW_TILE,),
        in_specs=[pl.BlockSpec((ROW_TILE, SELF_WIDTH), row),
                  pl.BlockSpec((ROW_TILE, MEM_WIDTH), row),
                  pl.BlockSpec((SELF_WIDTH, D_MODEL), const),
                  pl.BlockSpec((MEM_WIDTH, D_MODEL), lambda i: (SELF_WIDTH // MEM_WIDTH, 0)),
                  pl.BlockSpec((ROW_TILE, D_MODEL), row),
                  pl.BlockSpec((1, D_MODEL), const), pl.BlockSpec((1, D_MODEL), const)],
        out_specs=(pl.BlockSpec((ROW_TILE, D_MODEL), row), pl.BlockSpec((ROW_TILE, D_MODEL), row)),
        compiler_params=_params("parallel"),
        name="mix_ln",
    )(self_out, mem_out, w_mix, w_mix, h, g.reshape(1, D_MODEL), b.reshape(1, D_MODEL))


def _router_kernel(h_ref, whi_ref, wlo_ref, b_ref, e_ref, g_ref, r_ref, cnt_ref, run_ref):
    tm = ROUTER_TM

    @pl.when(pl.program_id(0) == 0)
    def _():
        run_ref[...] = jnp.zeros_like(run_ref)

    h = h_ref[...]
    h_hi = h.astype(jnp.bfloat16)
    h_lo = (h - h_hi.astype(jnp.float32)).astype(jnp.bfloat16)
    logits = (_dot(h_hi, whi_ref[...]) + _dot(h_lo, whi_ref[...]) + _dot(h_hi, wlo_ref[...])) + b_ref[...]
    lane = lax.broadcasted_iota(jnp.int32, (tm, LANES), 1).astype(jnp.float32)
    work = logits
    vals, idxs, hots = [], [], []
    for _ in range(TOP_K):
        m = jnp.max(work, axis=-1, keepdims=True)
        idx = jnp.min(jnp.where(work == m, lane, float(LANES)), axis=-1, keepdims=True)
        hot = lane == idx
        vals.append(m)
        idxs.append(idx)
        hots.append(hot)
        work = jnp.where(hot, -jnp.inf, work)
    exps = [jnp.exp(v - vals[0]) for v in vals]
    inv = 1.0 / (exps[0] + exps[1] + exps[2] + exps[3])
    cnt = jnp.zeros((tm, LANES), jnp.float32)
    for hot in hots:
        cnt = cnt + hot.astype(jnp.float32)
    row = lax.broadcasted_iota(jnp.int32, (tm, tm), 0)
    colm = lax.broadcasted_iota(jnp.int32, (tm, tm), 1)
    tri = (row > colm).astype(jnp.bfloat16)
    before = run_ref[...] + _dot(tri, cnt.astype(jnp.bfloat16))
    lane_i = lax.broadcasted_iota(jnp.int32, (tm, LANES), 1)
    e_out = jnp.zeros((tm, LANES), jnp.float32)
    g_out = jnp.zeros((tm, LANES), jnp.float32)
    r_out = jnp.zeros((tm, LANES), jnp.float32)
    for k in range(TOP_K):
        rank = jnp.sum(jnp.where(hots[k], before, 0.0), axis=-1, keepdims=True)
        e_out = jnp.where(lane_i == k, idxs[k], e_out)
        g_out = jnp.where(lane_i == k, exps[k] * inv, g_out)
        r_out = jnp.where(lane_i == k, rank, r_out)
    e_ref[...] = e_out.T[0:SUBLANES, :].astype(jnp.int32)
    g_ref[...] = g_out
    r_ref[...] = r_out.T[0:SUBLANES, :].astype(jnp.int32)
    run_ref[...] = run_ref[...] + jnp.sum(cnt, axis=0, keepdims=True)
    cnt_ref[...] = run_ref[...].astype(jnp.int32)


def _router(h, router_w, router_b):
    w = jnp.zeros((D_MODEL, LANES), jnp.float32).at[:, :N_EXPERTS].set(router_w)
    w_hi = w.astype(jnp.bfloat16)
    w_lo = (w - w_hi.astype(jnp.float32)).astype(jnp.bfloat16)
    b = jnp.full((1, LANES), NEG_INF, jnp.float32).at[0, :N_EXPERTS].set(router_b)
    tm = ROUTER_TM
    row = lambda i: (i, 0)
    const = lambda i: (0, 0)
    lanes_out = pl.BlockSpec((tm, LANES), row)
    slots_out = pl.BlockSpec((SUBLANES, tm), lambda i: (0, i))
    e, g, r, cnt = pl.pallas_call(
        _router_kernel,
        out_shape=(jax.ShapeDtypeStruct((SUBLANES, N_TOK), jnp.int32),
                   jax.ShapeDtypeStruct((N_TOK, LANES), jnp.float32),
                   jax.ShapeDtypeStruct((SUBLANES, N_TOK), jnp.int32),
                   jax.ShapeDtypeStruct((1, LANES), jnp.int32)),
        grid=(N_TOK // tm,),
        in_specs=[pl.BlockSpec((tm, D_MODEL), row), pl.BlockSpec((D_MODEL, LANES), const),
                  pl.BlockSpec((D_MODEL, LANES), const), pl.BlockSpec((1, LANES), const)],
        out_specs=(slots_out, lanes_out, slots_out, pl.BlockSpec((1, LANES), const)),
        scratch_shapes=[pltpu.VMEM((1, LANES), jnp.float32)],
        compiler_params=_params("arbitrary"),
        name="router",
    )(h, w_hi, w_lo, b)
    return e[:TOP_K], g, r[:TOP_K], cnt[0, :N_EXPERTS]


def _route_tables(top_e, rank, cnt):
    ntile = (cnt + MOE_TILE - 1) // MOE_TILE
    tile_end = jnp.cumsum(ntile)
    tile_base = tile_end - ntile
    experts = jnp.arange(N_EXPERTS, dtype=jnp.int32)[:, None, None]
    base = jnp.sum(jnp.where(top_e[None] == experts, tile_base[:, None, None], 0), axis=0)
    pos = (base * MOE_TILE + rank).astype(jnp.int32).reshape(-1)
    n_tiles = tile_end[-1:].astype(jnp.int32)
    w = jnp.minimum(jnp.arange(MOE_TILES, dtype=jnp.int32), n_tiles[0] - 1)
    tile_e = jnp.minimum(jnp.searchsorted(tile_end, w, side="right"), N_EXPERTS - 1).astype(jnp.int32)
    rows = jnp.clip(cnt[tile_e] - (w - tile_base[tile_e]) * MOE_TILE, 0, MOE_TILE)
    nsub = ((rows + MOE_SUB - 1) // MOE_SUB).astype(jnp.int32)
    pad_start = (tile_base * MOE_TILE + cnt).astype(jnp.int32)
    pad_n = ((-cnt) % MOE_SUB).astype(jnp.int32)
    return pos, n_tiles, tile_e, nsub, pad_start, pad_n


def _row_copy(src_vmem, src_row, dst_hbm, dst_row, sem):
    return pltpu.make_async_copy(src_vmem.at[pl.ds(src_row, 1)], dst_hbm.at[pl.ds(dst_row, 1)], sem)


def _pack_bf16_pairs(x):
    half = x.shape[1] // 2
    lo = lax.bitcast_convert_type(x[:, :half].astype(jnp.float32), jnp.uint32)
    hi = lax.bitcast_convert_type(x[:, half:].astype(jnp.float32), jnp.uint32)
    return (lo >> 16) | (hi & jnp.uint32(0xFFFF0000))


def _unpack_bf16_pairs(words):
    lo = lax.bitcast_convert_type(words << 16, jnp.float32)
    hi = lax.bitcast_convert_type(words & jnp.uint32(0xFFFF0000), jnp.float32)
    return lo.astype(jnp.bfloat16), hi.astype(jnp.bfloat16)


def _dispatch_kernel(pos_ref, pad_start_ref, pad_n_ref, hb_ref, xs_hbm, h_ref, sem):
    step = pl.program_id(0)
    base = step * DISPATCH_TB
    h_ref[...] = _pack_bf16_pairs(hb_ref[...])

    for r in range(DISPATCH_TB):
        for k in range(TOP_K):
            _row_copy(h_ref, r, xs_hbm, pos_ref[k * N_TOK + base + r], sem).start()

    @pl.when(step == 0)
    def _():
        def per_expert(e, carry):
            start = pad_start_ref[e]
            n = pad_n_ref[e]

            def fill(i, c):
                _row_copy(h_ref, 0, xs_hbm, start + i, sem).start()
                return c

            def drain(_, c):
                _row_copy(h_ref, 0, xs_hbm, 0, sem).wait()
                return c

            lax.fori_loop(0, n, fill, 0)
            lax.fori_loop(0, n, drain, 0)
            return carry

        lax.fori_loop(0, N_EXPERTS, per_expert, 0)

    for _ in range(TOP_K):
        pltpu.make_async_copy(h_ref, xs_hbm.at[pl.ds(0, DISPATCH_TB)], sem).wait()


def _dispatch(hb, pos, pad_start, pad_n):
    return pl.pallas_call(
        _dispatch_kernel,
        out_shape=jax.ShapeDtypeStruct((MOE_ROWS, D_MODEL // 2), jnp.uint32),
        grid_spec=pltpu.PrefetchScalarGridSpec(
            num_scalar_prefetch=3,
            grid=(N_TOK // DISPATCH_TB,),
            in_specs=[pl.BlockSpec((DISPATCH_TB, D_MODEL), lambda i, p, s, n: (i, 0))],
            out_specs=pl.BlockSpec(memory_space=pl.ANY),
            scratch_shapes=[pltpu.VMEM((DISPATCH_TB, D_MODEL // 2), jnp.uint32),
                            pltpu.SemaphoreType.DMA(())],
        ),
        compiler_params=_params("arbitrary"),
        name="moe_dispatch",
    )(pos, pad_start, pad_n, hb)


def _moe_weight_map(layer, first, hidden_axis):
    def index_map(w, j, tile_e, nsub):
        return (layer, tile_e[w], first + j, 0) if hidden_axis == 2 else (layer, tile_e[w], 0, first + j)

    return index_map


def _moe_rows(nsub, accumulate):
    fast = nsub < 0
    for s in MOE_FAST_SUBS:
        fast = jnp.logical_or(fast, nsub == s)
        pl.when(nsub == s)(functools.partial(accumulate, slice(0, s * MOE_SUB)))

    @pl.when(jnp.logical_and(nsub > 0, jnp.logical_not(fast)))
    def _():
        def one(i, carry):
            accumulate(pl.ds(pl.multiple_of(i * MOE_SUB, MOE_SUB), MOE_SUB))
            return carry

        lax.fori_loop(0, nsub, one, 0)


def _moe_expert_kernel(exp_ref, nsub_ref, x_ref, wg_ref, wl_ref, bg_ref, bl_ref, wo_ref, bo_ref, o_ref):
    first = pl.program_id(1) == 0
    nsub = nsub_ref[pl.program_id(0)]
    half = D_MODEL // 2

    @pl.when(first)
    def _():
        o_ref[...] = jnp.broadcast_to(bo_ref[...], (MOE_TILE, D_MODEL))

    def up(lo, hi, w_ref, b_ref):
        return (_dot(lo, w_ref[0:half, :].astype(jnp.bfloat16))
                + _dot(hi, w_ref[half:D_MODEL, :].astype(jnp.bfloat16)) + b_ref[...])

    def accumulate(rows):
        lo, hi = _unpack_bf16_pairs(x_ref[rows, :])
        gate = up(lo, hi, wg_ref, bg_ref)
        lin = up(lo, hi, wl_ref, bl_ref)
        gate = jnp.minimum(gate, SWIGLU_LIMIT)
        lin = jnp.clip(lin, -SWIGLU_LIMIT, SWIGLU_LIMIT)
        act = (gate * (1.0 / (1.0 + jnp.exp(-SWIGLU_ALPHA * gate))) * (lin + 1.0)).astype(jnp.bfloat16)
        for c in range(D_MODEL // MOE_DOWN_TN):
            cols = slice(c * MOE_DOWN_TN, (c + 1) * MOE_DOWN_TN)
            o_ref[rows, cols] = o_ref[rows, cols] + _dot(act, wo_ref[:, cols].astype(jnp.bfloat16))

    _moe_rows(nsub, accumulate)


def _moe_experts(xs, w_in, b_in, w_out, b_out, layer, n_tiles, tile_e, nsub):
    nj = D_EXPERT // MOE_TF
    b_in4 = b_in.reshape(DEPTH, N_EXPERTS, 1, 2 * D_EXPERT)
    b_out4 = b_out.reshape(DEPTH, N_EXPERTS, 1, D_MODEL)
    wblk = (None, None, D_MODEL, MOE_TF)
    bblk = (None, None, 1, MOE_TF)
    tile_map = lambda w, j, e, n: (w, 0)
    return pl.pallas_call(
        _moe_expert_kernel,
        out_shape=jax.ShapeDtypeStruct((MOE_ROWS, D_MODEL), jnp.float32),
        grid_spec=pltpu.PrefetchScalarGridSpec(
            num_scalar_prefetch=2,
            grid=(n_tiles[0], nj),
            in_specs=[pl.BlockSpec((MOE_TILE, D_MODEL // 2), tile_map),
                      pl.BlockSpec(wblk, _moe_weight_map(layer, 0, 3)),
                      pl.BlockSpec(wblk, _moe_weight_map(layer, nj, 3)),
                      pl.BlockSpec(bblk, _moe_weight_map(layer, 0, 3)),
                      pl.BlockSpec(bblk, _moe_weight_map(layer, nj, 3)),
                      pl.BlockSpec((None, None, MOE_TF, D_MODEL), _moe_weight_map(layer, 0, 2)),
                      pl.BlockSpec((None, None, 1, D_MODEL), lambda w, j, e, n: (layer, e[w], 0, 0))],
            out_specs=pl.BlockSpec((MOE_TILE, D_MODEL), tile_map),
        ),
        compiler_params=_params("arbitrary", "arbitrary", vmem_limit=MOE_VMEM_LIMIT),
        name="moe_experts",
    )(tile_e, nsub, xs, w_in, w_in, b_in4, b_in4, w_out, b_out4)


def _combine_ln_kernel(pos_ref, y_hbm, gates_ref, h_ref, g_ref, b_ref, hf_ref, hb_ref, buf0, buf1, sems):
    step = pl.program_id(0)
    last = pl.num_programs(0) - 1

    def row_copy(tile, r, k, buf, sem):
        src = y_hbm.at[pl.ds(pos_ref[k * N_TOK + tile * COMBINE_TB + r], 1)]
        return pltpu.make_async_copy(src, buf.at[k, pl.ds(r, 1)], sem)

    def wait_tile(buf, sem):
        for k in range(TOP_K):
            pltpu.make_async_copy(y_hbm.at[pl.ds(0, COMBINE_TB)], buf.at[k], sem).wait()

    @pl.when(step == 0)
    def _():
        def issue(r, carry):
            for k in range(TOP_K):
                row_copy(0, r, k, buf0, sems.at[0]).start()
            return carry

        lax.fori_loop(0, COMBINE_TB, issue, 0, unroll=8)

    def run(cur, cur_sem, nxt, nxt_sem):
        nxt_tile = jnp.minimum(step + 1, last)
        wait_tile(cur, cur_sem)
        for r in range(COMBINE_TB):
            for k in range(TOP_K):
                row_copy(nxt_tile, r, k, nxt, nxt_sem).start()
        gates = gates_ref[...]
        ffn = gates[:, 0:1] * cur[0]
        for k in range(1, TOP_K):
            ffn = ffn + gates[:, k:k + 1] * cur[k]
        y = _layer_norm_rows(DEEPNORM_ALPHA * h_ref[...] + ffn, g_ref[...], b_ref[...])
        hf_ref[...] = y
        hb_ref[...] = y.astype(hb_ref.dtype)

        @pl.when(step == last)
        def _():
            wait_tile(nxt, nxt_sem)

    pl.when(step % 2 == 0)(functools.partial(run, buf0, sems.at[0], buf1, sems.at[1]))
    pl.when(step % 2 == 1)(functools.partial(run, buf1, sems.at[1], buf0, sems.at[0]))


def _combine_ln(y, pos, gates, h, g, b):
    tb = COMBINE_TB
    row = lambda i, p: (i, 0)
    const = lambda i, p: (0, 0)
    return pl.pallas_call(
        _combine_ln_kernel,
        out_shape=(jax.ShapeDtypeStruct((N_TOK, D_MODEL), jnp.float32),
                   jax.ShapeDtypeStruct((N_TOK, D_MODEL), jnp.bfloat16)),
        grid_spec=pltpu.PrefetchScalarGridSpec(
            num_scalar_prefetch=1,
            grid=(N_TOK // tb,),
            in_specs=[pl.BlockSpec(memory_space=pl.ANY),
                      pl.BlockSpec((tb, LANES), row),
                      pl.BlockSpec((tb, D_MODEL), row),
                      pl.BlockSpec((1, D_MODEL), const), pl.BlockSpec((1, D_MODEL), const)],
            out_specs=(pl.BlockSpec((tb, D_MODEL), row), pl.BlockSpec((tb, D_MODEL), row)),
            scratch_shapes=[pltpu.VMEM((TOP_K, tb, D_MODEL), jnp.float32),
                            pltpu.VMEM((TOP_K, tb, D_MODEL), jnp.float32),
                            pltpu.SemaphoreType.DMA((2,))],
        ),
        compiler_params=_params("arbitrary"),
        name="moe_combine_ln",
    )(pos, y, gates, h, g.reshape(1, D_MODEL), b.reshape(1, D_MODEL))


def _moe(h, hb, layer, router_w, router_b, w_in, b_in, w_out, b_out, ln_g, ln_b):
    top_e, gates, rank, cnt = _router(h, router_w[layer], router_b[layer])
    pos, n_tiles, tile_e, nsub, pad_start, pad_n = _route_tables(top_e, rank, cnt)
    xs = _dispatch(hb, pos, pad_start, pad_n)
    y = _moe_experts(xs, w_in, b_in, w_out, b_out, layer, n_tiles, tile_e, nsub)
    return _combine_ln(y, pos, gates, h, ln_g[layer], ln_b[layer])


def kernel(x, mem, w_in_dil, w_in_ret, ret_gn_g, w_mem_kv, w_mix_out, ln_mix_g, ln_mix_b, router_w, router_b, moe_w_in, moe_b_in, moe_w_out, moe_b_out, ln_ffn_g, ln_ffn_b):
    bf16 = jnp.bfloat16
    h = x.reshape(N_TOK, D_MODEL)
    hb = h
    memb = mem.reshape(BATCH * MEM_LEN, D_MODEL)
    for layer in range(DEPTH):
        slot = layer // 2
        if layer % 2 == 0:
            kb = SELF_WIDTH // GROUP_WIDTH
            proj = _matmul(hb, w_in_dil, slot, bf16, col_tiles=(0, kb, kb + 1))
            self_out = _dilated_attention(hb, w_in_dil, slot, proj)
        else:
            proj = _matmul(hb, w_in_ret, slot, bf16)
            self_out = _retention(proj, ret_gn_g[slot])
        memkv = _matmul(memb, w_mem_kv, layer, bf16)
        mem_out = _memory_attention(proj, memkv)
        h, hb = _mix_ln(self_out, mem_out, w_mix_out[layer].astype(bf16), h,
                        ln_mix_g[layer], ln_mix_b[layer])
        h, hb = _moe(h, hb, layer, router_w, router_b, moe_w_in, moe_b_in, moe_w_out, moe_b_out,
                     ln_ffn_g, ln_ffn_b)
    return h.reshape(BATCH, SEQ, D_MODEL)
```

```python
import functools
import math

import jax
import jax.numpy as jnp
from jax import lax
from jax.experimental import pallas as pl
from jax.experimental.pallas import tpu as pltpu

D_MODEL = 2048
BATCH = 2
SEQ = 4096
DEPTH = 2
HEAD_DIM = 128
DILATED_GROUPS = ((128, 1), (512, 4), (2048, 16))
HEADS_PER_GROUP = 4
N_SELF_HEADS = len(DILATED_GROUPS) * HEADS_PER_GROUP
N_RET_HEADS = 12
SELF_WIDTH = N_SELF_HEADS * HEAD_DIM
MEM_HEADS = 4
MEM_LEN = 256
MEM_WIDTH = MEM_HEADS * HEAD_DIM
RET_CHUNK = 128
N_EXPERTS = 32
TOP_K = 4
D_EXPERT = D_MODEL
SWIGLU_ALPHA = 1.702
SWIGLU_LIMIT = 7.0
DEEPNORM_ALPHA = (2 * DEPTH) ** 0.25
LN_EPS = 1e-5
NEG_INF = -1e30

N_TOK = BATCH * SEQ
GROUP_WIDTH = HEADS_PER_GROUP * HEAD_DIM
SELF_QKV = 3 * GROUP_WIDTH
ATTN_STEPS = 128
ATTN_QB = 2

LANES = 128
SUBLANES = 8
VMEM_LIMIT = 56 * 1024 * 1024

MM_TM = 1024
MM_TN = 512
ROW_TILE = 512
COMBINE_ROWS = 512
ROUTER_TM = 256

MOE_SUB = 64
MOE_TILE = 1152
MOE_SUBS = MOE_TILE // MOE_SUB
MOE_FAST_SUBS = (15, 16, 17, 18)
MOE_TF = 512
MOE_DOWN_TN = 512
MOE_VMEM_LIMIT = 60 * 1024 * 1024
MOE_TILES = (N_TOK * TOP_K) // MOE_TILE + N_EXPERTS
MOE_ROWS = MOE_TILES * MOE_TILE
DISPATCH_TB = 256
COMBINE_TB = 128


def _alibi_slopes(n):
    def pow2(m):
        start = 2.0 ** (-8.0 / m)
        return [start ** (i + 1) for i in range(m)]

    if math.log2(n).is_integer():
        s = pow2(n)
    else:
        c = 2 ** math.floor(math.log2(n))
        s = pow2(c) + pow2(2 * c)[0::2][: n - c]
    return sorted(s, reverse=True)


def _params(*sem, vmem_limit=VMEM_LIMIT):
    return pltpu.CompilerParams(dimension_semantics=sem, vmem_limit_bytes=vmem_limit)


def _layer_norm_rows(z, g, b):
    mu = jnp.mean(z, axis=-1, keepdims=True)
    zc = z - mu
    var = jnp.mean(zc * zc, axis=-1, keepdims=True)
    return zc * lax.rsqrt(var + LN_EPS) * g + b


def _dot_nt(a, b):
    return lax.dot_general(a, b, (((1,), (1,)), ((), ())), preferred_element_type=jnp.float32)


def _dot(a, b):
    return jnp.dot(a, b, preferred_element_type=jnp.float32)


def _mm_kernel(x_ref, w_ref, o_ref):
    o_ref[...] = _dot(x_ref[...].astype(jnp.bfloat16), w_ref[...].astype(jnp.bfloat16)).astype(o_ref.dtype)


def _matmul(x, w, layer, out_dtype, col_tiles=None):
    m, k = x.shape
    first, stride, count = col_tiles if col_tiles else (0, 1, w.shape[2] // MM_TN)
    tm = min(MM_TM * (4 // x.dtype.itemsize), m)
    return pl.pallas_call(
        _mm_kernel,
        out_shape=jax.ShapeDtypeStruct((m, count * MM_TN), out_dtype),
        grid=(m // tm, count),
        in_specs=[pl.BlockSpec((tm, k), lambda i, j: (i, 0)),
                  pl.BlockSpec((None, k, MM_TN), lambda i, j: (layer, 0, first + stride * j))],
        out_specs=pl.BlockSpec((tm, MM_TN), lambda i, j: (i, j)),
        compiler_params=_params("parallel", "parallel"),
        name="dense_matmul",
    )(x, w)


def _proj_residue_kernel(x_ref, wq_ref, wk_ref, wv_ref, o_ref, scr_ref, *, dilation):
    x = x_ref[...].astype(jnp.bfloat16)
    rows = MM_TM // dilation
    for section, w_ref in enumerate((wq_ref, wk_ref, wv_ref)):
        res = _dot(x, w_ref[...].astype(jnp.bfloat16))
        for c in range(GROUP_WIDTH // LANES):
            scr_ref[c] = res[:, c * LANES:(c + 1) * LANES]
        for r in range(dilation):
            for c in range(GROUP_WIDTH // LANES):
                col = r * SELF_QKV + section * GROUP_WIDTH + c * LANES
                o_ref[:, col:col + LANES] = scr_ref[c, pl.ds(r, rows, stride=dilation), :].astype(o_ref.dtype)


def _proj_residue_major(x, w, layer, group):
    _, dilation = DILATED_GROUPS[group]
    k = x.shape[1]
    kb = SELF_WIDTH // GROUP_WIDTH

    def wspec(section):
        return pl.BlockSpec((None, k, GROUP_WIDTH), lambda i: (layer, 0, section * kb + group))

    return pl.pallas_call(
        functools.partial(_proj_residue_kernel, dilation=dilation),
        out_shape=jax.ShapeDtypeStruct((N_TOK // dilation, dilation * SELF_QKV), jnp.bfloat16),
        grid=(N_TOK // MM_TM,),
        in_specs=[pl.BlockSpec((MM_TM, k), lambda i: (i, 0)), wspec(0), wspec(1), wspec(2)],
        out_specs=pl.BlockSpec((MM_TM // dilation, dilation * SELF_QKV), lambda i: (i, 0)),
        scratch_shapes=[pltpu.VMEM((GROUP_WIDTH // LANES, MM_TM, LANES), jnp.float32)],
        compiler_params=_params("parallel"),
        name=f"proj_residue_g{group}",
    )(x, w, w, w)


def _dil_attn_kernel(q_ref, kp_ref, kc_ref, vp_ref, vc_ref, o_ref, lse_ref, *, slopes, dilation):
    n = pl.program_id(2)
    heads = HEADS_PER_GROUP
    rows = ATTN_QB * heads * ATTN_STEPS
    head_slices = [slice(h * HEAD_DIM, (h + 1) * HEAD_DIM) for h in range(heads)]
    blocks = [slice(s * ATTN_STEPS, (s + 1) * ATTN_STEPS) for s in range(ATTN_QB)]
    row = lax.broadcasted_iota(jnp.int32, (rows, ATTN_STEPS), 0)
    kj = lax.broadcasted_iota(jnp.int32, (rows, ATTN_STEPS), 1)
    diff = (row % ATTN_STEPS) - kj
    valid_c = diff >= 0
    has_prev = jnp.logical_or(row >= heads * ATTN_STEPS, n > 0)
    valid_p = jnp.logical_and(diff <= 0, has_prev)
    slope = jnp.full((rows, ATTN_STEPS), slopes[0], jnp.float32)
    head_of_row = (row // ATTN_STEPS) % heads
    for h in range(1, heads):
        slope = jnp.where(head_of_row == h, slopes[h], slope)
    bias_c = slope * (diff * dilation).astype(jnp.float32)
    bias_p = slope * ((diff + ATTN_STEPS) * dilation).astype(jnp.float32)
    scale = HEAD_DIM ** -0.5

    def prev_cur(cur_ref, prev_ref, s, sl):
        prev = prev_ref[:, sl] if s == 0 else cur_ref[blocks[s - 1], sl]
        return prev, cur_ref[blocks[s], sl]

    pairs = [(s, sl) for s in range(ATTN_QB) for sl in head_slices]
    s_c = jnp.concatenate([_dot_nt(q_ref[blocks[s], sl], prev_cur(kc_ref, kp_ref, s, sl)[1])
                           for s, sl in pairs], axis=0)
    s_p = jnp.concatenate([_dot_nt(q_ref[blocks[s], sl], prev_cur(kc_ref, kp_ref, s, sl)[0])
                           for s, sl in pairs], axis=0)
    s_c = jnp.where(valid_c, s_c * scale - bias_c, NEG_INF)
    s_p = jnp.where(valid_p, s_p * scale - bias_p, NEG_INF)
    m = jnp.max(jnp.maximum(s_c, s_p), axis=-1, keepdims=True)
    e_c = jnp.exp(s_c - m)
    e_p = jnp.exp(s_p - m)
    l = jnp.sum(e_c + e_p, axis=-1, keepdims=True)
    inv_l = 1.0 / l
    p_c = (e_c * inv_l).astype(jnp.bfloat16)
    p_p = (e_p * inv_l).astype(jnp.bfloat16)
    lse = m + jnp.log(l)
    for i, (s, sl) in enumerate(pairs):
        chain = slice(i * ATTN_STEPS, (i + 1) * ATTN_STEPS)
        v_prev, v_cur = prev_cur(vc_ref, vp_ref, s, sl)
        o_ref[blocks[s], sl] = _dot(p_c[chain], v_cur) + _dot(p_p[chain], v_prev)
        lse_ref[blocks[s], sl] = jnp.broadcast_to(lse[chain], (ATTN_STEPS, HEAD_DIM))


def _dilated_group(src, group):
    _, dilation = DILATED_GROUPS[group]
    length = SEQ // dilation
    span = ATTN_QB * ATTN_STEPS
    cb = src.shape[1] // dilation // GROUP_WIDTH
    view = src.reshape(BATCH, length, src.shape[1])
    blk = (None, span, GROUP_WIDTH)
    prev_blk = (None, ATTN_STEPS, GROUP_WIDTH)

    def col(section):
        return lambda b, r, n: (b, n, r * cb + section)

    def col_prev(section):
        return lambda b, r, n: (b, jnp.maximum(n * ATTN_QB - 1, 0), r * cb + section)

    slopes = tuple(_alibi_slopes(N_SELF_HEADS)[group * HEADS_PER_GROUP:(group + 1) * HEADS_PER_GROUP])
    out_shape = jax.ShapeDtypeStruct((BATCH, length, dilation * GROUP_WIDTH), jnp.float32)
    out_spec = pl.BlockSpec(blk, lambda b, r, n: (b, n, r))
    o, lse = pl.pallas_call(
        functools.partial(_dil_attn_kernel, slopes=slopes, dilation=dilation),
        out_shape=(out_shape, out_shape),
        grid=(BATCH, dilation, length // span),
        in_specs=[pl.BlockSpec(blk, col(0)),
                  pl.BlockSpec(prev_blk, col_prev(1)), pl.BlockSpec(blk, col(1)),
                  pl.BlockSpec(prev_blk, col_prev(2)), pl.BlockSpec(blk, col(2))],
        out_specs=(out_spec, out_spec),
        compiler_params=_params("parallel", "parallel", "arbitrary"),
        name=f"dilated_attn_g{group}",
    )(view, view, view, view, view)
    return o, lse


def _dil_combine_kernel(*refs):
    ngroups = len(DILATED_GROUPS)
    o_refs, l_refs, out_ref = refs[:ngroups], refs[ngroups:2 * ngroups], refs[2 * ngroups]
    scratch = list(refs[2 * ngroups + 1:])
    os, ls = [], []
    for g, (_, dilation) in enumerate(DILATED_GROUPS):
        if dilation == 1:
            os.append(o_refs[g][...])
            ls.append(l_refs[g][...])
            continue
        rows = COMBINE_ROWS // dilation
        planes = GROUP_WIDTH // LANES
        o_scr, l_scr = scratch.pop(0), scratch.pop(0)
        for r in range(dilation):
            for c in range(planes):
                cols = slice(r * GROUP_WIDTH + c * LANES, r * GROUP_WIDTH + (c + 1) * LANES)
                o_scr[c, pl.ds(r, rows, stride=dilation), :] = o_refs[g][:, cols]
                l_scr[c, pl.ds(r, rows, stride=dilation), :] = l_refs[g][:, cols]
        os.append(jnp.concatenate([o_scr[c] for c in range(planes)], axis=1))
        ls.append(jnp.concatenate([l_scr[c] for c in range(planes)], axis=1))
    m = functools.reduce(jnp.maximum, ls)
    es = [jnp.exp(l - m) for l in ls]
    inv = 1.0 / functools.reduce(lambda a, b: a + b, es)
    for g in range(ngroups):
        out_ref[:, g * GROUP_WIDTH:(g + 1) * GROUP_WIDTH] = (os[g] * (es[g] * inv)).astype(out_ref.dtype)


def _dilated_attention(x, w_in, layer, proj):
    sources = [proj] + [_proj_residue_major(x, w_in, layer, g) for g in range(1, len(DILATED_GROUPS))]
    outs, lses = zip(*[_dilated_group(src, g) for g, src in enumerate(sources)])

    def spec(dilation):
        return pl.BlockSpec((COMBINE_ROWS // dilation, dilation * GROUP_WIDTH), lambda i: (i, 0))

    def flat(a):
        return a.reshape(a.shape[0] * a.shape[1], a.shape[2])

    specs = [spec(d) for _, d in DILATED_GROUPS]
    scratch = [pltpu.VMEM((GROUP_WIDTH // LANES, COMBINE_ROWS, LANES), jnp.float32)
               for _, d in DILATED_GROUPS if d > 1 for _ in range(2)]
    return pl.pallas_call(
        _dil_combine_kernel,
        out_shape=jax.ShapeDtypeStruct((N_TOK, SELF_WIDTH), jnp.bfloat16),
        grid=(N_TOK // COMBINE_ROWS,),
        in_specs=specs + specs,
        out_specs=pl.BlockSpec((COMBINE_ROWS, SELF_WIDTH), lambda i: (i, 0)),
        scratch_shapes=scratch,
        compiler_params=_params("parallel"),
        name="dilated_combine",
    )(*[flat(o) for o in outs], *[flat(l) for l in lses])


def _retention_kernel(q_ref, k_ref, v_ref, g_ref, dmat_ref, kdec_ref, qdec_ref, cdec_ref, gn_ref,
                      o_ref, state_ref):
    @pl.when(pl.program_id(1) == 0)
    def _():
        state_ref[...] = jnp.zeros_like(state_ref)

    heads = range(N_RET_HEADS)
    cols = [slice(h * HEAD_DIM, (h + 1) * HEAD_DIM) for h in heads]
    bf16 = jnp.bfloat16
    scores = [_dot_nt(q_ref[:, c], k_ref[:, c]) * dmat_ref[h] for h, c in zip(heads, cols)]
    states = [state_ref[h] for h in heads]
    cross = [_dot(q_ref[:, c], states[h].astype(bf16)) * qdec_ref[:, c] for h, c in zip(heads, cols)]
    intra = [_dot(scores[h].astype(bf16), v_ref[:, c]) for h, c in zip(heads, cols)]
    kw = [(k_ref[:, c].astype(jnp.float32) * kdec_ref[:, c]).astype(bf16) for c in cols]
    for h, c in zip(heads, cols):
        kv = lax.dot_general(kw[h], v_ref[:, c], (((0,), (0,)), ((), ())), preferred_element_type=jnp.float32)
        state_ref[h] = states[h] * cdec_ref[h] + kv
    r = jnp.concatenate([intra[h] + cross[h] for h in heads], axis=0)
    mu = jnp.mean(r, axis=-1, keepdims=True)
    rc = r - mu
    var = jnp.mean(rc * rc, axis=-1, keepdims=True)
    rn = rc * lax.rsqrt(var + LN_EPS)
    for h, c in zip(heads, cols):
        gate = g_ref[:, c].astype(jnp.float32)
        normed = rn[h * RET_CHUNK:(h + 1) * RET_CHUNK] * gn_ref[:, c]
        o_ref[:, c] = (gate * (1.0 / (1.0 + jnp.exp(-gate))) * normed).astype(o_ref.dtype)


def _retention(proj, gn_gain):
    c = RET_CHUNK
    log_gamma = jnp.log1p(-jnp.exp2(-(5.0 + jnp.arange(N_RET_HEADS, dtype=jnp.float32))))
    idx = jnp.arange(c, dtype=jnp.float32)
    diff = idx[:, None] - idx[None, :]
    decay = jnp.where(diff >= 0, jnp.exp(jnp.maximum(diff, 0.0)[None] * log_gamma[:, None, None]), 0.0)
    scale = HEAD_DIM ** -0.5
    dmat = decay * scale
    k_decay = jnp.exp((c - 1 - idx)[:, None] * log_gamma[None, :]) * scale
    q_decay = jnp.exp((idx + 1.0)[:, None] * log_gamma[None, :])
    kdec = jnp.repeat(k_decay, HEAD_DIM, axis=1)
    qdec = jnp.repeat(q_decay, HEAD_DIM, axis=1)
    cdec = jnp.broadcast_to(jnp.exp(c * log_gamma)[:, None, None], (N_RET_HEADS, 1, HEAD_DIM))
    gn = gn_gain.reshape(1, SELF_WIDTH).astype(jnp.float32)
    nchunk = SEQ // c
    blk = (c, SELF_WIDTH)

    def section(s):
        return pl.BlockSpec(blk, lambda b, n: (b * nchunk + n, s))

    const2 = lambda b, n: (0, 0)
    const3 = lambda b, n: (0, 0, 0)
    return pl.pallas_call(
        _retention_kernel,
        out_shape=jax.ShapeDtypeStruct((N_TOK, SELF_WIDTH), jnp.bfloat16),
        grid=(BATCH, nchunk),
        in_specs=[section(0), section(1), section(2), section(3),
                  pl.BlockSpec((N_RET_HEADS, c, c), const3),
                  pl.BlockSpec(blk, const2), pl.BlockSpec(blk, const2),
                  pl.BlockSpec((N_RET_HEADS, 1, HEAD_DIM), const3),
                  pl.BlockSpec((1, SELF_WIDTH), const2)],
        out_specs=pl.BlockSpec(blk, lambda b, n: (b * nchunk + n, 0)),
        scratch_shapes=[pltpu.VMEM((N_RET_HEADS, HEAD_DIM, HEAD_DIM), jnp.float32)],
        compiler_params=_params("parallel", "arbitrary"),
        name="retention",
    )(proj, proj, proj, proj, dmat, kdec, qdec, cdec, gn)


def _mem_attn_kernel(q_ref, k_ref, v_ref, o_ref):
    scale = HEAD_DIM ** -0.5
    for h in range(MEM_HEADS):
        sl = slice(h * HEAD_DIM, (h + 1) * HEAD_DIM)
        s = _dot_nt(q_ref[:, sl], k_ref[:, sl]) * scale
        e = jnp.exp(s - jnp.max(s, axis=-1, keepdims=True))
        p = e * (1.0 / jnp.sum(e, axis=-1, keepdims=True))
        o_ref[:, sl] = _dot(p.astype(jnp.bfloat16), v_ref[:, sl]).astype(o_ref.dtype)


def _memory_attention(proj, memkv):
    qcol = proj.shape[1] // MEM_WIDTH - 1
    per_b = SEQ // ROW_TILE
    return pl.pallas_call(
        _mem_attn_kernel,
        out_shape=jax.ShapeDtypeStruct((N_TOK, MEM_WIDTH), jnp.bfloat16),
        grid=(BATCH, per_b),
        in_specs=[pl.BlockSpec((ROW_TILE, MEM_WIDTH), lambda b, i: (b * per_b + i, qcol)),
                  pl.BlockSpec((MEM_LEN, MEM_WIDTH), lambda b, i: (b, 0)),
                  pl.BlockSpec((MEM_LEN, MEM_WIDTH), lambda b, i: (b, 1))],
        out_specs=pl.BlockSpec((ROW_TILE, MEM_WIDTH), lambda b, i: (b * per_b + i, 0)),
        compiler_params=_params("parallel", "parallel"),
        name="memory_attn",
    )(proj, memkv, memkv)


def _mix_ln_kernel(so_ref, mo_ref, wt_ref, wb_ref, h_ref, g_ref, b_ref, hf_ref, hb_ref):
    half = ROW_TILE // 2
    for rows in (slice(0, half), slice(half, ROW_TILE)):
        mix = _dot(so_ref[rows, :], wt_ref[...]) + _dot(mo_ref[rows, :], wb_ref[...])
        y = _layer_norm_rows(DEEPNORM_ALPHA * h_ref[rows, :] + mix, g_ref[...], b_ref[...])
        hf_ref[rows, :] = y
        hb_ref[rows, :] = y.astype(hb_ref.dtype)


def _mix_ln(self_out, mem_out, w_mix, h, g, b):
    row = lambda i: (i, 0)
    const = lambda i: (0, 0)
    return pl.pallas_call(
        _mix_ln_kernel,
        out_shape=(jax.ShapeDtypeStruct((N_TOK, D_MODEL), jnp.float32),
                   jax.ShapeDtypeStruct((N_TOK, D_MODEL), jnp.bfloat16)),
        grid=(N_TOK // ROW_TILE,),
        in_specs=[pl.BlockSpec((ROW_TILE, SELF_WIDTH), row),
                  pl.BlockSpec((ROW_TILE, MEM_WIDTH), row),
                  pl.BlockSpec((SELF_WIDTH, D_MODEL), const),
                  pl.BlockSpec((MEM_WIDTH, D_MODEL), lambda i: (SELF_WIDTH // MEM_WIDTH, 0)),
                  pl.BlockSpec((ROW_TILE, D_MODEL), row),
                  pl.BlockSpec((1, D_MODEL), const), pl.BlockSpec((1, D_MODEL), const)],
        out_specs=(pl.BlockSpec((ROW_TILE, D_MODEL), row), pl.BlockSpec((ROW_TILE, D_MODEL), row)),
        compiler_params=_params("parallel"),
        name="mix_ln",
    )(self_out, mem_out, w_mix, w_mix, h, g.reshape(1, D_MODEL), b.reshape(1, D_MODEL))


def _router_kernel(h_ref, whi_ref, wlo_ref, b_ref, e_ref, g_ref, r_ref, cnt_ref, run_ref):
    tm = ROUTER_TM

    @pl.when(pl.program_id(0) == 0)
    def _():
        run_ref[...] = jnp.zeros_like(run_ref)

    h = h_ref[...]
    h_hi = h.astype(jnp.bfloat16)
    h_lo = (h - h_hi.astype(jnp.float32)).astype(jnp.bfloat16)
    logits = (_dot(h_hi, whi_ref[...]) + _dot(h_lo, whi_ref[...]) + _dot(h_hi, wlo_ref[...])) + b_ref[...]
    lane = lax.broadcasted_iota(jnp.int32, (tm, LANES), 1).astype(jnp.float32)
    work = logits
    vals, idxs, hots = [], [], []
    for _ in range(TOP_K):
        m = jnp.max(work, axis=-1, keepdims=True)
        idx = jnp.min(jnp.where(work == m, lane, float(LANES)), axis=-1, keepdims=True)
        hot = lane == idx
        vals.append(m)
        idxs.append(idx)
        hots.append(hot)
        work = jnp.where(hot, -jnp.inf, work)
    exps = [jnp.exp(v - vals[0]) for v in vals]
    inv = 1.0 / (exps[0] + exps[1] + exps[2] + exps[3])
    cnt = jnp.zeros((tm, LANES), jnp.float32)
    for hot in hots:
        cnt = cnt + hot.astype(jnp.float32)
    row = lax.broadcasted_iota(jnp.int32, (tm, tm), 0)
    colm = lax.broadcasted_iota(jnp.int32, (tm, tm), 1)
    tri = (row > colm).astype(jnp.bfloat16)
    before = run_ref[...] + _dot(tri, cnt.astype(jnp.bfloat16))
    lane_i = lax.broadcasted_iota(jnp.int32, (tm, LANES), 1)
    e_out = jnp.zeros((tm, LANES), jnp.float32)
    g_out = jnp.zeros((tm, LANES), jnp.float32)
    r_out = jnp.zeros((tm, LANES), jnp.float32)
    for k in range(TOP_K):
        rank = jnp.sum(jnp.where(hots[k], before, 0.0), axis=-1, keepdims=True)
        e_out = jnp.where(lane_i == k, idxs[k], e_out)
        g_out = jnp.where(lane_i == k, exps[k] * inv, g_out)
        r_out = jnp.where(lane_i == k, rank, r_out)
    e_ref[...] = e_out.T[0:SUBLANES, :].astype(jnp.int32)
    g_ref[...] = g_out
    r_ref[...] = r_out.T[0:SUBLANES, :].astype(jnp.int32)
    run_ref[...] = run_ref[...] + jnp.sum(cnt, axis=0, keepdims=True)
    cnt_ref[...] = run_ref[...].astype(jnp.int32)


def _router(h, router_w, router_b):
    w = jnp.zeros((D_MODEL, LANES), jnp.float32).at[:, :N_EXPERTS].set(router_w)
    w_hi = w.astype(jnp.bfloat16)
    w_lo = (w - w_hi.astype(jnp.float32)).astype(jnp.bfloat16)
    b = jnp.full((1, LANES), NEG_INF, jnp.float32).at[0, :N_EXPERTS].set(router_b)
    tm = ROUTER_TM
    row = lambda i: (i, 0)
    const = lambda i: (0, 0)
    lanes_out = pl.BlockSpec((tm, LANES), row)
    slots_out = pl.BlockSpec((SUBLANES, tm), lambda i: (0, i))
    e, g, r, cnt = pl.pallas_call(
        _router_kernel,
        out_shape=(jax.ShapeDtypeStruct((SUBLANES, N_TOK), jnp.int32),
                   jax.ShapeDtypeStruct((N_TOK, LANES), jnp.float32),
                   jax.ShapeDtypeStruct((SUBLANES, N_TOK), jnp.int32),
                   jax.ShapeDtypeStruct((1, LANES), jnp.int32)),
        grid=(N_TOK // tm,),
        in_specs=[pl.BlockSpec((tm, D_MODEL), row), pl.BlockSpec((D_MODEL, LANES), const),
                  pl.BlockSpec((D_MODEL, LANES), const), pl.BlockSpec((1, LANES), const)],
        out_specs=(slots_out, lanes_out, slots_out, pl.BlockSpec((1, LANES), const)),
        scratch_shapes=[pltpu.VMEM((1, LANES), jnp.float32)],
        compiler_params=_params("arbitrary"),
        name="router",
    )(h, w_hi, w_lo, b)
    return e[:TOP_K], g, r[:TOP_K], cnt[0, :N_EXPERTS]


def _route_tables(top_e, rank, cnt):
    ntile = (cnt + MOE_TILE - 1) // MOE_TILE
    tile_end = jnp.cumsum(ntile)
    tile_base = tile_end - ntile
    experts = jnp.arange(N_EXPERTS, dtype=jnp.int32)[:, None, None]
    base = jnp.sum(jnp.where(top_e[None] == experts, tile_base[:, None, None], 0), axis=0)
    pos = (base * MOE_TILE + rank).astype(jnp.int32).reshape(-1)
    n_tiles = tile_end[-1:].astype(jnp.int32)
    w = jnp.minimum(jnp.arange(MOE_TILES, dtype=jnp.int32), n_tiles[0] - 1)
    tile_e = jnp.minimum(jnp.searchsorted(tile_end, w, side="right"), N_EXPERTS - 1).astype(jnp.int32)
    rows = jnp.clip(cnt[tile_e] - (w - tile_base[tile_e]) * MOE_TILE, 0, MOE_TILE)
    nsub = ((rows + MOE_SUB - 1) // MOE_SUB).astype(jnp.int32)
    pad_start = (tile_base * MOE_TILE + cnt).astype(jnp.int32)
    pad_n = ((-cnt) % MOE_SUB).astype(jnp.int32)
    return pos, n_tiles, tile_e, nsub, pad_start, pad_n


def _row_copy(src_vmem, src_row, dst_hbm, dst_row, sem):
    return pltpu.make_async_copy(src_vmem.at[pl.ds(src_row, 1)], dst_hbm.at[pl.ds(dst_row, 1)], sem)


def _pack_bf16_pairs(x):
    half = x.shape[1] // 2
    lo = lax.bitcast_convert_type(x[:, :half].astype(jnp.float32), jnp.uint32)
    hi = lax.bitcast_convert_type(x[:, half:].astype(jnp.float32), jnp.uint32)
    return (lo >> 16) | (hi & jnp.uint32(0xFFFF0000))


def _unpack_bf16_pairs(words):
    lo = lax.bitcast_convert_type(words << 16, jnp.float32)
    hi = lax.bitcast_convert_type(words & jnp.uint32(0xFFFF0000), jnp.float32)
    return lo.astype(jnp.bfloat16), hi.astype(jnp.bfloat16)


def _dispatch_kernel(pos_ref, pad_start_ref, pad_n_ref, hb_ref, xs_hbm, h_ref, sem):
    step = pl.program_id(0)
    base = step * DISPATCH_TB
    h_ref[...] = _pack_bf16_pairs(hb_ref[...])

    for r in range(DISPATCH_TB):
        for k in range(TOP_K):
            _row_copy(h_ref, r, xs_hbm, pos_ref[k * N_TOK + base + r], sem).start(priority=k % 2)

    @pl.when(step == 0)
    def _():
        def per_expert(e, carry):
            start = pad_start_ref[e]
            n = pad_n_ref[e]

            def fill(i, c):
                _row_copy(h_ref, 0, xs_hbm, start + i, sem).start()
                return c

            def drain(_, c):
                _row_copy(h_ref, 0, xs_hbm, 0, sem).wait()
                return c

            lax.fori_loop(0, n, fill, 0)
            lax.fori_loop(0, n, drain, 0)
            return carry

        lax.fori_loop(0, N_EXPERTS, per_expert, 0)

    for _ in range(TOP_K):
        pltpu.make_async_copy(h_ref, xs_hbm.at[pl.ds(0, DISPATCH_TB)], sem).wait()


def _dispatch(hb, pos, pad_start, pad_n):
    return pl.pallas_call(
        _dispatch_kernel,
        out_shape=jax.ShapeDtypeStruct((MOE_ROWS, D_MODEL // 2), jnp.uint32),
        grid_spec=pltpu.PrefetchScalarGridSpec(
            num_scalar_prefetch=3,
            grid=(N_TOK // DISPATCH_TB,),
            in_specs=[pl.BlockSpec((DISPATCH_TB, D_MODEL), lambda i, p, s, n: (i, 0))],
            out_specs=pl.BlockSpec(memory_space=pl.ANY),
            scratch_shapes=[pltpu.VMEM((DISPATCH_TB, D_MODEL // 2), jnp.uint32),
                            pltpu.SemaphoreType.DMA(())],
        ),
        compiler_params=_params("arbitrary"),
        name="moe_dispatch",
    )(pos, pad_start, pad_n, hb)


def _moe_weight_map(layer, first, hidden_axis):
    def index_map(w, j, tile_e, nsub):
        return (layer, tile_e[w], first + j, 0) if hidden_axis == 2 else (layer, tile_e[w], 0, first + j)

    return index_map


def _moe_rows(nsub, accumulate):
    fast = nsub < 0
    for s in MOE_FAST_SUBS:
        fast = jnp.logical_or(fast, nsub == s)
        pl.when(nsub == s)(functools.partial(accumulate, slice(0, s * MOE_SUB)))

    @pl.when(jnp.logical_and(nsub > 0, jnp.logical_not(fast)))
    def _():
        def one(i, carry):
            accumulate(pl.ds(pl.multiple_of(i * MOE_SUB, MOE_SUB), MOE_SUB))
            return carry

        lax.fori_loop(0, nsub, one, 0)


def _moe_expert_kernel(exp_ref, nsub_ref, x_ref, wg_ref, wl_ref, bg_ref, bl_ref, wo_ref, bo_ref, o_ref):
    first = pl.program_id(1) == 0
    nsub = nsub_ref[pl.program_id(0)]
    half = D_MODEL // 2

    @pl.when(first)
    def _():
        o_ref[...] = jnp.broadcast_to(bo_ref[...], (MOE_TILE, D_MODEL))

    def up(lo, hi, w_ref, b_ref):
        return (_dot(lo, w_ref[0:half, :].astype(jnp.bfloat16))
                + _dot(hi, w_ref[half:D_MODEL, :].astype(jnp.bfloat16)) + b_ref[...])

    def accumulate(rows):
        lo, hi = _unpack_bf16_pairs(x_ref[rows, :])
        gate = up(lo, hi, wg_ref, bg_ref)
        lin = up(lo, hi, wl_ref, bl_ref)
        gate = jnp.minimum(gate, SWIGLU_LIMIT)
        lin = jnp.clip(lin, -SWIGLU_LIMIT, SWIGLU_LIMIT)
        act = (gate * (1.0 / (1.0 + jnp.exp(-SWIGLU_ALPHA * gate))) * (lin + 1.0)).astype(jnp.bfloat16)
        for c in range(D_MODEL // MOE_DOWN_TN):
            cols = slice(c * MOE_DOWN_TN, (c + 1) * MOE_DOWN_TN)
            o_ref[rows, cols] = o_ref[rows, cols] + _dot(act, wo_ref[:, cols].astype(jnp.bfloat16))

    _moe_rows(nsub, accumulate)


def _moe_experts(xs, w_in, b_in, w_out, b_out, layer, n_tiles, tile_e, nsub):
    nj = D_EXPERT // MOE_TF
    b_in4 = b_in.reshape(DEPTH, N_EXPERTS, 1, 2 * D_EXPERT)
    b_out4 = b_out.reshape(DEPTH, N_EXPERTS, 1, D_MODEL)
    wblk = (None, None, D_MODEL, MOE_TF)
    bblk = (None, None, 1, MOE_TF)
    tile_map = lambda w, j, e, n: (w, 0)
    return pl.pallas_call(
        _moe_expert_kernel,
        out_shape=jax.ShapeDtypeStruct((MOE_ROWS, D_MODEL), jnp.float32),
        grid_spec=pltpu.PrefetchScalarGridSpec(
            num_scalar_prefetch=2,
            grid=(n_tiles[0], nj),
            in_specs=[pl.BlockSpec((MOE_TILE, D_MODEL // 2), tile_map),
                      pl.BlockSpec(wblk, _moe_weight_map(layer, 0, 3)),
                      pl.BlockSpec(wblk, _moe_weight_map(layer, nj, 3)),
                      pl.BlockSpec(bblk, _moe_weight_map(layer, 0, 3)),
                      pl.BlockSpec(bblk, _moe_weight_map(layer, nj, 3)),
                      pl.BlockSpec((None, None, MOE_TF, D_MODEL), _moe_weight_map(layer, 0, 2)),
                      pl.BlockSpec((None, None, 1, D_MODEL), lambda w, j, e, n: (layer, e[w], 0, 0))],
            out_specs=pl.BlockSpec((MOE_TILE, D_MODEL), tile_map),
        ),
        compiler_params=_params("arbitrary", "arbitrary", vmem_limit=MOE_VMEM_LIMIT),
        name="moe_experts",
    )(tile_e, nsub, xs, w_in, w_in, b_in4, b_in4, w_out, b_out4)


def _combine_ln_kernel(pos_ref, y_hbm, gates_ref, h_ref, g_ref, b_ref, hf_ref, hb_ref, buf0, buf1, sems):
    step = pl.program_id(0)
    last = pl.num_programs(0) - 1

    def row_copy(tile, r, k, buf, sem):
        src = y_hbm.at[pl.ds(pos_ref[k * N_TOK + tile * COMBINE_TB + r], 1)]
        return pltpu.make_async_copy(src, buf.at[k, pl.ds(r, 1)], sem)

    def wait_tile(buf, sem):
        for k in range(TOP_K):
            pltpu.make_async_copy(y_hbm.at[pl.ds(0, COMBINE_TB)], buf.at[k], sem).wait()

    @pl.when(step == 0)
    def _():
        def issue(r, carry):
            for k in range(TOP_K):
                row_copy(0, r, k, buf0, sems.at[0]).start()
            return carry

        lax.fori_loop(0, COMBINE_TB, issue, 0, unroll=8)

    def run(cur, cur_sem, nxt, nxt_sem):
        nxt_tile = jnp.minimum(step + 1, last)
        wait_tile(cur, cur_sem)
        for r in range(COMBINE_TB):
            for k in range(TOP_K):
                row_copy(nxt_tile, r, k, nxt, nxt_sem).start(priority=k % 2)
        gates = gates_ref[...]
        ffn = gates[:, 0:1] * cur[0]
        for k in range(1, TOP_K):
            ffn = ffn + gates[:, k:k + 1] * cur[k]
        y = _layer_norm_rows(DEEPNORM_ALPHA * h_ref[...] + ffn, g_ref[...], b_ref[...])
        hf_ref[...] = y
        hb_ref[...] = y.astype(hb_ref.dtype)

        @pl.when(step == last)
        def _():
            wait_tile(nxt, nxt_sem)

    pl.when(step % 2 == 0)(functools.partial(run, buf0, sems.at[0], buf1, sems.at[1]))
    pl.when(step % 2 == 1)(functools.partial(run, buf1, sems.at[1], buf0, sems.at[0]))


def _combine_ln(y, pos, gates, h, g, b):
    tb = COMBINE_TB
    row = lambda i, p: (i, 0)
    const = lambda i, p: (0, 0)
    return pl.pallas_call(
        _combine_ln_kernel,
        out_shape=(jax.ShapeDtypeStruct((N_TOK, D_MODEL), jnp.float32),
                   jax.ShapeDtypeStruct((N_TOK, D_MODEL), jnp.bfloat16)),
        grid_spec=pltpu.PrefetchScalarGridSpec(
            num_scalar_prefetch=1,
            grid=(N_TOK // tb,),
            in_specs=[pl.BlockSpec(memory_space=pl.ANY),
                      pl.BlockSpec((tb, LANES), row),
                      pl.BlockSpec((tb, D_MODEL), row),
                      pl.BlockSpec((1, D_MODEL), const), pl.BlockSpec((1, D_MODEL), const)],
            out_specs=(pl.BlockSpec((tb, D_MODEL), row), pl.BlockSpec((tb, D_MODEL), row)),
            scratch_shapes=[pltpu.VMEM((TOP_K, tb, D_MODEL), jnp.float32),
                            pltpu.VMEM((TOP_K, tb, D_MODEL), jnp.float32),
                            pltpu.SemaphoreType.DMA((2,))],
        ),
        compiler_params=_params("arbitrary"),
        name="moe_combine_ln",
    )(pos, y, gates, h, g.reshape(1, D_MODEL), b.reshape(1, D_MODEL))


def _moe(h, hb, layer, router_w, router_b, w_in, b_in, w_out, b_out, ln_g, ln_b):
    top_e, gates, rank, cnt = _router(h, router_w[layer], router_b[layer])
    pos, n_tiles, tile_e, nsub, pad_start, pad_n = _route_tables(top_e, rank, cnt)
    xs = _dispatch(hb, pos, pad_start, pad_n)
    y = _moe_experts(xs, w_in, b_in, w_out, b_out, layer, n_tiles, tile_e, nsub)
    return _combine_ln(y, pos, gates, h, ln_g[layer], ln_b[layer])


def kernel(x, mem, w_in_dil, w_in_ret, ret_gn_g, w_mem_kv, w_mix_out, ln_mix_g, ln_mix_b, router_w, router_b, moe_w_in, moe_b_in, moe_w_out, moe_b_out, ln_ffn_g, ln_ffn_b):
    bf16 = jnp.bfloat16
    h = x.reshape(N_TOK, D_MODEL)
    hb = h
    memb = mem.reshape(BATCH * MEM_LEN, D_MODEL)
    for layer in range(DEPTH):
        slot = layer // 2
        if layer % 2 == 0:
            kb = SELF_WIDTH // GROUP_WIDTH
            proj = _matmul(hb, w_in_dil, slot, bf16, col_tiles=(0, kb, kb + 1))
            self_out = _dilated_attention(hb, w_in_dil, slot, proj)
        else:
            proj = _matmul(hb, w_in_ret, slot, bf16)
            self_out = _retention(proj, ret_gn_g[slot])
        memkv = _matmul(memb, w_mem_kv, layer, bf16)
        mem_out = _memory_attention(proj, memkv)
        h, hb = _mix_ln(self_out, mem_out, w_mix_out[layer].astype(bf16), h,
                        ln_mix_g[layer], ln_mix_b[layer])
        h, hb = _moe(h, hb, layer, router_w, router_b, moe_w_in, moe_b_in, moe_w_out, moe_b_out,
                     ln_ffn_g, ln_ffn_b)
    return h.reshape(BATCH, SEQ, D_MODEL)
```

```python
import functools
import math

import jax
import jax.numpy as jnp
from jax import lax
from jax.experimental import pallas as pl
from jax.experimental.pallas import tpu as pltpu

D_MODEL = 2048
BATCH = 2
SEQ = 4096
DEPTH = 2
HEAD_DIM = 128
DILATED_GROUPS = ((128, 1), (512, 4), (2048, 16))
HEADS_PER_GROUP = 4
N_SELF_HEADS = len(DILATED_GROUPS) * HEADS_PER_GROUP
N_RET_HEADS = 12
SELF_WIDTH = N_SELF_HEADS * HEAD_DIM
MEM_HEADS = 4
MEM_LEN = 256
MEM_WIDTH = MEM_HEADS * HEAD_DIM
RET_CHUNK = 128
N_EXPERTS = 32
TOP_K = 4
D_EXPERT = D_MODEL
SWIGLU_ALPHA = 1.702
SWIGLU_LIMIT = 7.0
DEEPNORM_ALPHA = (2 * DEPTH) ** 0.25
LN_EPS = 1e-5
NEG_INF = -1e30

N_TOK = BATCH * SEQ
GROUP_WIDTH = HEADS_PER_GROUP * HEAD_DIM
SELF_QKV = 3 * GROUP_WIDTH
ATTN_STEPS = 128
ATTN_QB = 2

LANES = 128
SUBLANES = 8
VMEM_LIMIT = 56 * 1024 * 1024

MM_TM = 1024
MM_TN = 512
ROW_TILE = 512
COMBINE_ROWS = 512
ROUTER_TM = 256

MOE_SUB = 64
MOE_TILE = 1152
MOE_SUBS = MOE_TILE // MOE_SUB
MOE_FAST_SUBS = (15, 16, 17, 18)
MOE_TF = 512
MOE_DOWN_TN = 512
MOE_VMEM_LIMIT = 60 * 1024 * 1024
MOE_TILES = (N_TOK * TOP_K) // MOE_TILE + N_EXPERTS
MOE_ROWS = MOE_TILES * MOE_TILE
DISPATCH_TB = 512
COMBINE_TB = 256


def _alibi_slopes(n):
    def pow2(m):
        start = 2.0 ** (-8.0 / m)
        return [start ** (i + 1) for i in range(m)]

    if math.log2(n).is_integer():
        s = pow2(n)
    else:
        c = 2 ** math.floor(math.log2(n))
        s = pow2(c) + pow2(2 * c)[0::2][: n - c]
    return sorted(s, reverse=True)


def _params(*sem, vmem_limit=VMEM_LIMIT):
    return pltpu.CompilerParams(dimension_semantics=sem, vmem_limit_bytes=vmem_limit)


def _layer_norm_rows(z, g, b):
    mu = jnp.mean(z, axis=-1, keepdims=True)
    zc = z - mu
    var = jnp.mean(zc * zc, axis=-1, keepdims=True)
    return zc * lax.rsqrt(var + LN_EPS) * g + b


def _dot_nt(a, b):
    return lax.dot_general(a, b, (((1,), (1,)), ((), ())), preferred_element_type=jnp.float32)


def _dot(a, b):
    return jnp.dot(a, b, preferred_element_type=jnp.float32)


def _mm_kernel(x_ref, w_ref, o_ref):
    o_ref[...] = _dot(x_ref[...].astype(jnp.bfloat16), w_ref[...].astype(jnp.bfloat16)).astype(o_ref.dtype)


def _matmul(x, w, layer, out_dtype, col_tiles=None):
    m, k = x.shape
    first, stride, count = col_tiles if col_tiles else (0, 1, w.shape[2] // MM_TN)
    tm = min(MM_TM * (4 // x.dtype.itemsize), m)
    return pl.pallas_call(
        _mm_kernel,
        out_shape=jax.ShapeDtypeStruct((m, count * MM_TN), out_dtype),
        grid=(m // tm, count),
        in_specs=[pl.BlockSpec((tm, k), lambda i, j: (i, 0)),
                  pl.BlockSpec((None, k, MM_TN), lambda i, j: (layer, 0, first + stride * j))],
        out_specs=pl.BlockSpec((tm, MM_TN), lambda i, j: (i, j)),
        compiler_params=_params("parallel", "parallel"),
        name="dense_matmul",
    )(x, w)


def _proj_residue_kernel(x_ref, wq_ref, wk_ref, wv_ref, o_ref, scr_ref, *, dilation):
    x = x_ref[...].astype(jnp.bfloat16)
    rows = MM_TM // dilation
    for section, w_ref in enumerate((wq_ref, wk_ref, wv_ref)):
        res = _dot(x, w_ref[...].astype(jnp.bfloat16))
        for c in range(GROUP_WIDTH // LANES):
            scr_ref[c] = res[:, c * LANES:(c + 1) * LANES]
        for r in range(dilation):
            for c in range(GROUP_WIDTH // LANES):
                col = r * SELF_QKV + section * GROUP_WIDTH + c * LANES
                o_ref[:, col:col + LANES] = scr_ref[c, pl.ds(r, rows, stride=dilation), :].astype(o_ref.dtype)


def _proj_residue_major(x, w, layer, group):
    _, dilation = DILATED_GROUPS[group]
    k = x.shape[1]
    kb = SELF_WIDTH // GROUP_WIDTH

    def wspec(section):
        return pl.BlockSpec((None, k, GROUP_WIDTH), lambda i: (layer, 0, section * kb + group))

    return pl.pallas_call(
        functools.partial(_proj_residue_kernel, dilation=dilation),
        out_shape=jax.ShapeDtypeStruct((N_TOK // dilation, dilation * SELF_QKV), jnp.bfloat16),
        grid=(N_TOK // MM_TM,),
        in_specs=[pl.BlockSpec((MM_TM, k), lambda i: (i, 0)), wspec(0), wspec(1), wspec(2)],
        out_specs=pl.BlockSpec((MM_TM // dilation, dilation * SELF_QKV), lambda i: (i, 0)),
        scratch_shapes=[pltpu.VMEM((GROUP_WIDTH // LANES, MM_TM, LANES), jnp.float32)],
        compiler_params=_params("parallel"),
        name=f"proj_residue_g{group}",
    )(x, w, w, w)


def _dil_attn_kernel(q_ref, kp_ref, kc_ref, vp_ref, vc_ref, o_ref, lse_ref, *, slopes, dilation):
    n = pl.program_id(2)
    heads = HEADS_PER_GROUP
    rows = ATTN_QB * heads * ATTN_STEPS
    head_slices = [slice(h * HEAD_DIM, (h + 1) * HEAD_DIM) for h in range(heads)]
    blocks = [slice(s * ATTN_STEPS, (s + 1) * ATTN_STEPS) for s in range(ATTN_QB)]
    row = lax.broadcasted_iota(jnp.int32, (rows, ATTN_STEPS), 0)
    kj = lax.broadcasted_iota(jnp.int32, (rows, ATTN_STEPS), 1)
    diff = (row % ATTN_STEPS) - kj
    valid_c = diff >= 0
    has_prev = jnp.logical_or(row >= heads * ATTN_STEPS, n > 0)
    valid_p = jnp.logical_and(diff <= 0, has_prev)
    slope = jnp.full((rows, ATTN_STEPS), slopes[0], jnp.float32)
    head_of_row = (row // ATTN_STEPS) % heads
    for h in range(1, heads):
        slope = jnp.where(head_of_row == h, slopes[h], slope)
    bias_c = slope * (diff * dilation).astype(jnp.float32)
    bias_p = slope * ((diff + ATTN_STEPS) * dilation).astype(jnp.float32)
    scale = HEAD_DIM ** -0.5

    def prev_cur(cur_ref, prev_ref, s, sl):
        prev = prev_ref[:, sl] if s == 0 else cur_ref[blocks[s - 1], sl]
        return prev, cur_ref[blocks[s], sl]

    pairs = [(s, sl) for s in range(ATTN_QB) for sl in head_slices]
    s_c = jnp.concatenate([_dot_nt(q_ref[blocks[s], sl], prev_cur(kc_ref, kp_ref, s, sl)[1])
                           for s, sl in pairs], axis=0)
    s_p = jnp.concatenate([_dot_nt(q_ref[blocks[s], sl], prev_cur(kc_ref, kp_ref, s, sl)[0])
                           for s, sl in pairs], axis=0)
    s_c = jnp.where(valid_c, s_c * scale - bias_c, NEG_INF)
    s_p = jnp.where(valid_p, s_p * scale - bias_p, NEG_INF)
    m = jnp.max(jnp.maximum(s_c, s_p), axis=-1, keepdims=True)
    e_c = jnp.exp(s_c - m)
    e_p = jnp.exp(s_p - m)
    l = jnp.sum(e_c + e_p, axis=-1, keepdims=True)
    inv_l = 1.0 / l
    p_c = (e_c * inv_l).astype(jnp.bfloat16)
    p_p = (e_p * inv_l).astype(jnp.bfloat16)
    lse = m + jnp.log(l)
    for i, (s, sl) in enumerate(pairs):
        chain = slice(i * ATTN_STEPS, (i + 1) * ATTN_STEPS)
        v_prev, v_cur = prev_cur(vc_ref, vp_ref, s, sl)
        o_ref[blocks[s], sl] = _dot(p_c[chain], v_cur) + _dot(p_p[chain], v_prev)
        lse_ref[blocks[s], sl] = jnp.broadcast_to(lse[chain], (ATTN_STEPS, HEAD_DIM))


def _dilated_group(src, group):
    _, dilation = DILATED_GROUPS[group]
    length = SEQ // dilation
    span = ATTN_QB * ATTN_STEPS
    cb = src.shape[1] // dilation // GROUP_WIDTH
    view = src.reshape(BATCH, length, src.shape[1])
    blk = (None, span, GROUP_WIDTH)
    prev_blk = (None, ATTN_STEPS, GROUP_WIDTH)

    def col(section):
        return lambda b, r, n: (b, n, r * cb + section)

    def col_prev(section):
        return lambda b, r, n: (b, jnp.maximum(n * ATTN_QB - 1, 0), r * cb + section)

    slopes = tuple(_alibi_slopes(N_SELF_HEADS)[group * HEADS_PER_GROUP:(group + 1) * HEADS_PER_GROUP])
    out_shape = jax.ShapeDtypeStruct((BATCH, length, dilation * GROUP_WIDTH), jnp.float32)
    out_spec = pl.BlockSpec(blk, lambda b, r, n: (b, n, r))
    o, lse = pl.pallas_call(
        functools.partial(_dil_attn_kernel, slopes=slopes, dilation=dilation),
        out_shape=(out_shape, out_shape),
        grid=(BATCH, dilation, length // span),
        in_specs=[pl.BlockSpec(blk, col(0)),
                  pl.BlockSpec(prev_blk, col_prev(1)), pl.BlockSpec(blk, col(1)),
                  pl.BlockSpec(prev_blk, col_prev(2)), pl.BlockSpec(blk, col(2))],
        out_specs=(out_spec, out_spec),
        compiler_params=_params("parallel", "parallel", "arbitrary"),
        name=f"dilated_attn_g{group}",
    )(view, view, view, view, view)
    return o, lse


def _dil_combine_kernel(*refs):
    ngroups = len(DILATED_GROUPS)
    o_refs, l_refs, out_ref = refs[:ngroups], refs[ngroups:2 * ngroups], refs[2 * ngroups]
    scratch = list(refs[2 * ngroups + 1:])
    os, ls = [], []
    for g, (_, dilation) in enumerate(DILATED_GROUPS):
        if dilation == 1:
            os.append(o_refs[g][...])
            ls.append(l_refs[g][...])
            continue
        rows = COMBINE_ROWS // dilation
        planes = GROUP_WIDTH // LANES
        o_scr, l_scr = scratch.pop(0), scratch.pop(0)
        for r in range(dilation):
            for c in range(planes):
                cols = slice(r * GROUP_WIDTH + c * LANES, r * GROUP_WIDTH + (c + 1) * LANES)
                o_scr[c, pl.ds(r, rows, stride=dilation), :] = o_refs[g][:, cols]
                l_scr[c, pl.ds(r, rows, stride=dilation), :] = l_refs[g][:, cols]
        os.append(jnp.concatenate([o_scr[c] for c in range(planes)], axis=1))
        ls.append(jnp.concatenate([l_scr[c] for c in range(planes)], axis=1))
    m = functools.reduce(jnp.maximum, ls)
    es = [jnp.exp(l - m) for l in ls]
    inv = 1.0 / functools.reduce(lambda a, b: a + b, es)
    for g in range(ngroups):
        out_ref[:, g * GROUP_WIDTH:(g + 1) * GROUP_WIDTH] = (os[g] * (es[g] * inv)).astype(out_ref.dtype)


def _dilated_attention(x, w_in, layer, proj):
    sources = [proj] + [_proj_residue_major(x, w_in, layer, g) for g in range(1, len(DILATED_GROUPS))]
    outs, lses = zip(*[_dilated_group(src, g) for g, src in enumerate(sources)])

    def spec(dilation):
        return pl.BlockSpec((COMBINE_ROWS // dilation, dilation * GROUP_WIDTH), lambda i: (i, 0))

    def flat(a):
        return a.reshape(a.shape[0] * a.shape[1], a.shape[2])

    specs = [spec(d) for _, d in DILATED_GROUPS]
    scratch = [pltpu.VMEM((GROUP_WIDTH // LANES, COMBINE_ROWS, LANES), jnp.float32)
               for _, d in DILATED_GROUPS if d > 1 for _ in range(2)]
    return pl.pallas_call(
        _dil_combine_kernel,
        out_shape=jax.ShapeDtypeStruct((N_TOK, SELF_WIDTH), jnp.bfloat16),
        grid=(N_TOK // COMBINE_ROWS,),
        in_specs=specs + specs,
        out_specs=pl.BlockSpec((COMBINE_ROWS, SELF_WIDTH), lambda i: (i, 0)),
        scratch_shapes=scratch,
        compiler_params=_params("parallel"),
        name="dilated_combine",
    )(*[flat(o) for o in outs], *[flat(l) for l in lses])


def _retention_kernel(q_ref, k_ref, v_ref, g_ref, dmat_ref, kdec_ref, qdec_ref, cdec_ref, gn_ref,
                      o_ref, state_ref):
    @pl.when(pl.program_id(1) == 0)
    def _():
        state_ref[...] = jnp.zeros_like(state_ref)

    heads = range(N_RET_HEADS)
    cols = [slice(h * HEAD_DIM, (h + 1) * HEAD_DIM) for h in heads]
    bf16 = jnp.bfloat16
    scores = [_dot_nt(q_ref[:, c], k_ref[:, c]) * dmat_ref[h] for h, c in zip(heads, cols)]
    states = [state_ref[h] for h in heads]
    cross = [_dot(q_ref[:, c], states[h].astype(bf16)) * qdec_ref[:, c] for h, c in zip(heads, cols)]
    intra = [_dot(scores[h].astype(bf16), v_ref[:, c]) for h, c in zip(heads, cols)]
    kw = [(k_ref[:, c].astype(jnp.float32) * kdec_ref[:, c]).astype(bf16) for c in cols]
    for h, c in zip(heads, cols):
        kv = lax.dot_general(kw[h], v_ref[:, c], (((0,), (0,)), ((), ())), preferred_element_type=jnp.float32)
        state_ref[h] = states[h] * cdec_ref[h] + kv
    r = jnp.concatenate([intra[h] + cross[h] for h in heads], axis=0)
    mu = jnp.mean(r, axis=-1, keepdims=True)
    rc = r - mu
    var = jnp.mean(rc * rc, axis=-1, keepdims=True)
    rn = rc * lax.rsqrt(var + LN_EPS)
    for h, c in zip(heads, cols):
        gate = g_ref[:, c].astype(jnp.float32)
        normed = rn[h * RET_CHUNK:(h + 1) * RET_CHUNK] * gn_ref[:, c]
        o_ref[:, c] = (gate * (1.0 / (1.0 + jnp.exp(-gate))) * normed).astype(o_ref.dtype)


def _retention(proj, gn_gain):
    c = RET_CHUNK
    log_gamma = jnp.log1p(-jnp.exp2(-(5.0 + jnp.arange(N_RET_HEADS, dtype=jnp.float32))))
    idx = jnp.arange(c, dtype=jnp.float32)
    diff = idx[:, None] - idx[None, :]
    decay = jnp.where(diff >= 0, jnp.exp(jnp.maximum(diff, 0.0)[None] * log_gamma[:, None, None]), 0.0)
    scale = HEAD_DIM ** -0.5
    dmat = decay * scale
    k_decay = jnp.exp((c - 1 - idx)[:, None] * log_gamma[None, :]) * scale
    q_decay = jnp.exp((idx + 1.0)[:, None] * log_gamma[None, :])
    kdec = jnp.repeat(k_decay, HEAD_DIM, axis=1)
    qdec = jnp.repeat(q_decay, HEAD_DIM, axis=1)
    cdec = jnp.broadcast_to(jnp.exp(c * log_gamma)[:, None, None], (N_RET_HEADS, 1, HEAD_DIM))
    gn = gn_gain.reshape(1, SELF_WIDTH).astype(jnp.float32)
    nchunk = SEQ // c
    blk = (c, SELF_WIDTH)

    def section(s):
        return pl.BlockSpec(blk, lambda b, n: (b * nchunk + n, s))

    const2 = lambda b, n: (0, 0)
    const3 = lambda b, n: (0, 0, 0)
    return pl.pallas_call(
        _retention_kernel,
        out_shape=jax.ShapeDtypeStruct((N_TOK, SELF_WIDTH), jnp.bfloat16),
        grid=(BATCH, nchunk),
        in_specs=[section(0), section(1), section(2), section(3),
                  pl.BlockSpec((N_RET_HEADS, c, c), const3),
                  pl.BlockSpec(blk, const2), pl.BlockSpec(blk, const2),
                  pl.BlockSpec((N_RET_HEADS, 1, HEAD_DIM), const3),
                  pl.BlockSpec((1, SELF_WIDTH), const2)],
        out_specs=pl.BlockSpec(blk, lambda b, n: (b * nchunk + n, 0)),
        scratch_shapes=[pltpu.VMEM((N_RET_HEADS, HEAD_DIM, HEAD_DIM), jnp.float32)],
        compiler_params=_params("parallel", "arbitrary"),
        name="retention",
    )(proj, proj, proj, proj, dmat, kdec, qdec, cdec, gn)


def _mem_attn_kernel(q_ref, k_ref, v_ref, o_ref):
    scale = HEAD_DIM ** -0.5
    for h in range(MEM_HEADS):
        sl = slice(h * HEAD_DIM, (h + 1) * HEAD_DIM)
        s = _dot_nt(q_ref[:, sl], k_ref[:, sl]) * scale
        e = jnp.exp(s - jnp.max(s, axis=-1, keepdims=True))
        p = e * (1.0 / jnp.sum(e, axis=-1, keepdims=True))
        o_ref[:, sl] = _dot(p.astype(jnp.bfloat16), v_ref[:, sl]).astype(o_ref.dtype)


def _memory_attention(proj, memkv):
    qcol = proj.shape[1] // MEM_WIDTH - 1
    per_b = SEQ // ROW_TILE
    return pl.pallas_call(
        _mem_attn_kernel,
        out_shape=jax.ShapeDtypeStruct((N_TOK, MEM_WIDTH), jnp.bfloat16),
        grid=(BATCH, per_b),
        in_specs=[pl.BlockSpec((ROW_TILE, MEM_WIDTH), lambda b, i: (b * per_b + i, qcol)),
                  pl.BlockSpec((MEM_LEN, MEM_WIDTH), lambda b, i: (b, 0)),
                  pl.BlockSpec((MEM_LEN, MEM_WIDTH), lambda b, i: (b, 1))],
        out_specs=pl.BlockSpec((ROW_TILE, MEM_WIDTH), lambda b, i: (b * per_b + i, 0)),
        compiler_params=_params("parallel", "parallel"),
        name="memory_attn",
    )(proj, memkv, memkv)


def _mix_ln_kernel(so_ref, mo_ref, wt_ref, wb_ref, h_ref, g_ref, b_ref, hf_ref, hb_ref):
    half = ROW_TILE // 2
    for rows in (slice(0, half), slice(half, ROW_TILE)):
        mix = _dot(so_ref[rows, :], wt_ref[...]) + _dot(mo_ref[rows, :], wb_ref[...])
        y = _layer_norm_rows(DEEPNORM_ALPHA * h_ref[rows, :] + mix, g_ref[...], b_ref[...])
        hf_ref[rows, :] = y
        hb_ref[rows, :] = y.astype(hb_ref.dtype)


def _mix_ln(self_out, mem_out, w_mix, h, g, b):
    row = lambda i: (i, 0)
    const = lambda i: (0, 0)
    return pl.pallas_call(
        _mix_ln_kernel,
        out_shape=(jax.ShapeDtypeStruct((N_TOK, D_MODEL), jnp.float32),
                   jax.ShapeDtypeStruct((N_TOK, D_MODEL), jnp.bfloat16)),
        grid=(N_TOK // ROW_TILE,),
        in_specs=[pl.BlockSpec((ROW_TILE, SELF_WIDTH), row),
                  pl.BlockSpec((ROW_TILE, MEM_WIDTH), row),
                  pl.BlockSpec((SELF_WIDTH, D_MODEL), const),
                  pl.BlockSpec((MEM_WIDTH, D_MODEL), lambda i: (SELF_WIDTH // MEM_WIDTH, 0)),
                  pl.BlockSpec((ROW_TILE, D_MODEL), row),
                  pl.BlockSpec((1, D_MODEL), const), pl.BlockSpec((1, D_MODEL), const)],
        out_specs=(pl.BlockSpec((ROW_TILE, D_MODEL), row), pl.BlockSpec((ROW_TILE, D_MODEL), row)),
        compiler_params=_params("parallel"),
        name="mix_ln",
    )(self_out, mem_out, w_mix, w_mix, h, g.reshape(1, D_MODEL), b.reshape(1, D_MODEL))


def _router_kernel(h_ref, whi_ref, wlo_ref, b_ref, e_ref, g_ref, r_ref, cnt_ref, run_ref):
    tm = ROUTER_TM

    @pl.when(pl.program_id(0) == 0)
    def _():
        run_ref[...] = jnp.zeros_like(run_ref)

    h = h_ref[...]
    h_hi = h.astype(jnp.bfloat16)
    h_lo = (h - h_hi.astype(jnp.float32)).astype(jnp.bfloat16)
    logits = (_dot(h_hi, whi_ref[...]) + _dot(h_lo, whi_ref[...]) + _dot(h_hi, wlo_ref[...])) + b_ref[...]
    lane = lax.broadcasted_iota(jnp.int32, (tm, LANES), 1).astype(jnp.float32)
    work = logits
    vals, idxs, hots = [], [], []
    for _ in range(TOP_K):
        m = jnp.max(work, axis=-1, keepdims=True)
        idx = jnp.min(jnp.where(work == m, lane, float(LANES)), axis=-1, keepdims=True)
        hot = lane == idx
        vals.append(m)
        idxs.append(idx)
        hots.append(hot)
        work = jnp.where(hot, -jnp.inf, work)
    exps = [jnp.exp(v - vals[0]) for v in vals]
    inv = 1.0 / (exps[0] + exps[1] + exps[2] + exps[3])
    cnt = jnp.zeros((tm, LANES), jnp.float32)
    for hot in hots:
        cnt = cnt + hot.astype(jnp.float32)
    row = lax.broadcasted_iota(jnp.int32, (tm, tm), 0)
    colm = lax.broadcasted_iota(jnp.int32, (tm, tm), 1)
    tri = (row > colm).astype(jnp.bfloat16)
    before = run_ref[...] + _dot(tri, cnt.astype(jnp.bfloat16))
    lane_i = lax.broadcasted_iota(jnp.int32, (tm, LANES), 1)
    e_out = jnp.zeros((tm, LANES), jnp.float32)
    g_out = jnp.zeros((tm, LANES), jnp.float32)
    r_out = jnp.zeros((tm, LANES), jnp.float32)
    for k in range(TOP_K):
        rank = jnp.sum(jnp.where(hots[k], before, 0.0), axis=-1, keepdims=True)
        e_out = jnp.where(lane_i == k, idxs[k], e_out)
        g_out = jnp.where(lane_i == k, exps[k] * inv, g_out)
        r_out = jnp.where(lane_i == k, rank, r_out)
    e_ref[...] = e_out.T[0:SUBLANES, :].astype(jnp.int32)
    g_ref[...] = g_out
    r_ref[...] = r_out.T[0:SUBLANES, :].astype(jnp.int32)
    run_ref[...] = run_ref[...] + jnp.sum(cnt, axis=0, keepdims=True)
    cnt_ref[...] = run_ref[...].astype(jnp.int32)


def _router(h, router_w, router_b):
    w = jnp.zeros((D_MODEL, LANES), jnp.float32).at[:, :N_EXPERTS].set(router_w)
    w_hi = w.astype(jnp.bfloat16)
    w_lo = (w - w_hi.astype(jnp.float32)).astype(jnp.bfloat16)
    b = jnp.full((1, LANES), NEG_INF, jnp.float32).at[0, :N_EXPERTS].set(router_b)
    tm = ROUTER_TM
    row = lambda i: (i, 0)
    const = lambda i: (0, 0)
    lanes_out = pl.BlockSpec((tm, LANES), row)
    slots_out = pl.BlockSpec((SUBLANES, tm), lambda i: (0, i))
    e, g, r, cnt = pl.pallas_call(
        _router_kernel,
        out_shape=(jax.ShapeDtypeStruct((SUBLANES, N_TOK), jnp.int32),
                   jax.ShapeDtypeStruct((N_TOK, LANES), jnp.float32),
                   jax.ShapeDtypeStruct((SUBLANES, N_TOK), jnp.int32),
                   jax.ShapeDtypeStruct((1, LANES), jnp.int32)),
        grid=(N_TOK // tm,),
        in_specs=[pl.BlockSpec((tm, D_MODEL), row), pl.BlockSpec((D_MODEL, LANES), const),
                  pl.BlockSpec((D_MODEL, LANES), const), pl.BlockSpec((1, LANES), const)],
        out_specs=(slots_out, lanes_out, slots_out, pl.BlockSpec((1, LANES), const)),
        scratch_shapes=[pltpu.VMEM((1, LANES), jnp.float32)],
        compiler_params=_params("arbitrary"),
        name="router",
    )(h, w_hi, w_lo, b)
    return e[:TOP_K], g, r[:TOP_K], cnt[0, :N_EXPERTS]


def _route_tables(top_e, rank, cnt):
    ntile = (cnt + MOE_TILE - 1) // MOE_TILE
    tile_end = jnp.cumsum(ntile)
    tile_base = tile_end - ntile
    experts = jnp.arange(N_EXPERTS, dtype=jnp.int32)[:, None, None]
    base = jnp.sum(jnp.where(top_e[None] == experts, tile_base[:, None, None], 0), axis=0)
    pos = (base * MOE_TILE + rank).astype(jnp.int32).reshape(-1)
    n_tiles = tile_end[-1:].astype(jnp.int32)
    w = jnp.minimum(jnp.arange(MOE_TILES, dtype=jnp.int32), n_tiles[0] - 1)
    tile_e = jnp.minimum(jnp.searchsorted(tile_end, w, side="right"), N_EXPERTS - 1).astype(jnp.int32)
    rows = jnp.clip(cnt[tile_e] - (w - tile_base[tile_e]) * MOE_TILE, 0, MOE_TILE)
    nsub = ((rows + MOE_SUB - 1) // MOE_SUB).astype(jnp.int32)
    pad_start = (tile_base * MOE_TILE + cnt).astype(jnp.int32)
    pad_n = ((-cnt) % MOE_SUB).astype(jnp.int32)
    return pos, n_tiles, tile_e, nsub, pad_start, pad_n


def _row_copy(src_vmem, src_row, dst_hbm, dst_row, sem):
    return pltpu.make_async_copy(src_vmem.at[pl.ds(src_row, 1)], dst_hbm.at[pl.ds(dst_row, 1)], sem)


def _pack_bf16_pairs(x):
    half = x.shape[1] // 2
    lo = lax.bitcast_convert_type(x[:, :half].astype(jnp.float32), jnp.uint32)
    hi = lax.bitcast_convert_type(x[:, half:].astype(jnp.float32), jnp.uint32)
    return (lo >> 16) | (hi & jnp.uint32(0xFFFF0000))


def _unpack_bf16_pairs(words):
    lo = lax.bitcast_convert_type(words << 16, jnp.float32)
    hi = lax.bitcast_convert_type(words & jnp.uint32(0xFFFF0000), jnp.float32)
    return lo.astype(jnp.bfloat16), hi.astype(jnp.bfloat16)


def _dispatch_kernel(pos_ref, pad_start_ref, pad_n_ref, hb_ref, xs_hbm, h_ref, sem):
    step = pl.program_id(0)
    base = step * DISPATCH_TB
    h_ref[...] = _pack_bf16_pairs(hb_ref[...])

    for r in range(DISPATCH_TB):
        for k in range(TOP_K):
            _row_copy(h_ref, r, xs_hbm, pos_ref[k * N_TOK + base + r], sem).start(priority=k % 2)

    @pl.when(step == 0)
    def _():
        def per_expert(e, carry):
            start = pad_start_ref[e]
            n = pad_n_ref[e]

            def fill(i, c):
                _row_copy(h_ref, 0, xs_hbm, start + i, sem).start()
                return c

            def drain(_, c):
                _row_copy(h_ref, 0, xs_hbm, 0, sem).wait()
                return c

            lax.fori_loop(0, n, fill, 0)
            lax.fori_loop(0, n, drain, 0)
            return carry

        lax.fori_loop(0, N_EXPERTS, per_expert, 0)

    for _ in range(TOP_K):
        pltpu.make_async_copy(h_ref, xs_hbm.at[pl.ds(0, DISPATCH_TB)], sem).wait()


def _dispatch(hb, pos, pad_start, pad_n):
    return pl.pallas_call(
        _dispatch_kernel,
        out_shape=jax.ShapeDtypeStruct((MOE_ROWS, D_MODEL // 2), jnp.uint32),
        grid_spec=pltpu.PrefetchScalarGridSpec(
            num_scalar_prefetch=3,
            grid=(N_TOK // DISPATCH_TB,),
            in_specs=[pl.BlockSpec((DISPATCH_TB, D_MODEL), lambda i, p, s, n: (i, 0))],
            out_specs=pl.BlockSpec(memory_space=pl.ANY),
            scratch_shapes=[pltpu.VMEM((DISPATCH_TB, D_MODEL // 2), jnp.uint32),
                            pltpu.SemaphoreType.DMA(())],
        ),
        compiler_params=_params("arbitrary"),
        name="moe_dispatch",
    )(pos, pad_start, pad_n, hb)


def _moe_weight_map(layer, first, hidden_axis):
    def index_map(w, j, tile_e, nsub):
        return (layer, tile_e[w], first + j, 0) if hidden_axis == 2 else (layer, tile_e[w], 0, first + j)

    return index_map


def _moe_rows(nsub, accumulate):
    fast = nsub < 0
    for s in MOE_FAST_SUBS:
        fast = jnp.logical_or(fast, nsub == s)
        pl.when(nsub == s)(functools.partial(accumulate, slice(0, s * MOE_SUB)))

    @pl.when(jnp.logical_and(nsub > 0, jnp.logical_not(fast)))
    def _():
        def one(i, carry):
            accumulate(pl.ds(pl.multiple_of(i * MOE_SUB, MOE_SUB), MOE_SUB))
            return carry

        lax.fori_loop(0, nsub, one, 0)


def _moe_expert_kernel(exp_ref, nsub_ref, x_ref, wg_ref, wl_ref, bg_ref, bl_ref, wo_ref, bo_ref, o_ref):
    first = pl.program_id(1) == 0
    nsub = nsub_ref[pl.program_id(0)]
    half = D_MODEL // 2

    @pl.when(first)
    def _():
        o_ref[...] = jnp.broadcast_to(bo_ref[...], (MOE_TILE, D_MODEL))

    def up(lo, hi, w_ref, b_ref):
        return (_dot(lo, w_ref[0:half, :].astype(jnp.bfloat16))
                + _dot(hi, w_ref[half:D_MODEL, :].astype(jnp.bfloat16)) + b_ref[...])

    def accumulate(rows):
        lo, hi = _unpack_bf16_pairs(x_ref[rows, :])
        gate = up(lo, hi, wg_ref, bg_ref)
        lin = up(lo, hi, wl_ref, bl_ref)
        gate = jnp.minimum(gate, SWIGLU_LIMIT)
        lin = jnp.clip(lin, -SWIGLU_LIMIT, SWIGLU_LIMIT)
        act = (gate * (1.0 / (1.0 + jnp.exp(-SWIGLU_ALPHA * gate))) * (lin + 1.0)).astype(jnp.bfloat16)
        for c in range(D_MODEL // MOE_DOWN_TN):
            cols = slice(c * MOE_DOWN_TN, (c + 1) * MOE_DOWN_TN)
            o_ref[rows, cols] = o_ref[rows, cols] + _dot(act, wo_ref[:, cols].astype(jnp.bfloat16))

    _moe_rows(nsub, accumulate)


def _moe_experts(xs, w_in, b_in, w_out, b_out, layer, n_tiles, tile_e, nsub):
    nj = D_EXPERT // MOE_TF
    b_in4 = b_in.reshape(DEPTH, N_EXPERTS, 1, 2 * D_EXPERT)
    b_out4 = b_out.reshape(DEPTH, N_EXPERTS, 1, D_MODEL)
    wblk = (None, None, D_MODEL, MOE_TF)
    bblk = (None, None, 1, MOE_TF)
    tile_map = lambda w, j, e, n: (w, 0)
    return pl.pallas_call(
        _moe_expert_kernel,
        out_shape=jax.ShapeDtypeStruct((MOE_ROWS, D_MODEL), jnp.float32),
        grid_spec=pltpu.PrefetchScalarGridSpec(
            num_scalar_prefetch=2,
            grid=(n_tiles[0], nj),
            in_specs=[pl.BlockSpec((MOE_TILE, D_MODEL // 2), tile_map),
                      pl.BlockSpec(wblk, _moe_weight_map(layer, 0, 3)),
                      pl.BlockSpec(wblk, _moe_weight_map(layer, nj, 3)),
                      pl.BlockSpec(bblk, _moe_weight_map(layer, 0, 3)),
                      pl.BlockSpec(bblk, _moe_weight_map(layer, nj, 3)),
                      pl.BlockSpec((None, None, MOE_TF, D_MODEL), _moe_weight_map(layer, 0, 2)),
                      pl.BlockSpec((None, None, 1, D_MODEL), lambda w, j, e, n: (layer, e[w], 0, 0))],
            out_specs=pl.BlockSpec((MOE_TILE, D_MODEL), tile_map),
        ),
        compiler_params=_params("arbitrary", "arbitrary", vmem_limit=MOE_VMEM_LIMIT),
        name="moe_experts",
    )(tile_e, nsub, xs, w_in, w_in, b_in4, b_in4, w_out, b_out4)


def _combine_ln_kernel(pos_ref, y_hbm, gates_ref, h_ref, g_ref, b_ref, hf_ref, hb_ref, buf0, buf1, sems):
    step = pl.program_id(0)
    last = pl.num_programs(0) - 1

    def row_copy(tile, r, k, buf, sem):
        src = y_hbm.at[pl.ds(pos_ref[k * N_TOK + tile * COMBINE_TB + r], 1)]
        return pltpu.make_async_copy(src, buf.at[k, pl.ds(r, 1)], sem)

    def wait_tile(buf, sem):
        for k in range(TOP_K):
            pltpu.make_async_copy(y_hbm.at[pl.ds(0, COMBINE_TB)], buf.at[k], sem).wait()

    @pl.when(step == 0)
    def _():
        def issue(r, carry):
            for k in range(TOP_K):
                row_copy(0, r, k, buf0, sems.at[0]).start()
            return carry

        lax.fori_loop(0, COMBINE_TB, issue, 0, unroll=8)

    def run(cur, cur_sem, nxt, nxt_sem):
        nxt_tile = jnp.minimum(step + 1, last)
        wait_tile(cur, cur_sem)
        for r in range(COMBINE_TB):
            for k in range(TOP_K):
                row_copy(nxt_tile, r, k, nxt, nxt_sem).start(priority=k % 2)
        gates = gates_ref[...]
        ffn = gates[:, 0:1] * cur[0]
        for k in range(1, TOP_K):
            ffn = ffn + gates[:, k:k + 1] * cur[k]
        y = _layer_norm_rows(DEEPNORM_ALPHA * h_ref[...] + ffn, g_ref[...], b_ref[...])
        hf_ref[...] = y
        hb_ref[...] = y.astype(hb_ref.dtype)

        @pl.when(step == last)
        def _():
            wait_tile(nxt, nxt_sem)

    pl.when(step % 2 == 0)(functools.partial(run, buf0, sems.at[0], buf1, sems.at[1]))
    pl.when(step % 2 == 1)(functools.partial(run, buf1, sems.at[1], buf0, sems.at[0]))


def _combine_ln(y, pos, gates, h, g, b):
    tb = COMBINE_TB
    row = lambda i, p: (i, 0)
    const = lambda i, p: (0, 0)
    return pl.pallas_call(
        _combine_ln_kernel,
        out_shape=(jax.ShapeDtypeStruct((N_TOK, D_MODEL), jnp.float32),
                   jax.ShapeDtypeStruct((N_TOK, D_MODEL), jnp.bfloat16)),
        grid_spec=pltpu.PrefetchScalarGridSpec(
            num_scalar_prefetch=1,
            grid=(N_TOK // tb,),
            in_specs=[pl.BlockSpec(memory_space=pl.ANY),
                      pl.BlockSpec((tb, LANES), row),
                      pl.BlockSpec((tb, D_MODEL), row),
                      pl.BlockSpec((1, D_MODEL), const), pl.BlockSpec((1, D_MODEL), const)],
            out_specs=(pl.BlockSpec((tb, D_MODEL), row), pl.BlockSpec((tb, D_MODEL), row)),
            scratch_shapes=[pltpu.VMEM((TOP_K, tb, D_MODEL), jnp.float32),
                            pltpu.VMEM((TOP_K, tb, D_MODEL), jnp.float32),
                            pltpu.SemaphoreType.DMA((2,))],
        ),
        compiler_params=_params("arbitrary"),
        name="moe_combine_ln",
    )(pos, y, gates, h, g.reshape(1, D_MODEL), b.reshape(1, D_MODEL))


def _moe(h, hb, layer, router_w, router_b, w_in, b_in, w_out, b_out, ln_g, ln_b):
    top_e, gates, rank, cnt = _router(h, router_w[layer], router_b[layer])
    pos, n_tiles, tile_e, nsub, pad_start, pad_n = _route_tables(top_e, rank, cnt)
    xs = _dispatch(hb, pos, pad_start, pad_n)
    y = _moe_experts(xs, w_in, b_in, w_out, b_out, layer, n_tiles, tile_e, nsub)
    return _combine_ln(y, pos, gates, h, ln_g[layer], ln_b[layer])


def kernel(x, mem, w_in_dil, w_in_ret, ret_gn_g, w_mem_kv, w_mix_out, ln_mix_g, ln_mix_b, router_w, router_b, moe_w_in, moe_b_in, moe_w_out, moe_b_out, ln_ffn_g, ln_ffn_b):
    bf16 = jnp.bfloat16
    h = x.reshape(N_TOK, D_MODEL)
    hb = h
    memb = mem.reshape(BATCH * MEM_LEN, D_MODEL)
    for layer in range(DEPTH):
        slot = layer // 2
        if layer % 2 == 0:
            kb = SELF_WIDTH // GROUP_WIDTH
            proj = _matmul(hb, w_in_dil, slot, bf16, col_tiles=(0, kb, kb + 1))
            self_out = _dilated_attention(hb, w_in_dil, slot, proj)
        else:
            proj = _matmul(hb, w_in_ret, slot, bf16)
            self_out = _retention(proj, ret_gn_g[slot])
        memkv = _matmul(memb, w_mem_kv, layer, bf16)
        mem_out = _memory_attention(proj, memkv)
        h, hb = _mix_ln(self_out, mem_out, w_mix_out[layer].astype(bf16), h,
                        ln_mix_g[layer], ln_mix_b[layer])
        h, hb = _moe(h, hb, layer, router_w, router_b, moe_w_in, moe_b_in, moe_w_out, moe_b_out,
                     ln_ffn_g, ln_ffn_b)
    return h.reshape(BATCH, SEQ, D_MODEL)
```

```python
import functools
import math

import jax
import jax.numpy as jnp
from jax import lax
from jax.experimental import pallas as pl
from jax.experimental.pallas import tpu as pltpu

D_MODEL = 2048
BATCH = 2
SEQ = 4096
DEPTH = 2
HEAD_DIM = 128
DILATED_GROUPS = ((128, 1), (512, 4), (2048, 16))
HEADS_PER_GROUP = 4
N_SELF_HEADS = len(DILATED_GROUPS) * HEADS_PER_GROUP
N_RET_HEADS = 12
SELF_WIDTH = N_SELF_HEADS * HEAD_DIM
MEM_HEADS = 4
MEM_LEN = 256
MEM_WIDTH = MEM_HEADS * HEAD_DIM
RET_CHUNK = 128
N_EXPERTS = 32
TOP_K = 4
D_EXPERT = D_MODEL
SWIGLU_ALPHA = 1.702
SWIGLU_LIMIT = 7.0
DEEPNORM_ALPHA = (2 * DEPTH) ** 0.25
LN_EPS = 1e-5
NEG_INF = -1e30

N_TOK = BATCH * SEQ
GROUP_WIDTH = HEADS_PER_GROUP * HEAD_DIM
SELF_QKV = 3 * GROUP_WIDTH
ATTN_STEPS = 128
ATTN_QB = 2

LANES = 128
SUBLANES = 8
VMEM_LIMIT = 56 * 1024 * 1024

MM_TM = 1024
MM_TN = 512
ROW_TILE = 512
COMBINE_ROWS = 512
ROUTER_TM = 256
ROUTER_PARTS = 2

MOE_SUB = 64
MOE_TILE = 1152
MOE_SUBS = MOE_TILE // MOE_SUB
MOE_FAST_SUBS = (15, 16, 17, 18)
MOE_TF = 512
MOE_DOWN_TN = 512
MOE_VMEM_LIMIT = 60 * 1024 * 1024
MOE_TILES = (N_TOK * TOP_K) // MOE_TILE + N_EXPERTS
MOE_ROWS = MOE_TILES * MOE_TILE
DISPATCH_TB = 512
COMBINE_TB = 256


def _alibi_slopes(n):
    def pow2(m):
        start = 2.0 ** (-8.0 / m)
        return [start ** (i + 1) for i in range(m)]

    if math.log2(n).is_integer():
        s = pow2(n)
    else:
        c = 2 ** math.floor(math.log2(n))
        s = pow2(c) + pow2(2 * c)[0::2][: n - c]
    return sorted(s, reverse=True)


def _params(*sem, vmem_limit=VMEM_LIMIT):
    return pltpu.CompilerParams(dimension_semantics=sem, vmem_limit_bytes=vmem_limit)


def _layer_norm_rows(z, g, b):
    mu = jnp.mean(z, axis=-1, keepdims=True)
    zc = z - mu
    var = jnp.mean(zc * zc, axis=-1, keepdims=True)
    return zc * lax.rsqrt(var + LN_EPS) * g + b


def _dot_nt(a, b):
    return lax.dot_general(a, b, (((1,), (1,)), ((), ())), preferred_element_type=jnp.float32)


def _dot(a, b):
    return jnp.dot(a, b, preferred_element_type=jnp.float32)


def _mm_kernel(x_ref, w_ref, o_ref):
    o_ref[...] = _dot(x_ref[...].astype(jnp.bfloat16), w_ref[...].astype(jnp.bfloat16)).astype(o_ref.dtype)


def _matmul(x, w, layer, out_dtype, col_tiles=None):
    m, k = x.shape
    first, stride, count = col_tiles if col_tiles else (0, 1, w.shape[2] // MM_TN)
    tm = min(MM_TM * (4 // x.dtype.itemsize), m)
    return pl.pallas_call(
        _mm_kernel,
        out_shape=jax.ShapeDtypeStruct((m, count * MM_TN), out_dtype),
        grid=(m // tm, count),
        in_specs=[pl.BlockSpec((tm, k), lambda i, j: (i, 0)),
                  pl.BlockSpec((None, k, MM_TN), lambda i, j: (layer, 0, first + stride * j))],
        out_specs=pl.BlockSpec((tm, MM_TN), lambda i, j: (i, j)),
        compiler_params=_params("parallel", "parallel"),
        name="dense_matmul",
    )(x, w)


def _proj_residue_kernel(x_ref, wq_ref, wk_ref, wv_ref, o_ref, scr_ref, *, dilation):
    x = x_ref[...].astype(jnp.bfloat16)
    rows = MM_TM // dilation
    for section, w_ref in enumerate((wq_ref, wk_ref, wv_ref)):
        res = _dot(x, w_ref[...].astype(jnp.bfloat16))
        for c in range(GROUP_WIDTH // LANES):
            scr_ref[c] = res[:, c * LANES:(c + 1) * LANES]
        for r in range(dilation):
            for c in range(GROUP_WIDTH // LANES):
                col = r * SELF_QKV + section * GROUP_WIDTH + c * LANES
                o_ref[:, col:col + LANES] = scr_ref[c, pl.ds(r, rows, stride=dilation), :].astype(o_ref.dtype)


def _proj_residue_major(x, w, layer, group):
    _, dilation = DILATED_GROUPS[group]
    k = x.shape[1]
    kb = SELF_WIDTH // GROUP_WIDTH

    def wspec(section):
        return pl.BlockSpec((None, k, GROUP_WIDTH), lambda i: (layer, 0, section * kb + group))

    return pl.pallas_call(
        functools.partial(_proj_residue_kernel, dilation=dilation),
        out_shape=jax.ShapeDtypeStruct((N_TOK // dilation, dilation * SELF_QKV), jnp.bfloat16),
        grid=(N_TOK // MM_TM,),
        in_specs=[pl.BlockSpec((MM_TM, k), lambda i: (i, 0)), wspec(0), wspec(1), wspec(2)],
        out_specs=pl.BlockSpec((MM_TM // dilation, dilation * SELF_QKV), lambda i: (i, 0)),
        scratch_shapes=[pltpu.VMEM((GROUP_WIDTH // LANES, MM_TM, LANES), jnp.float32)],
        compiler_params=_params("parallel"),
        name=f"proj_residue_g{group}",
    )(x, w, w, w)


def _dil_attn_kernel(q_ref, kp_ref, kc_ref, vp_ref, vc_ref, o_ref, lse_ref, *, slopes, dilation):
    n = pl.program_id(2)
    heads = HEADS_PER_GROUP
    rows = ATTN_QB * heads * ATTN_STEPS
    head_slices = [slice(h * HEAD_DIM, (h + 1) * HEAD_DIM) for h in range(heads)]
    blocks = [slice(s * ATTN_STEPS, (s + 1) * ATTN_STEPS) for s in range(ATTN_QB)]
    row = lax.broadcasted_iota(jnp.int32, (rows, ATTN_STEPS), 0)
    kj = lax.broadcasted_iota(jnp.int32, (rows, ATTN_STEPS), 1)
    diff = (row % ATTN_STEPS) - kj
    valid_c = diff >= 0
    has_prev = jnp.logical_or(row >= heads * ATTN_STEPS, n > 0)
    valid_p = jnp.logical_and(diff <= 0, has_prev)
    slope = jnp.full((rows, ATTN_STEPS), slopes[0], jnp.float32)
    head_of_row = (row // ATTN_STEPS) % heads
    for h in range(1, heads):
        slope = jnp.where(head_of_row == h, slopes[h], slope)
    bias_c = slope * (diff * dilation).astype(jnp.float32)
    bias_p = slope * ((diff + ATTN_STEPS) * dilation).astype(jnp.float32)
    scale = HEAD_DIM ** -0.5

    def prev_cur(cur_ref, prev_ref, s, sl):
        prev = prev_ref[:, sl] if s == 0 else cur_ref[blocks[s - 1], sl]
        return prev, cur_ref[blocks[s], sl]

    pairs = [(s, sl) for s in range(ATTN_QB) for sl in head_slices]
    s_c = jnp.concatenate([_dot_nt(q_ref[blocks[s], sl], prev_cur(kc_ref, kp_ref, s, sl)[1])
                           for s, sl in pairs], axis=0)
    s_p = jnp.concatenate([_dot_nt(q_ref[blocks[s], sl], prev_cur(kc_ref, kp_ref, s, sl)[0])
                           for s, sl in pairs], axis=0)
    s_c = jnp.where(valid_c, s_c * scale - bias_c, NEG_INF)
    s_p = jnp.where(valid_p, s_p * scale - bias_p, NEG_INF)
    m = jnp.max(jnp.maximum(s_c, s_p), axis=-1, keepdims=True)
    e_c = jnp.exp(s_c - m)
    e_p = jnp.exp(s_p - m)
    l = jnp.sum(e_c + e_p, axis=-1, keepdims=True)
    inv_l = 1.0 / l
    p_c = (e_c * inv_l).astype(jnp.bfloat16)
    p_p = (e_p * inv_l).astype(jnp.bfloat16)
    lse = m + jnp.log(l)
    for i, (s, sl) in enumerate(pairs):
        chain = slice(i * ATTN_STEPS, (i + 1) * ATTN_STEPS)
        v_prev, v_cur = prev_cur(vc_ref, vp_ref, s, sl)
        o_ref[blocks[s], sl] = _dot(p_c[chain], v_cur) + _dot(p_p[chain], v_prev)
        lse_ref[blocks[s], sl] = jnp.broadcast_to(lse[chain], (ATTN_STEPS, HEAD_DIM))


def _dilated_group(src, group):
    _, dilation = DILATED_GROUPS[group]
    length = SEQ // dilation
    span = ATTN_QB * ATTN_STEPS
    cb = src.shape[1] // dilation // GROUP_WIDTH
    view = src.reshape(BATCH, length, src.shape[1])
    blk = (None, span, GROUP_WIDTH)
    prev_blk = (None, ATTN_STEPS, GROUP_WIDTH)

    def col(section):
        return lambda b, r, n: (b, n, r * cb + section)

    def col_prev(section):
        return lambda b, r, n: (b, jnp.maximum(n * ATTN_QB - 1, 0), r * cb + section)

    slopes = tuple(_alibi_slopes(N_SELF_HEADS)[group * HEADS_PER_GROUP:(group + 1) * HEADS_PER_GROUP])
    out_shape = jax.ShapeDtypeStruct((BATCH, length, dilation * GROUP_WIDTH), jnp.float32)
    out_spec = pl.BlockSpec(blk, lambda b, r, n: (b, n, r))
    o, lse = pl.pallas_call(
        functools.partial(_dil_attn_kernel, slopes=slopes, dilation=dilation),
        out_shape=(out_shape, out_shape),
        grid=(BATCH, dilation, length // span),
        in_specs=[pl.BlockSpec(blk, col(0)),
                  pl.BlockSpec(prev_blk, col_prev(1)), pl.BlockSpec(blk, col(1)),
                  pl.BlockSpec(prev_blk, col_prev(2)), pl.BlockSpec(blk, col(2))],
        out_specs=(out_spec, out_spec),
        compiler_params=_params("parallel", "parallel", "arbitrary"),
        name=f"dilated_attn_g{group}",
    )(view, view, view, view, view)
    return o, lse


def _dil_combine_kernel(*refs):
    ngroups = len(DILATED_GROUPS)
    o_refs, l_refs, out_ref = refs[:ngroups], refs[ngroups:2 * ngroups], refs[2 * ngroups]
    scratch = list(refs[2 * ngroups + 1:])
    os, ls = [], []
    for g, (_, dilation) in enumerate(DILATED_GROUPS):
        if dilation == 1:
            os.append(o_refs[g][...])
            ls.append(l_refs[g][...])
            continue
        rows = COMBINE_ROWS // dilation
        planes = GROUP_WIDTH // LANES
        o_scr, l_scr = scratch.pop(0), scratch.pop(0)
        for r in range(dilation):
            for c in range(planes):
                cols = slice(r * GROUP_WIDTH + c * LANES, r * GROUP_WIDTH + (c + 1) * LANES)
                o_scr[c, pl.ds(r, rows, stride=dilation), :] = o_refs[g][:, cols]
                l_scr[c, pl.ds(r, rows, stride=dilation), :] = l_refs[g][:, cols]
        os.append(jnp.concatenate([o_scr[c] for c in range(planes)], axis=1))
        ls.append(jnp.concatenate([l_scr[c] for c in range(planes)], axis=1))
    m = functools.reduce(jnp.maximum, ls)
    es = [jnp.exp(l - m) for l in ls]
    inv = 1.0 / functools.reduce(lambda a, b: a + b, es)
    for g in range(ngroups):
        out_ref[:, g * GROUP_WIDTH:(g + 1) * GROUP_WIDTH] = (os[g] * (es[g] * inv)).astype(out_ref.dtype)


def _dilated_attention(x, w_in, layer, proj):
    sources = [proj] + [_proj_residue_major(x, w_in, layer, g) for g in range(1, len(DILATED_GROUPS))]
    outs, lses = zip(*[_dilated_group(src, g) for g, src in enumerate(sources)])

    def spec(dilation):
        return pl.BlockSpec((COMBINE_ROWS // dilation, dilation * GROUP_WIDTH), lambda i: (i, 0))

    def flat(a):
        return a.reshape(a.shape[0] * a.shape[1], a.shape[2])

    specs = [spec(d) for _, d in DILATED_GROUPS]
    scratch = [pltpu.VMEM((GROUP_WIDTH // LANES, COMBINE_ROWS, LANES), jnp.float32)
               for _, d in DILATED_GROUPS if d > 1 for _ in range(2)]
    return pl.pallas_call(
        _dil_combine_kernel,
        out_shape=jax.ShapeDtypeStruct((N_TOK, SELF_WIDTH), jnp.bfloat16),
        grid=(N_TOK // COMBINE_ROWS,),
        in_specs=specs + specs,
        out_specs=pl.BlockSpec((COMBINE_ROWS, SELF_WIDTH), lambda i: (i, 0)),
        scratch_shapes=scratch,
        compiler_params=_params("parallel"),
        name="dilated_combine",
    )(*[flat(o) for o in outs], *[flat(l) for l in lses])


def _retention_kernel(q_ref, k_ref, v_ref, g_ref, dmat_ref, kdec_ref, qdec_ref, cdec_ref, gn_ref,
                      o_ref, state_ref):
    @pl.when(pl.program_id(1) == 0)
    def _():
        state_ref[...] = jnp.zeros_like(state_ref)

    heads = range(N_RET_HEADS)
    cols = [slice(h * HEAD_DIM, (h + 1) * HEAD_DIM) for h in heads]
    bf16 = jnp.bfloat16
    scores = [_dot_nt(q_ref[:, c], k_ref[:, c]) * dmat_ref[h] for h, c in zip(heads, cols)]
    states = [state_ref[h] for h in heads]
    cross = [_dot(q_ref[:, c], states[h].astype(bf16)) * qdec_ref[:, c] for h, c in zip(heads, cols)]
    intra = [_dot(scores[h].astype(bf16), v_ref[:, c]) for h, c in zip(heads, cols)]
    kw = [(k_ref[:, c].astype(jnp.float32) * kdec_ref[:, c]).astype(bf16) for c in cols]
    for h, c in zip(heads, cols):
        kv = lax.dot_general(kw[h], v_ref[:, c], (((0,), (0,)), ((), ())), preferred_element_type=jnp.float32)
        state_ref[h] = states[h] * cdec_ref[h] + kv
    r = jnp.concatenate([intra[h] + cross[h] for h in heads], axis=0)
    mu = jnp.mean(r, axis=-1, keepdims=True)
    rc = r - mu
    var = jnp.mean(rc * rc, axis=-1, keepdims=True)
    rn = rc * lax.rsqrt(var + LN_EPS)
    for h, c in zip(heads, cols):
        gate = g_ref[:, c].astype(jnp.float32)
        normed = rn[h * RET_CHUNK:(h + 1) * RET_CHUNK] * gn_ref[:, c]
        o_ref[:, c] = (gate * (1.0 / (1.0 + jnp.exp(-gate))) * normed).astype(o_ref.dtype)


def _retention(proj, gn_gain):
    c = RET_CHUNK
    log_gamma = jnp.log1p(-jnp.exp2(-(5.0 + jnp.arange(N_RET_HEADS, dtype=jnp.float32))))
    idx = jnp.arange(c, dtype=jnp.float32)
    diff = idx[:, None] - idx[None, :]
    decay = jnp.where(diff >= 0, jnp.exp(jnp.maximum(diff, 0.0)[None] * log_gamma[:, None, None]), 0.0)
    scale = HEAD_DIM ** -0.5
    dmat = decay * scale
    k_decay = jnp.exp((c - 1 - idx)[:, None] * log_gamma[None, :]) * scale
    q_decay = jnp.exp((idx + 1.0)[:, None] * log_gamma[None, :])
    kdec = jnp.repeat(k_decay, HEAD_DIM, axis=1)
    qdec = jnp.repeat(q_decay, HEAD_DIM, axis=1)
    cdec = jnp.broadcast_to(jnp.exp(c * log_gamma)[:, None, None], (N_RET_HEADS, 1, HEAD_DIM))
    gn = gn_gain.reshape(1, SELF_WIDTH).astype(jnp.float32)
    nchunk = SEQ // c
    blk = (c, SELF_WIDTH)

    def section(s):
        return pl.BlockSpec(blk, lambda b, n: (b * nchunk + n, s))

    const2 = lambda b, n: (0, 0)
    const3 = lambda b, n: (0, 0, 0)
    return pl.pallas_call(
        _retention_kernel,
        out_shape=jax.ShapeDtypeStruct((N_TOK, SELF_WIDTH), jnp.bfloat16),
        grid=(BATCH, nchunk),
        in_specs=[section(0), section(1), section(2), section(3),
                  pl.BlockSpec((N_RET_HEADS, c, c), const3),
                  pl.BlockSpec(blk, const2), pl.BlockSpec(blk, const2),
                  pl.BlockSpec((N_RET_HEADS, 1, HEAD_DIM), const3),
                  pl.BlockSpec((1, SELF_WIDTH), const2)],
        out_specs=pl.BlockSpec(blk, lambda b, n: (b * nchunk + n, 0)),
        scratch_shapes=[pltpu.VMEM((N_RET_HEADS, HEAD_DIM, HEAD_DIM), jnp.float32)],
        compiler_params=_params("parallel", "arbitrary"),
        name="retention",
    )(proj, proj, proj, proj, dmat, kdec, qdec, cdec, gn)


def _mem_attn_kernel(q_ref, k_ref, v_ref, o_ref):
    scale = HEAD_DIM ** -0.5
    for h in range(MEM_HEADS):
        sl = slice(h * HEAD_DIM, (h + 1) * HEAD_DIM)
        s = _dot_nt(q_ref[:, sl], k_ref[:, sl]) * scale
        e = jnp.exp(s - jnp.max(s, axis=-1, keepdims=True))
        p = e * (1.0 / jnp.sum(e, axis=-1, keepdims=True))
        o_ref[:, sl] = _dot(p.astype(jnp.bfloat16), v_ref[:, sl]).astype(o_ref.dtype)


def _memory_attention(proj, memkv):
    qcol = proj.shape[1] // MEM_WIDTH - 1
    per_b = SEQ // ROW_TILE
    return pl.pallas_call(
        _mem_attn_kernel,
        out_shape=jax.ShapeDtypeStruct((N_TOK, MEM_WIDTH), jnp.bfloat16),
        grid=(BATCH, per_b),
        in_specs=[pl.BlockSpec((ROW_TILE, MEM_WIDTH), lambda b, i: (b * per_b + i, qcol)),
                  pl.BlockSpec((MEM_LEN, MEM_WIDTH), lambda b, i: (b, 0)),
                  pl.BlockSpec((MEM_LEN, MEM_WIDTH), lambda b, i: (b, 1))],
        out_specs=pl.BlockSpec((ROW_TILE, MEM_WIDTH), lambda b, i: (b * per_b + i, 0)),
        compiler_params=_params("parallel", "parallel"),
        name="memory_attn",
    )(proj, memkv, memkv)


def _mix_ln_kernel(so_ref, mo_ref, wt_ref, wb_ref, h_ref, g_ref, b_ref, hf_ref, hb_ref):
    half = ROW_TILE // 2
    for rows in (slice(0, half), slice(half, ROW_TILE)):
        mix = _dot(so_ref[rows, :], wt_ref[...]) + _dot(mo_ref[rows, :], wb_ref[...])
        y = _layer_norm_rows(DEEPNORM_ALPHA * h_ref[rows, :] + mix, g_ref[...], b_ref[...])
        hf_ref[rows, :] = y
        hb_ref[rows, :] = y.astype(hb_ref.dtype)


def _mix_ln(self_out, mem_out, w_mix, h, g, b):
    row = lambda i: (i, 0)
    const = lambda i: (0, 0)
    return pl.pallas_call(
        _mix_ln_kernel,
        out_shape=(jax.ShapeDtypeStruct((N_TOK, D_MODEL), jnp.float32),
                   jax.ShapeDtypeStruct((N_TOK, D_MODEL), jnp.bfloat16)),
        grid=(N_TOK // ROW_TILE,),
        in_specs=[pl.BlockSpec((ROW_TILE, SELF_WIDTH), row),
                  pl.BlockSpec((ROW_TILE, MEM_WIDTH), row),
                  pl.BlockSpec((SELF_WIDTH, D_MODEL), const),
                  pl.BlockSpec((MEM_WIDTH, D_MODEL), lambda i: (SELF_WIDTH // MEM_WIDTH, 0)),
                  pl.BlockSpec((ROW_TILE, D_MODEL), row),
                  pl.BlockSpec((1, D_MODEL), const), pl.BlockSpec((1, D_MODEL), const)],
        out_specs=(pl.BlockSpec((ROW_TILE, D_MODEL), row), pl.BlockSpec((ROW_TILE, D_MODEL), row)),
        compiler_params=_params("parallel"),
        name="mix_ln",
    )(self_out, mem_out, w_mix, w_mix, h, g.reshape(1, D_MODEL), b.reshape(1, D_MODEL))


def _router_kernel(h_ref, whi_ref, wlo_ref, b_ref, e_ref, g_ref, r_ref, cnt_ref, run_ref):
    tm = ROUTER_TM

    @pl.when(pl.program_id(0) == 0)
    def _():
        run_ref[...] = jnp.zeros_like(run_ref)

    parts = [slice(p * tm, (p + 1) * tm) for p in range(ROUTER_PARTS)]
    both = range(ROUTER_PARTS)
    hs = [h_ref[rows, :] for rows in parts]
    h_his = [h.astype(jnp.bfloat16) for h in hs]
    h_los = [(hs[p] - h_his[p].astype(jnp.float32)).astype(jnp.bfloat16) for p in both]
    works = [(_dot(h_his[p], whi_ref[...]) + _dot(h_los[p], whi_ref[...]) + _dot(h_his[p], wlo_ref[...]))
             + b_ref[...] for p in both]
    lane = lax.broadcasted_iota(jnp.int32, (tm, LANES), 1).astype(jnp.float32)
    vals, idxs, hots = [[] for _ in both], [[] for _ in both], [[] for _ in both]
    for _ in range(TOP_K):
        for p in both:
            m = jnp.max(works[p], axis=-1, keepdims=True)
            idx = jnp.min(jnp.where(works[p] == m, lane, float(LANES)), axis=-1, keepdims=True)
            hot = lane == idx
            vals[p].append(m)
            idxs[p].append(idx)
            hots[p].append(hot)
            works[p] = jnp.where(hot, -jnp.inf, works[p])
    row = lax.broadcasted_iota(jnp.int32, (tm, tm), 0)
    colm = lax.broadcasted_iota(jnp.int32, (tm, tm), 1)
    tri = (row > colm).astype(jnp.bfloat16)
    lane_i = lax.broadcasted_iota(jnp.int32, (tm, LANES), 1)
    run = run_ref[...]
    for p in both:
        exps = [jnp.exp(v - vals[p][0]) for v in vals[p]]
        inv = 1.0 / (exps[0] + exps[1] + exps[2] + exps[3])
        cnt = jnp.zeros((tm, LANES), jnp.float32)
        for hot in hots[p]:
            cnt = cnt + hot.astype(jnp.float32)
        before = run + _dot(tri, cnt.astype(jnp.bfloat16))
        e_out = jnp.zeros((tm, LANES), jnp.float32)
        g_out = jnp.zeros((tm, LANES), jnp.float32)
        r_out = jnp.zeros((tm, LANES), jnp.float32)
        for k in range(TOP_K):
            rank = jnp.sum(jnp.where(hots[p][k], before, 0.0), axis=-1, keepdims=True)
            e_out = jnp.where(lane_i == k, idxs[p][k], e_out)
            g_out = jnp.where(lane_i == k, exps[k] * inv, g_out)
            r_out = jnp.where(lane_i == k, rank, r_out)
        e_ref[:, parts[p]] = e_out.T[0:SUBLANES, :].astype(jnp.int32)
        g_ref[parts[p], :] = g_out
        r_ref[:, parts[p]] = r_out.T[0:SUBLANES, :].astype(jnp.int32)
        run = run + jnp.sum(cnt, axis=0, keepdims=True)
    run_ref[...] = run
    cnt_ref[...] = run.astype(jnp.int32)


def _router(h, router_w, router_b):
    w = jnp.zeros((D_MODEL, LANES), jnp.float32).at[:, :N_EXPERTS].set(router_w)
    w_hi = w.astype(jnp.bfloat16)
    w_lo = (w - w_hi.astype(jnp.float32)).astype(jnp.bfloat16)
    b = jnp.full((1, LANES), NEG_INF, jnp.float32).at[0, :N_EXPERTS].set(router_b)
    tm = ROUTER_TM * ROUTER_PARTS
    row = lambda i: (i, 0)
    const = lambda i: (0, 0)
    lanes_out = pl.BlockSpec((tm, LANES), row)
    slots_out = pl.BlockSpec((SUBLANES, tm), lambda i: (0, i))
    e, g, r, cnt = pl.pallas_call(
        _router_kernel,
        out_shape=(jax.ShapeDtypeStruct((SUBLANES, N_TOK), jnp.int32),
                   jax.ShapeDtypeStruct((N_TOK, LANES), jnp.float32),
                   jax.ShapeDtypeStruct((SUBLANES, N_TOK), jnp.int32),
                   jax.ShapeDtypeStruct((1, LANES), jnp.int32)),
        grid=(N_TOK // tm,),
        in_specs=[pl.BlockSpec((tm, D_MODEL), row), pl.BlockSpec((D_MODEL, LANES), const),
                  pl.BlockSpec((D_MODEL, LANES), const), pl.BlockSpec((1, LANES), const)],
        out_specs=(slots_out, lanes_out, slots_out, pl.BlockSpec((1, LANES), const)),
        scratch_shapes=[pltpu.VMEM((1, LANES), jnp.float32)],
        compiler_params=_params("arbitrary"),
        name="router",
    )(h, w_hi, w_lo, b)
    return e[:TOP_K], g, r[:TOP_K], cnt[0, :N_EXPERTS]


def _route_tables(top_e, rank, cnt):
    ntile = (cnt + MOE_TILE - 1) // MOE_TILE
    tile_end = jnp.cumsum(ntile)
    tile_base = tile_end - ntile
    experts = jnp.arange(N_EXPERTS, dtype=jnp.int32)[:, None, None]
    base = jnp.sum(jnp.where(top_e[None] == experts, tile_base[:, None, None], 0), axis=0)
    pos = (base * MOE_TILE + rank).astype(jnp.int32).reshape(-1)
    n_tiles = tile_end[-1:].astype(jnp.int32)
    w = jnp.minimum(jnp.arange(MOE_TILES, dtype=jnp.int32), n_tiles[0] - 1)
    tile_e = jnp.minimum(jnp.searchsorted(tile_end, w, side="right"), N_EXPERTS - 1).astype(jnp.int32)
    rows = jnp.clip(cnt[tile_e] - (w - tile_base[tile_e]) * MOE_TILE, 0, MOE_TILE)
    nsub = ((rows + MOE_SUB - 1) // MOE_SUB).astype(jnp.int32)
    pad_start = (tile_base * MOE_TILE + cnt).astype(jnp.int32)
    pad_n = ((-cnt) % MOE_SUB).astype(jnp.int32)
    return pos, n_tiles, tile_e, nsub, pad_start, pad_n


def _row_copy(src_vmem, src_row, dst_hbm, dst_row, sem):
    return pltpu.make_async_copy(src_vmem.at[pl.ds(src_row, 1)], dst_hbm.at[pl.ds(dst_row, 1)], sem)


def _pack_bf16_pairs(x):
    half = x.shape[1] // 2
    lo = lax.bitcast_convert_type(x[:, :half].astype(jnp.float32), jnp.uint32)
    hi = lax.bitcast_convert_type(x[:, half:].astype(jnp.float32), jnp.uint32)
    return (lo >> 16) | (hi & jnp.uint32(0xFFFF0000))


def _unpack_bf16_pairs(words):
    lo = lax.bitcast_convert_type(words << 16, jnp.float32)
    hi = lax.bitcast_convert_type(words & jnp.uint32(0xFFFF0000), jnp.float32)
    return lo.astype(jnp.bfloat16), hi.astype(jnp.bfloat16)


def _dispatch_kernel(pos_ref, pad_start_ref, pad_n_ref, hb_ref, xs_hbm, h_ref, sem):
    step = pl.program_id(0)
    base = step * DISPATCH_TB
    h_ref[...] = _pack_bf16_pairs(hb_ref[...])

    for r in range(DISPATCH_TB):
        for k in range(TOP_K):
            _row_copy(h_ref, r, xs_hbm, pos_ref[k * N_TOK + base + r], sem).start(priority=k % 2)

    @pl.when(step == 0)
    def _():
        def per_expert(e, carry):
            start = pad_start_ref[e]
            n = pad_n_ref[e]

            def fill(i, c):
                _row_copy(h_ref, 0, xs_hbm, start + i, sem).start()
                return c

            def drain(_, c):
                _row_copy(h_ref, 0, xs_hbm, 0, sem).wait()
                return c

            lax.fori_loop(0, n, fill, 0)
            lax.fori_loop(0, n, drain, 0)
            return carry

        lax.fori_loop(0, N_EXPERTS, per_expert, 0)

    for _ in range(TOP_K):
        pltpu.make_async_copy(h_ref, xs_hbm.at[pl.ds(0, DISPATCH_TB)], sem).wait()


def _dispatch(hb, pos, pad_start, pad_n):
    return pl.pallas_call(
        _dispatch_kernel,
        out_shape=jax.ShapeDtypeStruct((MOE_ROWS, D_MODEL // 2), jnp.uint32),
        grid_spec=pltpu.PrefetchScalarGridSpec(
            num_scalar_prefetch=3,
            grid=(N_TOK // DISPATCH_TB,),
            in_specs=[pl.BlockSpec((DISPATCH_TB, D_MODEL), lambda i, p, s, n: (i, 0))],
            out_specs=pl.BlockSpec(memory_space=pl.ANY),
            scratch_shapes=[pltpu.VMEM((DISPATCH_TB, D_MODEL // 2), jnp.uint32),
                            pltpu.SemaphoreType.DMA(())],
        ),
        compiler_params=_params("arbitrary"),
        name="moe_dispatch",
    )(pos, pad_start, pad_n, hb)


def _moe_weight_map(layer, first, hidden_axis):
    def index_map(w, j, tile_e, nsub):
        return (layer, tile_e[w], first + j, 0) if hidden_axis == 2 else (layer, tile_e[w], 0, first + j)

    return index_map


def _moe_rows(nsub, accumulate):
    fast = nsub < 0
    for s in MOE_FAST_SUBS:
        fast = jnp.logical_or(fast, nsub == s)
        pl.when(nsub == s)(functools.partial(accumulate, slice(0, s * MOE_SUB)))

    @pl.when(jnp.logical_and(nsub > 0, jnp.logical_not(fast)))
    def _():
        def one(i, carry):
            accumulate(pl.ds(pl.multiple_of(i * MOE_SUB, MOE_SUB), MOE_SUB))
            return carry

        lax.fori_loop(0, nsub, one, 0)


def _moe_expert_kernel(exp_ref, nsub_ref, x_ref, wg_ref, wl_ref, bg_ref, bl_ref, wo_ref, bo_ref, o_ref):
    first = pl.program_id(1) == 0
    nsub = nsub_ref[pl.program_id(0)]
    half = D_MODEL // 2

    @pl.when(first)
    def _():
        o_ref[...] = jnp.broadcast_to(bo_ref[...], (MOE_TILE, D_MODEL))

    def up(lo, hi, w_ref, b_ref):
        return (_dot(lo, w_ref[0:half, :].astype(jnp.bfloat16))
                + _dot(hi, w_ref[half:D_MODEL, :].astype(jnp.bfloat16)) + b_ref[...])

    def accumulate(rows):
        lo, hi = _unpack_bf16_pairs(x_ref[rows, :])
        gate = up(lo, hi, wg_ref, bg_ref)
        lin = up(lo, hi, wl_ref, bl_ref)
        gate = jnp.minimum(gate, SWIGLU_LIMIT)
        lin = jnp.clip(lin, -SWIGLU_LIMIT, SWIGLU_LIMIT)
        act = (gate * (1.0 / (1.0 + jnp.exp(-SWIGLU_ALPHA * gate))) * (lin + 1.0)).astype(jnp.bfloat16)
        for c in range(D_MODEL // MOE_DOWN_TN):
            cols = slice(c * MOE_DOWN_TN, (c + 1) * MOE_DOWN_TN)
            o_ref[rows, cols] = o_ref[rows, cols] + _dot(act, wo_ref[:, cols].astype(jnp.bfloat16))

    _moe_rows(nsub, accumulate)


def _moe_experts(xs, w_in, b_in, w_out, b_out, layer, n_tiles, tile_e, nsub):
    nj = D_EXPERT // MOE_TF
    b_in4 = b_in.reshape(DEPTH, N_EXPERTS, 1, 2 * D_EXPERT)
    b_out4 = b_out.reshape(DEPTH, N_EXPERTS, 1, D_MODEL)
    wblk = (None, None, D_MODEL, MOE_TF)
    bblk = (None, None, 1, MOE_TF)
    tile_map = lambda w, j, e, n: (w, 0)
    return pl.pallas_call(
        _moe_expert_kernel,
        out_shape=jax.ShapeDtypeStruct((MOE_ROWS, D_MODEL), jnp.float32),
        grid_spec=pltpu.PrefetchScalarGridSpec(
            num_scalar_prefetch=2,
            grid=(n_tiles[0], nj),
            in_specs=[pl.BlockSpec((MOE_TILE, D_MODEL // 2), tile_map),
                      pl.BlockSpec(wblk, _moe_weight_map(layer, 0, 3)),
                      pl.BlockSpec(wblk, _moe_weight_map(layer, nj, 3)),
                      pl.BlockSpec(bblk, _moe_weight_map(layer, 0, 3)),
                      pl.BlockSpec(bblk, _moe_weight_map(layer, nj, 3)),
                      pl.BlockSpec((None, None, MOE_TF, D_MODEL), _moe_weight_map(layer, 0, 2)),
                      pl.BlockSpec((None, None, 1, D_MODEL), lambda w, j, e, n: (layer, e[w], 0, 0))],
            out_specs=pl.BlockSpec((MOE_TILE, D_MODEL), tile_map),
        ),
        compiler_params=_params("arbitrary", "arbitrary", vmem_limit=MOE_VMEM_LIMIT),
        name="moe_experts",
    )(tile_e, nsub, xs, w_in, w_in, b_in4, b_in4, w_out, b_out4)


def _combine_ln_kernel(pos_ref, y_hbm, gates_ref, h_ref, g_ref, b_ref, hf_ref, hb_ref, buf0, buf1, sems):
    step = pl.program_id(0)
    last = pl.num_programs(0) - 1

    def row_copy(tile, r, k, buf, sem):
        src = y_hbm.at[pl.ds(pos_ref[k * N_TOK + tile * COMBINE_TB + r], 1)]
        return pltpu.make_async_copy(src, buf.at[k, pl.ds(r, 1)], sem)

    def wait_tile(buf, sem):
        for k in range(TOP_K):
            pltpu.make_async_copy(y_hbm.at[pl.ds(0, COMBINE_TB)], buf.at[k], sem).wait()

    @pl.when(step == 0)
    def _():
        def issue(r, carry):
            for k in range(TOP_K):
                row_copy(0, r, k, buf0, sems.at[0]).start()
            return carry

        lax.fori_loop(0, COMBINE_TB, issue, 0, unroll=8)

    def run(cur, cur_sem, nxt, nxt_sem):
        nxt_tile = jnp.minimum(step + 1, last)
        wait_tile(cur, cur_sem)
        for r in range(COMBINE_TB):
            for k in range(TOP_K):
                row_copy(nxt_tile, r, k, nxt, nxt_sem).start(priority=k % 2)
        gates = gates_ref[...]
        ffn = gates[:, 0:1] * cur[0]
        for k in range(1, TOP_K):
            ffn = ffn + gates[:, k:k + 1] * cur[k]
        y = _layer_norm_rows(DEEPNORM_ALPHA * h_ref[...] + ffn, g_ref[...], b_ref[...])
        hf_ref[...] = y
        hb_ref[...] = y.astype(hb_ref.dtype)

        @pl.when(step == last)
        def _():
            wait_tile(nxt, nxt_sem)

    pl.when(step % 2 == 0)(functools.partial(run, buf0, sems.at[0], buf1, sems.at[1]))
    pl.when(step % 2 == 1)(functools.partial(run, buf1, sems.at[1], buf0, sems.at[0]))


def _combine_ln(y, pos, gates, h, g, b):
    tb = COMBINE_TB
    row = lambda i, p: (i, 0)
    const = lambda i, p: (0, 0)
    return pl.pallas_call(
        _combine_ln_kernel,
        out_shape=(jax.ShapeDtypeStruct((N_TOK, D_MODEL), jnp.float32),
                   jax.ShapeDtypeStruct((N_TOK, D_MODEL), jnp.bfloat16)),
        grid_spec=pltpu.PrefetchScalarGridSpec(
            num_scalar_prefetch=1,
            grid=(N_TOK // tb,),
            in_specs=[pl.BlockSpec(memory_space=pl.ANY),
                      pl.BlockSpec((tb, LANES), row),
                      pl.BlockSpec((tb, D_MODEL), row),
                      pl.BlockSpec((1, D_MODEL), const), pl.BlockSpec((1, D_MODEL), const)],
            out_specs=(pl.BlockSpec((tb, D_MODEL), row), pl.BlockSpec((tb, D_MODEL), row)),
            scratch_shapes=[pltpu.VMEM((TOP_K, tb, D_MODEL), jnp.float32),
                            pltpu.VMEM((TOP_K, tb, D_MODEL), jnp.float32),
                            pltpu.SemaphoreType.DMA((2,))],
        ),
        compiler_params=_params("arbitrary"),
        name="moe_combine_ln",
    )(pos, y, gates, h, g.reshape(1, D_MODEL), b.reshape(1, D_MODEL))


def _moe(h, hb, layer, router_w, router_b, w_in, b_in, w_out, b_out, ln_g, ln_b):
    top_e, gates, rank, cnt = _router(h, router_w[layer], router_b[layer])
    pos, n_tiles, tile_e, nsub, pad_start, pad_n = _route_tables(top_e, rank, cnt)
    xs = _dispatch(hb, pos, pad_start, pad_n)
    y = _moe_experts(xs, w_in, b_in, w_out, b_out, layer, n_tiles, tile_e, nsub)
    return _combine_ln(y, pos, gates, h, ln_g[layer], ln_b[layer])


def kernel(x, mem, w_in_dil, w_in_ret, ret_gn_g, w_mem_kv, w_mix_out, ln_mix_g, ln_mix_b, router_w, router_b, moe_w_in, moe_b_in, moe_w_out, moe_b_out, ln_ffn_g, ln_ffn_b):
    bf16 = jnp.bfloat16
    h = x.reshape(N_TOK, D_MODEL)
    hb = h
    memb = mem.reshape(BATCH * MEM_LEN, D_MODEL)
    for layer in range(DEPTH):
        slot = layer // 2
        if layer % 2 == 0:
            kb = SELF_WIDTH // GROUP_WIDTH
            proj = _matmul(hb, w_in_dil, slot, bf16, col_tiles=(0, kb, kb + 1))
            self_out = _dilated_attention(hb, w_in_dil, slot, proj)
        else:
            proj = _matmul(hb, w_in_ret, slot, bf16)
            self_out = _retention(proj, ret_gn_g[slot])
        memkv = _matmul(memb, w_mem_kv, layer, bf16)
        mem_out = _memory_attention(proj, memkv)
        h, hb = _mix_ln(self_out, mem_out, w_mix_out[layer].astype(bf16), h,
                        ln_mix_g[layer], ln_mix_b[layer])
        h, hb = _moe(h, hb, layer, router_w, router_b, moe_w_in, moe_b_in, moe_w_out, moe_b_out,
                     ln_ffn_g, ln_ffn_b)
    return h.reshape(BATCH, SEQ, D_MODEL)
```

```python
import functools
import math

import jax
import jax.numpy as jnp
from jax import lax
from jax.experimental import pallas as pl
from jax.experimental.pallas import tpu as pltpu

D_MODEL = 2048
BATCH = 2
SEQ = 4096
DEPTH = 2
HEAD_DIM = 128
DILATED_GROUPS = ((128, 1), (512, 4), (2048, 16))
HEADS_PER_GROUP = 4
N_SELF_HEADS = len(DILATED_GROUPS) * HEADS_PER_GROUP
N_RET_HEADS = 12
SELF_WIDTH = N_SELF_HEADS * HEAD_DIM
MEM_HEADS = 4
MEM_LEN = 256
MEM_WIDTH = MEM_HEADS * HEAD_DIM
RET_CHUNK = 128
N_EXPERTS = 32
TOP_K = 4
D_EXPERT = D_MODEL
SWIGLU_ALPHA = 1.702
SWIGLU_LIMIT = 7.0
DEEPNORM_ALPHA = (2 * DEPTH) ** 0.25
LN_EPS = 1e-5
NEG_INF = -1e30

N_TOK = BATCH * SEQ
GROUP_WIDTH = HEADS_PER_GROUP * HEAD_DIM
SELF_QKV = 3 * GROUP_WIDTH
ATTN_STEPS = 128
ATTN_QB = 2
RET_STEP_CHUNKS = 2

LANES = 128
SUBLANES = 8
VMEM_LIMIT = 56 * 1024 * 1024

MM_TM = 1024
MM_TN = 512
ROW_TILE = 512
COMBINE_ROWS = 512
ROUTER_TM = 256
ROUTER_PARTS = 2

MOE_SUB = 64
MOE_TILE = 1152
MOE_SUBS = MOE_TILE // MOE_SUB
MOE_FAST_SUBS = (15, 16, 17, 18)
MOE_TF = 512
MOE_DOWN_TN = 512
MOE_VMEM_LIMIT = 60 * 1024 * 1024
MOE_TILES = (N_TOK * TOP_K) // MOE_TILE + N_EXPERTS
MOE_ROWS = MOE_TILES * MOE_TILE
DISPATCH_TB = 512
COMBINE_TB = 256


def _alibi_slopes(n):
    def pow2(m):
        start = 2.0 ** (-8.0 / m)
        return [start ** (i + 1) for i in range(m)]

    if math.log2(n).is_integer():
        s = pow2(n)
    else:
        c = 2 ** math.floor(math.log2(n))
        s = pow2(c) + pow2(2 * c)[0::2][: n - c]
    return sorted(s, reverse=True)


def _params(*sem, vmem_limit=VMEM_LIMIT):
    return pltpu.CompilerParams(dimension_semantics=sem, vmem_limit_bytes=vmem_limit)


def _layer_norm_rows(z, g, b):
    mu = jnp.mean(z, axis=-1, keepdims=True)
    zc = z - mu
    var = jnp.mean(zc * zc, axis=-1, keepdims=True)
    return zc * lax.rsqrt(var + LN_EPS) * g + b


def _dot_nt(a, b):
    return lax.dot_general(a, b, (((1,), (1,)), ((), ())), preferred_element_type=jnp.float32)


def _dot(a, b):
    return jnp.dot(a, b, preferred_element_type=jnp.float32)


def _mm_kernel(x_ref, w_ref, o_ref):
    o_ref[...] = _dot(x_ref[...].astype(jnp.bfloat16), w_ref[...].astype(jnp.bfloat16)).astype(o_ref.dtype)


def _matmul(x, w, layer, out_dtype, col_tiles=None):
    m, k = x.shape
    first, stride, count = col_tiles if col_tiles else (0, 1, w.shape[2] // MM_TN)
    tm = min(MM_TM * (4 // x.dtype.itemsize), m)
    return pl.pallas_call(
        _mm_kernel,
        out_shape=jax.ShapeDtypeStruct((m, count * MM_TN), out_dtype),
        grid=(m // tm, count),
        in_specs=[pl.BlockSpec((tm, k), lambda i, j: (i, 0)),
                  pl.BlockSpec((None, k, MM_TN), lambda i, j: (layer, 0, first + stride * j))],
        out_specs=pl.BlockSpec((tm, MM_TN), lambda i, j: (i, j)),
        compiler_params=_params("parallel", "parallel"),
        name="dense_matmul",
    )(x, w)


def _proj_residue_kernel(x_ref, wq_ref, wk_ref, wv_ref, o_ref, scr_ref, *, dilation):
    x = x_ref[...].astype(jnp.bfloat16)
    rows = MM_TM // dilation
    for section, w_ref in enumerate((wq_ref, wk_ref, wv_ref)):
        res = _dot(x, w_ref[...].astype(jnp.bfloat16))
        for c in range(GROUP_WIDTH // LANES):
            scr_ref[c] = res[:, c * LANES:(c + 1) * LANES]
        for r in range(dilation):
            for c in range(GROUP_WIDTH // LANES):
                col = r * SELF_QKV + section * GROUP_WIDTH + c * LANES
                o_ref[:, col:col + LANES] = scr_ref[c, pl.ds(r, rows, stride=dilation), :].astype(o_ref.dtype)


def _proj_residue_major(x, w, layer, group):
    _, dilation = DILATED_GROUPS[group]
    k = x.shape[1]
    kb = SELF_WIDTH // GROUP_WIDTH

    def wspec(section):
        return pl.BlockSpec((None, k, GROUP_WIDTH), lambda i: (layer, 0, section * kb + group))

    return pl.pallas_call(
        functools.partial(_proj_residue_kernel, dilation=dilation),
        out_shape=jax.ShapeDtypeStruct((N_TOK // dilation, dilation * SELF_QKV), jnp.bfloat16),
        grid=(N_TOK // MM_TM,),
        in_specs=[pl.BlockSpec((MM_TM, k), lambda i: (i, 0)), wspec(0), wspec(1), wspec(2)],
        out_specs=pl.BlockSpec((MM_TM // dilation, dilation * SELF_QKV), lambda i: (i, 0)),
        scratch_shapes=[pltpu.VMEM((GROUP_WIDTH // LANES, MM_TM, LANES), jnp.float32)],
        compiler_params=_params("parallel"),
        name=f"proj_residue_g{group}",
    )(x, w, w, w)


def _dil_attn_kernel(q_ref, kp_ref, kc_ref, vp_ref, vc_ref, o_ref, lse_ref, *, slopes, dilation):
    n = pl.program_id(2)
    heads = HEADS_PER_GROUP
    rows = ATTN_QB * heads * ATTN_STEPS
    head_slices = [slice(h * HEAD_DIM, (h + 1) * HEAD_DIM) for h in range(heads)]
    blocks = [slice(s * ATTN_STEPS, (s + 1) * ATTN_STEPS) for s in range(ATTN_QB)]
    row = lax.broadcasted_iota(jnp.int32, (rows, ATTN_STEPS), 0)
    kj = lax.broadcasted_iota(jnp.int32, (rows, ATTN_STEPS), 1)
    diff = (row % ATTN_STEPS) - kj
    valid_c = diff >= 0
    has_prev = jnp.logical_or(row >= heads * ATTN_STEPS, n > 0)
    valid_p = jnp.logical_and(diff <= 0, has_prev)
    slope = jnp.full((rows, ATTN_STEPS), slopes[0], jnp.float32)
    head_of_row = (row // ATTN_STEPS) % heads
    for h in range(1, heads):
        slope = jnp.where(head_of_row == h, slopes[h], slope)
    bias_c = slope * (diff * dilation).astype(jnp.float32)
    bias_p = slope * ((diff + ATTN_STEPS) * dilation).astype(jnp.float32)
    scale = HEAD_DIM ** -0.5

    def prev_cur(cur_ref, prev_ref, s, sl):
        prev = prev_ref[:, sl] if s == 0 else cur_ref[blocks[s - 1], sl]
        return prev, cur_ref[blocks[s], sl]

    pairs = [(s, sl) for s in range(ATTN_QB) for sl in head_slices]
    s_c = jnp.concatenate([_dot_nt(q_ref[blocks[s], sl], prev_cur(kc_ref, kp_ref, s, sl)[1])
                           for s, sl in pairs], axis=0)
    s_p = jnp.concatenate([_dot_nt(q_ref[blocks[s], sl], prev_cur(kc_ref, kp_ref, s, sl)[0])
                           for s, sl in pairs], axis=0)
    s_c = jnp.where(valid_c, s_c * scale - bias_c, NEG_INF)
    s_p = jnp.where(valid_p, s_p * scale - bias_p, NEG_INF)
    m = jnp.max(jnp.maximum(s_c, s_p), axis=-1, keepdims=True)
    e_c = jnp.exp(s_c - m)
    e_p = jnp.exp(s_p - m)
    l = jnp.sum(e_c + e_p, axis=-1, keepdims=True)
    inv_l = 1.0 / l
    p_c = (e_c * inv_l).astype(jnp.bfloat16)
    p_p = (e_p * inv_l).astype(jnp.bfloat16)
    lse = m + jnp.log(l)
    for i, (s, sl) in enumerate(pairs):
        chain = slice(i * ATTN_STEPS, (i + 1) * ATTN_STEPS)
        v_prev, v_cur = prev_cur(vc_ref, vp_ref, s, sl)
        o_ref[blocks[s], sl] = _dot(p_c[chain], v_cur) + _dot(p_p[chain], v_prev)
        lse_ref[blocks[s], sl] = jnp.broadcast_to(lse[chain], (ATTN_STEPS, HEAD_DIM))


def _dilated_group(src, group):
    _, dilation = DILATED_GROUPS[group]
    length = SEQ // dilation
    span = ATTN_QB * ATTN_STEPS
    cb = src.shape[1] // dilation // GROUP_WIDTH
    view = src.reshape(BATCH, length, src.shape[1])
    blk = (None, span, GROUP_WIDTH)
    prev_blk = (None, ATTN_STEPS, GROUP_WIDTH)

    def col(section):
        return lambda b, r, n: (b, n, r * cb + section)

    def col_prev(section):
        return lambda b, r, n: (b, jnp.maximum(n * ATTN_QB - 1, 0), r * cb + section)

    slopes = tuple(_alibi_slopes(N_SELF_HEADS)[group * HEADS_PER_GROUP:(group + 1) * HEADS_PER_GROUP])
    out_shape = jax.ShapeDtypeStruct((BATCH, length, dilation * GROUP_WIDTH), jnp.float32)
    out_spec = pl.BlockSpec(blk, lambda b, r, n: (b, n, r))
    o, lse = pl.pallas_call(
        functools.partial(_dil_attn_kernel, slopes=slopes, dilation=dilation),
        out_shape=(out_shape, out_shape),
        grid=(BATCH, dilation, length // span),
        in_specs=[pl.BlockSpec(blk, col(0)),
                  pl.BlockSpec(prev_blk, col_prev(1)), pl.BlockSpec(blk, col(1)),
                  pl.BlockSpec(prev_blk, col_prev(2)), pl.BlockSpec(blk, col(2))],
        out_specs=(out_spec, out_spec),
        compiler_params=_params("parallel", "parallel", "arbitrary"),
        name=f"dilated_attn_g{group}",
    )(view, view, view, view, view)
    return o, lse


def _dil_combine_kernel(*refs):
    ngroups = len(DILATED_GROUPS)
    o_refs, l_refs, out_ref = refs[:ngroups], refs[ngroups:2 * ngroups], refs[2 * ngroups]
    scratch = list(refs[2 * ngroups + 1:])
    os, ls = [], []
    for g, (_, dilation) in enumerate(DILATED_GROUPS):
        if dilation == 1:
            os.append(o_refs[g][...])
            ls.append(l_refs[g][...])
            continue
        rows = COMBINE_ROWS // dilation
        planes = GROUP_WIDTH // LANES
        o_scr, l_scr = scratch.pop(0), scratch.pop(0)
        for r in range(dilation):
            for c in range(planes):
                cols = slice(r * GROUP_WIDTH + c * LANES, r * GROUP_WIDTH + (c + 1) * LANES)
                o_scr[c, pl.ds(r, rows, stride=dilation), :] = o_refs[g][:, cols]
                l_scr[c, pl.ds(r, rows, stride=dilation), :] = l_refs[g][:, cols]
        os.append(jnp.concatenate([o_scr[c] for c in range(planes)], axis=1))
        ls.append(jnp.concatenate([l_scr[c] for c in range(planes)], axis=1))
    m = functools.reduce(jnp.maximum, ls)
    es = [jnp.exp(l - m) for l in ls]
    inv = 1.0 / functools.reduce(lambda a, b: a + b, es)
    for g in range(ngroups):
        out_ref[:, g * GROUP_WIDTH:(g + 1) * GROUP_WIDTH] = (os[g] * (es[g] * inv)).astype(out_ref.dtype)


def _dilated_attention(x, w_in, layer, proj):
    sources = [proj] + [_proj_residue_major(x, w_in, layer, g) for g in range(1, len(DILATED_GROUPS))]
    outs, lses = zip(*[_dilated_group(src, g) for g, src in enumerate(sources)])

    def spec(dilation):
        return pl.BlockSpec((COMBINE_ROWS // dilation, dilation * GROUP_WIDTH), lambda i: (i, 0))

    def flat(a):
        return a.reshape(a.shape[0] * a.shape[1], a.shape[2])

    specs = [spec(d) for _, d in DILATED_GROUPS]
    scratch = [pltpu.VMEM((GROUP_WIDTH // LANES, COMBINE_ROWS, LANES), jnp.float32)
               for _, d in DILATED_GROUPS if d > 1 for _ in range(2)]
    return pl.pallas_call(
        _dil_combine_kernel,
        out_shape=jax.ShapeDtypeStruct((N_TOK, SELF_WIDTH), jnp.bfloat16),
        grid=(N_TOK // COMBINE_ROWS,),
        in_specs=specs + specs,
        out_specs=pl.BlockSpec((COMBINE_ROWS, SELF_WIDTH), lambda i: (i, 0)),
        scratch_shapes=scratch,
        compiler_params=_params("parallel"),
        name="dilated_combine",
    )(*[flat(o) for o in outs], *[flat(l) for l in lses])


def _retention_kernel(q_ref, k_ref, v_ref, g_ref, dmat_ref, kdec_ref, qdec_ref, cdec_ref, gn_ref,
                      o_ref, state_ref):
    @pl.when(pl.program_id(1) == 0)
    def _():
        state_ref[...] = jnp.zeros_like(state_ref)

    heads = range(N_RET_HEADS)
    cols = [slice(h * HEAD_DIM, (h + 1) * HEAD_DIM) for h in heads]
    chunks = [slice(i * RET_CHUNK, (i + 1) * RET_CHUNK) for i in range(RET_STEP_CHUNKS)]
    pairs = [(i, h) for i in range(RET_STEP_CHUNKS) for h in heads]
    bf16 = jnp.bfloat16
    scores = {(i, h): _dot_nt(q_ref[chunks[i], cols[h]], k_ref[chunks[i], cols[h]]) * dmat_ref[h]
              for i, h in pairs}
    intra = {(i, h): _dot(scores[i, h].astype(bf16), v_ref[chunks[i], cols[h]]) for i, h in pairs}
    kw = {(i, h): (k_ref[chunks[i], cols[h]].astype(jnp.float32) * kdec_ref[:, cols[h]]).astype(bf16)
          for i, h in pairs}
    kv = {(i, h): lax.dot_general(kw[i, h], v_ref[chunks[i], cols[h]], (((0,), (0,)), ((), ())),
                                  preferred_element_type=jnp.float32) for i, h in pairs}
    states = [state_ref[h] for h in heads]
    cross = {}
    for i in range(RET_STEP_CHUNKS):
        for h in heads:
            cross[i, h] = _dot(q_ref[chunks[i], cols[h]], states[h].astype(bf16)) * qdec_ref[:, cols[h]]
        states = [states[h] * cdec_ref[h] + kv[i, h] for h in heads]
    for h in heads:
        state_ref[h] = states[h]
    r = jnp.concatenate([intra[p] + cross[p] for p in pairs], axis=0)
    mu = jnp.mean(r, axis=-1, keepdims=True)
    rc = r - mu
    var = jnp.mean(rc * rc, axis=-1, keepdims=True)
    rn = rc * lax.rsqrt(var + LN_EPS)
    for n, (i, h) in enumerate(pairs):
        gate = g_ref[chunks[i], cols[h]].astype(jnp.float32)
        normed = rn[n * RET_CHUNK:(n + 1) * RET_CHUNK] * gn_ref[:, cols[h]]
        o_ref[chunks[i], cols[h]] = (gate * (1.0 / (1.0 + jnp.exp(-gate))) * normed).astype(o_ref.dtype)


def _retention(proj, gn_gain):
    c = RET_CHUNK
    log_gamma = jnp.log1p(-jnp.exp2(-(5.0 + jnp.arange(N_RET_HEADS, dtype=jnp.float32))))
    idx = jnp.arange(c, dtype=jnp.float32)
    diff = idx[:, None] - idx[None, :]
    decay = jnp.where(diff >= 0, jnp.exp(jnp.maximum(diff, 0.0)[None] * log_gamma[:, None, None]), 0.0)
    scale = HEAD_DIM ** -0.5
    dmat = decay * scale
    k_decay = jnp.exp((c - 1 - idx)[:, None] * log_gamma[None, :]) * scale
    q_decay = jnp.exp((idx + 1.0)[:, None] * log_gamma[None, :])
    kdec = jnp.repeat(k_decay, HEAD_DIM, axis=1)
    qdec = jnp.repeat(q_decay, HEAD_DIM, axis=1)
    cdec = jnp.broadcast_to(jnp.exp(c * log_gamma)[:, None, None], (N_RET_HEADS, 1, HEAD_DIM))
    gn = gn_gain.reshape(1, SELF_WIDTH).astype(jnp.float32)
    nstep = SEQ // (c * RET_STEP_CHUNKS)
    blk = (c * RET_STEP_CHUNKS, SELF_WIDTH)
    chunk_blk = (c, SELF_WIDTH)

    def section(s):
        return pl.BlockSpec(blk, lambda b, n: (b * nstep + n, s))

    const2 = lambda b, n: (0, 0)
    const3 = lambda b, n: (0, 0, 0)
    return pl.pallas_call(
        _retention_kernel,
        out_shape=jax.ShapeDtypeStruct((N_TOK, SELF_WIDTH), jnp.bfloat16),
        grid=(BATCH, nstep),
        in_specs=[section(0), section(1), section(2), section(3),
                  pl.BlockSpec((N_RET_HEADS, c, c), const3),
                  pl.BlockSpec(chunk_blk, const2), pl.BlockSpec(chunk_blk, const2),
                  pl.BlockSpec((N_RET_HEADS, 1, HEAD_DIM), const3),
                  pl.BlockSpec((1, SELF_WIDTH), const2)],
        out_specs=pl.BlockSpec(blk, lambda b, n: (b * nstep + n, 0)),
        scratch_shapes=[pltpu.VMEM((N_RET_HEADS, HEAD_DIM, HEAD_DIM), jnp.float32)],
        compiler_params=_params("parallel", "arbitrary"),
        name="retention",
    )(proj, proj, proj, proj, dmat, kdec, qdec, cdec, gn)


def _mem_attn_kernel(q_ref, k_ref, v_ref, o_ref):
    scale = HEAD_DIM ** -0.5
    for h in range(MEM_HEADS):
        sl = slice(h * HEAD_DIM, (h + 1) * HEAD_DIM)
        s = _dot_nt(q_ref[:, sl], k_ref[:, sl]) * scale
        e = jnp.exp(s - jnp.max(s, axis=-1, keepdims=True))
        p = e * (1.0 / jnp.sum(e, axis=-1, keepdims=True))
        o_ref[:, sl] = _dot(p.astype(jnp.bfloat16), v_ref[:, sl]).astype(o_ref.dtype)


def _memory_attention(proj, memkv):
    qcol = proj.shape[1] // MEM_WIDTH - 1
    per_b = SEQ // ROW_TILE
    return pl.pallas_call(
        _mem_attn_kernel,
        out_shape=jax.ShapeDtypeStruct((N_TOK, MEM_WIDTH), jnp.bfloat16),
        grid=(BATCH, per_b),
        in_specs=[pl.BlockSpec((ROW_TILE, MEM_WIDTH), lambda b, i: (b * per_b + i, qcol)),
                  pl.BlockSpec((MEM_LEN, MEM_WIDTH), lambda b, i: (b, 0)),
                  pl.BlockSpec((MEM_LEN, MEM_WIDTH), lambda b, i: (b, 1))],
        out_specs=pl.BlockSpec((ROW_TILE, MEM_WIDTH), lambda b, i: (b * per_b + i, 0)),
        compiler_params=_params("parallel", "parallel"),
        name="memory_attn",
    )(proj, memkv, memkv)


def _mix_ln_kernel(so_ref, mo_ref, wt_ref, wb_ref, h_ref, g_ref, b_ref, hf_ref, hb_ref):
    half = ROW_TILE // 2
    for rows in (slice(0, half), slice(half, ROW_TILE)):
        mix = _dot(so_ref[rows, :], wt_ref[...]) + _dot(mo_ref[rows, :], wb_ref[...])
        y = _layer_norm_rows(DEEPNORM_ALPHA * h_ref[rows, :] + mix, g_ref[...], b_ref[...])
        hf_ref[rows, :] = y
        hb_ref[rows, :] = y.astype(hb_ref.dtype)


def _mix_ln(self_out, mem_out, w_mix, h, g, b):
    row = lambda i: (i, 0)
    const = lambda i: (0, 0)
    return pl.pallas_call(
        _mix_ln_kernel,
        out_shape=(jax.ShapeDtypeStruct((N_TOK, D_MODEL), jnp.float32),
                   jax.ShapeDtypeStruct((N_TOK, D_MODEL), jnp.bfloat16)),
        grid=(N_TOK // ROW_TILE,),
        in_specs=[pl.BlockSpec((ROW_TILE, SELF_WIDTH), row),
                  pl.BlockSpec((ROW_TILE, MEM_WIDTH), row),
                  pl.BlockSpec((SELF_WIDTH, D_MODEL), const),
                  pl.BlockSpec((MEM_WIDTH, D_MODEL), lambda i: (SELF_WIDTH // MEM_WIDTH, 0)),
                  pl.BlockSpec((ROW_TILE, D_MODEL), row),
                  pl.BlockSpec((1, D_MODEL), const), pl.BlockSpec((1, D_MODEL), const)],
        out_specs=(pl.BlockSpec((ROW_TILE, D_MODEL), row), pl.BlockSpec((ROW_TILE, D_MODEL), row)),
        compiler_params=_params("parallel"),
        name="mix_ln",
    )(self_out, mem_out, w_mix, w_mix, h, g.reshape(1, D_MODEL), b.reshape(1, D_MODEL))


def _router_kernel(h_ref, whi_ref, wlo_ref, b_ref, e_ref, g_ref, r_ref, cnt_ref, run_ref):
    tm = ROUTER_TM

    @pl.when(pl.program_id(0) == 0)
    def _():
        run_ref[...] = jnp.zeros_like(run_ref)

    parts = [slice(p * tm, (p + 1) * tm) for p in range(ROUTER_PARTS)]
    both = range(ROUTER_PARTS)
    hs = [h_ref[rows, :] for rows in parts]
    h_his = [h.astype(jnp.bfloat16) for h in hs]
    h_los = [(hs[p] - h_his[p].astype(jnp.float32)).astype(jnp.bfloat16) for p in both]
    works = [(_dot(h_his[p], whi_ref[...]) + _dot(h_los[p], whi_ref[...]) + _dot(h_his[p], wlo_ref[...]))
             + b_ref[...] for p in both]
    lane = lax.broadcasted_iota(jnp.int32, (tm, LANES), 1).astype(jnp.float32)
    vals, idxs, hots = [[] for _ in both], [[] for _ in both], [[] for _ in both]
    for _ in range(TOP_K):
        for p in both:
            m = jnp.max(works[p], axis=-1, keepdims=True)
            idx = jnp.min(jnp.where(works[p] == m, lane, float(LANES)), axis=-1, keepdims=True)
            hot = lane == idx
            vals[p].append(m)
            idxs[p].append(idx)
            hots[p].append(hot)
            works[p] = jnp.where(hot, -jnp.inf, works[p])
    row = lax.broadcasted_iota(jnp.int32, (tm, tm), 0)
    colm = lax.broadcasted_iota(jnp.int32, (tm, tm), 1)
    tri = (row > colm).astype(jnp.bfloat16)
    lane_i = lax.broadcasted_iota(jnp.int32, (tm, LANES), 1)
    run = run_ref[...]
    for p in both:
        exps = [jnp.exp(v - vals[p][0]) for v in vals[p]]
        inv = 1.0 / (exps[0] + exps[1] + exps[2] + exps[3])
        cnt = jnp.zeros((tm, LANES), jnp.float32)
        for hot in hots[p]:
            cnt = cnt + hot.astype(jnp.float32)
        before = run + _dot(tri, cnt.astype(jnp.bfloat16))
        e_out = jnp.zeros((tm, LANES), jnp.float32)
        g_out = jnp.zeros((tm, LANES), jnp.float32)
        r_out = jnp.zeros((tm, LANES), jnp.float32)
        for k in range(TOP_K):
            rank = jnp.sum(jnp.where(hots[p][k], before, 0.0), axis=-1, keepdims=True)
            e_out = jnp.where(lane_i == k, idxs[p][k], e_out)
            g_out = jnp.where(lane_i == k, exps[k] * inv, g_out)
            r_out = jnp.where(lane_i == k, rank, r_out)
        e_ref[:, parts[p]] = e_out.T[0:SUBLANES, :].astype(jnp.int32)
        g_ref[parts[p], :] = g_out
        r_ref[:, parts[p]] = r_out.T[0:SUBLANES, :].astype(jnp.int32)
        run = run + jnp.sum(cnt, axis=0, keepdims=True)
    run_ref[...] = run
    cnt_ref[...] = run.astype(jnp.int32)


def _router(h, router_w, router_b):
    w = jnp.zeros((D_MODEL, LANES), jnp.float32).at[:, :N_EXPERTS].set(router_w)
    w_hi = w.astype(jnp.bfloat16)
    w_lo = (w - w_hi.astype(jnp.float32)).astype(jnp.bfloat16)
    b = jnp.full((1, LANES), NEG_INF, jnp.float32).at[0, :N_EXPERTS].set(router_b)
    tm = ROUTER_TM * ROUTER_PARTS
    row = lambda i: (i, 0)
    const = lambda i: (0, 0)
    lanes_out = pl.BlockSpec((tm, LANES), row)
    slots_out = pl.BlockSpec((SUBLANES, tm), lambda i: (0, i))
    e, g, r, cnt = pl.pallas_call(
        _router_kernel,
        out_shape=(jax.ShapeDtypeStruct((SUBLANES, N_TOK), jnp.int32),
                   jax.ShapeDtypeStruct((N_TOK, LANES), jnp.float32),
                   jax.ShapeDtypeStruct((SUBLANES, N_TOK), jnp.int32),
                   jax.ShapeDtypeStruct((1, LANES), jnp.int32)),
        grid=(N_TOK // tm,),
        in_specs=[pl.BlockSpec((tm, D_MODEL), row), pl.BlockSpec((D_MODEL, LANES), const),
                  pl.BlockSpec((D_MODEL, LANES), const), pl.BlockSpec((1, LANES), const)],
        out_specs=(slots_out, lanes_out, slots_out, pl.BlockSpec((1, LANES), const)),
        scratch_shapes=[pltpu.VMEM((1, LANES), jnp.float32)],
        compiler_params=_params("arbitrary"),
        name="router",
    )(h, w_hi, w_lo, b)
    return e[:TOP_K], g, r[:TOP_K], cnt[0, :N_EXPERTS]


def _route_tables(top_e, rank, cnt):
    ntile = (cnt + MOE_TILE - 1) // MOE_TILE
    tile_end = jnp.cumsum(ntile)
    tile_base = tile_end - ntile
    experts = jnp.arange(N_EXPERTS, dtype=jnp.int32)[:, None, None]
    base = jnp.sum(jnp.where(top_e[None] == experts, tile_base[:, None, None], 0), axis=0)
    pos = (base * MOE_TILE + rank).astype(jnp.int32).reshape(-1)
    n_tiles = tile_end[-1:].astype(jnp.int32)
    w = jnp.minimum(jnp.arange(MOE_TILES, dtype=jnp.int32), n_tiles[0] - 1)
    tile_e = jnp.minimum(jnp.searchsorted(tile_end, w, side="right"), N_EXPERTS - 1).astype(jnp.int32)
    rows = jnp.clip(cnt[tile_e] - (w - tile_base[tile_e]) * MOE_TILE, 0, MOE_TILE)
    nsub = ((rows + MOE_SUB - 1) // MOE_SUB).astype(jnp.int32)
    pad_start = (tile_base * MOE_TILE + cnt).astype(jnp.int32)
    pad_n = ((-cnt) % MOE_SUB).astype(jnp.int32)
    return pos, n_tiles, tile_e, nsub, pad_start, pad_n


def _row_copy(src_vmem, src_row, dst_hbm, dst_row, sem):
    return pltpu.make_async_copy(src_vmem.at[pl.ds(src_row, 1)], dst_hbm.at[pl.ds(dst_row, 1)], sem)


def _pack_bf16_pairs(x):
    half = x.shape[1] // 2
    lo = lax.bitcast_convert_type(x[:, :half].astype(jnp.float32), jnp.uint32)
    hi = lax.bitcast_convert_type(x[:, half:].astype(jnp.float32), jnp.uint32)
    return (lo >> 16) | (hi & jnp.uint32(0xFFFF0000))


def _unpack_bf16_pairs(words):
    lo = lax.bitcast_convert_type(words << 16, jnp.float32)
    hi = lax.bitcast_convert_type(words & jnp.uint32(0xFFFF0000), jnp.float32)
    return lo.astype(jnp.bfloat16), hi.astype(jnp.bfloat16)


def _dispatch_kernel(pos_ref, pad_start_ref, pad_n_ref, hb_ref, xs_hbm, h_ref, sem):
    step = pl.program_id(0)
    base = step * DISPATCH_TB
    h_ref[...] = _pack_bf16_pairs(hb_ref[...])

    for r in range(DISPATCH_TB):
        for k in range(TOP_K):
            _row_copy(h_ref, r, xs_hbm, pos_ref[k * N_TOK + base + r], sem).start(priority=k % 2)

    @pl.when(step == 0)
    def _():
        def per_expert(e, carry):
            start = pad_start_ref[e]
            n = pad_n_ref[e]

            def fill(i, c):
                _row_copy(h_ref, 0, xs_hbm, start + i, sem).start()
                return c

            def drain(_, c):
                _row_copy(h_ref, 0, xs_hbm, 0, sem).wait()
                return c

            lax.fori_loop(0, n, fill, 0)
            lax.fori_loop(0, n, drain, 0)
            return carry

        lax.fori_loop(0, N_EXPERTS, per_expert, 0)

    for _ in range(TOP_K):
        pltpu.make_async_copy(h_ref, xs_hbm.at[pl.ds(0, DISPATCH_TB)], sem).wait()


def _dispatch(hb, pos, pad_start, pad_n):
    return pl.pallas_call(
        _dispatch_kernel,
        out_shape=jax.ShapeDtypeStruct((MOE_ROWS, D_MODEL // 2), jnp.uint32),
        grid_spec=pltpu.PrefetchScalarGridSpec(
            num_scalar_prefetch=3,
            grid=(N_TOK // DISPATCH_TB,),
            in_specs=[pl.BlockSpec((DISPATCH_TB, D_MODEL), lambda i, p, s, n: (i, 0))],
            out_specs=pl.BlockSpec(memory_space=pl.ANY),
            scratch_shapes=[pltpu.VMEM((DISPATCH_TB, D_MODEL // 2), jnp.uint32),
                            pltpu.SemaphoreType.DMA(())],
        ),
        compiler_params=_params("arbitrary"),
        name="moe_dispatch",
    )(pos, pad_start, pad_n, hb)


def _moe_weight_map(layer, first, hidden_axis):
    def index_map(w, j, tile_e, nsub):
        return (layer, tile_e[w], first + j, 0) if hidden_axis == 2 else (layer, tile_e[w], 0, first + j)

    return index_map


def _moe_rows(nsub, accumulate):
    fast = nsub < 0
    for s in MOE_FAST_SUBS:
        fast = jnp.logical_or(fast, nsub == s)
        pl.when(nsub == s)(functools.partial(accumulate, slice(0, s * MOE_SUB)))

    @pl.when(jnp.logical_and(nsub > 0, jnp.logical_not(fast)))
    def _():
        def one(i, carry):
            accumulate(pl.ds(pl.multiple_of(i * MOE_SUB, MOE_SUB), MOE_SUB))
            return carry

        lax.fori_loop(0, nsub, one, 0)


def _moe_expert_kernel(exp_ref, nsub_ref, x_ref, wg_ref, wl_ref, bg_ref, bl_ref, wo_ref, bo_ref, o_ref):
    first = pl.program_id(1) == 0
    nsub = nsub_ref[pl.program_id(0)]
    half = D_MODEL // 2

    @pl.when(first)
    def _():
        o_ref[...] = jnp.broadcast_to(bo_ref[...], (MOE_TILE, D_MODEL))

    def up(lo, hi, w_ref, b_ref):
        return (_dot(lo, w_ref[0:half, :].astype(jnp.bfloat16))
                + _dot(hi, w_ref[half:D_MODEL, :].astype(jnp.bfloat16)) + b_ref[...])

    def accumulate(rows):
        lo, hi = _unpack_bf16_pairs(x_ref[rows, :])
        gate = up(lo, hi, wg_ref, bg_ref)
        lin = up(lo, hi, wl_ref, bl_ref)
        gate = jnp.minimum(gate, SWIGLU_LIMIT)
        lin = jnp.clip(lin, -SWIGLU_LIMIT, SWIGLU_LIMIT)
        act = (gate * (1.0 / (1.0 + jnp.exp(-SWIGLU_ALPHA * gate))) * (lin + 1.0)).astype(jnp.bfloat16)
        for c in range(D_MODEL // MOE_DOWN_TN):
            cols = slice(c * MOE_DOWN_TN, (c + 1) * MOE_DOWN_TN)
            o_ref[rows, cols] = o_ref[rows, cols] + _dot(act, wo_ref[:, cols].astype(jnp.bfloat16))

    _moe_rows(nsub, accumulate)


def _moe_experts(xs, w_in, b_in, w_out, b_out, layer, n_tiles, tile_e, nsub):
    nj = D_EXPERT // MOE_TF
    b_in4 = b_in.reshape(DEPTH, N_EXPERTS, 1, 2 * D_EXPERT)
    b_out4 = b_out.reshape(DEPTH, N_EXPERTS, 1, D_MODEL)
    wblk = (None, None, D_MODEL, MOE_TF)
    bblk = (None, None, 1, MOE_TF)
    tile_map = lambda w, j, e, n: (w, 0)
    return pl.pallas_call(
        _moe_expert_kernel,
        out_shape=jax.ShapeDtypeStruct((MOE_ROWS, D_MODEL), jnp.float32),
        grid_spec=pltpu.PrefetchScalarGridSpec(
            num_scalar_prefetch=2,
            grid=(n_tiles[0], nj),
            in_specs=[pl.BlockSpec((MOE_TILE, D_MODEL // 2), tile_map),
                      pl.BlockSpec(wblk, _moe_weight_map(layer, 0, 3)),
                      pl.BlockSpec(wblk, _moe_weight_map(layer, nj, 3)),
                      pl.BlockSpec(bblk, _moe_weight_map(layer, 0, 3)),
                      pl.BlockSpec(bblk, _moe_weight_map(layer, nj, 3)),
                      pl.BlockSpec((None, None, MOE_TF, D_MODEL), _moe_weight_map(layer, 0, 2)),
                      pl.BlockSpec((None, None, 1, D_MODEL), lambda w, j, e, n: (layer, e[w], 0, 0))],
            out_specs=pl.BlockSpec((MOE_TILE, D_MODEL), tile_map),
        ),
        compiler_params=_params("arbitrary", "arbitrary", vmem_limit=MOE_VMEM_LIMIT),
        name="moe_experts",
    )(tile_e, nsub, xs, w_in, w_in, b_in4, b_in4, w_out, b_out4)


def _combine_ln_kernel(pos_ref, y_hbm, gates_ref, h_ref, g_ref, b_ref, hf_ref, hb_ref, buf0, buf1, sems):
    step = pl.program_id(0)
    last = pl.num_programs(0) - 1

    def row_copy(tile, r, k, buf, sem):
        src = y_hbm.at[pl.ds(pos_ref[k * N_TOK + tile * COMBINE_TB + r], 1)]
        return pltpu.make_async_copy(src, buf.at[k, pl.ds(r, 1)], sem)

    def wait_tile(buf, sem):
        for k in range(TOP_K):
            pltpu.make_async_copy(y_hbm.at[pl.ds(0, COMBINE_TB)], buf.at[k], sem).wait()

    @pl.when(step == 0)
    def _():
        def issue(r, carry):
            for k in range(TOP_K):
                row_copy(0, r, k, buf0, sems.at[0]).start()
            return carry

        lax.fori_loop(0, COMBINE_TB, issue, 0, unroll=8)

    def run(cur, cur_sem, nxt, nxt_sem):
        nxt_tile = jnp.minimum(step + 1, last)
        wait_tile(cur, cur_sem)
        for r in range(COMBINE_TB):
            for k in range(TOP_K):
                row_copy(nxt_tile, r, k, nxt, nxt_sem).start(priority=k % 2)
        gates = gates_ref[...]
        ffn = gates[:, 0:1] * cur[0]
        for k in range(1, TOP_K):
            ffn = ffn + gates[:, k:k + 1] * cur[k]
        y = _layer_norm_rows(DEEPNORM_ALPHA * h_ref[...] + ffn, g_ref[...], b_ref[...])
        hf_ref[...] = y
        hb_ref[...] = y.astype(hb_ref.dtype)

        @pl.when(step == last)
        def _():
            wait_tile(nxt, nxt_sem)

    pl.when(step % 2 == 0)(functools.partial(run, buf0, sems.at[0], buf1, sems.at[1]))
    pl.when(step % 2 == 1)(functools.partial(run, buf1, sems.at[1], buf0, sems.at[0]))


def _combine_ln(y, pos, gates, h, g, b):
    tb = COMBINE_TB
    row = lambda i, p: (i, 0)
    const = lambda i, p: (0, 0)
    return pl.pallas_call(
        _combine_ln_kernel,
        out_shape=(jax.ShapeDtypeStruct((N_TOK, D_MODEL), jnp.float32),
                   jax.ShapeDtypeStruct((N_TOK, D_MODEL), jnp.bfloat16)),
        grid_spec=pltpu.PrefetchScalarGridSpec(
            num_scalar_prefetch=1,
            grid=(N_TOK // tb,),
            in_specs=[pl.BlockSpec(memory_space=pl.ANY),
                      pl.BlockSpec((tb, LANES), row),
                      pl.BlockSpec((tb, D_MODEL), row),
                      pl.BlockSpec((1, D_MODEL), const), pl.BlockSpec((1, D_MODEL), const)],
            out_specs=(pl.BlockSpec((tb, D_MODEL), row), pl.BlockSpec((tb, D_MODEL), row)),
            scratch_shapes=[pltpu.VMEM((TOP_K, tb, D_MODEL), jnp.float32),
                            pltpu.VMEM((TOP_K, tb, D_MODEL), jnp.float32),
                            pltpu.SemaphoreType.DMA((2,))],
        ),
        compiler_params=_params("arbitrary"),
        name="moe_combine_ln",
    )(pos, y, gates, h, g.reshape(1, D_MODEL), b.reshape(1, D_MODEL))


def _moe(h, hb, layer, router_w, router_b, w_in, b_in, w_out, b_out, ln_g, ln_b):
    top_e, gates, rank, cnt = _router(h, router_w[layer], router_b[layer])
    pos, n_tiles, tile_e, nsub, pad_start, pad_n = _route_tables(top_e, rank, cnt)
    xs = _dispatch(hb, pos, pad_start, pad_n)
    y = _moe_experts(xs, w_in, b_in, w_out, b_out, layer, n_tiles, tile_e, nsub)
    return _combine_ln(y, pos, gates, h, ln_g[layer], ln_b[layer])


def kernel(x, mem, w_in_dil, w_in_ret, ret_gn_g, w_mem_kv, w_mix_out, ln_mix_g, ln_mix_b, router_w, router_b, moe_w_in, moe_b_in, moe_w_out, moe_b_out, ln_ffn_g, ln_ffn_b):
    bf16 = jnp.bfloat16
    h = x.reshape(N_TOK, D_MODEL)
    hb = h
    memb = mem.reshape(BATCH * MEM_LEN, D_MODEL)
    for layer in range(DEPTH):
        slot = layer // 2
        if layer % 2 == 0:
            kb = SELF_WIDTH // GROUP_WIDTH
            proj = _matmul(hb, w_in_dil, slot, bf16, col_tiles=(0, kb, kb + 1))
            self_out = _dilated_attention(hb, w_in_dil, slot, proj)
        else:
            proj = _matmul(hb, w_in_ret, slot, bf16)
            self_out = _retention(proj, ret_gn_g[slot])
        memkv = _matmul(memb, w_mem_kv, layer, bf16)
        mem_out = _memory_attention(proj, memkv)
        h, hb = _mix_ln(self_out, mem_out, w_mix_out[layer].astype(bf16), h,
                        ln_mix_g[layer], ln_mix_b[layer])
        h, hb = _moe(h, hb, layer, router_w, router_b, moe_w_in, moe_b_in, moe_w_out, moe_b_out,
                     ln_ffn_g, ln_ffn_b)
    return h.reshape(BATCH, SEQ, D_MODEL)
```

```python
import functools
import math

import jax
import jax.numpy as jnp
from jax import lax
from jax.experimental import pallas as pl
from jax.experimental.pallas import tpu as pltpu

D_MODEL = 2048
BATCH = 2
SEQ = 4096
DEPTH = 2
HEAD_DIM = 128
DILATED_GROUPS = ((128, 1), (512, 4), (2048, 16))
HEADS_PER_GROUP = 4
N_SELF_HEADS = len(DILATED_GROUPS) * HEADS_PER_GROUP
N_RET_HEADS = 12
SELF_WIDTH = N_SELF_HEADS * HEAD_DIM
MEM_HEADS = 4
MEM_LEN = 256
MEM_WIDTH = MEM_HEADS * HEAD_DIM
RET_CHUNK = 128
N_EXPERTS = 32
TOP_K = 4
D_EXPERT = D_MODEL
SWIGLU_ALPHA = 1.702
SWIGLU_LIMIT = 7.0
DEEPNORM_ALPHA = (2 * DEPTH) ** 0.25
LN_EPS = 1e-5
NEG_INF = -1e30

N_TOK = BATCH * SEQ
GROUP_WIDTH = HEADS_PER_GROUP * HEAD_DIM
SELF_QKV = 3 * GROUP_WIDTH
ATTN_STEPS = 128
ATTN_QB = 2
RET_STEP_CHUNKS = 2

LANES = 128
SUBLANES = 8
VMEM_LIMIT = 56 * 1024 * 1024

MM_TM = 1024
MM_TN = 512
ROW_TILE = 512
COMBINE_ROWS = 512
ROUTER_TM = 256
ROUTER_PARTS = 4

MOE_SUB = 64
MOE_TILE = 1152
MOE_SUBS = MOE_TILE // MOE_SUB
MOE_FAST_SUBS = (15, 16, 17, 18)
MOE_TF = 512
MOE_DOWN_TN = 512
MOE_VMEM_LIMIT = 60 * 1024 * 1024
MOE_TILES = (N_TOK * TOP_K) // MOE_TILE + N_EXPERTS
MOE_ROWS = MOE_TILES * MOE_TILE
DISPATCH_TB = 512
COMBINE_TB = 256


def _alibi_slopes(n):
    def pow2(m):
        start = 2.0 ** (-8.0 / m)
        return [start ** (i + 1) for i in range(m)]

    if math.log2(n).is_integer():
        s = pow2(n)
    else:
        c = 2 ** math.floor(math.log2(n))
        s = pow2(c) + pow2(2 * c)[0::2][: n - c]
    return sorted(s, reverse=True)


def _params(*sem, vmem_limit=VMEM_LIMIT):
    return pltpu.CompilerParams(dimension_semantics=sem, vmem_limit_bytes=vmem_limit)


def _layer_norm_rows(z, g, b):
    mu = jnp.mean(z, axis=-1, keepdims=True)
    zc = z - mu
    var = jnp.mean(zc * zc, axis=-1, keepdims=True)
    return zc * lax.rsqrt(var + LN_EPS) * g + b


def _dot_nt(a, b):
    return lax.dot_general(a, b, (((1,), (1,)), ((), ())), preferred_element_type=jnp.float32)


def _dot(a, b):
    return jnp.dot(a, b, preferred_element_type=jnp.float32)


def _mm_kernel(x_ref, w_ref, o_ref):
    o_ref[...] = _dot(x_ref[...].astype(jnp.bfloat16), w_ref[...].astype(jnp.bfloat16)).astype(o_ref.dtype)


def _matmul(x, w, layer, out_dtype, col_tiles=None):
    m, k = x.shape
    first, stride, count = col_tiles if col_tiles else (0, 1, w.shape[2] // MM_TN)
    tm = min(MM_TM * (4 // x.dtype.itemsize), m)
    return pl.pallas_call(
        _mm_kernel,
        out_shape=jax.ShapeDtypeStruct((m, count * MM_TN), out_dtype),
        grid=(m // tm, count),
        in_specs=[pl.BlockSpec((tm, k), lambda i, j: (i, 0)),
                  pl.BlockSpec((None, k, MM_TN), lambda i, j: (layer, 0, first + stride * j))],
        out_specs=pl.BlockSpec((tm, MM_TN), lambda i, j: (i, j)),
        compiler_params=_params("parallel", "parallel"),
        name="dense_matmul",
    )(x, w)


def _proj_residue_kernel(x_ref, wq_ref, wk_ref, wv_ref, o_ref, scr_ref, *, dilation):
    x = x_ref[...].astype(jnp.bfloat16)
    rows = MM_TM // dilation
    for section, w_ref in enumerate((wq_ref, wk_ref, wv_ref)):
        res = _dot(x, w_ref[...].astype(jnp.bfloat16))
        for c in range(GROUP_WIDTH // LANES):
            scr_ref[c] = res[:, c * LANES:(c + 1) * LANES]
        for r in range(dilation):
            for c in range(GROUP_WIDTH // LANES):
                col = r * SELF_QKV + section * GROUP_WIDTH + c * LANES
                o_ref[:, col:col + LANES] = scr_ref[c, pl.ds(r, rows, stride=dilation), :].astype(o_ref.dtype)


def _proj_residue_major(x, w, layer, group):
    _, dilation = DILATED_GROUPS[group]
    k = x.shape[1]
    kb = SELF_WIDTH // GROUP_WIDTH

    def wspec(section):
        return pl.BlockSpec((None, k, GROUP_WIDTH), lambda i: (layer, 0, section * kb + group))

    return pl.pallas_call(
        functools.partial(_proj_residue_kernel, dilation=dilation),
        out_shape=jax.ShapeDtypeStruct((N_TOK // dilation, dilation * SELF_QKV), jnp.bfloat16),
        grid=(N_TOK // MM_TM,),
        in_specs=[pl.BlockSpec((MM_TM, k), lambda i: (i, 0)), wspec(0), wspec(1), wspec(2)],
        out_specs=pl.BlockSpec((MM_TM // dilation, dilation * SELF_QKV), lambda i: (i, 0)),
        scratch_shapes=[pltpu.VMEM((GROUP_WIDTH // LANES, MM_TM, LANES), jnp.float32)],
        compiler_params=_params("parallel"),
        name=f"proj_residue_g{group}",
    )(x, w, w, w)


def _dil_attn_kernel(q_ref, kp_ref, kc_ref, vp_ref, vc_ref, o_ref, lse_ref, *, slopes, dilation):
    n = pl.program_id(2)
    heads = HEADS_PER_GROUP
    rows = ATTN_QB * heads * ATTN_STEPS
    head_slices = [slice(h * HEAD_DIM, (h + 1) * HEAD_DIM) for h in range(heads)]
    blocks = [slice(s * ATTN_STEPS, (s + 1) * ATTN_STEPS) for s in range(ATTN_QB)]
    row = lax.broadcasted_iota(jnp.int32, (rows, ATTN_STEPS), 0)
    kj = lax.broadcasted_iota(jnp.int32, (rows, ATTN_STEPS), 1)
    diff = (row % ATTN_STEPS) - kj
    valid_c = diff >= 0
    has_prev = jnp.logical_or(row >= heads * ATTN_STEPS, n > 0)
    valid_p = jnp.logical_and(diff <= 0, has_prev)
    slope = jnp.full((rows, ATTN_STEPS), slopes[0], jnp.float32)
    head_of_row = (row // ATTN_STEPS) % heads
    for h in range(1, heads):
        slope = jnp.where(head_of_row == h, slopes[h], slope)
    bias_c = slope * (diff * dilation).astype(jnp.float32)
    bias_p = slope * ((diff + ATTN_STEPS) * dilation).astype(jnp.float32)
    scale = HEAD_DIM ** -0.5

    def prev_cur(cur_ref, prev_ref, s, sl):
        prev = prev_ref[:, sl] if s == 0 else cur_ref[blocks[s - 1], sl]
        return prev, cur_ref[blocks[s], sl]

    pairs = [(s, sl) for s in range(ATTN_QB) for sl in head_slices]
    s_c = jnp.concatenate([_dot_nt(q_ref[blocks[s], sl], prev_cur(kc_ref, kp_ref, s, sl)[1])
                           for s, sl in pairs], axis=0)
    s_p = jnp.concatenate([_dot_nt(q_ref[blocks[s], sl], prev_cur(kc_ref, kp_ref, s, sl)[0])
                           for s, sl in pairs], axis=0)
    s_c = jnp.where(valid_c, s_c * scale - bias_c, NEG_INF)
    s_p = jnp.where(valid_p, s_p * scale - bias_p, NEG_INF)
    m = jnp.max(jnp.maximum(s_c, s_p), axis=-1, keepdims=True)
    e_c = jnp.exp(s_c - m)
    e_p = jnp.exp(s_p - m)
    l = jnp.sum(e_c + e_p, axis=-1, keepdims=True)
    inv_l = 1.0 / l
    p_c = (e_c * inv_l).astype(jnp.bfloat16)
    p_p = (e_p * inv_l).astype(jnp.bfloat16)
    lse = m + jnp.log(l)
    for i, (s, sl) in enumerate(pairs):
        chain = slice(i * ATTN_STEPS, (i + 1) * ATTN_STEPS)
        v_prev, v_cur = prev_cur(vc_ref, vp_ref, s, sl)
        o_ref[blocks[s], sl] = _dot(p_c[chain], v_cur) + _dot(p_p[chain], v_prev)
        lse_ref[blocks[s], sl] = jnp.broadcast_to(lse[chain], (ATTN_STEPS, HEAD_DIM))


def _dilated_group(src, group):
    _, dilation = DILATED_GROUPS[group]
    length = SEQ // dilation
    span = ATTN_QB * ATTN_STEPS
    cb = src.shape[1] // dilation // GROUP_WIDTH
    view = src.reshape(BATCH, length, src.shape[1])
    blk = (None, span, GROUP_WIDTH)
    prev_blk = (None, ATTN_STEPS, GROUP_WIDTH)

    def col(section):
        return lambda b, r, n: (b, n, r * cb + section)

    def col_prev(section):
        return lambda b, r, n: (b, jnp.maximum(n * ATTN_QB - 1, 0), r * cb + section)

    slopes = tuple(_alibi_slopes(N_SELF_HEADS)[group * HEADS_PER_GROUP:(group + 1) * HEADS_PER_GROUP])
    out_shape = jax.ShapeDtypeStruct((BATCH, length, dilation * GROUP_WIDTH), jnp.float32)
    out_spec = pl.BlockSpec(blk, lambda b, r, n: (b, n, r))
    o, lse = pl.pallas_call(
        functools.partial(_dil_attn_kernel, slopes=slopes, dilation=dilation),
        out_shape=(out_shape, out_shape),
        grid=(BATCH, dilation, length // span),
        in_specs=[pl.BlockSpec(blk, col(0)),
                  pl.BlockSpec(prev_blk, col_prev(1)), pl.BlockSpec(blk, col(1)),
                  pl.BlockSpec(prev_blk, col_prev(2)), pl.BlockSpec(blk, col(2))],
        out_specs=(out_spec, out_spec),
        compiler_params=_params("parallel", "parallel", "arbitrary"),
        name=f"dilated_attn_g{group}",
    )(view, view, view, view, view)
    return o, lse


def _dil_combine_kernel(*refs):
    ngroups = len(DILATED_GROUPS)
    o_refs, l_refs, out_ref = refs[:ngroups], refs[ngroups:2 * ngroups], refs[2 * ngroups]
    scratch = list(refs[2 * ngroups + 1:])
    os, ls = [], []
    for g, (_, dilation) in enumerate(DILATED_GROUPS):
        if dilation == 1:
            os.append(o_refs[g][...])
            ls.append(l_refs[g][...])
            continue
        rows = COMBINE_ROWS // dilation
        planes = GROUP_WIDTH // LANES
        o_scr, l_scr = scratch.pop(0), scratch.pop(0)
        for r in range(dilation):
            for c in range(planes):
                cols = slice(r * GROUP_WIDTH + c * LANES, r * GROUP_WIDTH + (c + 1) * LANES)
                o_scr[c, pl.ds(r, rows, stride=dilation), :] = o_refs[g][:, cols]
                l_scr[c, pl.ds(r, rows, stride=dilation), :] = l_refs[g][:, cols]
        os.append(jnp.concatenate([o_scr[c] for c in range(planes)], axis=1))
        ls.append(jnp.concatenate([l_scr[c] for c in range(planes)], axis=1))
    m = functools.reduce(jnp.maximum, ls)
    es = [jnp.exp(l - m) for l in ls]
    inv = 1.0 / functools.reduce(lambda a, b: a + b, es)
    for g in range(ngroups):
        out_ref[:, g * GROUP_WIDTH:(g + 1) * GROUP_WIDTH] = (os[g] * (es[g] * inv)).astype(out_ref.dtype)


def _dilated_attention(x, w_in, layer, proj):
    sources = [proj] + [_proj_residue_major(x, w_in, layer, g) for g in range(1, len(DILATED_GROUPS))]
    outs, lses = zip(*[_dilated_group(src, g) for g, src in enumerate(sources)])

    def spec(dilation):
        return pl.BlockSpec((COMBINE_ROWS // dilation, dilation * GROUP_WIDTH), lambda i: (i, 0))

    def flat(a):
        return a.reshape(a.shape[0] * a.shape[1], a.shape[2])

    specs = [spec(d) for _, d in DILATED_GROUPS]
    scratch = [pltpu.VMEM((GROUP_WIDTH // LANES, COMBINE_ROWS, LANES), jnp.float32)
               for _, d in DILATED_GROUPS if d > 1 for _ in range(2)]
    return pl.pallas_call(
        _dil_combine_kernel,
        out_shape=jax.ShapeDtypeStruct((N_TOK, SELF_WIDTH), jnp.bfloat16),
        grid=(N_TOK // COMBINE_ROWS,),
        in_specs=specs + specs,
        out_specs=pl.BlockSpec((COMBINE_ROWS, SELF_WIDTH), lambda i: (i, 0)),
        scratch_shapes=scratch,
        compiler_params=_params("parallel"),
        name="dilated_combine",
    )(*[flat(o) for o in outs], *[flat(l) for l in lses])


def _retention_kernel(q_ref, k_ref, v_ref, g_ref, dmat_ref, kdec_ref, qdec_ref, cdec_ref, gn_ref,
                      o_ref, state_ref):
    @pl.when(pl.program_id(1) == 0)
    def _():
        state_ref[...] = jnp.zeros_like(state_ref)

    heads = range(N_RET_HEADS)
    cols = [slice(h * HEAD_DIM, (h + 1) * HEAD_DIM) for h in heads]
    chunks = [slice(i * RET_CHUNK, (i + 1) * RET_CHUNK) for i in range(RET_STEP_CHUNKS)]
    pairs = [(i, h) for i in range(RET_STEP_CHUNKS) for h in heads]
    bf16 = jnp.bfloat16
    scores = {(i, h): _dot_nt(q_ref[chunks[i], cols[h]], k_ref[chunks[i], cols[h]]) * dmat_ref[h]
              for i, h in pairs}
    intra = {(i, h): _dot(scores[i, h].astype(bf16), v_ref[chunks[i], cols[h]]) for i, h in pairs}
    kw = {(i, h): (k_ref[chunks[i], cols[h]].astype(jnp.float32) * kdec_ref[:, cols[h]]).astype(bf16)
          for i, h in pairs}
    kv = {(i, h): lax.dot_general(kw[i, h], v_ref[chunks[i], cols[h]], (((0,), (0,)), ((), ())),
                                  preferred_element_type=jnp.float32) for i, h in pairs}
    states = [state_ref[h] for h in heads]
    cross = {}
    for i in range(RET_STEP_CHUNKS):
        for h in heads:
            cross[i, h] = _dot(q_ref[chunks[i], cols[h]], states[h].astype(bf16)) * qdec_ref[:, cols[h]]
        states = [states[h] * cdec_ref[h] + kv[i, h] for h in heads]
    for h in heads:
        state_ref[h] = states[h]
    r = jnp.concatenate([intra[p] + cross[p] for p in pairs], axis=0)
    mu = jnp.mean(r, axis=-1, keepdims=True)
    rc = r - mu
    var = jnp.mean(rc * rc, axis=-1, keepdims=True)
    rn = rc * lax.rsqrt(var + LN_EPS)
    for n, (i, h) in enumerate(pairs):
        gate = g_ref[chunks[i], cols[h]].astype(jnp.float32)
        normed = rn[n * RET_CHUNK:(n + 1) * RET_CHUNK] * gn_ref[:, cols[h]]
        o_ref[chunks[i], cols[h]] = (gate * (1.0 / (1.0 + jnp.exp(-gate))) * normed).astype(o_ref.dtype)


def _retention(proj, gn_gain):
    c = RET_CHUNK
    log_gamma = jnp.log1p(-jnp.exp2(-(5.0 + jnp.arange(N_RET_HEADS, dtype=jnp.float32))))
    idx = jnp.arange(c, dtype=jnp.float32)
    diff = idx[:, None] - idx[None, :]
    decay = jnp.where(diff >= 0, jnp.exp(jnp.maximum(diff, 0.0)[None] * log_gamma[:, None, None]), 0.0)
    scale = HEAD_DIM ** -0.5
    dmat = decay * scale
    k_decay = jnp.exp((c - 1 - idx)[:, None] * log_gamma[None, :]) * scale
    q_decay = jnp.exp((idx + 1.0)[:, None] * log_gamma[None, :])
    kdec = jnp.repeat(k_decay, HEAD_DIM, axis=1)
    qdec = jnp.repeat(q_decay, HEAD_DIM, axis=1)
    cdec = jnp.broadcast_to(jnp.exp(c * log_gamma)[:, None, None], (N_RET_HEADS, 1, HEAD_DIM))
    gn = gn_gain.reshape(1, SELF_WIDTH).astype(jnp.float32)
    nstep = SEQ // (c * RET_STEP_CHUNKS)
    blk = (c * RET_STEP_CHUNKS, SELF_WIDTH)
    chunk_blk = (c, SELF_WIDTH)

    def section(s):
        return pl.BlockSpec(blk, lambda b, n: (b * nstep + n, s))

    const2 = lambda b, n: (0, 0)
    const3 = lambda b, n: (0, 0, 0)
    return pl.pallas_call(
        _retention_kernel,
        out_shape=jax.ShapeDtypeStruct((N_TOK, SELF_WIDTH), jnp.bfloat16),
        grid=(BATCH, nstep),
        in_specs=[section(0), section(1), section(2), section(3),
                  pl.BlockSpec((N_RET_HEADS, c, c), const3),
                  pl.BlockSpec(chunk_blk, const2), pl.BlockSpec(chunk_blk, const2),
                  pl.BlockSpec((N_RET_HEADS, 1, HEAD_DIM), const3),
                  pl.BlockSpec((1, SELF_WIDTH), const2)],
        out_specs=pl.BlockSpec(blk, lambda b, n: (b * nstep + n, 0)),
        scratch_shapes=[pltpu.VMEM((N_RET_HEADS, HEAD_DIM, HEAD_DIM), jnp.float32)],
        compiler_params=_params("parallel", "arbitrary"),
        name="retention",
    )(proj, proj, proj, proj, dmat, kdec, qdec, cdec, gn)


def _mem_attn_kernel(q_ref, k_ref, v_ref, o_ref):
    scale = HEAD_DIM ** -0.5
    for h in range(MEM_HEADS):
        sl = slice(h * HEAD_DIM, (h + 1) * HEAD_DIM)
        s = _dot_nt(q_ref[:, sl], k_ref[:, sl]) * scale
        e = jnp.exp(s - jnp.max(s, axis=-1, keepdims=True))
        p = e * (1.0 / jnp.sum(e, axis=-1, keepdims=True))
        o_ref[:, sl] = _dot(p.astype(jnp.bfloat16), v_ref[:, sl]).astype(o_ref.dtype)


def _memory_attention(proj, memkv):
    qcol = proj.shape[1] // MEM_WIDTH - 1
    per_b = SEQ // ROW_TILE
    return pl.pallas_call(
        _mem_attn_kernel,
        out_shape=jax.ShapeDtypeStruct((N_TOK, MEM_WIDTH), jnp.bfloat16),
        grid=(BATCH, per_b),
        in_specs=[pl.BlockSpec((ROW_TILE, MEM_WIDTH), lambda b, i: (b * per_b + i, qcol)),
                  pl.BlockSpec((MEM_LEN, MEM_WIDTH), lambda b, i: (b, 0)),
                  pl.BlockSpec((MEM_LEN, MEM_WIDTH), lambda b, i: (b, 1))],
        out_specs=pl.BlockSpec((ROW_TILE, MEM_WIDTH), lambda b, i: (b * per_b + i, 0)),
        compiler_params=_params("parallel", "parallel"),
        name="memory_attn",
    )(proj, memkv, memkv)


def _mix_ln_kernel(so_ref, mo_ref, wt_ref, wb_ref, h_ref, g_ref, b_ref, hf_ref, hb_ref):
    half = ROW_TILE // 2
    for rows in (slice(0, half), slice(half, ROW_TILE)):
        mix = _dot(so_ref[rows, :], wt_ref[...]) + _dot(mo_ref[rows, :], wb_ref[...])
        y = _layer_norm_rows(DEEPNORM_ALPHA * h_ref[rows, :] + mix, g_ref[...], b_ref[...])
        hf_ref[rows, :] = y
        hb_ref[rows, :] = y.astype(hb_ref.dtype)


def _mix_ln(self_out, mem_out, w_mix, h, g, b):
    row = lambda i: (i, 0)
    const = lambda i: (0, 0)
    return pl.pallas_call(
        _mix_ln_kernel,
        out_shape=(jax.ShapeDtypeStruct((N_TOK, D_MODEL), jnp.float32),
                   jax.ShapeDtypeStruct((N_TOK, D_MODEL), jnp.bfloat16)),
        grid=(N_TOK // ROW_TILE,),
        in_specs=[pl.BlockSpec((ROW_TILE, SELF_WIDTH), row),
                  pl.BlockSpec((ROW_TILE, MEM_WIDTH), row),
                  pl.BlockSpec((SELF_WIDTH, D_MODEL), const),
                  pl.BlockSpec((MEM_WIDTH, D_MODEL), lambda i: (SELF_WIDTH // MEM_WIDTH, 0)),
                  pl.BlockSpec((ROW_TILE, D_MODEL), row),
                  pl.BlockSpec((1, D_MODEL), const), pl.BlockSpec((1, D_MODEL), const)],
        out_specs=(pl.BlockSpec((ROW_TILE, D_MODEL), row), pl.BlockSpec((ROW_TILE, D_MODEL), row)),
        compiler_params=_params("parallel"),
        name="mix_ln",
    )(self_out, mem_out, w_mix, w_mix, h, g.reshape(1, D_MODEL), b.reshape(1, D_MODEL))


def _router_kernel(h_ref, whi_ref, wlo_ref, b_ref, e_ref, g_ref, r_ref, cnt_ref, run_ref):
    tm = ROUTER_TM

    @pl.when(pl.program_id(0) == 0)
    def _():
        run_ref[...] = jnp.zeros_like(run_ref)

    parts = [slice(p * tm, (p + 1) * tm) for p in range(ROUTER_PARTS)]
    both = range(ROUTER_PARTS)
    hs = [h_ref[rows, :] for rows in parts]
    h_his = [h.astype(jnp.bfloat16) for h in hs]
    h_los = [(hs[p] - h_his[p].astype(jnp.float32)).astype(jnp.bfloat16) for p in both]
    works = [(_dot(h_his[p], whi_ref[...]) + _dot(h_los[p], whi_ref[...]) + _dot(h_his[p], wlo_ref[...]))
             + b_ref[...] for p in both]
    lane = lax.broadcasted_iota(jnp.int32, (tm, LANES), 1).astype(jnp.float32)
    vals, idxs, hots = [[] for _ in both], [[] for _ in both], [[] for _ in both]
    for _ in range(TOP_K):
        for p in both:
            m = jnp.max(works[p], axis=-1, keepdims=True)
            idx = jnp.min(jnp.where(works[p] == m, lane, float(LANES)), axis=-1, keepdims=True)
            hot = lane == idx
            vals[p].append(m)
            idxs[p].append(idx)
            hots[p].append(hot)
            works[p] = jnp.where(hot, -jnp.inf, works[p])
    row = lax.broadcasted_iota(jnp.int32, (tm, tm), 0)
    colm = lax.broadcasted_iota(jnp.int32, (tm, tm), 1)
    tri = (row > colm).astype(jnp.bfloat16)
    lane_i = lax.broadcasted_iota(jnp.int32, (tm, LANES), 1)
    run = run_ref[...]
    for p in both:
        exps = [jnp.exp(v - vals[p][0]) for v in vals[p]]
        inv = 1.0 / (exps[0] + exps[1] + exps[2] + exps[3])
        cnt = jnp.zeros((tm, LANES), jnp.float32)
        for hot in hots[p]:
            cnt = cnt + hot.astype(jnp.float32)
        before = run + _dot(tri, cnt.astype(jnp.bfloat16))
        e_out = jnp.zeros((tm, LANES), jnp.float32)
        g_out = jnp.zeros((tm, LANES), jnp.float32)
        r_out = jnp.zeros((tm, LANES), jnp.float32)
        for k in range(TOP_K):
            rank = jnp.sum(jnp.where(hots[p][k], before, 0.0), axis=-1, keepdims=True)
            e_out = jnp.where(lane_i == k, idxs[p][k], e_out)
            g_out = jnp.where(lane_i == k, exps[k] * inv, g_out)
            r_out = jnp.where(lane_i == k, rank, r_out)
        e_ref[:, parts[p]] = e_out.T[0:SUBLANES, :].astype(jnp.int32)
        g_ref[parts[p], :] = g_out
        r_ref[:, parts[p]] = r_out.T[0:SUBLANES, :].astype(jnp.int32)
        run = run + jnp.sum(cnt, axis=0, keepdims=True)
    run_ref[...] = run
    cnt_ref[...] = run.astype(jnp.int32)


def _router(h, router_w, router_b):
    w = jnp.zeros((D_MODEL, LANES), jnp.float32).at[:, :N_EXPERTS].set(router_w)
    w_hi = w.astype(jnp.bfloat16)
    w_lo = (w - w_hi.astype(jnp.float32)).astype(jnp.bfloat16)
    b = jnp.full((1, LANES), NEG_INF, jnp.float32).at[0, :N_EXPERTS].set(router_b)
    tm = ROUTER_TM * ROUTER_PARTS
    row = lambda i: (i, 0)
    const = lambda i: (0, 0)
    lanes_out = pl.BlockSpec((tm, LANES), row)
    slots_out = pl.BlockSpec((SUBLANES, tm), lambda i: (0, i))
    e, g, r, cnt = pl.pallas_call(
        _router_kernel,
        out_shape=(jax.ShapeDtypeStruct((SUBLANES, N_TOK), jnp.int32),
                   jax.ShapeDtypeStruct((N_TOK, LANES), jnp.float32),
                   jax.ShapeDtypeStruct((SUBLANES, N_TOK), jnp.int32),
                   jax.ShapeDtypeStruct((1, LANES), jnp.int32)),
        grid=(N_TOK // tm,),
        in_specs=[pl.BlockSpec((tm, D_MODEL), row), pl.BlockSpec((D_MODEL, LANES), const),
                  pl.BlockSpec((D_MODEL, LANES), const), pl.BlockSpec((1, LANES), const)],
        out_specs=(slots_out, lanes_out, slots_out, pl.BlockSpec((1, LANES), const)),
        scratch_shapes=[pltpu.VMEM((1, LANES), jnp.float32)],
        compiler_params=_params("arbitrary"),
        name="router",
    )(h, w_hi, w_lo, b)
    return e[:TOP_K], g, r[:TOP_K], cnt[0, :N_EXPERTS]


def _route_tables(top_e, rank, cnt):
    ntile = (cnt + MOE_TILE - 1) // MOE_TILE
    tile_end = jnp.cumsum(ntile)
    tile_base = tile_end - ntile
    experts = jnp.arange(N_EXPERTS, dtype=jnp.int32)[:, None, None]
    base = jnp.sum(jnp.where(top_e[None] == experts, tile_base[:, None, None], 0), axis=0)
    pos = (base * MOE_TILE + rank).astype(jnp.int32).reshape(-1)
    n_tiles = tile_end[-1:].astype(jnp.int32)
    w = jnp.minimum(jnp.arange(MOE_TILES, dtype=jnp.int32), n_tiles[0] - 1)
    tile_e = jnp.minimum(jnp.searchsorted(tile_end, w, side="right"), N_EXPERTS - 1).astype(jnp.int32)
    rows = jnp.clip(cnt[tile_e] - (w - tile_base[tile_e]) * MOE_TILE, 0, MOE_TILE)
    nsub = ((rows + MOE_SUB - 1) // MOE_SUB).astype(jnp.int32)
    pad_start = (tile_base * MOE_TILE + cnt).astype(jnp.int32)
    pad_n = ((-cnt) % MOE_SUB).astype(jnp.int32)
    return pos, n_tiles, tile_e, nsub, pad_start, pad_n


def _row_copy(src_vmem, src_row, dst_hbm, dst_row, sem):
    return pltpu.make_async_copy(src_vmem.at[pl.ds(src_row, 1)], dst_hbm.at[pl.ds(dst_row, 1)], sem)


def _pack_bf16_pairs(x):
    half = x.shape[1] // 2
    lo = lax.bitcast_convert_type(x[:, :half].astype(jnp.float32), jnp.uint32)
    hi = lax.bitcast_convert_type(x[:, half:].astype(jnp.float32), jnp.uint32)
    return (lo >> 16) | (hi & jnp.uint32(0xFFFF0000))


def _unpack_bf16_pairs(words):
    lo = lax.bitcast_convert_type(words << 16, jnp.float32)
    hi = lax.bitcast_convert_type(words & jnp.uint32(0xFFFF0000), jnp.float32)
    return lo.astype(jnp.bfloat16), hi.astype(jnp.bfloat16)


def _dispatch_kernel(pos_ref, pad_start_ref, pad_n_ref, hb_ref, xs_hbm, h_ref, sem):
    step = pl.program_id(0)
    base = step * DISPATCH_TB
    h_ref[...] = _pack_bf16_pairs(hb_ref[...])

    for r in range(DISPATCH_TB):
        for k in range(TOP_K):
            _row_copy(h_ref, r, xs_hbm, pos_ref[k * N_TOK + base + r], sem).start(priority=k % 2)

    @pl.when(step == 0)
    def _():
        def per_expert(e, carry):
            start = pad_start_ref[e]
            n = pad_n_ref[e]

            def fill(i, c):
                _row_copy(h_ref, 0, xs_hbm, start + i, sem).start()
                return c

            def drain(_, c):
                _row_copy(h_ref, 0, xs_hbm, 0, sem).wait()
                return c

            lax.fori_loop(0, n, fill, 0)
            lax.fori_loop(0, n, drain, 0)
            return carry

        lax.fori_loop(0, N_EXPERTS, per_expert, 0)

    for _ in range(TOP_K):
        pltpu.make_async_copy(h_ref, xs_hbm.at[pl.ds(0, DISPATCH_TB)], sem).wait()


def _dispatch(hb, pos, pad_start, pad_n):
    return pl.pallas_call(
        _dispatch_kernel,
        out_shape=jax.ShapeDtypeStruct((MOE_ROWS, D_MODEL // 2), jnp.uint32),
        grid_spec=pltpu.PrefetchScalarGridSpec(
            num_scalar_prefetch=3,
            grid=(N_TOK // DISPATCH_TB,),
            in_specs=[pl.BlockSpec((DISPATCH_TB, D_MODEL), lambda i, p, s, n: (i, 0))],
            out_specs=pl.BlockSpec(memory_space=pl.ANY),
            scratch_shapes=[pltpu.VMEM((DISPATCH_TB, D_MODEL // 2), jnp.uint32),
                            pltpu.SemaphoreType.DMA(())],
        ),
        compiler_params=_params("arbitrary"),
        name="moe_dispatch",
    )(pos, pad_start, pad_n, hb)


def _moe_weight_map(layer, first, hidden_axis):
    def index_map(w, j, tile_e, nsub):
        return (layer, tile_e[w], first + j, 0) if hidden_axis == 2 else (layer, tile_e[w], 0, first + j)

    return index_map


def _moe_rows(nsub, accumulate):
    fast = nsub < 0
    for s in MOE_FAST_SUBS:
        fast = jnp.logical_or(fast, nsub == s)
        pl.when(nsub == s)(functools.partial(accumulate, slice(0, s * MOE_SUB)))

    @pl.when(jnp.logical_and(nsub > 0, jnp.logical_not(fast)))
    def _():
        def one(i, carry):
            accumulate(pl.ds(pl.multiple_of(i * MOE_SUB, MOE_SUB), MOE_SUB))
            return carry

        lax.fori_loop(0, nsub, one, 0)


def _moe_expert_kernel(exp_ref, nsub_ref, x_ref, wg_ref, wl_ref, bg_ref, bl_ref, wo_ref, bo_ref, o_ref):
    first = pl.program_id(1) == 0
    nsub = nsub_ref[pl.program_id(0)]
    half = D_MODEL // 2

    @pl.when(first)
    def _():
        o_ref[...] = jnp.broadcast_to(bo_ref[...], (MOE_TILE, D_MODEL))

    def up(lo, hi, w_ref, b_ref):
        return (_dot(lo, w_ref[0:half, :].astype(jnp.bfloat16))
                + _dot(hi, w_ref[half:D_MODEL, :].astype(jnp.bfloat16)) + b_ref[...])

    def accumulate(rows):
        lo, hi = _unpack_bf16_pairs(x_ref[rows, :])
        gate = up(lo, hi, wg_ref, bg_ref)
        lin = up(lo, hi, wl_ref, bl_ref)
        gate = jnp.minimum(gate, SWIGLU_LIMIT)
        lin = jnp.clip(lin, -SWIGLU_LIMIT, SWIGLU_LIMIT)
        act = (gate * (1.0 / (1.0 + jnp.exp(-SWIGLU_ALPHA * gate))) * (lin + 1.0)).astype(jnp.bfloat16)
        for c in range(D_MODEL // MOE_DOWN_TN):
            cols = slice(c * MOE_DOWN_TN, (c + 1) * MOE_DOWN_TN)
            o_ref[rows, cols] = o_ref[rows, cols] + _dot(act, wo_ref[:, cols].astype(jnp.bfloat16))

    _moe_rows(nsub, accumulate)


def _moe_experts(xs, w_in, b_in, w_out, b_out, layer, n_tiles, tile_e, nsub):
    nj = D_EXPERT // MOE_TF
    b_in4 = b_in.reshape(DEPTH, N_EXPERTS, 1, 2 * D_EXPERT)
    b_out4 = b_out.reshape(DEPTH, N_EXPERTS, 1, D_MODEL)
    wblk = (None, None, D_MODEL, MOE_TF)
    bblk = (None, None, 1, MOE_TF)
    tile_map = lambda w, j, e, n: (w, 0)
    return pl.pallas_call(
        _moe_expert_kernel,
        out_shape=jax.ShapeDtypeStruct((MOE_ROWS, D_MODEL), jnp.float32),
        grid_spec=pltpu.PrefetchScalarGridSpec(
            num_scalar_prefetch=2,
            grid=(n_tiles[0], nj),
            in_specs=[pl.BlockSpec((MOE_TILE, D_MODEL // 2), tile_map),
                      pl.BlockSpec(wblk, _moe_weight_map(layer, 0, 3)),
                      pl.BlockSpec(wblk, _moe_weight_map(layer, nj, 3)),
                      pl.BlockSpec(bblk, _moe_weight_map(layer, 0, 3)),
                      pl.BlockSpec(bblk, _moe_weight_map(layer, nj, 3)),
                      pl.BlockSpec((None, None, MOE_TF, D_MODEL), _moe_weight_map(layer, 0, 2)),
                      pl.BlockSpec((None, None, 1, D_MODEL), lambda w, j, e, n: (layer, e[w], 0, 0))],
            out_specs=pl.BlockSpec((MOE_TILE, D_MODEL), tile_map),
        ),
        compiler_params=_params("arbitrary", "arbitrary", vmem_limit=MOE_VMEM_LIMIT),
        name="moe_experts",
    )(tile_e, nsub, xs, w_in, w_in, b_in4, b_in4, w_out, b_out4)


def _combine_ln_kernel(pos_ref, y_hbm, gates_ref, h_ref, g_ref, b_ref, hf_ref, hb_ref, buf0, buf1, sems):
    step = pl.program_id(0)
    last = pl.num_programs(0) - 1

    def row_copy(tile, r, k, buf, sem):
        src = y_hbm.at[pl.ds(pos_ref[k * N_TOK + tile * COMBINE_TB + r], 1)]
        return pltpu.make_async_copy(src, buf.at[k, pl.ds(r, 1)], sem)

    def wait_tile(buf, sem):
        for k in range(TOP_K):
            pltpu.make_async_copy(y_hbm.at[pl.ds(0, COMBINE_TB)], buf.at[k], sem).wait()

    @pl.when(step == 0)
    def _():
        def issue(r, carry):
            for k in range(TOP_K):
                row_copy(0, r, k, buf0, sems.at[0]).start()
            return carry

        lax.fori_loop(0, COMBINE_TB, issue, 0, unroll=8)

    def run(cur, cur_sem, nxt, nxt_sem):
        nxt_tile = jnp.minimum(step + 1, last)
        wait_tile(cur, cur_sem)
        for r in range(COMBINE_TB):
            for k in range(TOP_K):
                row_copy(nxt_tile, r, k, nxt, nxt_sem).start(priority=k % 2)
        gates = gates_ref[...]
        ffn = gates[:, 0:1] * cur[0]
        for k in range(1, TOP_K):
            ffn = ffn + gates[:, k:k + 1] * cur[k]
        y = _layer_norm_rows(DEEPNORM_ALPHA * h_ref[...] + ffn, g_ref[...], b_ref[...])
        hf_ref[...] = y
        hb_ref[...] = y.astype(hb_ref.dtype)

        @pl.when(step == last)
        def _():
            wait_tile(nxt, nxt_sem)

    pl.when(step % 2 == 0)(functools.partial(run, buf0, sems.at[0], buf1, sems.at[1]))
    pl.when(step % 2 == 1)(functools.partial(run, buf1, sems.at[1], buf0, sems.at[0]))


def _combine_ln(y, pos, gates, h, g, b):
    tb = COMBINE_TB
    row = lambda i, p: (i, 0)
    const = lambda i, p: (0, 0)
    return pl.pallas_call(
        _combine_ln_kernel,
        out_shape=(jax.ShapeDtypeStruct((N_TOK, D_MODEL), jnp.float32),
                   jax.ShapeDtypeStruct((N_TOK, D_MODEL), jnp.bfloat16)),
        grid_spec=pltpu.PrefetchScalarGridSpec(
            num_scalar_prefetch=1,
            grid=(N_TOK // tb,),
            in_specs=[pl.BlockSpec(memory_space=pl.ANY),
                      pl.BlockSpec((tb, LANES), row),
                      pl.BlockSpec((tb, D_MODEL), row),
                      pl.BlockSpec((1, D_MODEL), const), pl.BlockSpec((1, D_MODEL), const)],
            out_specs=(pl.BlockSpec((tb, D_MODEL), row), pl.BlockSpec((tb, D_MODEL), row)),
            scratch_shapes=[pltpu.VMEM((TOP_K, tb, D_MODEL), jnp.float32),
                            pltpu.VMEM((TOP_K, tb, D_MODEL), jnp.float32),
                            pltpu.SemaphoreType.DMA((2,))],
        ),
        compiler_params=_params("arbitrary"),
        name="moe_combine_ln",
    )(pos, y, gates, h, g.reshape(1, D_MODEL), b.reshape(1, D_MODEL))


def _moe(h, hb, layer, router_w, router_b, w_in, b_in, w_out, b_out, ln_g, ln_b):
    top_e, gates, rank, cnt = _router(h, router_w[layer], router_b[layer])
    pos, n_tiles, tile_e, nsub, pad_start, pad_n = _route_tables(top_e, rank, cnt)
    xs = _dispatch(hb, pos, pad_start, pad_n)
    y = _moe_experts(xs, w_in, b_in, w_out, b_out, layer, n_tiles, tile_e, nsub)
    return _combine_ln(y, pos, gates, h, ln_g[layer], ln_b[layer])


def kernel(x, mem, w_in_dil, w_in_ret, ret_gn_g, w_mem_kv, w_mix_out, ln_mix_g, ln_mix_b, router_w, router_b, moe_w_in, moe_b_in, moe_w_out, moe_b_out, ln_ffn_g, ln_ffn_b):
    bf16 = jnp.bfloat16
    h = x.reshape(N_TOK, D_MODEL)
    hb = h
    memb = mem.reshape(BATCH * MEM_LEN, D_MODEL)
    for layer in range(DEPTH):
        slot = layer // 2
        if layer % 2 == 0:
            kb = SELF_WIDTH // GROUP_WIDTH
            proj = _matmul(hb, w_in_dil, slot, bf16, col_tiles=(0, kb, kb + 1))
            self_out = _dilated_attention(hb, w_in_dil, slot, proj)
        else:
            proj = _matmul(hb, w_in_ret, slot, bf16)
            self_out = _retention(proj, ret_gn_g[slot])
        memkv = _matmul(memb, w_mem_kv, layer, bf16)
        mem_out = _memory_attention(proj, memkv)
        h, hb = _mix_ln(self_out, mem_out, w_mix_out[layer].astype(bf16), h,
                        ln_mix_g[layer], ln_mix_b[layer])
        h, hb = _moe(h, hb, layer, router_w, router_b, moe_w_in, moe_b_in, moe_w_out, moe_b_out,
                     ln_ffn_g, ln_ffn_b)
    return h.reshape(BATCH, SEQ, D_MODEL)
```

```python
import functools
import math

import jax
import jax.numpy as jnp
from jax import lax
from jax.experimental import pallas as pl
from jax.experimental.pallas import tpu as pltpu

D_MODEL = 2048
BATCH = 2
SEQ = 4096
DEPTH = 2
HEAD_DIM = 128
DILATED_GROUPS = ((128, 1), (512, 4), (2048, 16))
HEADS_PER_GROUP = 4
N_SELF_HEADS = len(DILATED_GROUPS) * HEADS_PER_GROUP
N_RET_HEADS = 12
SELF_WIDTH = N_SELF_HEADS * HEAD_DIM
MEM_HEADS = 4
MEM_LEN = 256
MEM_WIDTH = MEM_HEADS * HEAD_DIM
RET_CHUNK = 128
N_EXPERTS = 32
TOP_K = 4
D_EXPERT = D_MODEL
SWIGLU_ALPHA = 1.702
SWIGLU_LIMIT = 7.0
DEEPNORM_ALPHA = (2 * DEPTH) ** 0.25
LN_EPS = 1e-5
NEG_INF = -1e30

N_TOK = BATCH * SEQ
GROUP_WIDTH = HEADS_PER_GROUP * HEAD_DIM
SELF_QKV = 3 * GROUP_WIDTH
ATTN_STEPS = 128
ATTN_QB = 2
RET_STEP_CHUNKS = 2

LANES = 128
SUBLANES = 8
VMEM_LIMIT = 56 * 1024 * 1024

MM_TM = 1024
MM_TN = 512
ROW_TILE = 512
COMBINE_ROWS = 512
ROUTER_TM = 256
ROUTER_PARTS = 4

MOE_SUB = 64
MOE_TILE = 1152
MOE_SUBS = MOE_TILE // MOE_SUB
MOE_FAST_SUBS = (15, 16, 17, 18)
MOE_TF = 512
MOE_DOWN_TN = 512
MOE_VMEM_LIMIT = 60 * 1024 * 1024
MOE_TILES = (N_TOK * TOP_K) // MOE_TILE + N_EXPERTS
MOE_ROWS = MOE_TILES * MOE_TILE
DISPATCH_TB = 512
COMBINE_TB = 256


def _alibi_slopes(n):
    def pow2(m):
        start = 2.0 ** (-8.0 / m)
        return [start ** (i + 1) for i in range(m)]

    if math.log2(n).is_integer():
        s = pow2(n)
    else:
        c = 2 ** math.floor(math.log2(n))
        s = pow2(c) + pow2(2 * c)[0::2][: n - c]
    return sorted(s, reverse=True)


def _params(*sem, vmem_limit=VMEM_LIMIT):
    return pltpu.CompilerParams(dimension_semantics=sem, vmem_limit_bytes=vmem_limit)


def _layer_norm_rows(z, g, b):
    mu = jnp.mean(z, axis=-1, keepdims=True)
    zc = z - mu
    var = jnp.mean(zc * zc, axis=-1, keepdims=True)
    return zc * lax.rsqrt(var + LN_EPS) * g + b


def _dot_nt(a, b):
    return lax.dot_general(a, b, (((1,), (1,)), ((), ())), preferred_element_type=jnp.float32)


def _dot(a, b):
    return jnp.dot(a, b, preferred_element_type=jnp.float32)


def _mm_kernel(x_ref, w_ref, o_ref):
    o_ref[...] = _dot(x_ref[...].astype(jnp.bfloat16), w_ref[...].astype(jnp.bfloat16)).astype(o_ref.dtype)


def _matmul(x, w, layer, out_dtype, col_tiles=None):
    m, k = x.shape
    first, stride, count = col_tiles if col_tiles else (0, 1, w.shape[2] // MM_TN)
    tm = min(MM_TM * (4 // x.dtype.itemsize), m)
    return pl.pallas_call(
        _mm_kernel,
        out_shape=jax.ShapeDtypeStruct((m, count * MM_TN), out_dtype),
        grid=(m // tm, count),
        in_specs=[pl.BlockSpec((tm, k), lambda i, j: (i, 0)),
                  pl.BlockSpec((None, k, MM_TN), lambda i, j: (layer, 0, first + stride * j))],
        out_specs=pl.BlockSpec((tm, MM_TN), lambda i, j: (i, j)),
        compiler_params=_params("parallel", "parallel"),
        name="dense_matmul",
    )(x, w)


def _proj_residue_kernel(x_ref, wq_ref, wk_ref, wv_ref, o_ref, scr_ref, *, dilation):
    x = x_ref[...].astype(jnp.bfloat16)
    rows = MM_TM // dilation
    for section, w_ref in enumerate((wq_ref, wk_ref, wv_ref)):
        res = _dot(x, w_ref[...].astype(jnp.bfloat16))
        for c in range(GROUP_WIDTH // LANES):
            scr_ref[c] = res[:, c * LANES:(c + 1) * LANES]
        for r in range(dilation):
            for c in range(GROUP_WIDTH // LANES):
                col = r * SELF_QKV + section * GROUP_WIDTH + c * LANES
                o_ref[:, col:col + LANES] = scr_ref[c, pl.ds(r, rows, stride=dilation), :].astype(o_ref.dtype)


def _proj_residue_major(x, w, layer, group):
    _, dilation = DILATED_GROUPS[group]
    k = x.shape[1]
    kb = SELF_WIDTH // GROUP_WIDTH

    def wspec(section):
        return pl.BlockSpec((None, k, GROUP_WIDTH), lambda i: (layer, 0, section * kb + group))

    return pl.pallas_call(
        functools.partial(_proj_residue_kernel, dilation=dilation),
        out_shape=jax.ShapeDtypeStruct((N_TOK // dilation, dilation * SELF_QKV), jnp.bfloat16),
        grid=(N_TOK // MM_TM,),
        in_specs=[pl.BlockSpec((MM_TM, k), lambda i: (i, 0)), wspec(0), wspec(1), wspec(2)],
        out_specs=pl.BlockSpec((MM_TM // dilation, dilation * SELF_QKV), lambda i: (i, 0)),
        scratch_shapes=[pltpu.VMEM((GROUP_WIDTH // LANES, MM_TM, LANES), jnp.float32)],
        compiler_params=_params("parallel"),
        name=f"proj_residue_g{group}",
    )(x, w, w, w)


def _dil_attn_kernel(q_ref, kp_ref, kc_ref, vp_ref, vc_ref, o_ref, lse_ref, *, slopes, dilation):
    n = pl.program_id(2)
    heads = HEADS_PER_GROUP
    rows = ATTN_QB * heads * ATTN_STEPS
    head_slices = [slice(h * HEAD_DIM, (h + 1) * HEAD_DIM) for h in range(heads)]
    blocks = [slice(s * ATTN_STEPS, (s + 1) * ATTN_STEPS) for s in range(ATTN_QB)]
    row = lax.broadcasted_iota(jnp.int32, (rows, ATTN_STEPS), 0)
    kj = lax.broadcasted_iota(jnp.int32, (rows, ATTN_STEPS), 1)
    diff = (row % ATTN_STEPS) - kj
    valid_c = diff >= 0
    has_prev = jnp.logical_or(row >= heads * ATTN_STEPS, n > 0)
    valid_p = jnp.logical_and(diff <= 0, has_prev)
    slope = jnp.full((rows, ATTN_STEPS), slopes[0], jnp.float32)
    head_of_row = (row // ATTN_STEPS) % heads
    for h in range(1, heads):
        slope = jnp.where(head_of_row == h, slopes[h], slope)
    bias_c = slope * (diff * dilation).astype(jnp.float32)
    bias_p = slope * ((diff + ATTN_STEPS) * dilation).astype(jnp.float32)
    scale = HEAD_DIM ** -0.5

    def prev_cur(cur_ref, prev_ref, s, sl):
        prev = prev_ref[:, sl] if s == 0 else cur_ref[blocks[s - 1], sl]
        return prev, cur_ref[blocks[s], sl]

    pairs = [(s, sl) for s in range(ATTN_QB) for sl in head_slices]
    s_c = jnp.concatenate([_dot_nt(q_ref[blocks[s], sl], prev_cur(kc_ref, kp_ref, s, sl)[1])
                           for s, sl in pairs], axis=0)
    s_p = jnp.concatenate([_dot_nt(q_ref[blocks[s], sl], prev_cur(kc_ref, kp_ref, s, sl)[0])
                           for s, sl in pairs], axis=0)
    s_c = jnp.where(valid_c, s_c * scale - bias_c, NEG_INF)
    s_p = jnp.where(valid_p, s_p * scale - bias_p, NEG_INF)
    m = jnp.max(jnp.maximum(s_c, s_p), axis=-1, keepdims=True)
    e_c = jnp.exp(s_c - m)
    e_p = jnp.exp(s_p - m)
    l = jnp.sum(e_c + e_p, axis=-1, keepdims=True)
    inv_l = 1.0 / l
    p_c = (e_c * inv_l).astype(jnp.bfloat16)
    p_p = (e_p * inv_l).astype(jnp.bfloat16)
    lse = m + jnp.log(l)
    for i, (s, sl) in enumerate(pairs):
        chain = slice(i * ATTN_STEPS, (i + 1) * ATTN_STEPS)
        v_prev, v_cur = prev_cur(vc_ref, vp_ref, s, sl)
        o_ref[blocks[s], sl] = _dot(p_c[chain], v_cur) + _dot(p_p[chain], v_prev)
        lse_ref[blocks[s], sl] = jnp.broadcast_to(lse[chain], (ATTN_STEPS, HEAD_DIM))


def _dilated_group(src, group):
    _, dilation = DILATED_GROUPS[group]
    length = SEQ // dilation
    span = ATTN_QB * ATTN_STEPS
    cb = src.shape[1] // dilation // GROUP_WIDTH
    view = src.reshape(BATCH, length, src.shape[1])
    blk = (None, span, GROUP_WIDTH)
    prev_blk = (None, ATTN_STEPS, GROUP_WIDTH)

    def col(section):
        return lambda b, r, n: (b, n, r * cb + section)

    def col_prev(section):
        return lambda b, r, n: (b, jnp.maximum(n * ATTN_QB - 1, 0), r * cb + section)

    slopes = tuple(_alibi_slopes(N_SELF_HEADS)[group * HEADS_PER_GROUP:(group + 1) * HEADS_PER_GROUP])
    out_shape = jax.ShapeDtypeStruct((BATCH, length, dilation * GROUP_WIDTH), jnp.float32)
    out_spec = pl.BlockSpec(blk, lambda b, r, n: (b, n, r))
    o, lse = pl.pallas_call(
        functools.partial(_dil_attn_kernel, slopes=slopes, dilation=dilation),
        out_shape=(out_shape, out_shape),
        grid=(BATCH, dilation, length // span),
        in_specs=[pl.BlockSpec(blk, col(0)),
                  pl.BlockSpec(prev_blk, col_prev(1)), pl.BlockSpec(blk, col(1)),
                  pl.BlockSpec(prev_blk, col_prev(2)), pl.BlockSpec(blk, col(2))],
        out_specs=(out_spec, out_spec),
        compiler_params=_params("parallel", "parallel", "arbitrary"),
        name=f"dilated_attn_g{group}",
    )(view, view, view, view, view)
    return o, lse


def _dil_combine_kernel(*refs):
    ngroups = len(DILATED_GROUPS)
    o_refs, l_refs, out_ref = refs[:ngroups], refs[ngroups:2 * ngroups], refs[2 * ngroups]
    scratch = list(refs[2 * ngroups + 1:])
    os, ls = [], []
    for g, (_, dilation) in enumerate(DILATED_GROUPS):
        if dilation == 1:
            os.append(o_refs[g][...])
            ls.append(l_refs[g][...])
            continue
        rows = COMBINE_ROWS // dilation
        planes = GROUP_WIDTH // LANES
        o_scr, l_scr = scratch.pop(0), scratch.pop(0)
        for r in range(dilation):
            for c in range(planes):
                cols = slice(r * GROUP_WIDTH + c * LANES, r * GROUP_WIDTH + (c + 1) * LANES)
                o_scr[c, pl.ds(r, rows, stride=dilation), :] = o_refs[g][:, cols]
                l_scr[c, pl.ds(r, rows, stride=dilation), :] = l_refs[g][:, cols]
        os.append(jnp.concatenate([o_scr[c] for c in range(planes)], axis=1))
        ls.append(jnp.concatenate([l_scr[c] for c in range(planes)], axis=1))
    m = functools.reduce(jnp.maximum, ls)
    es = [jnp.exp(l - m) for l in ls]
    inv = 1.0 / functools.reduce(lambda a, b: a + b, es)
    for g in range(ngroups):
        out_ref[:, g * GROUP_WIDTH:(g + 1) * GROUP_WIDTH] = (os[g] * (es[g] * inv)).astype(out_ref.dtype)


def _dilated_attention(x, w_in, layer, proj):
    sources = [proj] + [_proj_residue_major(x, w_in, layer, g) for g in range(1, len(DILATED_GROUPS))]
    outs, lses = zip(*[_dilated_group(src, g) for g, src in enumerate(sources)])

    def spec(dilation):
        return pl.BlockSpec((COMBINE_ROWS // dilation, dilation * GROUP_WIDTH), lambda i: (i, 0))

    def flat(a):
        return a.reshape(a.shape[0] * a.shape[1], a.shape[2])

    specs = [spec(d) for _, d in DILATED_GROUPS]
    scratch = [pltpu.VMEM((GROUP_WIDTH // LANES, COMBINE_ROWS, LANES), jnp.float32)
               for _, d in DILATED_GROUPS if d > 1 for _ in range(2)]
    return pl.pallas_call(
        _dil_combine_kernel,
        out_shape=jax.ShapeDtypeStruct((N_TOK, SELF_WIDTH), jnp.bfloat16),
        grid=(N_TOK // COMBINE_ROWS,),
        in_specs=specs + specs,
        out_specs=pl.BlockSpec((COMBINE_ROWS, SELF_WIDTH), lambda i: (i, 0)),
        scratch_shapes=scratch,
        compiler_params=_params("parallel"),
        name="dilated_combine",
    )(*[flat(o) for o in outs], *[flat(l) for l in lses])


def _retention_kernel(q_ref, k_ref, v_ref, g_ref, dmat_ref, kdec_ref, qdec_ref, cdec_ref, gn_ref,
                      o_ref, state_ref):
    @pl.when(pl.program_id(1) == 0)
    def _():
        state_ref[...] = jnp.zeros_like(state_ref)

    heads = range(N_RET_HEADS)
    cols = [slice(h * HEAD_DIM, (h + 1) * HEAD_DIM) for h in heads]
    chunks = [slice(i * RET_CHUNK, (i + 1) * RET_CHUNK) for i in range(RET_STEP_CHUNKS)]
    pairs = [(i, h) for i in range(RET_STEP_CHUNKS) for h in heads]
    bf16 = jnp.bfloat16
    scores = {(i, h): _dot_nt(q_ref[chunks[i], cols[h]], k_ref[chunks[i], cols[h]]) * dmat_ref[h]
              for i, h in pairs}
    intra = {(i, h): _dot(scores[i, h].astype(bf16), v_ref[chunks[i], cols[h]]) for i, h in pairs}
    kw = {(i, h): (k_ref[chunks[i], cols[h]].astype(jnp.float32) * kdec_ref[:, cols[h]]).astype(bf16)
          for i, h in pairs}
    kv = {(i, h): lax.dot_general(kw[i, h], v_ref[chunks[i], cols[h]], (((0,), (0,)), ((), ())),
                                  preferred_element_type=jnp.float32) for i, h in pairs}
    states = [state_ref[h] for h in heads]
    cross = {}
    for i in range(RET_STEP_CHUNKS):
        for h in heads:
            cross[i, h] = _dot(q_ref[chunks[i], cols[h]], states[h].astype(bf16)) * qdec_ref[:, cols[h]]
        states = [states[h] * cdec_ref[h] + kv[i, h] for h in heads]
    for h in heads:
        state_ref[h] = states[h]
    r = jnp.concatenate([intra[p] + cross[p] for p in pairs], axis=0)
    mu = jnp.mean(r, axis=-1, keepdims=True)
    rc = r - mu
    var = jnp.mean(rc * rc, axis=-1, keepdims=True)
    rn = rc * lax.rsqrt(var + LN_EPS)
    for n, (i, h) in enumerate(pairs):
        gate = g_ref[chunks[i], cols[h]].astype(jnp.float32)
        normed = rn[n * RET_CHUNK:(n + 1) * RET_CHUNK] * gn_ref[:, cols[h]]
        o_ref[chunks[i], cols[h]] = (gate * (1.0 / (1.0 + jnp.exp(-gate))) * normed).astype(o_ref.dtype)


def _retention(proj, gn_gain):
    c = RET_CHUNK
    log_gamma = jnp.log1p(-jnp.exp2(-(5.0 + jnp.arange(N_RET_HEADS, dtype=jnp.float32))))
    idx = jnp.arange(c, dtype=jnp.float32)
    diff = idx[:, None] - idx[None, :]
    decay = jnp.where(diff >= 0, jnp.exp(jnp.maximum(diff, 0.0)[None] * log_gamma[:, None, None]), 0.0)
    scale = HEAD_DIM ** -0.5
    dmat = decay * scale
    k_decay = jnp.exp((c - 1 - idx)[:, None] * log_gamma[None, :]) * scale
    q_decay = jnp.exp((idx + 1.0)[:, None] * log_gamma[None, :])
    kdec = jnp.repeat(k_decay, HEAD_DIM, axis=1)
    qdec = jnp.repeat(q_decay, HEAD_DIM, axis=1)
    cdec = jnp.broadcast_to(jnp.exp(c * log_gamma)[:, None, None], (N_RET_HEADS, 1, HEAD_DIM))
    gn = gn_gain.reshape(1, SELF_WIDTH).astype(jnp.float32)
    nstep = SEQ // (c * RET_STEP_CHUNKS)
    blk = (c * RET_STEP_CHUNKS, SELF_WIDTH)
    chunk_blk = (c, SELF_WIDTH)

    def section(s):
        return pl.BlockSpec(blk, lambda b, n: (b * nstep + n, s))

    const2 = lambda b, n: (0, 0)
    const3 = lambda b, n: (0, 0, 0)
    return pl.pallas_call(
        _retention_kernel,
        out_shape=jax.ShapeDtypeStruct((N_TOK, SELF_WIDTH), jnp.bfloat16),
        grid=(BATCH, nstep),
        in_specs=[section(0), section(1), section(2), section(3),
                  pl.BlockSpec((N_RET_HEADS, c, c), const3),
                  pl.BlockSpec(chunk_blk, const2), pl.BlockSpec(chunk_blk, const2),
                  pl.BlockSpec((N_RET_HEADS, 1, HEAD_DIM), const3),
                  pl.BlockSpec((1, SELF_WIDTH), const2)],
        out_specs=pl.BlockSpec(blk, lambda b, n: (b * nstep + n, 0)),
        scratch_shapes=[pltpu.VMEM((N_RET_HEADS, HEAD_DIM, HEAD_DIM), jnp.float32)],
        compiler_params=_params("parallel", "arbitrary"),
        name="retention",
    )(proj, proj, proj, proj, dmat, kdec, qdec, cdec, gn)


def _mem_attn_kernel(q_ref, k_ref, v_ref, o_ref):
    scale = HEAD_DIM ** -0.5
    cols = [slice(h * HEAD_DIM, (h + 1) * HEAD_DIM) for h in range(MEM_HEADS)]
    s = jnp.concatenate([_dot_nt(q_ref[:, sl], k_ref[:, sl]) for sl in cols], axis=0) * scale
    e = jnp.exp(s - jnp.max(s, axis=-1, keepdims=True))
    p = (e * (1.0 / jnp.sum(e, axis=-1, keepdims=True))).astype(jnp.bfloat16)
    for h, sl in enumerate(cols):
        o_ref[:, sl] = _dot(p[h * ROW_TILE:(h + 1) * ROW_TILE], v_ref[:, sl]).astype(o_ref.dtype)


def _memory_attention(proj, memkv):
    qcol = proj.shape[1] // MEM_WIDTH - 1
    per_b = SEQ // ROW_TILE
    return pl.pallas_call(
        _mem_attn_kernel,
        out_shape=jax.ShapeDtypeStruct((N_TOK, MEM_WIDTH), jnp.bfloat16),
        grid=(BATCH, per_b),
        in_specs=[pl.BlockSpec((ROW_TILE, MEM_WIDTH), lambda b, i: (b * per_b + i, qcol)),
                  pl.BlockSpec((MEM_LEN, MEM_WIDTH), lambda b, i: (b, 0)),
                  pl.BlockSpec((MEM_LEN, MEM_WIDTH), lambda b, i: (b, 1))],
        out_specs=pl.BlockSpec((ROW_TILE, MEM_WIDTH), lambda b, i: (b * per_b + i, 0)),
        compiler_params=_params("parallel", "parallel"),
        name="memory_attn",
    )(proj, memkv, memkv)


def _mix_ln_kernel(so_ref, mo_ref, wt_ref, wb_ref, h_ref, g_ref, b_ref, hf_ref, hb_ref):
    half = ROW_TILE // 2
    for rows in (slice(0, half), slice(half, ROW_TILE)):
        mix = _dot(so_ref[rows, :], wt_ref[...]) + _dot(mo_ref[rows, :], wb_ref[...])
        y = _layer_norm_rows(DEEPNORM_ALPHA * h_ref[rows, :] + mix, g_ref[...], b_ref[...])
        hf_ref[rows, :] = y
        hb_ref[rows, :] = y.astype(hb_ref.dtype)


def _mix_ln(self_out, mem_out, w_mix, h, g, b):
    row = lambda i: (i, 0)
    const = lambda i: (0, 0)
    return pl.pallas_call(
        _mix_ln_kernel,
        out_shape=(jax.ShapeDtypeStruct((N_TOK, D_MODEL), jnp.float32),
                   jax.ShapeDtypeStruct((N_TOK, D_MODEL), jnp.bfloat16)),
        grid=(N_TOK // ROW_TILE,),
        in_specs=[pl.BlockSpec((ROW_TILE, SELF_WIDTH), row),
                  pl.BlockSpec((ROW_TILE, MEM_WIDTH), row),
                  pl.BlockSpec((SELF_WIDTH, D_MODEL), const),
                  pl.BlockSpec((MEM_WIDTH, D_MODEL), lambda i: (SELF_WIDTH // MEM_WIDTH, 0)),
                  pl.BlockSpec((ROW_TILE, D_MODEL), row),
                  pl.BlockSpec((1, D_MODEL), const), pl.BlockSpec((1, D_MODEL), const)],
        out_specs=(pl.BlockSpec((ROW_TILE, D_MODEL), row), pl.BlockSpec((ROW_TILE, D_MODEL), row)),
        compiler_params=_params("parallel"),
        name="mix_ln",
    )(self_out, mem_out, w_mix, w_mix, h, g.reshape(1, D_MODEL), b.reshape(1, D_MODEL))


def _router_kernel(h_ref, whi_ref, wlo_ref, b_ref, e_ref, g_ref, r_ref, cnt_ref, run_ref):
    tm = ROUTER_TM

    @pl.when(pl.program_id(0) == 0)
    def _():
        run_ref[...] = jnp.zeros_like(run_ref)

    parts = [slice(p * tm, (p + 1) * tm) for p in range(ROUTER_PARTS)]
    both = range(ROUTER_PARTS)
    hs = [h_ref[rows, :] for rows in parts]
    h_his = [h.astype(jnp.bfloat16) for h in hs]
    h_los = [(hs[p] - h_his[p].astype(jnp.float32)).astype(jnp.bfloat16) for p in both]
    works = [(_dot(h_his[p], whi_ref[...]) + _dot(h_los[p], whi_ref[...]) + _dot(h_his[p], wlo_ref[...]))
             + b_ref[...] for p in both]
    lane = lax.broadcasted_iota(jnp.int32, (tm, LANES), 1).astype(jnp.float32)
    vals, idxs, hots = [[] for _ in both], [[] for _ in both], [[] for _ in both]
    for _ in range(TOP_K):
        for p in both:
            m = jnp.max(works[p], axis=-1, keepdims=True)
            idx = jnp.min(jnp.where(works[p] == m, lane, float(LANES)), axis=-1, keepdims=True)
            hot = lane == idx
            vals[p].append(m)
            idxs[p].append(idx)
            hots[p].append(hot)
            works[p] = jnp.where(hot, -jnp.inf, works[p])
    row = lax.broadcasted_iota(jnp.int32, (tm, tm), 0)
    colm = lax.broadcasted_iota(jnp.int32, (tm, tm), 1)
    tri = (row > colm).astype(jnp.bfloat16)
    lane_i = lax.broadcasted_iota(jnp.int32, (tm, LANES), 1)
    run = run_ref[...]
    for p in both:
        exps = [jnp.exp(v - vals[p][0]) for v in vals[p]]
        inv = 1.0 / (exps[0] + exps[1] + exps[2] + exps[3])
        cnt = jnp.zeros((tm, LANES), jnp.float32)
        for hot in hots[p]:
            cnt = cnt + hot.astype(jnp.float32)
        before = run + _dot(tri, cnt.astype(jnp.bfloat16))
        e_out = jnp.zeros((tm, LANES), jnp.float32)
        g_out = jnp.zeros((tm, LANES), jnp.float32)
        r_out = jnp.zeros((tm, LANES), jnp.float32)
        for k in range(TOP_K):
            rank = jnp.sum(jnp.where(hots[p][k], before, 0.0), axis=-1, keepdims=True)
            e_out = jnp.where(lane_i == k, idxs[p][k], e_out)
            g_out = jnp.where(lane_i == k, exps[k] * inv, g_out)
            r_out = jnp.where(lane_i == k, rank, r_out)
        e_ref[:, parts[p]] = e_out.T[0:SUBLANES, :].astype(jnp.int32)
        g_ref[parts[p], :] = g_out
        r_ref[:, parts[p]] = r_out.T[0:SUBLANES, :].astype(jnp.int32)
        run = run + jnp.sum(cnt, axis=0, keepdims=True)
    run_ref[...] = run
    cnt_ref[...] = run.astype(jnp.int32)


def _router(h, router_w, router_b):
    w = jnp.zeros((D_MODEL, LANES), jnp.float32).at[:, :N_EXPERTS].set(router_w)
    w_hi = w.astype(jnp.bfloat16)
    w_lo = (w - w_hi.astype(jnp.float32)).astype(jnp.bfloat16)
    b = jnp.full((1, LANES), NEG_INF, jnp.float32).at[0, :N_EXPERTS].set(router_b)
    tm = ROUTER_TM * ROUTER_PARTS
    row = lambda i: (i, 0)
    const = lambda i: (0, 0)
    lanes_out = pl.BlockSpec((tm, LANES), row)
    slots_out = pl.BlockSpec((SUBLANES, tm), lambda i: (0, i))
    e, g, r, cnt = pl.pallas_call(
        _router_kernel,
        out_shape=(jax.ShapeDtypeStruct((SUBLANES, N_TOK), jnp.int32),
                   jax.ShapeDtypeStruct((N_TOK, LANES), jnp.float32),
                   jax.ShapeDtypeStruct((SUBLANES, N_TOK), jnp.int32),
                   jax.ShapeDtypeStruct((1, LANES), jnp.int32)),
        grid=(N_TOK // tm,),
        in_specs=[pl.BlockSpec((tm, D_MODEL), row), pl.BlockSpec((D_MODEL, LANES), const),
                  pl.BlockSpec((D_MODEL, LANES), const), pl.BlockSpec((1, LANES), const)],
        out_specs=(slots_out, lanes_out, slots_out, pl.BlockSpec((1, LANES), const)),
        scratch_shapes=[pltpu.VMEM((1, LANES), jnp.float32)],
        compiler_params=_params("arbitrary"),
        name="router",
    )(h, w_hi, w_lo, b)
    return e[:TOP_K], g, r[:TOP_K], cnt[0, :N_EXPERTS]


def _route_tables(top_e, rank, cnt):
    ntile = (cnt + MOE_TILE - 1) // MOE_TILE
    tile_end = jnp.cumsum(ntile)
    tile_base = tile_end - ntile
    experts = jnp.arange(N_EXPERTS, dtype=jnp.int32)[:, None, None]
    base = jnp.sum(jnp.where(top_e[None] == experts, tile_base[:, None, None], 0), axis=0)
    pos = (base * MOE_TILE + rank).astype(jnp.int32).reshape(-1)
    n_tiles = tile_end[-1:].astype(jnp.int32)
    w = jnp.minimum(jnp.arange(MOE_TILES, dtype=jnp.int32), n_tiles[0] - 1)
    tile_e = jnp.minimum(jnp.searchsorted(tile_end, w, side="right"), N_EXPERTS - 1).astype(jnp.int32)
    rows = jnp.clip(cnt[tile_e] - (w - tile_base[tile_e]) * MOE_TILE, 0, MOE_TILE)
    nsub = ((rows + MOE_SUB - 1) // MOE_SUB).astype(jnp.int32)
    pad_start = (tile_base * MOE_TILE + cnt).astype(jnp.int32)
    pad_n = ((-cnt) % MOE_SUB).astype(jnp.int32)
    return pos, n_tiles, tile_e, nsub, pad_start, pad_n


def _row_copy(src_vmem, src_row, dst_hbm, dst_row, sem):
    return pltpu.make_async_copy(src_vmem.at[pl.ds(src_row, 1)], dst_hbm.at[pl.ds(dst_row, 1)], sem)


def _pack_bf16_pairs(x):
    half = x.shape[1] // 2
    lo = lax.bitcast_convert_type(x[:, :half].astype(jnp.float32), jnp.uint32)
    hi = lax.bitcast_convert_type(x[:, half:].astype(jnp.float32), jnp.uint32)
    return (lo >> 16) | (hi & jnp.uint32(0xFFFF0000))


def _unpack_bf16_pairs(words):
    lo = lax.bitcast_convert_type(words << 16, jnp.float32)
    hi = lax.bitcast_convert_type(words & jnp.uint32(0xFFFF0000), jnp.float32)
    return lo.astype(jnp.bfloat16), hi.astype(jnp.bfloat16)


def _dispatch_kernel(pos_ref, pad_start_ref, pad_n_ref, hb_ref, xs_hbm, h_ref, sem):
    step = pl.program_id(0)
    base = step * DISPATCH_TB
    h_ref[...] = _pack_bf16_pairs(hb_ref[...])

    for r in range(DISPATCH_TB):
        for k in range(TOP_K):
            _row_copy(h_ref, r, xs_hbm, pos_ref[k * N_TOK + base + r], sem).start(priority=k % 2)

    @pl.when(step == 0)
    def _():
        def per_expert(e, carry):
            start = pad_start_ref[e]
            n = pad_n_ref[e]

            def fill(i, c):
                _row_copy(h_ref, 0, xs_hbm, start + i, sem).start()
                return c

            def drain(_, c):
                _row_copy(h_ref, 0, xs_hbm, 0, sem).wait()
                return c

            lax.fori_loop(0, n, fill, 0)
            lax.fori_loop(0, n, drain, 0)
            return carry

        lax.fori_loop(0, N_EXPERTS, per_expert, 0)

    for _ in range(TOP_K):
        pltpu.make_async_copy(h_ref, xs_hbm.at[pl.ds(0, DISPATCH_TB)], sem).wait()


def _dispatch(hb, pos, pad_start, pad_n):
    return pl.pallas_call(
        _dispatch_kernel,
        out_shape=jax.ShapeDtypeStruct((MOE_ROWS, D_MODEL // 2), jnp.uint32),
        grid_spec=pltpu.PrefetchScalarGridSpec(
            num_scalar_prefetch=3,
            grid=(N_TOK // DISPATCH_TB,),
            in_specs=[pl.BlockSpec((DISPATCH_TB, D_MODEL), lambda i, p, s, n: (i, 0))],
            out_specs=pl.BlockSpec(memory_space=pl.ANY),
            scratch_shapes=[pltpu.VMEM((DISPATCH_TB, D_MODEL // 2), jnp.uint32),
                            pltpu.SemaphoreType.DMA(())],
        ),
        compiler_params=_params("arbitrary"),
        name="moe_dispatch",
    )(pos, pad_start, pad_n, hb)


def _moe_weight_map(layer, first, hidden_axis):
    def index_map(w, j, tile_e, nsub):
        return (layer, tile_e[w], first + j, 0) if hidden_axis == 2 else (layer, tile_e[w], 0, first + j)

    return index_map


def _moe_rows(nsub, accumulate):
    fast = nsub < 0
    for s in MOE_FAST_SUBS:
        fast = jnp.logical_or(fast, nsub == s)
        pl.when(nsub == s)(functools.partial(accumulate, slice(0, s * MOE_SUB)))

    @pl.when(jnp.logical_and(nsub > 0, jnp.logical_not(fast)))
    def _():
        def one(i, carry):
            accumulate(pl.ds(pl.multiple_of(i * MOE_SUB, MOE_SUB), MOE_SUB))
            return carry

        lax.fori_loop(0, nsub, one, 0)


def _moe_expert_kernel(exp_ref, nsub_ref, x_ref, wg_ref, wl_ref, bg_ref, bl_ref, wo_ref, bo_ref, o_ref):
    first = pl.program_id(1) == 0
    nsub = nsub_ref[pl.program_id(0)]
    half = D_MODEL // 2

    @pl.when(first)
    def _():
        o_ref[...] = jnp.broadcast_to(bo_ref[...], (MOE_TILE, D_MODEL))

    def up(lo, hi, w_ref, b_ref):
        return (_dot(lo, w_ref[0:half, :].astype(jnp.bfloat16))
                + _dot(hi, w_ref[half:D_MODEL, :].astype(jnp.bfloat16)) + b_ref[...])

    def accumulate(rows):
        lo, hi = _unpack_bf16_pairs(x_ref[rows, :])
        gate = up(lo, hi, wg_ref, bg_ref)
        lin = up(lo, hi, wl_ref, bl_ref)
        gate = jnp.minimum(gate, SWIGLU_LIMIT)
        lin = jnp.clip(lin, -SWIGLU_LIMIT, SWIGLU_LIMIT)
        act = (gate * (1.0 / (1.0 + jnp.exp(-SWIGLU_ALPHA * gate))) * (lin + 1.0)).astype(jnp.bfloat16)
        for c in range(D_MODEL // MOE_DOWN_TN):
            cols = slice(c * MOE_DOWN_TN, (c + 1) * MOE_DOWN_TN)
            o_ref[rows, cols] = o_ref[rows, cols] + _dot(act, wo_ref[:, cols].astype(jnp.bfloat16))

    _moe_rows(nsub, accumulate)


def _moe_experts(xs, w_in, b_in, w_out, b_out, layer, n_tiles, tile_e, nsub):
    nj = D_EXPERT // MOE_TF
    b_in4 = b_in.reshape(DEPTH, N_EXPERTS, 1, 2 * D_EXPERT)
    b_out4 = b_out.reshape(DEPTH, N_EXPERTS, 1, D_MODEL)
    wblk = (None, None, D_MODEL, MOE_TF)
    bblk = (None, None, 1, MOE_TF)
    tile_map = lambda w, j, e, n: (w, 0)
    return pl.pallas_call(
        _moe_expert_kernel,
        out_shape=jax.ShapeDtypeStruct((MOE_ROWS, D_MODEL), jnp.float32),
        grid_spec=pltpu.PrefetchScalarGridSpec(
            num_scalar_prefetch=2,
            grid=(n_tiles[0], nj),
            in_specs=[pl.BlockSpec((MOE_TILE, D_MODEL // 2), tile_map),
                      pl.BlockSpec(wblk, _moe_weight_map(layer, 0, 3)),
                      pl.BlockSpec(wblk, _moe_weight_map(layer, nj, 3)),
                      pl.BlockSpec(bblk, _moe_weight_map(layer, 0, 3)),
                      pl.BlockSpec(bblk, _moe_weight_map(layer, nj, 3)),
                      pl.BlockSpec((None, None, MOE_TF, D_MODEL), _moe_weight_map(layer, 0, 2)),
                      pl.BlockSpec((None, None, 1, D_MODEL), lambda w, j, e, n: (layer, e[w], 0, 0))],
            out_specs=pl.BlockSpec((MOE_TILE, D_MODEL), tile_map),
        ),
        compiler_params=_params("arbitrary", "arbitrary", vmem_limit=MOE_VMEM_LIMIT),
        name="moe_experts",
    )(tile_e, nsub, xs, w_in, w_in, b_in4, b_in4, w_out, b_out4)


def _combine_ln_kernel(pos_ref, y_hbm, gates_ref, h_ref, g_ref, b_ref, hf_ref, hb_ref, buf0, buf1, sems):
    step = pl.program_id(0)
    last = pl.num_programs(0) - 1

    def row_copy(tile, r, k, buf, sem):
        src = y_hbm.at[pl.ds(pos_ref[k * N_TOK + tile * COMBINE_TB + r], 1)]
        return pltpu.make_async_copy(src, buf.at[k, pl.ds(r, 1)], sem)

    def wait_tile(buf, sem):
        for k in range(TOP_K):
            pltpu.make_async_copy(y_hbm.at[pl.ds(0, COMBINE_TB)], buf.at[k], sem).wait()

    @pl.when(step == 0)
    def _():
        def issue(r, carry):
            for k in range(TOP_K):
                row_copy(0, r, k, buf0, sems.at[0]).start()
            return carry

        lax.fori_loop(0, COMBINE_TB, issue, 0, unroll=8)

    def run(cur, cur_sem, nxt, nxt_sem):
        nxt_tile = jnp.minimum(step + 1, last)
        wait_tile(cur, cur_sem)
        for r in range(COMBINE_TB):
            for k in range(TOP_K):
                row_copy(nxt_tile, r, k, nxt, nxt_sem).start(priority=k % 2)
        gates = gates_ref[...]
        ffn = gates[:, 0:1] * cur[0]
        for k in range(1, TOP_K):
            ffn = ffn + gates[:, k:k + 1] * cur[k]
        y = _layer_norm_rows(DEEPNORM_ALPHA * h_ref[...] + ffn, g_ref[...], b_ref[...])
        hf_ref[...] = y
        hb_ref[...] = y.astype(hb_ref.dtype)

        @pl.when(step == last)
        def _():
            wait_tile(nxt, nxt_sem)

    pl.when(step % 2 == 0)(functools.partial(run, buf0, sems.at[0], buf1, sems.at[1]))
    pl.when(step % 2 == 1)(functools.partial(run, buf1, sems.at[1], buf0, sems.at[0]))


def _combine_ln(y, pos, gates, h, g, b):
    tb = COMBINE_TB
    row = lambda i, p: (i, 0)
    const = lambda i, p: (0, 0)
    return pl.pallas_call(
        _combine_ln_kernel,
        out_shape=(jax.ShapeDtypeStruct((N_TOK, D_MODEL), jnp.float32),
                   jax.ShapeDtypeStruct((N_TOK, D_MODEL), jnp.bfloat16)),
        grid_spec=pltpu.PrefetchScalarGridSpec(
            num_scalar_prefetch=1,
            grid=(N_TOK // tb,),
            in_specs=[pl.BlockSpec(memory_space=pl.ANY),
                      pl.BlockSpec((tb, LANES), row),
                      pl.BlockSpec((tb, D_MODEL), row),
                      pl.BlockSpec((1, D_MODEL), const), pl.BlockSpec((1, D_MODEL), const)],
            out_specs=(pl.BlockSpec((tb, D_MODEL), row), pl.BlockSpec((tb, D_MODEL), row)),
            scratch_shapes=[pltpu.VMEM((TOP_K, tb, D_MODEL), jnp.float32),
                            pltpu.VMEM((TOP_K, tb, D_MODEL), jnp.float32),
                            pltpu.SemaphoreType.DMA((2,))],
        ),
        compiler_params=_params("arbitrary"),
        name="moe_combine_ln",
    )(pos, y, gates, h, g.reshape(1, D_MODEL), b.reshape(1, D_MODEL))


def _moe(h, hb, layer, router_w, router_b, w_in, b_in, w_out, b_out, ln_g, ln_b):
    top_e, gates, rank, cnt = _router(h, router_w[layer], router_b[layer])
    pos, n_tiles, tile_e, nsub, pad_start, pad_n = _route_tables(top_e, rank, cnt)
    xs = _dispatch(hb, pos, pad_start, pad_n)
    y = _moe_experts(xs, w_in, b_in, w_out, b_out, layer, n_tiles, tile_e, nsub)
    return _combine_ln(y, pos, gates, h, ln_g[layer], ln_b[layer])


def kernel(x, mem, w_in_dil, w_in_ret, ret_gn_g, w_mem_kv, w_mix_out, ln_mix_g, ln_mix_b, router_w, router_b, moe_w_in, moe_b_in, moe_w_out, moe_b_out, ln_ffn_g, ln_ffn_b):
    bf16 = jnp.bfloat16
    h = x.reshape(N_TOK, D_MODEL)
    hb = h
    memb = mem.reshape(BATCH * MEM_LEN, D_MODEL)
    for layer in range(DEPTH):
        slot = layer // 2
        if layer % 2 == 0:
            kb = SELF_WIDTH // GROUP_WIDTH
            proj = _matmul(hb, w_in_dil, slot, bf16, col_tiles=(0, kb, kb + 1))
            self_out = _dilated_attention(hb, w_in_dil, slot, proj)
        else:
            proj = _matmul(hb, w_in_ret, slot, bf16)
            self_out = _retention(proj, ret_gn_g[slot])
        memkv = _matmul(memb, w_mem_kv, layer, bf16)
        mem_out = _memory_attention(proj, memkv)
        h, hb = _mix_ln(self_out, mem_out, w_mix_out[layer].astype(bf16), h,
                        ln_mix_g[layer], ln_mix_b[layer])
        h, hb = _moe(h, hb, layer, router_w, router_b, moe_w_in, moe_b_in, moe_w_out, moe_b_out,
                     ln_ffn_g, ln_ffn_b)
    return h.reshape(BATCH, SEQ, D_MODEL)
```
